```python
import math, functools
import jax, jax.numpy as jnp
from jax import lax
import numpy as np

D_MODEL = 2048
BATCH = 16
SEQ = 256
DEPTH = 4
DEC_BATCH = 2
DEC_SEQ = 1024
PAST_LEN = 512

GRID_W = 64
N_MIXERS = 3
N_A_LAYERS = (DEPTH + 2) // 3
N_B_LAYERS = (DEPTH + 1) // 3
N_C_LAYERS = DEPTH // 3
A_HEAD_DIM = 128
A_HEADS = D_MODEL // (2 * A_HEAD_DIM)
A_QKV = 3 * A_HEADS * 2 * A_HEAD_DIM
B_HEAD_DIM = 128
B_HEADS = D_MODEL // B_HEAD_DIM
B_KV_HEADS = B_HEADS // 4
B_GROUP = B_HEADS // B_KV_HEADS
B_QKV = (B_HEADS + 2 * B_KV_HEADS) * B_HEAD_DIM
C_GROUPS = 4
C_GROUP_DIM = D_MODEL // C_GROUPS
D_FF = 4 * D_MODEL
Q_BLOCK = 128
ROPE_BASE = 10000.0
ALPHA = (2 * DEPTH) ** 0.25
BETA = (8 * DEPTH) ** -0.25
LN_EPS = 1e-5
RMS_EPS = 1e-6
N_MOD = 6

kernel_name = 'hybrid_diffattn_gqa_fourier_denoise_step'


def layer_norm(x, g, b):
    xf = x.astype(jnp.float32)
    mu = xf.mean(-1, keepdims=True)
    var = jnp.square(xf - mu).mean(-1, keepdims=True)
    y = (xf - mu) * lax.rsqrt(var + LN_EPS) * g.astype(jnp.float32) + b.astype(jnp.float32)
    return y.astype(x.dtype)


def rms_norm(x, g):
    xf = x.astype(jnp.float32)
    y = xf * lax.rsqrt(jnp.mean(jnp.square(xf), -1, keepdims=True) + RMS_EPS) * g.astype(jnp.float32)
    return y.astype(x.dtype)


def axial_rope_tables(n_tokens, dim):
    rows = n_tokens // GRID_W
    row = jnp.repeat(jnp.arange(rows), GRID_W).astype(jnp.float32)
    col = jnp.tile(jnp.arange(GRID_W), rows).astype(jnp.float32)
    n_freq = dim // 4
    inv_freq = ROPE_BASE ** (-jnp.arange(n_freq, dtype=jnp.float32) / n_freq)
    ang_r = row[:, None] * inv_freq
    ang_c = col[:, None] * inv_freq
    return (jnp.cos(ang_r), jnp.sin(ang_r), jnp.cos(ang_c), jnp.sin(ang_c))


def _rotate(x, cos, sin):
    x1, x2 = jnp.split(x, 2, axis=-1)
    return jnp.concatenate([x1 * cos - x2 * sin, x1 * sin + x2 * cos], axis=-1)


def apply_axial_rope(x, tables):
    cos_r, sin_r, cos_c, sin_c = tables
    shape = (x.shape[1],) + (1,) * (x.ndim - 3) + (cos_r.shape[-1],)
    half = x.shape[-1] // 2
    xf = x.astype(jnp.float32)
    out = jnp.concatenate([
        _rotate(xf[..., :half], cos_r.reshape(shape), sin_r.reshape(shape)),
        _rotate(xf[..., half:], cos_c.reshape(shape), sin_c.reshape(shape))], axis=-1)
    return out.astype(x.dtype)


def sweep_query_blocks(attend, q):
    b, s = q.shape[0], q.shape[1]
    n_blocks = s // Q_BLOCK
    qb = jnp.moveaxis(q.reshape((b, n_blocks, Q_BLOCK) + q.shape[2:]), 1, 0)
    out = jnp.moveaxis(lax.map(attend, qb), 0, 1)
    return out.reshape((b, s) + out.shape[3:])


def diff_attention(q, k, v, lam):
    scale = A_HEAD_DIM ** -0.5

    def attend(qb):
        s = jnp.einsum('bqhpd,bkhpd->bhpqk', qb, k, preferred_element_type=jnp.float32) * scale
        p = jax.nn.softmax(s, axis=-1)
        a = p[:, :, 0] - lam * p[:, :, 1]
        return jnp.einsum('bhqk,bkhe->bqhe', a.astype(v.dtype), v)

    return sweep_query_blocks(attend, q)


def diff_project(u, w_qkv):
    b, s, _ = u.shape
    q, k, v = jnp.split(u @ w_qkv, 3, axis=-1)
    return (q.reshape(b, s, A_HEADS, 2, A_HEAD_DIM),
            k.reshape(b, s, A_HEADS, 2, A_HEAD_DIM),
            v.reshape(b, s, A_HEADS, 2 * A_HEAD_DIM))


def diff_lambda(lam_params, layer_idx):
    lam_init = 0.8 - 0.6 * math.exp(-0.3 * layer_idx)
    lp = lam_params.astype(jnp.float32)
    lam = jnp.exp(jnp.sum(lp[0] * lp[1])) - jnp.exp(jnp.sum(lp[2] * lp[3])) + lam_init
    return lam, lam_init


def diff_output(o, subln_g, lam_init, w_o):
    b, s = o.shape[0], o.shape[1]
    o = rms_norm(o, subln_g) * (1.0 - lam_init)
    return o.reshape(b, s, -1) @ w_o


def diff_mixer_context(u, w_qkv, w_o, lam, lam_init, subln_g):
    q, k, v = diff_project(u, w_qkv)
    o = diff_attention(q, k, v, lam)
    return diff_output(o, subln_g, lam_init, w_o), (k, v)


def diff_mixer_latent(u, ctx_k, ctx_v, rope, w_qkv, w_o, lam, lam_init, subln_g):
    q, k, v = diff_project(u, w_qkv)
    q = apply_axial_rope(q, rope)
    k = apply_axial_rope(k, rope)
    k_all = jnp.concatenate([ctx_k, k], axis=1)
    v_all = jnp.concatenate([ctx_v, v], axis=1)
    o = diff_attention(q, k_all, v_all, lam)
    return diff_output(o, subln_g, lam_init, w_o), None


def gqa_attention(q, k, v):
    scale = B_HEAD_DIM ** -0.5

    def attend(qb):
        s = jnp.einsum('bqngd,bknd->bngqk', qb, k, preferred_element_type=jnp.float32) * scale
        p = jax.nn.softmax(s, axis=-1)
        return jnp.einsum('bngqk,bknd->bqngd', p.astype(v.dtype), v)

    return sweep_query_blocks(attend, q)


def gqa_project(u, w_qkv, q_g, k_g):
    b, s, _ = u.shape
    qkv = u @ w_qkv
    hq = B_HEADS * B_HEAD_DIM
    hk = B_KV_HEADS * B_HEAD_DIM
    q = qkv[..., :hq].reshape(b, s, B_KV_HEADS, B_GROUP, B_HEAD_DIM)
    k = qkv[..., hq:hq + hk].reshape(b, s, B_KV_HEADS, B_HEAD_DIM)
    v = qkv[..., hq + hk:].reshape(b, s, B_KV_HEADS, B_HEAD_DIM)
    return rms_norm(q, q_g), rms_norm(k, k_g), v


def gqa_mixer_context(u, w_qkv, w_o, q_g, k_g):
    b, s, _ = u.shape
    q, k, v = gqa_project(u, w_qkv, q_g, k_g)
    o = gqa_attention(q, k, v)
    return o.reshape(b, s, -1) @ w_o, (k, v)


def gqa_mixer_latent(u, ctx_k, ctx_v, rope, w_qkv, w_o, q_g, k_g):
    b, s, _ = u.shape
    q, k, v = gqa_project(u, w_qkv, q_g, k_g)
    q = apply_axial_rope(q, rope)
    k = apply_axial_rope(k, rope)
    k_all = jnp.concatenate([ctx_k, k], axis=1)
    v_all = jnp.concatenate([ctx_v, v], axis=1)
    o = gqa_attention(q, k_all, v_all)
    return o.reshape(b, s, -1) @ w_o, None


def fourier_mixer(u, w_f):
    b, s, _ = u.shape
    uf = u.astype(jnp.float32).reshape(b, s, C_GROUPS, C_GROUP_DIM)
    y = jnp.fft.fft2(uf, axes=(1, 3), norm='ortho').real
    return y.reshape(b, s, D_MODEL).astype(u.dtype) @ w_f, None


def modulation(cond, w_mod, b_mod):
    return (jax.nn.silu(cond) @ w_mod + b_mod).reshape(cond.shape[0], N_MOD, D_MODEL)


def sq_relu_mlp(u, w_up, w_down):
    return jnp.square(jax.nn.relu(u @ w_up)) @ w_down


def trunk_block(x, mod, mixer, ln_g, ln_b, w_up, w_down):
    shift1, scale1, gate1, shift2, scale2, gate2 = (mod[:, None, i] for i in range(N_MOD))
    mixed, aux = mixer(x * (1 + scale1) + shift1)
    x = layer_norm(ALPHA * x + gate1 * mixed, ln_g[0], ln_b[0])
    ff = sq_relu_mlp(x * (1 + scale2) + shift2, w_up, w_down)
    x = layer_norm(ALPHA * x + gate2 * ff, ln_g[1], ln_b[1])
    return x, aux


def setup_inputs(seed: int = 0) -> dict:
    key = jax.random.key(seed)
    ks = jax.random.split(key, 24)
    f32 = jnp.float32

    def nrm(k, shape, scale):
        return jax.random.normal(k, shape, f32) * scale

    return {
        'x_prompt': nrm(ks[0], (BATCH, SEQ, D_MODEL), 1.0),
        'x_sample': nrm(ks[1], (DEC_BATCH, DEC_SEQ, D_MODEL), 1.0),
        'cache_a_k': nrm(ks[2], (DEC_BATCH, N_A_LAYERS, PAST_LEN, A_HEADS, 2, A_HEAD_DIM), 1.0),
        'cache_a_v': nrm(ks[3], (DEC_BATCH, N_A_LAYERS, PAST_LEN, A_HEADS, 2 * A_HEAD_DIM), 1.0),
        'cache_b_k': nrm(ks[4], (DEC_BATCH, N_B_LAYERS, PAST_LEN, B_KV_HEADS, B_HEAD_DIM), 1.0),
        'cache_b_v': nrm(ks[5], (DEC_BATCH, N_B_LAYERS, PAST_LEN, B_KV_HEADS, B_HEAD_DIM), 1.0),
        'c': nrm(ks[6], (DEC_BATCH, D_MODEL), 1.0),
        'c_ctx': nrm(ks[7], (D_MODEL,), 1.0),
        'w_mod': nrm(ks[8], (DEPTH, D_MODEL, N_MOD * D_MODEL), 0.5 * D_MODEL ** -0.5),
        'b_mod': nrm(ks[9], (DEPTH, N_MOD * D_MODEL), 0.02),
        'ln_g': 1.0 + nrm(ks[10], (DEPTH, 2, D_MODEL), 0.02),
        'ln_b': nrm(ks[11], (DEPTH, 2, D_MODEL), 0.02),
        'w_up': nrm(ks[12], (DEPTH, D_MODEL, D_FF), D_MODEL ** -0.5),
        'w_down': nrm(ks[13], (DEPTH, D_FF, D_MODEL), BETA * D_FF ** -0.5),
        'a_w_qkv': nrm(ks[14], (N_A_LAYERS, D_MODEL, A_QKV), D_MODEL ** -0.5),
        'a_w_o': nrm(ks[15], (N_A_LAYERS, 2 * A_HEADS * A_HEAD_DIM, D_MODEL), BETA * (2 * A_HEADS * A_HEAD_DIM) ** -0.5),
        'a_lambda': nrm(ks[16], (N_A_LAYERS, 4, A_HEAD_DIM), 0.1),
        'a_subln_g': 1.0 + nrm(ks[17], (N_A_LAYERS, 2 * A_HEAD_DIM), 0.02),
        'b_w_qkv': nrm(ks[18], (N_B_LAYERS, D_MODEL, B_QKV), D_MODEL ** -0.5),
        'b_w_o': nrm(ks[19], (N_B_LAYERS, B_HEADS * B_HEAD_DIM, D_MODEL), BETA * (B_HEADS * B_HEAD_DIM) ** -0.5),
        'b_q_norm_g': 1.0 + nrm(ks[20], (N_B_LAYERS, B_HEAD_DIM), 0.02),
        'b_k_norm_g': 1.0 + nrm(ks[21], (N_B_LAYERS, B_HEAD_DIM), 0.02),
        'c_w_f': nrm(ks[22], (N_C_LAYERS, D_MODEL, D_MODEL), BETA * D_MODEL ** -0.5),
    }


def reference(x_prompt, x_sample, cache_a_k, cache_a_v, cache_b_k, cache_b_v, c, c_ctx,
              w_mod, b_mod, ln_g, ln_b, w_up, w_down,
              a_w_qkv, a_w_o, a_lambda, a_subln_g,
              b_w_qkv, b_w_o, b_q_norm_g, b_k_norm_g, c_w_f):
    n_lat = x_sample.shape[1]
    rope_a = axial_rope_tables(n_lat, A_HEAD_DIM)
    rope_b = axial_rope_tables(n_lat, B_HEAD_DIM)

    y_p, y_s = x_prompt, x_sample
    a_ks, a_vs, b_ks, b_vs = [], [], [], []
    for l in range(DEPTH):
        kind, j = l % N_MIXERS, l // N_MIXERS
        mod_p = modulation(c_ctx[None, :], w_mod[l], b_mod[l])
        mod_s = modulation(c, w_mod[l], b_mod[l])
        if kind == 0:
            lam, lam_init = diff_lambda(a_lambda[j], l)
            ctx_mix = functools.partial(diff_mixer_context, w_qkv=a_w_qkv[j], w_o=a_w_o[j],
                                        lam=lam, lam_init=lam_init, subln_g=a_subln_g[j])
            lat_mix = functools.partial(diff_mixer_latent, ctx_k=cache_a_k[:, j], ctx_v=cache_a_v[:, j],
                                        rope=rope_a, w_qkv=a_w_qkv[j], w_o=a_w_o[j],
                                        lam=lam, lam_init=lam_init, subln_g=a_subln_g[j])
        elif kind == 1:
            ctx_mix = functools.partial(gqa_mixer_context, w_qkv=b_w_qkv[j], w_o=b_w_o[j],
                                        q_g=b_q_norm_g[j], k_g=b_k_norm_g[j])
            lat_mix = functools.partial(gqa_mixer_latent, ctx_k=cache_b_k[:, j], ctx_v=cache_b_v[:, j],
                                        rope=rope_b, w_qkv=b_w_qkv[j], w_o=b_w_o[j],
                                        q_g=b_q_norm_g[j], k_g=b_k_norm_g[j])
        else:
            ctx_mix = functools.partial(fourier_mixer, w_f=c_w_f[j])
            lat_mix = functools.partial(fourier_mixer, w_f=c_w_f[j])

        y_p, aux = trunk_block(y_p, mod_p, ctx_mix, ln_g[l], ln_b[l], w_up[l], w_down[l])
        y_s, _ = trunk_block(y_s, mod_s, lat_mix, ln_g[l], ln_b[l], w_up[l], w_down[l])
        if kind == 0:
            a_ks.append(aux[0])
            a_vs.append(aux[1])
        elif kind == 1:
            b_ks.append(aux[0])
            b_vs.append(aux[1])

    new_a_k = jnp.stack(a_ks, axis=1)
    new_a_v = jnp.stack(a_vs, axis=1)
    new_b_k = jnp.stack(b_ks, axis=1)
    new_b_v = jnp.stack(b_vs, axis=1)
    return (y_p, y_s, new_a_k, new_a_v, new_b_k, new_b_v)
```

```python
import functools
import math

import jax
import jax.numpy as jnp
import numpy as np
from jax import lax
from jax.experimental import pallas as pl
from jax.experimental.pallas import tpu as pltpu

D_MODEL = 2048
BATCH = 16
SEQ = 256
DEPTH = 4
DEC_BATCH = 2
DEC_SEQ = 1024
PAST_LEN = 512
GRID_W = 64
N_MIXERS = 3
A_HEAD_DIM = 128
A_HEADS = D_MODEL // (2 * A_HEAD_DIM)
A_QKV = 3 * D_MODEL
B_HEAD_DIM = 128
B_HEADS = D_MODEL // B_HEAD_DIM
B_KV_HEADS = B_HEADS // 4
B_GROUP = B_HEADS // B_KV_HEADS
B_KV = B_KV_HEADS * B_HEAD_DIM
B_QKV = D_MODEL + 2 * B_KV
C_GROUPS = 4
C_GROUP_DIM = D_MODEL // C_GROUPS
D_FF = 4 * D_MODEL
ROPE_BASE = 10000.0
ALPHA = (2 * DEPTH) ** 0.25
LN_EPS = 1e-5
RMS_EPS = 1e-6
N_MOD = 6

M_P = BATCH * SEQ
M_S = DEC_BATCH * DEC_SEQ
M_ALL = M_P + M_S
N_COND = 1 + DEC_BATCH
COND_PAD = 8

LANE = 128
VMEM_LIMIT = 58 * 1024 * 1024

BF16 = jnp.bfloat16
F32 = jnp.float32


def _params(*sem, vmem=VMEM_LIMIT):
    return pltpu.CompilerParams(dimension_semantics=sem, vmem_limit_bytes=vmem)


def _cond_row(i, tm):
    start = i * tm
    return jnp.where(start < M_P, 0, 1 + (start - M_P) // DEC_SEQ)


def _mod_spec(l, which, tm, grid_rank, m_axis, m0=0):
    def idx(*g):
        return (l, _cond_row(g[m_axis] + m0, tm), which, 0, 0)
    return pl.BlockSpec((None, None, None, 1, D_MODEL), idx)


def _vec_spec(l, which):
    return pl.BlockSpec((None, None, 1, D_MODEL), lambda *g: (l, which, 0, 0))


def _mod_kernel(c_ref, w_ref, b_ref, o_ref):
    c = c_ref[...]
    s = c * (1.0 / (1.0 + jnp.exp(-c)))
    o_ref[...] = jnp.dot(s.astype(BF16), w_ref[...].astype(BF16),
                         preferred_element_type=F32) + b_ref[...]


def _modulation(cond, w_mod, b_mod):
    tn = 1024
    n_out = N_MOD * D_MODEL
    return pl.pallas_call(
        _mod_kernel,
        out_shape=jax.ShapeDtypeStruct((DEPTH, COND_PAD, n_out), F32),
        grid=(DEPTH, n_out // tn),
        in_specs=[
            pl.BlockSpec((COND_PAD, D_MODEL), lambda l, n: (0, 0)),
            pl.BlockSpec((None, D_MODEL, tn), lambda l, n: (l, 0, n)),
            pl.BlockSpec((None, 1, tn), lambda l, n: (l, 0, n)),
        ],
        out_specs=pl.BlockSpec((None, COND_PAD, tn), lambda l, n: (l, 0, n)),
        compiler_params=_params("arbitrary", "arbitrary"),
        name="modulation",
    )(cond, w_mod, b_mod.reshape(DEPTH, 1, n_out))


def _embed_kernel(xp_ref, xs_ref, sc_ref, sh_ref, x_ref, u_ref):
    i = pl.program_id(0)

    def emit(x):
        x_ref[...] = x
        u_ref[...] = (x * (1.0 + sc_ref[...]) + sh_ref[...]).astype(BF16)

    @pl.when(i * EMBED_ROWS < M_P)
    def _():
        emit(xp_ref[...])

    @pl.when(i * EMBED_ROWS >= M_P)
    def _():
        emit(xs_ref[...])


EMBED_ROWS = 256


def _embed(xp, xs, mod):
    tm = EMBED_ROWS
    n_p = M_P // tm
    return pl.pallas_call(
        _embed_kernel,
        out_shape=(jax.ShapeDtypeStruct((M_ALL, D_MODEL), F32),
                   jax.ShapeDtypeStruct((M_ALL, D_MODEL), BF16)),
        grid=(M_ALL // tm,),
        in_specs=[
            pl.BlockSpec((tm, D_MODEL), lambda i: (jnp.minimum(i, n_p - 1), 0)),
            pl.BlockSpec((tm, D_MODEL), lambda i: (jnp.maximum(i - n_p, 0), 0)),
            _mod_spec(0, 1, tm, 1, 0),
            _mod_spec(0, 0, tm, 1, 0),
        ],
        out_specs=(pl.BlockSpec((tm, D_MODEL), lambda i: (i, 0)),
                   pl.BlockSpec((tm, D_MODEL), lambda i: (i, 0))),
        compiler_params=_params("arbitrary"),
        name="embed",
    )(xp, xs, mod, mod)


def _rms_chunk(x, g):
    ms = jnp.mean(x * x, axis=-1, keepdims=True)
    return x * lax.rsqrt(ms + RMS_EPS) * g


def _rope_chunk(x, cos, sin_a, sin_b):
    return x * cos + pltpu.roll(x, LANE - 32, 1) * sin_a + pltpu.roll(x, 32, 1) * sin_b


def _proj_kernel(*refs, tn, rms, rope, has_dst, out_rows):
    it = iter(refs)
    u_ref, w_ref = next(it), next(it)
    g_ref = next(it) if rms else None
    tabs = (next(it), next(it), next(it)) if rope else None
    if has_dst:
        next(it)
    o_ref, wbf = next(it), next(it)

    @pl.when(pl.program_id(1) == 0)
    def _():
        wbf[...] = w_ref[...].astype(BF16)

    acc = jnp.dot(u_ref[...], wbf[...], preferred_element_type=F32)

    def fold(x):
        return x if out_rows is None else x.reshape(x.shape[0] // out_rows, out_rows, x.shape[1])

    if rms or rope:
        g = g_ref[...] if rms else None
        tab = tuple(t[...] for t in tabs) if rope else None
        for c in range(tn // LANE):
            x = acc[:, c * LANE:(c + 1) * LANE]
            if rms:
                x = _rms_chunk(x, g)
            if rope:
                x = _rope_chunk(x, *tab)
            o_ref[..., c * LANE:(c + 1) * LANE] = fold(x).astype(o_ref.dtype)
    else:
        o_ref[...] = fold(acc).astype(o_ref.dtype)


def _proj(u, w, w_layer, col0, ncols, m0, m_rows, *, out_dtype, tn=1024, tm=1024,
          gains=None, gain_split=None, rope=None, dst=None, dst_layer=None, name="proj"):
    k = u.shape[1]
    assert col0 % tn == 0 and ncols % tn == 0 and m0 % tm == 0 and m_rows % tm == 0
    n_t, m_t, mt0, nt0 = ncols // tn, m_rows // tm, m0 // tm, col0 // tn
    ins = [u, w]
    in_specs = [
        pl.BlockSpec((tm, k), lambda n, m: (mt0 + m, 0)),
        pl.BlockSpec((None, k, tn), lambda n, m: (w_layer, 0, nt0 + n)),
    ]
    if gains is not None:
        ins.append(gains)
        in_specs.append(pl.BlockSpec(
            (None, 1, LANE), lambda n, m: (jnp.where(n * tn < gain_split, 0, 1), 0, 0)))
    if rope is not None:
        assert m0 >= M_P
        per = DEC_SEQ // tm
        for t in rope:
            ins.append(t)
            in_specs.append(pl.BlockSpec((tm, LANE), lambda n, m: ((mt0 + m) % per, 0)))
    out_rows = None
    if dst_layer is not None:
        assert m0 == 0 and m_rows == M_P and tm % SEQ == 0
        out_rows = SEQ
        out_shape = jax.ShapeDtypeStruct((BATCH, w.shape[0], SEQ, ncols), out_dtype)
        out_spec = pl.BlockSpec((tm // SEQ, None, SEQ, tn), lambda n, m: (m, dst_layer, 0, n))
    else:
        out_shape = jax.ShapeDtypeStruct((m_rows, ncols), out_dtype)
        out_spec = pl.BlockSpec((tm, tn), lambda n, m: (m, n))
    aliases = {}
    if dst is not None:
        aliases = {len(ins): 0}
        ins.append(dst)
        in_specs.append(pl.BlockSpec(memory_space=pl.ANY))
    kern = functools.partial(_proj_kernel, tn=tn, rms=gains is not None, rope=rope is not None,
                             has_dst=dst is not None, out_rows=out_rows)
    return pl.pallas_call(
        kern,
        out_shape=out_shape,
        grid=(n_t, m_t),
        in_specs=in_specs,
        out_specs=out_spec,
        scratch_shapes=[pltpu.VMEM((k, tn), BF16)],
        input_output_aliases=aliases,
        compiler_params=_params("arbitrary", "arbitrary"),
        name=name,
    )(*ins)


def _qkt(q, k):
    return lax.dot_general(q, k, (((1,), (1,)), ((), ())), preferred_element_type=F32)


def _softmax_parts(scores, scale):
    m = scores[0].max(axis=-1, keepdims=True)
    for s in scores[1:]:
        m = jnp.maximum(m, s.max(axis=-1, keepdims=True))
    es = [jnp.exp((s - m) * scale) for s in scores]
    tot = es[0].sum(axis=-1, keepdims=True)
    for e in es[1:]:
        tot = tot + e.sum(axis=-1, keepdims=True)
    return es, 1.0 / tot


def _diff_lambda(lp_ref, lam_init):
    lp = lp_ref[...]
    s1 = jnp.sum(lp[0:1] * lp[1:2], axis=-1, keepdims=True)
    s2 = jnp.sum(lp[2:3] * lp[3:4], axis=-1, keepdims=True)
    return jnp.exp(s1) - jnp.exp(s2) + lam_init


def _diff_head(q, ks, vs, lam, gain):
    scale = A_HEAD_DIM ** -0.5
    q0, q1 = q[:, :A_HEAD_DIM], q[:, A_HEAD_DIM:]
    e0, r0 = _softmax_parts([_qkt(q0, k[:, :A_HEAD_DIM]) for k in ks], scale)
    e1, r1 = _softmax_parts([_qkt(q1, k[:, A_HEAD_DIM:]) for k in ks], scale)
    r1 = lam * r1
    o = None
    for a0, a1, v in zip(e0, e1, vs):
        a = (a0 * r0 - a1 * r1).astype(BF16)
        part = jnp.dot(a, v, preferred_element_type=F32)
        o = part if o is None else o + part
    ms = jnp.mean(o * o, axis=-1, keepdims=True)
    return (o * lax.rsqrt(ms + RMS_EPS) * gain).astype(BF16)


def _gqa_head(q, ks, vs):
    es, r = _softmax_parts([_qkt(q, k) for k in ks], B_HEAD_DIM ** -0.5)
    o = None
    for e, v in zip(es, vs):
        part = jnp.dot((e * r).astype(BF16), v, preferred_element_type=F32)
        o = part if o is None else o + part
    return o.astype(BF16)


def _diff_prompt_kernel(lp_ref, g_ref, q_ref, k_ref, v_ref, o_ref, *, lam_init):
    lam = _diff_lambda(lp_ref, lam_init)
    gain = g_ref[...] * (1.0 - lam_init)
    w = 2 * A_HEAD_DIM
    for h in range(A_HEADS):
        sl = slice(h * w, (h + 1) * w)
        o_ref[:, sl] = _diff_head(q_ref[:, sl], [k_ref[:, sl].astype(BF16)],
                                  [v_ref[:, sl].astype(BF16)], lam, gain)


def _diff_prompt(lp, g, q, kbuf, vbuf, j, lam_init):
    kv_spec = pl.BlockSpec((None, None, SEQ, D_MODEL), lambda b: (b, j, 0, 0))
    return pl.pallas_call(
        functools.partial(_diff_prompt_kernel, lam_init=lam_init),
        out_shape=jax.ShapeDtypeStruct((M_ALL, D_MODEL), BF16),
        grid=(BATCH,),
        in_specs=[
            pl.BlockSpec((None, 4, A_HEAD_DIM), lambda b: (j, 0, 0)),
            pl.BlockSpec((None, 1, 2 * A_HEAD_DIM), lambda b: (j, 0, 0)),
            pl.BlockSpec((SEQ, D_MODEL), lambda b: (b, 0)),
            kv_spec, kv_spec,
        ],
        out_specs=pl.BlockSpec((SEQ, D_MODEL), lambda b: (b, 0)),
        compiler_params=_params("arbitrary"),
        name="diff_attn_prompt",
    )(lp, g, q, kbuf, vbuf)


def _diff_latent_kernel(lp_ref, g_ref, q_ref, k_ref, v_ref, ck_ref, cv_ref, dst_ref, o_ref,
                        *, lam_init):
    del dst_ref
    lam = _diff_lambda(lp_ref, lam_init)
    gain = g_ref[...] * (1.0 - lam_init)
    o_ref[...] = _diff_head(q_ref[...], [ck_ref[...].astype(BF16), k_ref[...]],
                            [cv_ref[...].astype(BF16), v_ref[...]], lam, gain)


def _diff_latent(lp, g, qk, v, ck, cv, dst, j, lam_init, tq=512):
    w = 2 * A_HEAD_DIM
    per = DEC_SEQ // tq
    row0 = M_P // tq
    ctx_spec = pl.BlockSpec((None, None, PAST_LEN, w), lambda b, h, t: (b, j, 0, h))
    return pl.pallas_call(
        functools.partial(_diff_latent_kernel, lam_init=lam_init),
        out_shape=jax.ShapeDtypeStruct((M_ALL, D_MODEL), BF16),
        grid=(DEC_BATCH, A_HEADS, per),
        in_specs=[
            pl.BlockSpec((None, 4, A_HEAD_DIM), lambda b, h, t: (j, 0, 0)),
            pl.BlockSpec((None, 1, w), lambda b, h, t: (j, 0, 0)),
            pl.BlockSpec((tq, w), lambda b, h, t: (b * per + t, h)),
            pl.BlockSpec((DEC_SEQ, w), lambda b, h, t: (b, A_HEADS + h)),
            pl.BlockSpec((DEC_SEQ, w), lambda b, h, t: (b, h)),
            ctx_spec, ctx_spec,
            pl.BlockSpec(memory_space=pl.ANY),
        ],
        out_specs=pl.BlockSpec((tq, w), lambda b, h, t: (row0 + b * per + t, h)),
        input_output_aliases={7: 0},
        compiler_params=_params("arbitrary", "arbitrary", "arbitrary"),
        name="diff_attn_latent",
    )(lp, g, qk, qk, v, ck, cv, dst)


def _gqa_prompt_kernel(q_ref, k_ref, v_ref, o_ref):
    d = B_HEAD_DIM
    for n in range(B_KV_HEADS):
        k = [k_ref[:, n * d:(n + 1) * d].astype(BF16)]
        v = [v_ref[:, n * d:(n + 1) * d].astype(BF16)]
        for g in range(B_GROUP):
            sl = slice((n * B_GROUP + g) * d, (n * B_GROUP + g + 1) * d)
            o_ref[:, sl] = _gqa_head(q_ref[:, sl], k, v)


def _gqa_prompt(q, kbuf, vbuf, j):
    kv_spec = pl.BlockSpec((None, None, SEQ, B_KV), lambda b: (b, j, 0, 0))
    return pl.pallas_call(
        _gqa_prompt_kernel,
        out_shape=jax.ShapeDtypeStruct((M_ALL, D_MODEL), BF16),
        grid=(BATCH,),
        in_specs=[
            pl.BlockSpec((SEQ, D_MODEL), lambda b: (b, 0)),
            kv_spec, kv_spec,
        ],
        out_specs=pl.BlockSpec((SEQ, D_MODEL), lambda b: (b, 0)),
        compiler_params=_params("arbitrary"),
        name="gqa_attn_prompt",
    )(q, kbuf, vbuf)


def _gqa_latent_kernel(q_ref, k_ref, v_ref, ck_ref, cv_ref, dst_ref, o_ref):
    del dst_ref
    d = B_HEAD_DIM
    ks = [ck_ref[...].astype(BF16), k_ref[...]]
    vs = [cv_ref[...].astype(BF16), v_ref[...]]
    for g in range(B_GROUP):
        o_ref[:, g * d:(g + 1) * d] = _gqa_head(q_ref[:, g * d:(g + 1) * d], ks, vs)


def _gqa_latent(qk, v, ck, cv, dst, j, tq=512):
    d = B_HEAD_DIM
    gw = B_GROUP * d
    per = DEC_SEQ // tq
    row0 = M_P // tq
    ctx_spec = pl.BlockSpec((None, None, PAST_LEN, d), lambda b, n, t: (b, j, 0, n))
    return pl.pallas_call(
        _gqa_latent_kernel,
        out_shape=jax.ShapeDtypeStruct((M_ALL, D_MODEL), BF16),
        grid=(DEC_BATCH, B_KV_HEADS, per),
        in_specs=[
            pl.BlockSpec((tq, gw), lambda b, n, t: (b * per + t, n)),
            pl.BlockSpec((DEC_SEQ, d), lambda b, n, t: (b, D_MODEL // d + n)),
            pl.BlockSpec((DEC_SEQ, d), lambda b, n, t: (b, n)),
            ctx_spec, ctx_spec,
            pl.BlockSpec(memory_space=pl.ANY),
        ],
        out_specs=pl.BlockSpec((tq, gw), lambda b, n, t: (row0 + b * per + t, n)),
        input_output_aliases={5: 0},
        compiler_params=_params("arbitrary", "arbitrary", "arbitrary"),
        name="gqa_attn_latent",
    )(qk, qk, v, ck, cv, dst)


def _dft_tables(s):
    def cs(n):
        idx = (np.arange(n)[:, None] * np.arange(n)[None, :]) % n
        ang = 2.0 * np.pi * idx / n
        return np.cos(ang), np.sin(ang)
    cc, sc = cs(C_GROUP_DIM)
    c_s, s_s = cs(s)
    t_chan = np.concatenate([cc, sc], axis=1).astype(np.float32)
    t_pos = np.concatenate([c_s, -s_s], axis=1).astype(np.float32)
    return jnp.asarray(t_chan).astype(BF16), jnp.asarray(t_pos).astype(BF16)


def _dft_group(x, t_chan, t_pos, norm):
    xcs = jnp.dot(x, t_chan, preferred_element_type=F32).astype(BF16)
    stacked = jnp.concatenate([xcs[:, :C_GROUP_DIM], xcs[:, C_GROUP_DIM:]], axis=0)
    y = jnp.dot(t_pos, stacked, preferred_element_type=F32)
    return (y * norm).astype(BF16)


def _dft_kernel(u_ref, tc_ref, tp_ref, *rest, groups, norm):
    o_ref = rest[-1]
    tc, tp = tc_ref[...], tp_ref[...]
    for g in range(groups):
        sl = slice(g * C_GROUP_DIM, (g + 1) * C_GROUP_DIM)
        o_ref[:, sl] = _dft_group(u_ref[:, sl], tc, tp, norm)


def _dft(u, dst, s, batch, row0, groups_per_step):
    t_chan, t_pos = _dft_tables(s)
    gw = groups_per_step * C_GROUP_DIM
    n_g = C_GROUPS // groups_per_step
    rb0 = row0 // s
    ins = [u, t_chan, t_pos]
    in_specs = [
        pl.BlockSpec((s, gw), lambda b, g: (rb0 + b, g)),
        pl.BlockSpec(t_chan.shape, lambda b, g: (0, 0)),
        pl.BlockSpec(t_pos.shape, lambda b, g: (0, 0)),
    ]
    aliases = {}
    if dst is not None:
        aliases = {3: 0}
        ins.append(dst)
        in_specs.append(pl.BlockSpec(memory_space=pl.ANY))
    return pl.pallas_call(
        functools.partial(_dft_kernel, groups=groups_per_step,
                          norm=1.0 / math.sqrt(s * C_GROUP_DIM)),
        out_shape=jax.ShapeDtypeStruct((M_ALL, D_MODEL), BF16),
        grid=(batch, n_g),
        in_specs=in_specs,
        out_specs=pl.BlockSpec((s, gw), lambda b, g: (rb0 + b, g)),
        input_output_aliases=aliases,
        compiler_params=_params("arbitrary", "arbitrary"),
        name="dft",
    )(*ins)


def _post_norm(y, g, b):
    mu = jnp.mean(y, axis=-1, keepdims=True)
    yc = y - mu
    var = jnp.mean(yc * yc, axis=-1, keepdims=True)
    return yc * lax.rsqrt(var + LN_EPS) * g + b


def _oproj_kernel(a_ref, w_ref, x_ref, gate_ref, g_ref, b_ref, sc_ref, sh_ref,
                  x1_ref, u_ref, wbf):
    @pl.when(pl.program_id(0) == 0)
    def _():
        wbf[...] = w_ref[...].astype(BF16)

    mixed = jnp.dot(a_ref[...], wbf[...], preferred_element_type=F32)
    x1 = _post_norm(ALPHA * x_ref[...] + gate_ref[...] * mixed, g_ref[...], b_ref[...])
    x1_ref[...] = x1
    u_ref[...] = (x1 * (1.0 + sc_ref[...]) + sh_ref[...]).astype(BF16)


def _oproj(a, w, w_layer, x, mod, ln_g, ln_b, l, tm=256):
    return pl.pallas_call(
        _oproj_kernel,
        out_shape=(jax.ShapeDtypeStruct((M_ALL, D_MODEL), F32),
                   jax.ShapeDtypeStruct((M_ALL, D_MODEL), BF16)),
        grid=(M_ALL // tm,),
        in_specs=[
            pl.BlockSpec((tm, D_MODEL), lambda i: (i, 0)),
            pl.BlockSpec((None, D_MODEL, D_MODEL), lambda i: (w_layer, 0, 0),
                         pipeline_mode=pl.Buffered(1)),
            pl.BlockSpec((tm, D_MODEL), lambda i: (i, 0)),
            _mod_spec(l, 2, tm, 1, 0),
            _vec_spec(l, 0), _vec_spec(l, 0),
            _mod_spec(l, 4, tm, 1, 0),
            _mod_spec(l, 3, tm, 1, 0),
        ],
        out_specs=(pl.BlockSpec((tm, D_MODEL), lambda i: (i, 0)),
                   pl.BlockSpec((tm, D_MODEL), lambda i: (i, 0))),
        scratch_shapes=[pltpu.VMEM((D_MODEL, D_MODEL), BF16)],
        compiler_params=_params("arbitrary"),
        name="oproj_ln",
    )(a, w, x, mod, ln_g, ln_b, mod, mod)


EPILOGUE_ROWS = 256


def _mlp_kernel(u_ref, wu_ref, wd_ref, x_ref, gate_ref, g_ref, b_ref, sc_ref, sh_ref,
                x2_ref, un_ref):
    f = pl.program_id(1)
    h = jnp.dot(u_ref[...], wu_ref[...].astype(BF16), preferred_element_type=F32)
    h = jnp.maximum(h, 0.0)
    h = (h * h).astype(BF16)
    ff = jnp.dot(h, wd_ref[...].astype(BF16), preferred_element_type=F32)

    @pl.when(f == 0)
    def _():
        x2_ref[...] = ff

    @pl.when(f > 0)
    def _():
        x2_ref[...] += ff

    @pl.when(f == pl.num_programs(1) - 1)
    def _():
        gate, g, b = gate_ref[...], g_ref[...], b_ref[...]
        sc1, sh = 1.0 + sc_ref[...], sh_ref[...]

        def rows_body(r, carry):
            rows = pl.ds(pl.multiple_of(r * EPILOGUE_ROWS, EPILOGUE_ROWS), EPILOGUE_ROWS)
            x2 = _post_norm(ALPHA * x_ref[rows, :] + gate * x2_ref[rows, :], g, b)
            x2_ref[rows, :] = x2
            un_ref[rows, :] = (x2 * sc1 + sh).astype(BF16)
            return carry

        lax.fori_loop(0, x2_ref.shape[0] // EPILOGUE_ROWS, rows_body, 0)


def _mlp(u, w_up, w_down, x, mod, ln_g, ln_b, l, tm=1024, tf=256):
    ln = min(l + 1, DEPTH - 1)
    return pl.pallas_call(
        _mlp_kernel,
        out_shape=(jax.ShapeDtypeStruct((M_ALL, D_MODEL), F32),
                   jax.ShapeDtypeStruct((M_ALL, D_MODEL), BF16)),
        grid=(M_ALL // tm, D_FF // tf),
        in_specs=[
            pl.BlockSpec((tm, D_MODEL), lambda i, f: (i, 0)),
            pl.BlockSpec((None, D_MODEL, tf), lambda i, f: (l, 0, f)),
            pl.BlockSpec((None, tf, D_MODEL), lambda i, f: (l, f, 0)),
            pl.BlockSpec((tm, D_MODEL), lambda i, f: (i, 0), pipeline_mode=pl.Buffered(1)),
            _mod_spec(l, 5, tm, 2, 0),
            _vec_spec(l, 1), _vec_spec(l, 1),
            _mod_spec(ln, 1, tm, 2, 0),
            _mod_spec(ln, 0, tm, 2, 0),
        ],
        out_specs=(pl.BlockSpec((tm, D_MODEL), lambda i, f: (i, 0)),
                   pl.BlockSpec((tm, D_MODEL), lambda i, f: (i, 0))),
        compiler_params=_params("arbitrary", "arbitrary"),
        name="mlp_ln",
    )(u, w_up, w_down, x, mod, ln_g, ln_b, mod, mod)


def _rope_tables():
    n_freq = A_HEAD_DIM // 4
    pos = np.arange(DEC_SEQ)
    row = (pos // GRID_W).astype(np.float32)
    col = (pos % GRID_W).astype(np.float32)
    inv_freq = (ROPE_BASE ** (-np.arange(n_freq, dtype=np.float32) / n_freq)).astype(np.float32)
    ar = row[:, None] * inv_freq
    ac = col[:, None] * inv_freq
    cr, sr, cc, sc = np.cos(ar), np.sin(ar), np.cos(ac), np.sin(ac)
    z = np.zeros_like(sr)
    cos = np.concatenate([cr, cr, cc, cc], axis=1)
    sin_a = np.concatenate([-sr, z, -sc, z], axis=1)
    sin_b = np.concatenate([z, sr, z, sc], axis=1)
    return tuple(jnp.asarray(t.astype(np.float32)) for t in (cos, sin_a, sin_b))


def kernel(x_prompt, x_sample, cache_a_k, cache_a_v, cache_b_k, cache_b_v, c, c_ctx, w_mod, b_mod, ln_g, ln_b, w_up, w_down, a_w_qkv, a_w_o, a_lambda, a_subln_g, b_w_qkv, b_w_o, b_q_norm_g, b_k_norm_g, c_w_f):
    n_a = a_w_qkv.shape[0]
    n_b = b_w_qkv.shape[0]
    cond = jnp.zeros((COND_PAD, D_MODEL), F32).at[0].set(c_ctx).at[1:N_COND].set(c)
    mod = _modulation(cond, w_mod, b_mod).reshape(DEPTH, COND_PAD, N_MOD, 1, D_MODEL)
    ln_g4 = ln_g.reshape(DEPTH, 2, 1, D_MODEL)
    ln_b4 = ln_b.reshape(DEPTH, 2, 1, D_MODEL)
    rope = _rope_tables()

    ck_a = cache_a_k.reshape(DEC_BATCH, n_a, PAST_LEN, D_MODEL)
    cv_a = cache_a_v.reshape(DEC_BATCH, n_a, PAST_LEN, D_MODEL)
    ck_b = cache_b_k.reshape(DEC_BATCH, n_b, PAST_LEN, B_KV)
    cv_b = cache_b_v.reshape(DEC_BATCH, n_b, PAST_LEN, B_KV)
    subln = a_subln_g.reshape(n_a, 1, 2 * A_HEAD_DIM)
    b_gains = jnp.stack([b_q_norm_g, b_k_norm_g], axis=1).reshape(n_b, 2, 1, B_HEAD_DIM)

    x, u = _embed(x_prompt.reshape(M_P, D_MODEL), x_sample.reshape(M_S, D_MODEL), mod)

    a_k = a_v = b_k = b_v = None
    for l in range(DEPTH):
        kind, j = l % N_MIXERS, l // N_MIXERS
        if kind == 0:
            lam_init = 0.8 - 0.6 * math.exp(-0.3 * l)
            q_p = _proj(u, a_w_qkv, j, 0, D_MODEL, 0, M_P, out_dtype=BF16, name="a_q_prompt")
            a_k = _proj(u, a_w_qkv, j, D_MODEL, D_MODEL, 0, M_P, out_dtype=F32,
                        dst=a_k, dst_layer=j, name="a_k_prompt")
            a_v = _proj(u, a_w_qkv, j, 2 * D_MODEL, D_MODEL, 0, M_P, out_dtype=F32,
                        dst=a_v, dst_layer=j, name="a_v_prompt")
            qk_s = _proj(u, a_w_qkv, j, 0, 2 * D_MODEL, M_P, M_S, out_dtype=BF16, rope=rope,
                         name="a_qk_latent")
            v_s = _proj(u, a_w_qkv, j, 2 * D_MODEL, D_MODEL, M_P, M_S, out_dtype=BF16,
                        name="a_v_latent")
            att = _diff_prompt(a_lambda, subln, q_p, a_k, a_v, j, lam_init)
            att = _diff_latent(a_lambda, subln, qk_s, v_s, ck_a, cv_a, att, j, lam_init)
            w_o, w_o_layer = a_w_o, j
        elif kind == 1:
            gj = b_gains[j]
            q_p = _proj(u, b_w_qkv, j, 0, D_MODEL, 0, M_P, out_dtype=BF16, gains=gj,
                        gain_split=D_MODEL, name="b_q_prompt")
            b_k = _proj(u, b_w_qkv, j, D_MODEL, B_KV, 0, M_P, out_dtype=F32, tn=B_KV,
                        gains=gj[1:], gain_split=B_KV, dst=b_k, dst_layer=j, name="b_k_prompt")
            b_v = _proj(u, b_w_qkv, j, D_MODEL + B_KV, B_KV, 0, M_P, out_dtype=F32, tn=B_KV,
                        dst=b_v, dst_layer=j, name="b_v_prompt")
            qk_s = _proj(u, b_w_qkv, j, 0, D_MODEL + B_KV, M_P, M_S, out_dtype=BF16, tn=B_KV,
                         gains=gj, gain_split=D_MODEL, rope=rope, name="b_qk_latent")
            v_s = _proj(u, b_w_qkv, j, D_MODEL + B_KV, B_KV, M_P, M_S, out_dtype=BF16, tn=B_KV,
                        name="b_v_latent")
            att = _gqa_prompt(q_p, b_k, b_v, j)
            att = _gqa_latent(qk_s, v_s, ck_b, cv_b, att, j)
            w_o, w_o_layer = b_w_o, j
        else:
            att = _dft(u, None, SEQ, BATCH, 0, C_GROUPS)
            att = _dft(u, att, DEC_SEQ, DEC_BATCH, M_P, 1)
            w_o, w_o_layer = c_w_f, j
        x, u = _oproj(att, w_o, w_o_layer, x, mod, ln_g4, ln_b4, l)
        x, u = _mlp(u, w_up, w_down, x, mod, ln_g4, ln_b4, l)

    y_p = x[:M_P].reshape(BATCH, SEQ, D_MODEL)
    y_s = x[M_P:].reshape(DEC_BATCH, DEC_SEQ, D_MODEL)
    new_a_k = a_k.reshape(BATCH, n_a, SEQ, A_HEADS, 2, A_HEAD_DIM)
    new_a_v = a_v.reshape(BATCH, n_a, SEQ, A_HEADS, 2 * A_HEAD_DIM)
    new_b_k = b_k.reshape(BATCH, n_b, SEQ, B_KV_HEADS, B_HEAD_DIM)
    new_b_v = b_v.reshape(BATCH, n_b, SEQ, B_KV_HEADS, B_HEAD_DIM)
    return (y_p, y_s, new_a_k, new_a_v, new_b_k, new_b_v)
```

```python
import functools
import math

import jax
import jax.numpy as jnp
import numpy as np
from jax import lax
from jax.experimental import pallas as pl
from jax.experimental.pallas import tpu as pltpu

D_MODEL = 2048
BATCH = 16
SEQ = 256
DEPTH = 4
DEC_BATCH = 2
DEC_SEQ = 1024
PAST_LEN = 512
GRID_W = 64
N_MIXERS = 3
A_HEAD_DIM = 128
A_HEADS = D_MODEL // (2 * A_HEAD_DIM)
A_QKV = 3 * D_MODEL
B_HEAD_DIM = 128
B_HEADS = D_MODEL // B_HEAD_DIM
B_KV_HEADS = B_HEADS // 4
B_GROUP = B_HEADS // B_KV_HEADS
B_KV = B_KV_HEADS * B_HEAD_DIM
B_QKV = D_MODEL + 2 * B_KV
C_GROUPS = 4
C_GROUP_DIM = D_MODEL // C_GROUPS
D_FF = 4 * D_MODEL
ROPE_BASE = 10000.0
ALPHA = (2 * DEPTH) ** 0.25
LN_EPS = 1e-5
RMS_EPS = 1e-6
N_MOD = 6

M_P = BATCH * SEQ
M_S = DEC_BATCH * DEC_SEQ
M_ALL = M_P + M_S
N_COND = 1 + DEC_BATCH
COND_PAD = 8

LANE = 128
VMEM_LIMIT = 58 * 1024 * 1024

BF16 = jnp.bfloat16
F32 = jnp.float32


def _params(*sem, vmem=VMEM_LIMIT):
    return pltpu.CompilerParams(dimension_semantics=sem, vmem_limit_bytes=vmem)


def _cond_row(i, tm):
    start = i * tm
    return jnp.where(start < M_P, 0, 1 + (start - M_P) // DEC_SEQ)


def _mod_spec(l, which, tm, grid_rank, m_axis, m0=0):
    def idx(*g):
        return (l, _cond_row(g[m_axis] + m0, tm), which, 0, 0)
    return pl.BlockSpec((None, None, None, 1, D_MODEL), idx)


def _vec_spec(l, which):
    return pl.BlockSpec((None, None, 1, D_MODEL), lambda *g: (l, which, 0, 0))


def _mod_kernel(c_ref, w_ref, b_ref, o_ref):
    c = c_ref[...]
    s = c * (1.0 / (1.0 + jnp.exp(-c)))
    o_ref[...] = jnp.dot(s.astype(BF16), w_ref[...].astype(BF16),
                         preferred_element_type=F32) + b_ref[...]


def _modulation(cond, w_mod, b_mod):
    tn = 1024
    n_out = N_MOD * D_MODEL
    return pl.pallas_call(
        _mod_kernel,
        out_shape=jax.ShapeDtypeStruct((DEPTH, COND_PAD, n_out), F32),
        grid=(DEPTH, n_out // tn),
        in_specs=[
            pl.BlockSpec((COND_PAD, D_MODEL), lambda l, n: (0, 0)),
            pl.BlockSpec((None, D_MODEL, tn), lambda l, n: (l, 0, n)),
            pl.BlockSpec((None, 1, tn), lambda l, n: (l, 0, n)),
        ],
        out_specs=pl.BlockSpec((None, COND_PAD, tn), lambda l, n: (l, 0, n)),
        compiler_params=_params("arbitrary", "arbitrary"),
        name="modulation",
    )(cond, w_mod, b_mod.reshape(DEPTH, 1, n_out))


def _embed_kernel(xp_ref, xs_ref, sc_ref, sh_ref, x_ref, u_ref):
    i = pl.program_id(0)

    def emit(x):
        x_ref[...] = x
        u_ref[...] = (x * (1.0 + sc_ref[...]) + sh_ref[...]).astype(BF16)

    @pl.when(i * EMBED_ROWS < M_P)
    def _():
        emit(xp_ref[...])

    @pl.when(i * EMBED_ROWS >= M_P)
    def _():
        emit(xs_ref[...])


EMBED_ROWS = 256


def _embed(xp, xs, mod):
    tm = EMBED_ROWS
    n_p = M_P // tm
    return pl.pallas_call(
        _embed_kernel,
        out_shape=(jax.ShapeDtypeStruct((M_ALL, D_MODEL), F32),
                   jax.ShapeDtypeStruct((M_ALL, D_MODEL), BF16)),
        grid=(M_ALL // tm,),
        in_specs=[
            pl.BlockSpec((tm, D_MODEL), lambda i: (jnp.minimum(i, n_p - 1), 0)),
            pl.BlockSpec((tm, D_MODEL), lambda i: (jnp.maximum(i - n_p, 0), 0)),
            _mod_spec(0, 1, tm, 1, 0),
            _mod_spec(0, 0, tm, 1, 0),
        ],
        out_specs=(pl.BlockSpec((tm, D_MODEL), lambda i: (i, 0)),
                   pl.BlockSpec((tm, D_MODEL), lambda i: (i, 0))),
        compiler_params=_params("arbitrary"),
        name="embed",
    )(xp, xs, mod, mod)


def _rms_chunk(x, g):
    ms = jnp.mean(x * x, axis=-1, keepdims=True)
    return x * lax.rsqrt(ms + RMS_EPS) * g


def _rope_chunk(x, cos, sin_a, sin_b):
    return x * cos + pltpu.roll(x, LANE - 32, 1) * sin_a + pltpu.roll(x, 32, 1) * sin_b


def _proj_kernel(*refs, tn, rms, rope, has_dst, out_rows):
    it = iter(refs)
    u_ref, w_ref = next(it), next(it)
    g_ref = next(it) if rms else None
    tabs = (next(it), next(it), next(it)) if rope else None
    if has_dst:
        next(it)
    o_ref, wbf = next(it), next(it)

    @pl.when(pl.program_id(1) == 0)
    def _():
        wbf[...] = w_ref[...].astype(BF16)

    acc = jnp.dot(u_ref[...], wbf[...], preferred_element_type=F32)

    def fold(x):
        return x if out_rows is None else x.reshape(x.shape[0] // out_rows, out_rows, x.shape[1])

    if rms or rope:
        g = g_ref[...] if rms else None
        tab = tuple(t[...] for t in tabs) if rope else None
        for c in range(tn // LANE):
            x = acc[:, c * LANE:(c + 1) * LANE]
            if rms:
                x = _rms_chunk(x, g)
            if rope:
                x = _rope_chunk(x, *tab)
            o_ref[..., c * LANE:(c + 1) * LANE] = fold(x).astype(o_ref.dtype)
    else:
        o_ref[...] = fold(acc).astype(o_ref.dtype)


def _proj(u, w, w_layer, col0, ncols, m0, m_rows, *, out_dtype, tn=1024, tm=1024,
          gains=None, gain_split=None, rope=None, dst=None, dst_layer=None, name="proj"):
    k = u.shape[1]
    assert col0 % tn == 0 and ncols % tn == 0 and m0 % tm == 0 and m_rows % tm == 0
    n_t, m_t, mt0, nt0 = ncols // tn, m_rows // tm, m0 // tm, col0 // tn
    ins = [u, w]
    in_specs = [
        pl.BlockSpec((tm, k), lambda n, m: (mt0 + m, 0)),
        pl.BlockSpec((None, k, tn), lambda n, m: (w_layer, 0, nt0 + n)),
    ]
    if gains is not None:
        ins.append(gains)
        in_specs.append(pl.BlockSpec(
            (None, 1, LANE), lambda n, m: (jnp.where(n * tn < gain_split, 0, 1), 0, 0)))
    if rope is not None:
        assert m0 >= M_P
        per = DEC_SEQ // tm
        for t in rope:
            ins.append(t)
            in_specs.append(pl.BlockSpec((tm, LANE), lambda n, m: ((mt0 + m) % per, 0)))
    out_rows = None
    if dst_layer is not None:
        assert m0 == 0 and m_rows == M_P and tm % SEQ == 0
        out_rows = SEQ
        out_shape = jax.ShapeDtypeStruct((BATCH, w.shape[0], SEQ, ncols), out_dtype)
        out_spec = pl.BlockSpec((tm // SEQ, None, SEQ, tn), lambda n, m: (m, dst_layer, 0, n))
    else:
        out_shape = jax.ShapeDtypeStruct((m_rows, ncols), out_dtype)
        out_spec = pl.BlockSpec((tm, tn), lambda n, m: (m, n))
    aliases = {}
    if dst is not None:
        aliases = {len(ins): 0}
        ins.append(dst)
        in_specs.append(pl.BlockSpec(memory_space=pl.ANY))
    kern = functools.partial(_proj_kernel, tn=tn, rms=gains is not None, rope=rope is not None,
                             has_dst=dst is not None, out_rows=out_rows)
    return pl.pallas_call(
        kern,
        out_shape=out_shape,
        grid=(n_t, m_t),
        in_specs=in_specs,
        out_specs=out_spec,
        scratch_shapes=[pltpu.VMEM((k, tn), BF16)],
        input_output_aliases=aliases,
        compiler_params=_params("arbitrary", "arbitrary"),
        name=name,
    )(*ins)


def _qkt(q, k):
    return lax.dot_general(q, k, (((1,), (1,)), ((), ())), preferred_element_type=F32)


def _softmax_parts(scores, scale):
    m = scores[0].max(axis=-1, keepdims=True)
    for s in scores[1:]:
        m = jnp.maximum(m, s.max(axis=-1, keepdims=True))
    es = [jnp.exp((s - m) * scale) for s in scores]
    tot = es[0].sum(axis=-1, keepdims=True)
    for e in es[1:]:
        tot = tot + e.sum(axis=-1, keepdims=True)
    return es, 1.0 / tot


def _diff_lambda(lp_ref, lam_init):
    lp = lp_ref[...]
    s1 = jnp.sum(lp[0:1] * lp[1:2], axis=-1, keepdims=True)
    s2 = jnp.sum(lp[2:3] * lp[3:4], axis=-1, keepdims=True)
    return jnp.exp(s1) - jnp.exp(s2) + lam_init


def _diff_head(q, ks, vs, lam, gain):
    scale = A_HEAD_DIM ** -0.5
    q0, q1 = q[:, :A_HEAD_DIM], q[:, A_HEAD_DIM:]
    e0, r0 = _softmax_parts([_qkt(q0, k[:, :A_HEAD_DIM]) for k in ks], scale)
    e1, r1 = _softmax_parts([_qkt(q1, k[:, A_HEAD_DIM:]) for k in ks], scale)
    r1 = lam * r1
    o = None
    for a0, a1, v in zip(e0, e1, vs):
        a = (a0 * r0 - a1 * r1).astype(BF16)
        part = jnp.dot(a, v, preferred_element_type=F32)
        o = part if o is None else o + part
    ms = jnp.mean(o * o, axis=-1, keepdims=True)
    return (o * lax.rsqrt(ms + RMS_EPS) * gain).astype(BF16)


def _gqa_head(q, ks, vs):
    es, r = _softmax_parts([_qkt(q, k) for k in ks], B_HEAD_DIM ** -0.5)
    o = None
    for e, v in zip(es, vs):
        part = jnp.dot((e * r).astype(BF16), v, preferred_element_type=F32)
        o = part if o is None else o + part
    return o.astype(BF16)


def _diff_prompt_kernel(lp_ref, g_ref, q_ref, k_ref, v_ref, o_ref, *, lam_init):
    lam = _diff_lambda(lp_ref, lam_init)
    gain = g_ref[...] * (1.0 - lam_init)
    w = 2 * A_HEAD_DIM
    for h in range(A_HEADS):
        sl = slice(h * w, (h + 1) * w)
        o_ref[:, sl] = _diff_head(q_ref[:, sl], [k_ref[:, sl].astype(BF16)],
                                  [v_ref[:, sl].astype(BF16)], lam, gain)


def _diff_prompt(lp, g, q, kbuf, vbuf, j, lam_init):
    kv_spec = pl.BlockSpec((None, None, SEQ, D_MODEL), lambda b: (b, j, 0, 0))
    return pl.pallas_call(
        functools.partial(_diff_prompt_kernel, lam_init=lam_init),
        out_shape=jax.ShapeDtypeStruct((M_ALL, D_MODEL), BF16),
        grid=(BATCH,),
        in_specs=[
            pl.BlockSpec((None, 4, A_HEAD_DIM), lambda b: (j, 0, 0)),
            pl.BlockSpec((None, 1, 2 * A_HEAD_DIM), lambda b: (j, 0, 0)),
            pl.BlockSpec((SEQ, D_MODEL), lambda b: (b, 0)),
            kv_spec, kv_spec,
        ],
        out_specs=pl.BlockSpec((SEQ, D_MODEL), lambda b: (b, 0)),
        compiler_params=_params("arbitrary"),
        name="diff_attn_prompt",
    )(lp, g, q, kbuf, vbuf)


def _diff_latent_kernel(lp_ref, g_ref, q_ref, k_ref, v_ref, ck_ref, cv_ref, dst_ref, o_ref,
                        *, lam_init):
    del dst_ref
    lam = _diff_lambda(lp_ref, lam_init)
    gain = g_ref[...] * (1.0 - lam_init)
    o_ref[...] = _diff_head(q_ref[...], [ck_ref[...].astype(BF16), k_ref[...]],
                            [cv_ref[...].astype(BF16), v_ref[...]], lam, gain)


def _diff_latent(lp, g, qk, v, ck, cv, dst, j, lam_init, tq=512):
    w = 2 * A_HEAD_DIM
    per = DEC_SEQ // tq
    row0 = M_P // tq
    ctx_spec = pl.BlockSpec((None, None, PAST_LEN, w), lambda b, h, t: (b, j, 0, h))
    return pl.pallas_call(
        functools.partial(_diff_latent_kernel, lam_init=lam_init),
        out_shape=jax.ShapeDtypeStruct((M_ALL, D_MODEL), BF16),
        grid=(DEC_BATCH, A_HEADS, per),
        in_specs=[
            pl.BlockSpec((None, 4, A_HEAD_DIM), lambda b, h, t: (j, 0, 0)),
            pl.BlockSpec((None, 1, w), lambda b, h, t: (j, 0, 0)),
            pl.BlockSpec((tq, w), lambda b, h, t: (b * per + t, h)),
            pl.BlockSpec((DEC_SEQ, w), lambda b, h, t: (b, A_HEADS + h)),
            pl.BlockSpec((DEC_SEQ, w), lambda b, h, t: (b, h)),
            ctx_spec, ctx_spec,
            pl.BlockSpec(memory_space=pl.ANY),
        ],
        out_specs=pl.BlockSpec((tq, w), lambda b, h, t: (row0 + b * per + t, h)),
        input_output_aliases={7: 0},
        compiler_params=_params("arbitrary", "arbitrary", "arbitrary"),
        name="diff_attn_latent",
    )(lp, g, qk, qk, v, ck, cv, dst)


def _gqa_prompt_kernel(q_ref, k_ref, v_ref, o_ref):
    d = B_HEAD_DIM
    for n in range(B_KV_HEADS):
        k = [k_ref[:, n * d:(n + 1) * d].astype(BF16)]
        v = [v_ref[:, n * d:(n + 1) * d].astype(BF16)]
        for g in range(B_GROUP):
            sl = slice((n * B_GROUP + g) * d, (n * B_GROUP + g + 1) * d)
            o_ref[:, sl] = _gqa_head(q_ref[:, sl], k, v)


def _gqa_prompt(q, kbuf, vbuf, j):
    kv_spec = pl.BlockSpec((None, None, SEQ, B_KV), lambda b: (b, j, 0, 0))
    return pl.pallas_call(
        _gqa_prompt_kernel,
        out_shape=jax.ShapeDtypeStruct((M_ALL, D_MODEL), BF16),
        grid=(BATCH,),
        in_specs=[
            pl.BlockSpec((SEQ, D_MODEL), lambda b: (b, 0)),
            kv_spec, kv_spec,
        ],
        out_specs=pl.BlockSpec((SEQ, D_MODEL), lambda b: (b, 0)),
        compiler_params=_params("arbitrary"),
        name="gqa_attn_prompt",
    )(q, kbuf, vbuf)


def _gqa_latent_kernel(q_ref, k_ref, v_ref, ck_ref, cv_ref, dst_ref, o_ref):
    del dst_ref
    d = B_HEAD_DIM
    ks = [ck_ref[...].astype(BF16), k_ref[...]]
    vs = [cv_ref[...].astype(BF16), v_ref[...]]
    for g in range(B_GROUP):
        o_ref[:, g * d:(g + 1) * d] = _gqa_head(q_ref[:, g * d:(g + 1) * d], ks, vs)


def _gqa_latent(qk, v, ck, cv, dst, j, tq=512):
    d = B_HEAD_DIM
    gw = B_GROUP * d
    per = DEC_SEQ // tq
    row0 = M_P // tq
    ctx_spec = pl.BlockSpec((None, None, PAST_LEN, d), lambda b, n, t: (b, j, 0, n))
    return pl.pallas_call(
        _gqa_latent_kernel,
        out_shape=jax.ShapeDtypeStruct((M_ALL, D_MODEL), BF16),
        grid=(DEC_BATCH, B_KV_HEADS, per),
        in_specs=[
            pl.BlockSpec((tq, gw), lambda b, n, t: (b * per + t, n)),
            pl.BlockSpec((DEC_SEQ, d), lambda b, n, t: (b, D_MODEL // d + n)),
            pl.BlockSpec((DEC_SEQ, d), lambda b, n, t: (b, n)),
            ctx_spec, ctx_spec,
            pl.BlockSpec(memory_space=pl.ANY),
        ],
        out_specs=pl.BlockSpec((tq, gw), lambda b, n, t: (row0 + b * per + t, n)),
        input_output_aliases={5: 0},
        compiler_params=_params("arbitrary", "arbitrary", "arbitrary"),
        name="gqa_attn_latent",
    )(qk, qk, v, ck, cv, dst)


def _dft_tables(s):
    def cs(n):
        idx = (np.arange(n)[:, None] * np.arange(n)[None, :]) % n
        ang = 2.0 * np.pi * idx / n
        return np.cos(ang), np.sin(ang)
    cc, sc = cs(C_GROUP_DIM)
    c_s, s_s = cs(s)
    t_chan = np.concatenate([cc, sc], axis=1).astype(np.float32)
    t_pos = np.concatenate([c_s, -s_s], axis=1).astype(np.float32)
    return jnp.asarray(t_chan).astype(BF16), jnp.asarray(t_pos).astype(BF16)


def _dft_group(x, t_chan, t_pos, norm):
    xcs = jnp.dot(x, t_chan, preferred_element_type=F32).astype(BF16)
    stacked = jnp.concatenate([xcs[:, :C_GROUP_DIM], xcs[:, C_GROUP_DIM:]], axis=0)
    y = jnp.dot(t_pos, stacked, preferred_element_type=F32)
    return (y * norm).astype(BF16)


def _dft_kernel(u_ref, tc_ref, tp_ref, *rest, groups, norm):
    o_ref = rest[-1]
    tc, tp = tc_ref[...], tp_ref[...]
    for g in range(groups):
        sl = slice(g * C_GROUP_DIM, (g + 1) * C_GROUP_DIM)
        o_ref[:, sl] = _dft_group(u_ref[:, sl], tc, tp, norm)


def _dft(u, dst, s, batch, row0, groups_per_step):
    t_chan, t_pos = _dft_tables(s)
    gw = groups_per_step * C_GROUP_DIM
    n_g = C_GROUPS // groups_per_step
    rb0 = row0 // s
    ins = [u, t_chan, t_pos]
    in_specs = [
        pl.BlockSpec((s, gw), lambda b, g: (rb0 + b, g)),
        pl.BlockSpec(t_chan.shape, lambda b, g: (0, 0)),
        pl.BlockSpec(t_pos.shape, lambda b, g: (0, 0)),
    ]
    aliases = {}
    if dst is not None:
        aliases = {3: 0}
        ins.append(dst)
        in_specs.append(pl.BlockSpec(memory_space=pl.ANY))
    return pl.pallas_call(
        functools.partial(_dft_kernel, groups=groups_per_step,
                          norm=1.0 / math.sqrt(s * C_GROUP_DIM)),
        out_shape=jax.ShapeDtypeStruct((M_ALL, D_MODEL), BF16),
        grid=(batch, n_g),
        in_specs=in_specs,
        out_specs=pl.BlockSpec((s, gw), lambda b, g: (rb0 + b, g)),
        input_output_aliases=aliases,
        compiler_params=_params("arbitrary", "arbitrary"),
        name="dft",
    )(*ins)


def _post_norm(y, g, b):
    mu = jnp.mean(y, axis=-1, keepdims=True)
    yc = y - mu
    var = jnp.mean(yc * yc, axis=-1, keepdims=True)
    return yc * lax.rsqrt(var + LN_EPS) * g + b


def _oproj_kernel(a_ref, w_ref, x_ref, gate_ref, g_ref, b_ref, sc_ref, sh_ref,
                  x1_ref, u_ref, wbf):
    @pl.when(pl.program_id(0) == 0)
    def _():
        wbf[...] = w_ref[...].astype(BF16)

    mixed = jnp.dot(a_ref[...], wbf[...], preferred_element_type=F32)
    x1 = _post_norm(ALPHA * x_ref[...] + gate_ref[...] * mixed, g_ref[...], b_ref[...])
    x1_ref[...] = x1
    u_ref[...] = (x1 * (1.0 + sc_ref[...]) + sh_ref[...]).astype(BF16)


def _oproj(a, w, w_layer, x, mod, ln_g, ln_b, l, tm=256):
    return pl.pallas_call(
        _oproj_kernel,
        out_shape=(jax.ShapeDtypeStruct((M_ALL, D_MODEL), F32),
                   jax.ShapeDtypeStruct((M_ALL, D_MODEL), BF16)),
        grid=(M_ALL // tm,),
        in_specs=[
            pl.BlockSpec((tm, D_MODEL), lambda i: (i, 0)),
            pl.BlockSpec((None, D_MODEL, D_MODEL), lambda i: (w_layer, 0, 0),
                         pipeline_mode=pl.Buffered(1)),
            pl.BlockSpec((tm, D_MODEL), lambda i: (i, 0)),
            _mod_spec(l, 2, tm, 1, 0),
            _vec_spec(l, 0), _vec_spec(l, 0),
            _mod_spec(l, 4, tm, 1, 0),
            _mod_spec(l, 3, tm, 1, 0),
        ],
        out_specs=(pl.BlockSpec((tm, D_MODEL), lambda i: (i, 0)),
                   pl.BlockSpec((tm, D_MODEL), lambda i: (i, 0))),
        scratch_shapes=[pltpu.VMEM((D_MODEL, D_MODEL), BF16)],
        compiler_params=_params("arbitrary"),
        name="oproj_ln",
    )(a, w, x, mod, ln_g, ln_b, mod, mod)


MLP_ROW_SPLIT = 2


def _mlp_kernel(u_ref, wu_ref, wd_ref, ff_ref):
    @pl.when(pl.program_id(1) == 0)
    def _():
        ff_ref[...] = jnp.zeros_like(ff_ref)

    wu = wu_ref[...].astype(BF16)
    wd = wd_ref[...].astype(BF16)
    rows = u_ref.shape[0] // MLP_ROW_SPLIT
    for r in range(MLP_ROW_SPLIT):
        sl = slice(r * rows, (r + 1) * rows)
        h = jnp.dot(u_ref[sl, :], wu, preferred_element_type=F32)
        h = jnp.maximum(h, 0.0)
        h = (h * h).astype(BF16)
        ff_ref[sl, :] += jnp.dot(h, wd, preferred_element_type=F32)


def _mlp(u, w_up, w_down, l, tm=1024, tf=512):
    return pl.pallas_call(
        _mlp_kernel,
        out_shape=jax.ShapeDtypeStruct((M_ALL, D_MODEL), F32),
        grid=(M_ALL // tm, D_FF // tf),
        in_specs=[
            pl.BlockSpec((tm, D_MODEL), lambda i, f: (i, 0), pipeline_mode=pl.Buffered(1)),
            pl.BlockSpec((None, D_MODEL, tf), lambda i, f: (l, 0, f)),
            pl.BlockSpec((None, tf, D_MODEL), lambda i, f: (l, f, 0)),
        ],
        out_specs=pl.BlockSpec((tm, D_MODEL), lambda i, f: (i, 0)),
        compiler_params=_params("arbitrary", "arbitrary"),
        name="mlp",
    )(u, w_up, w_down)


def _close_kernel(ff_ref, x_ref, gate_ref, g_ref, b_ref, sc_ref, sh_ref, x2_ref, un_ref):
    x2 = _post_norm(ALPHA * x_ref[...] + gate_ref[...] * ff_ref[...], g_ref[...], b_ref[...])
    x2_ref[...] = x2
    un_ref[...] = (x2 * (1.0 + sc_ref[...]) + sh_ref[...]).astype(BF16)


def _close(ff, x, mod, ln_g, ln_b, l, tm=256):
    ln = min(l + 1, DEPTH - 1)
    row = pl.BlockSpec((tm, D_MODEL), lambda i: (i, 0))
    return pl.pallas_call(
        _close_kernel,
        out_shape=(jax.ShapeDtypeStruct((M_ALL, D_MODEL), F32),
                   jax.ShapeDtypeStruct((M_ALL, D_MODEL), BF16)),
        grid=(M_ALL // tm,),
        in_specs=[
            row, row,
            _mod_spec(l, 5, tm, 1, 0),
            _vec_spec(l, 1), _vec_spec(l, 1),
            _mod_spec(ln, 1, tm, 1, 0),
            _mod_spec(ln, 0, tm, 1, 0),
        ],
        out_specs=(row, row),
        compiler_params=_params("arbitrary"),
        name="mlp_close",
    )(ff, x, mod, ln_g, ln_b, mod, mod)


def _rope_tables():
    n_freq = A_HEAD_DIM // 4
    pos = np.arange(DEC_SEQ)
    row = (pos // GRID_W).astype(np.float32)
    col = (pos % GRID_W).astype(np.float32)
    inv_freq = (ROPE_BASE ** (-np.arange(n_freq, dtype=np.float32) / n_freq)).astype(np.float32)
    ar = row[:, None] * inv_freq
    ac = col[:, None] * inv_freq
    cr, sr, cc, sc = np.cos(ar), np.sin(ar), np.cos(ac), np.sin(ac)
    z = np.zeros_like(sr)
    cos = np.concatenate([cr, cr, cc, cc], axis=1)
    sin_a = np.concatenate([-sr, z, -sc, z], axis=1)
    sin_b = np.concatenate([z, sr, z, sc], axis=1)
    return tuple(jnp.asarray(t.astype(np.float32)) for t in (cos, sin_a, sin_b))


def kernel(x_prompt, x_sample, cache_a_k, cache_a_v, cache_b_k, cache_b_v, c, c_ctx, w_mod, b_mod, ln_g, ln_b, w_up, w_down, a_w_qkv, a_w_o, a_lambda, a_subln_g, b_w_qkv, b_w_o, b_q_norm_g, b_k_norm_g, c_w_f):
    n_a = a_w_qkv.shape[0]
    n_b = b_w_qkv.shape[0]
    cond = jnp.zeros((COND_PAD, D_MODEL), F32).at[0].set(c_ctx).at[1:N_COND].set(c)
    mod = _modulation(cond, w_mod, b_mod).reshape(DEPTH, COND_PAD, N_MOD, 1, D_MODEL)
    ln_g4 = ln_g.reshape(DEPTH, 2, 1, D_MODEL)
    ln_b4 = ln_b.reshape(DEPTH, 2, 1, D_MODEL)
    rope = _rope_tables()

    ck_a = cache_a_k.reshape(DEC_BATCH, n_a, PAST_LEN, D_MODEL)
    cv_a = cache_a_v.reshape(DEC_BATCH, n_a, PAST_LEN, D_MODEL)
    ck_b = cache_b_k.reshape(DEC_BATCH, n_b, PAST_LEN, B_KV)
    cv_b = cache_b_v.reshape(DEC_BATCH, n_b, PAST_LEN, B_KV)
    subln = a_subln_g.reshape(n_a, 1, 2 * A_HEAD_DIM)
    b_gains = jnp.stack([b_q_norm_g, b_k_norm_g], axis=1).reshape(n_b, 2, 1, B_HEAD_DIM)

    x, u = _embed(x_prompt.reshape(M_P, D_MODEL), x_sample.reshape(M_S, D_MODEL), mod)

    a_k = a_v = b_k = b_v = None
    for l in range(DEPTH):
        kind, j = l % N_MIXERS, l // N_MIXERS
        if kind == 0:
            lam_init = 0.8 - 0.6 * math.exp(-0.3 * l)
            q_p = _proj(u, a_w_qkv, j, 0, D_MODEL, 0, M_P, out_dtype=BF16, name="a_q_prompt")
            a_k = _proj(u, a_w_qkv, j, D_MODEL, D_MODEL, 0, M_P, out_dtype=F32,
                        dst=a_k, dst_layer=j, name="a_k_prompt")
            a_v = _proj(u, a_w_qkv, j, 2 * D_MODEL, D_MODEL, 0, M_P, out_dtype=F32,
                        dst=a_v, dst_layer=j, name="a_v_prompt")
            qk_s = _proj(u, a_w_qkv, j, 0, 2 * D_MODEL, M_P, M_S, out_dtype=BF16, rope=rope,
                         name="a_qk_latent")
            v_s = _proj(u, a_w_qkv, j, 2 * D_MODEL, D_MODEL, M_P, M_S, out_dtype=BF16,
                        name="a_v_latent")
            att = _diff_prompt(a_lambda, subln, q_p, a_k, a_v, j, lam_init)
            att = _diff_latent(a_lambda, subln, qk_s, v_s, ck_a, cv_a, att, j, lam_init)
            w_o, w_o_layer = a_w_o, j
        elif kind == 1:
            gj = b_gains[j]
            q_p = _proj(u, b_w_qkv, j, 0, D_MODEL, 0, M_P, out_dtype=BF16, gains=gj,
                        gain_split=D_MODEL, name="b_q_prompt")
            b_k = _proj(u, b_w_qkv, j, D_MODEL, B_KV, 0, M_P, out_dtype=F32, tn=B_KV,
                        gains=gj[1:], gain_split=B_KV, dst=b_k, dst_layer=j, name="b_k_prompt")
            b_v = _proj(u, b_w_qkv, j, D_MODEL + B_KV, B_KV, 0, M_P, out_dtype=F32, tn=B_KV,
                        dst=b_v, dst_layer=j, name="b_v_prompt")
            qk_s = _proj(u, b_w_qkv, j, 0, D_MODEL + B_KV, M_P, M_S, out_dtype=BF16, tn=B_KV,
                         gains=gj, gain_split=D_MODEL, rope=rope, name="b_qk_latent")
            v_s = _proj(u, b_w_qkv, j, D_MODEL + B_KV, B_KV, M_P, M_S, out_dtype=BF16, tn=B_KV,
                        name="b_v_latent")
            att = _gqa_prompt(q_p, b_k, b_v, j)
            att = _gqa_latent(qk_s, v_s, ck_b, cv_b, att, j)
            w_o, w_o_layer = b_w_o, j
        else:
            att = _dft(u, None, SEQ, BATCH, 0, C_GROUPS)
            att = _dft(u, att, DEC_SEQ, DEC_BATCH, M_P, 1)
            w_o, w_o_layer = c_w_f, j
        x, u = _oproj(att, w_o, w_o_layer, x, mod, ln_g4, ln_b4, l)
        x, u = _close(_mlp(u, w_up, w_down, l), x, mod, ln_g4, ln_b4, l)

    y_p = x[:M_P].reshape(BATCH, SEQ, D_MODEL)
    y_s = x[M_P:].reshape(DEC_BATCH, DEC_SEQ, D_MODEL)
    new_a_k = a_k.reshape(BATCH, n_a, SEQ, A_HEADS, 2, A_HEAD_DIM)
    new_a_v = a_v.reshape(BATCH, n_a, SEQ, A_HEADS, 2 * A_HEAD_DIM)
    new_b_k = b_k.reshape(BATCH, n_b, SEQ, B_KV_HEADS, B_HEAD_DIM)
    new_b_v = b_v.reshape(BATCH, n_b, SEQ, B_KV_HEADS, B_HEAD_DIM)
    return (y_p, y_s, new_a_k, new_a_v, new_b_k, new_b_v)
```

```python
import functools
import math

import jax
import jax.numpy as jnp
import numpy as np
from jax import lax
from jax.experimental import pallas as pl
from jax.experimental.pallas import tpu as pltpu

D_MODEL = 2048
BATCH = 16
SEQ = 256
DEPTH = 4
DEC_BATCH = 2
DEC_SEQ = 1024
PAST_LEN = 512
GRID_W = 64
N_MIXERS = 3
A_HEAD_DIM = 128
A_HEADS = D_MODEL // (2 * A_HEAD_DIM)
B_HEAD_DIM = 128
B_HEADS = D_MODEL // B_HEAD_DIM
B_KV_HEADS = B_HEADS // 4
B_GROUP = B_HEADS // B_KV_HEADS
B_KV = B_KV_HEADS * B_HEAD_DIM
C_GROUPS = 4
C_GROUP_DIM = D_MODEL // C_GROUPS
D_FF = 4 * D_MODEL
ROPE_BASE = 10000.0
ALPHA = (2 * DEPTH) ** 0.25
LN_EPS = 1e-5
RMS_EPS = 1e-6
N_MOD = 6
LOG2E = 1.4426950408889634

M_P = BATCH * SEQ
M_S = DEC_BATCH * DEC_SEQ
M_ALL = M_P + M_S
N_COND = 1 + DEC_BATCH
COND_PAD = 8

LANE = 128
VMEM_LIMIT = 58 * 1024 * 1024
ROW_TILE = 512

BF16 = jnp.bfloat16
F32 = jnp.float32


def _params(*sem, vmem=VMEM_LIMIT):
    return pltpu.CompilerParams(dimension_semantics=sem, vmem_limit_bytes=vmem)


def _cond_row(i, tm):
    start = i * tm
    return jnp.where(start < M_P, 0, 1 + (start - M_P) // DEC_SEQ)


def _mod_spec(l, which, tm):
    def idx(*g):
        return (l, _cond_row(g[0], tm), which, 0, 0)
    return pl.BlockSpec((None, None, None, 1, D_MODEL), idx)


def _vec_spec(l, which):
    return pl.BlockSpec((None, None, 1, D_MODEL), lambda *g: (l, which, 0, 0))


def _prompt_rows_spec(tm):
    last = M_P // tm - 1
    return pl.BlockSpec((tm, D_MODEL), lambda i: (jnp.minimum(i, last), 0))


def _latent_rows_spec(tm):
    first = M_P // tm
    return pl.BlockSpec((tm, D_MODEL), lambda i: (jnp.maximum(i - first, 0), 0))


def _mod_kernel(c_ref, w_ref, b_ref, o_ref):
    c = c_ref[...]
    s = c * (1.0 / (1.0 + jnp.exp(-c)))
    o_ref[...] = jnp.dot(s.astype(BF16), w_ref[...].astype(BF16),
                         preferred_element_type=F32) + b_ref[...]


def _modulation(cond, w_mod, b_mod):
    tn = 1024
    n_out = N_MOD * D_MODEL
    return pl.pallas_call(
        _mod_kernel,
        out_shape=jax.ShapeDtypeStruct((DEPTH, COND_PAD, n_out), F32),
        grid=(DEPTH, n_out // tn),
        in_specs=[
            pl.BlockSpec((COND_PAD, D_MODEL), lambda l, n: (0, 0)),
            pl.BlockSpec((None, D_MODEL, tn), lambda l, n: (l, 0, n)),
            pl.BlockSpec((None, 1, tn), lambda l, n: (l, 0, n)),
        ],
        out_specs=pl.BlockSpec((None, COND_PAD, tn), lambda l, n: (l, 0, n)),
        compiler_params=_params("arbitrary", "arbitrary"),
        name="modulation",
    )(cond, w_mod, b_mod.reshape(DEPTH, 1, n_out))


def _cast_kernel(w_ref, o_ref):
    o_ref[...] = w_ref[...].astype(BF16)


def _cast_bf16(w, rows=512):
    n_l, k, n = w.shape
    spec = pl.BlockSpec((None, rows, n), lambda l, r: (l, r, 0))
    return pl.pallas_call(
        _cast_kernel,
        out_shape=jax.ShapeDtypeStruct(w.shape, BF16),
        grid=(n_l, k // rows),
        in_specs=[spec],
        out_specs=spec,
        compiler_params=_params("arbitrary", "arbitrary"),
        name="cast_weight",
    )(w)


def _embed_kernel(xp_ref, xs_ref, sc_ref, sh_ref, x_ref, u_ref):
    x = jnp.where(pl.program_id(0) * ROW_TILE < M_P, xp_ref[...], xs_ref[...])
    x_ref[...] = x
    u_ref[...] = (x * (1.0 + sc_ref[...]) + sh_ref[...]).astype(BF16)


def _embed(xp, xs, mod):
    tm = ROW_TILE
    row = pl.BlockSpec((tm, D_MODEL), lambda i: (i, 0))
    return pl.pallas_call(
        _embed_kernel,
        out_shape=(jax.ShapeDtypeStruct((M_ALL, D_MODEL), F32),
                   jax.ShapeDtypeStruct((M_ALL, D_MODEL), BF16)),
        grid=(M_ALL // tm,),
        in_specs=[_prompt_rows_spec(tm), _latent_rows_spec(tm),
                  _mod_spec(0, 1, tm), _mod_spec(0, 0, tm)],
        out_specs=(row, row),
        compiler_params=_params("arbitrary"),
        name="embed",
    )(xp, xs, mod, mod)


def _rms_chunk(x, g):
    ms = jnp.mean(x * x, axis=-1, keepdims=True)
    return x * lax.rsqrt(ms + RMS_EPS) * g


def _rope_chunk(x, cos, sin_a, sin_b):
    return x * cos + pltpu.roll(x, LANE - 32, 1) * sin_a + pltpu.roll(x, 32, 1) * sin_b


def _proj_kernel(*refs, tn, rms, rope):
    it = iter(refs)
    u_ref, w_ref = next(it), next(it)
    g_ref = next(it) if rms else None
    tabs = (next(it), next(it), next(it)) if rope else None
    o_ref, wbf = next(it), next(it)

    @pl.when(pl.program_id(1) == 0)
    def _():
        wbf[...] = w_ref[...].astype(BF16)

    acc = jnp.dot(u_ref[...], wbf[...], preferred_element_type=F32)
    if rms or rope:
        g = g_ref[...] if rms else None
        tab = tuple(t[...] for t in tabs) if rope else None
        for c in range(tn // LANE):
            x = acc[:, c * LANE:(c + 1) * LANE]
            if rms:
                x = _rms_chunk(x, g)
            if rope:
                x = _rope_chunk(x, *tab)
            o_ref[:, c * LANE:(c + 1) * LANE] = x.astype(o_ref.dtype)
    else:
        o_ref[...] = acc.astype(o_ref.dtype)


def _proj(u, w, w_layer, col0, ncols, m0, m_rows, *, out_dtype, tn=1024, tm=1024,
          gains=None, gain_split=None, rope=None, name="proj"):
    k = u.shape[1]
    assert col0 % tn == 0 and ncols % tn == 0 and m0 % tm == 0 and m_rows % tm == 0
    n_t, m_t, mt0, nt0 = ncols // tn, m_rows // tm, m0 // tm, col0 // tn
    ins = [u, w]
    in_specs = [
        pl.BlockSpec((tm, k), lambda n, m: (mt0 + m, 0)),
        pl.BlockSpec((None, k, tn), lambda n, m: (w_layer, 0, nt0 + n)),
    ]
    if gains is not None:
        ins.append(gains)
        in_specs.append(pl.BlockSpec(
            (None, 1, LANE), lambda n, m: (jnp.where(n * tn < gain_split, 0, 1), 0, 0)))
    if rope is not None:
        assert m0 >= M_P
        per = DEC_SEQ // tm
        for t in rope:
            ins.append(t)
            in_specs.append(pl.BlockSpec((tm, LANE), lambda n, m: ((mt0 + m) % per, 0)))
    kern = functools.partial(_proj_kernel, tn=tn, rms=gains is not None, rope=rope is not None)
    return pl.pallas_call(
        kern,
        out_shape=jax.ShapeDtypeStruct((m_rows, ncols), out_dtype),
        grid=(n_t, m_t),
        in_specs=in_specs,
        out_specs=pl.BlockSpec((tm, tn), lambda n, m: (m, n)),
        scratch_shapes=[pltpu.VMEM((k, tn), BF16)],
        compiler_params=_params("arbitrary", "arbitrary"),
        name=name,
    )(*ins)


def _qkt(q, k):
    return lax.dot_general(q, k, (((1,), (1,)), ((), ())), preferred_element_type=F32)


def _softmax_parts(scores, scale):
    m = scores[0].max(axis=-1, keepdims=True)
    for s in scores[1:]:
        m = jnp.maximum(m, s.max(axis=-1, keepdims=True))
    c = scale * LOG2E
    es = [jnp.exp2((s - m) * c) for s in scores]
    tot = es[0].sum(axis=-1, keepdims=True)
    for e in es[1:]:
        tot = tot + e.sum(axis=-1, keepdims=True)
    return es, 1.0 / tot


def _diff_lambda(lp_ref, lam_init):
    lp = lp_ref[...]
    s1 = jnp.sum(lp[0:1] * lp[1:2], axis=-1, keepdims=True)
    s2 = jnp.sum(lp[2:3] * lp[3:4], axis=-1, keepdims=True)
    return jnp.exp(s1) - jnp.exp(s2) + lam_init


def _diff_head(q, ks, vs, lam, gain):
    scale = A_HEAD_DIM ** -0.5
    q0, q1 = q[:, :A_HEAD_DIM], q[:, A_HEAD_DIM:]
    e0, r0 = _softmax_parts([_qkt(q0, k[:, :A_HEAD_DIM]) for k in ks], scale)
    e1, r1 = _softmax_parts([_qkt(q1, k[:, A_HEAD_DIM:]) for k in ks], scale)
    r1 = lam * r1
    o = None
    for a0, a1, v in zip(e0, e1, vs):
        a = (a0 * r0 - a1 * r1).astype(BF16)
        part = jnp.dot(a, v, preferred_element_type=F32)
        o = part if o is None else o + part
    ms = jnp.mean(o * o, axis=-1, keepdims=True)
    return (o * lax.rsqrt(ms + RMS_EPS) * gain).astype(BF16)


def _gqa_head(q, ks, vs):
    es, r = _softmax_parts([_qkt(q, k) for k in ks], B_HEAD_DIM ** -0.5)
    o = None
    for e, v in zip(es, vs):
        part = jnp.dot((e * r).astype(BF16), v, preferred_element_type=F32)
        o = part if o is None else o + part
    return o.astype(BF16)


def _diff_prompt_kernel(lp_ref, g_ref, q_ref, k_ref, v_ref, o_ref, *, lam_init):
    lam = _diff_lambda(lp_ref, lam_init)
    gain = g_ref[...] * (1.0 - lam_init)
    w = 2 * A_HEAD_DIM
    for h in range(A_HEADS):
        sl = slice(h * w, (h + 1) * w)
        o_ref[:, sl] = _diff_head(q_ref[:, sl], [k_ref[:, sl].astype(BF16)],
                                  [v_ref[:, sl].astype(BF16)], lam, gain)


def _diff_prompt(lp, g, q, kv, j, lam_init):
    return pl.pallas_call(
        functools.partial(_diff_prompt_kernel, lam_init=lam_init),
        out_shape=jax.ShapeDtypeStruct((M_P, D_MODEL), BF16),
        grid=(BATCH,),
        in_specs=[
            pl.BlockSpec((None, 4, A_HEAD_DIM), lambda b: (j, 0, 0)),
            pl.BlockSpec((None, 1, 2 * A_HEAD_DIM), lambda b: (j, 0, 0)),
            pl.BlockSpec((SEQ, D_MODEL), lambda b: (b, 0)),
            pl.BlockSpec((SEQ, D_MODEL), lambda b: (b, 0)),
            pl.BlockSpec((SEQ, D_MODEL), lambda b: (b, 1)),
        ],
        out_specs=pl.BlockSpec((SEQ, D_MODEL), lambda b: (b, 0)),
        compiler_params=_params("arbitrary"),
        name="diff_attn_prompt",
    )(lp, g, q, kv, kv)


def _diff_latent_kernel(lp_ref, g_ref, q_ref, k_ref, v_ref, ck_ref, cv_ref, o_ref, *, lam_init):
    lam = _diff_lambda(lp_ref, lam_init)
    gain = g_ref[...] * (1.0 - lam_init)
    o_ref[...] = _diff_head(q_ref[...], [ck_ref[...].astype(BF16), k_ref[...]],
                            [cv_ref[...].astype(BF16), v_ref[...]], lam, gain)


def _diff_latent(lp, g, qk, v, ck, cv, j, lam_init, tq=512):
    w = 2 * A_HEAD_DIM
    per = DEC_SEQ // tq
    ctx_spec = pl.BlockSpec((None, None, PAST_LEN, w), lambda b, h, t: (b, j, 0, h))
    return pl.pallas_call(
        functools.partial(_diff_latent_kernel, lam_init=lam_init),
        out_shape=jax.ShapeDtypeStruct((M_S, D_MODEL), BF16),
        grid=(DEC_BATCH, A_HEADS, per),
        in_specs=[
            pl.BlockSpec((None, 4, A_HEAD_DIM), lambda b, h, t: (j, 0, 0)),
            pl.BlockSpec((None, 1, w), lambda b, h, t: (j, 0, 0)),
            pl.BlockSpec((tq, w), lambda b, h, t: (b * per + t, h)),
            pl.BlockSpec((DEC_SEQ, w), lambda b, h, t: (b, A_HEADS + h)),
            pl.BlockSpec((DEC_SEQ, w), lambda b, h, t: (b, h)),
            ctx_spec, ctx_spec,
        ],
        out_specs=pl.BlockSpec((tq, w), lambda b, h, t: (b * per + t, h)),
        compiler_params=_params("arbitrary", "arbitrary", "arbitrary"),
        name="diff_attn_latent",
    )(lp, g, qk, qk, v, ck, cv)


def _gqa_prompt_kernel(q_ref, k_ref, v_ref, o_ref):
    d = B_HEAD_DIM
    for n in range(B_KV_HEADS):
        k = [k_ref[:, n * d:(n + 1) * d].astype(BF16)]
        v = [v_ref[:, n * d:(n + 1) * d].astype(BF16)]
        for g in range(B_GROUP):
            sl = slice((n * B_GROUP + g) * d, (n * B_GROUP + g + 1) * d)
            o_ref[:, sl] = _gqa_head(q_ref[:, sl], k, v)


def _gqa_prompt(q, k, v):
    kv_spec = pl.BlockSpec((SEQ, B_KV), lambda b: (b, 0))
    return pl.pallas_call(
        _gqa_prompt_kernel,
        out_shape=jax.ShapeDtypeStruct((M_P, D_MODEL), BF16),
        grid=(BATCH,),
        in_specs=[pl.BlockSpec((SEQ, D_MODEL), lambda b: (b, 0)), kv_spec, kv_spec],
        out_specs=pl.BlockSpec((SEQ, D_MODEL), lambda b: (b, 0)),
        compiler_params=_params("arbitrary"),
        name="gqa_attn_prompt",
    )(q, k, v)


def _gqa_latent_kernel(q_ref, k_ref, v_ref, ck_ref, cv_ref, o_ref):
    d = B_HEAD_DIM
    ks = [ck_ref[...].astype(BF16), k_ref[...]]
    vs = [cv_ref[...].astype(BF16), v_ref[...]]
    for g in range(B_GROUP):
        o_ref[:, g * d:(g + 1) * d] = _gqa_head(q_ref[:, g * d:(g + 1) * d], ks, vs)


def _gqa_latent(qk, v, ck, cv, j, tq=512):
    d = B_HEAD_DIM
    gw = B_GROUP * d
    per = DEC_SEQ // tq
    ctx_spec = pl.BlockSpec((None, None, PAST_LEN, d), lambda b, n, t: (b, j, 0, n))
    return pl.pallas_call(
        _gqa_latent_kernel,
        out_shape=jax.ShapeDtypeStruct((M_S, D_MODEL), BF16),
        grid=(DEC_BATCH, B_KV_HEADS, per),
        in_specs=[
            pl.BlockSpec((tq, gw), lambda b, n, t: (b * per + t, n)),
            pl.BlockSpec((DEC_SEQ, d), lambda b, n, t: (b, D_MODEL // d + n)),
            pl.BlockSpec((DEC_SEQ, d), lambda b, n, t: (b, n)),
            ctx_spec, ctx_spec,
        ],
        out_specs=pl.BlockSpec((tq, gw), lambda b, n, t: (b * per + t, n)),
        compiler_params=_params("arbitrary", "arbitrary", "arbitrary"),
        name="gqa_attn_latent",
    )(qk, qk, v, ck, cv)


def _dft_tables(s):
    def cs(n):
        idx = (np.arange(n)[:, None] * np.arange(n)[None, :]) % n
        ang = 2.0 * np.pi * idx / n
        return np.cos(ang), np.sin(ang)
    cc, sc = cs(C_GROUP_DIM)
    c_s, s_s = cs(s)
    t_chan = np.concatenate([cc, sc], axis=1).astype(np.float32)
    t_pos = np.concatenate([c_s, -s_s], axis=1).astype(np.float32)
    return jnp.asarray(t_chan).astype(BF16), jnp.asarray(t_pos).astype(BF16)


def _dft_group(x, t_chan, t_pos, norm):
    xcs = jnp.dot(x, t_chan, preferred_element_type=F32).astype(BF16)
    stacked = jnp.concatenate([xcs[:, :C_GROUP_DIM], xcs[:, C_GROUP_DIM:]], axis=0)
    y = jnp.dot(t_pos, stacked, preferred_element_type=F32)
    return (y * norm).astype(BF16)


def _dft_kernel(u_ref, tc_ref, tp_ref, o_ref, *, groups, norm):
    tc, tp = tc_ref[...], tp_ref[...]
    for g in range(groups):
        sl = slice(g * C_GROUP_DIM, (g + 1) * C_GROUP_DIM)
        o_ref[:, sl] = _dft_group(u_ref[:, sl], tc, tp, norm)


def _dft(u, s, batch, row0, groups_per_step):
    t_chan, t_pos = _dft_tables(s)
    gw = groups_per_step * C_GROUP_DIM
    rb0 = row0 // s
    return pl.pallas_call(
        functools.partial(_dft_kernel, groups=groups_per_step,
                          norm=1.0 / math.sqrt(s * C_GROUP_DIM)),
        out_shape=jax.ShapeDtypeStruct((batch * s, D_MODEL), BF16),
        grid=(batch, C_GROUPS // groups_per_step),
        in_specs=[
            pl.BlockSpec((s, gw), lambda b, g: (rb0 + b, g)),
            pl.BlockSpec(t_chan.shape, lambda b, g: (0, 0)),
            pl.BlockSpec(t_pos.shape, lambda b, g: (0, 0)),
        ],
        out_specs=pl.BlockSpec((s, gw), lambda b, g: (b, g)),
        compiler_params=_params("arbitrary", "arbitrary"),
        name="dft",
    )(u, t_chan, t_pos)


ROW_GROUPS = 2


def _post_norm(y, g, b):
    mu = jnp.mean(y, axis=-1, keepdims=True)
    yc = y - mu
    var = jnp.mean(yc * yc, axis=-1, keepdims=True)
    return yc * lax.rsqrt(var + LN_EPS) * g + b


def _oproj_kernel(ap_ref, as_ref, w_ref, x_ref, gate_ref, g_ref, b_ref, sc_ref, sh_ref,
                  x1_ref, u_ref):
    is_prompt = pl.program_id(0) * ROW_TILE < M_P
    gate, g, b = gate_ref[...], g_ref[...], b_ref[...]
    sc1, sh = 1.0 + sc_ref[...], sh_ref[...]
    rows = ROW_TILE // ROW_GROUPS
    for r in range(ROW_GROUPS):
        sl = slice(r * rows, (r + 1) * rows)
        a = jnp.where(is_prompt, ap_ref[sl, :], as_ref[sl, :])
        mixed = jnp.dot(a, w_ref[...], preferred_element_type=F32)
        x1 = _post_norm(ALPHA * x_ref[sl, :] + gate * mixed, g, b)
        x1_ref[sl, :] = x1
        u_ref[sl, :] = (x1 * sc1 + sh).astype(BF16)


def _oproj(a_p, a_s, w_bf, w_layer, x, mod, ln_g, ln_b, l):
    tm = ROW_TILE
    row = pl.BlockSpec((tm, D_MODEL), lambda i: (i, 0))
    return pl.pallas_call(
        _oproj_kernel,
        out_shape=(jax.ShapeDtypeStruct((M_ALL, D_MODEL), F32),
                   jax.ShapeDtypeStruct((M_ALL, D_MODEL), BF16)),
        grid=(M_ALL // tm,),
        in_specs=[
            _prompt_rows_spec(tm), _latent_rows_spec(tm),
            pl.BlockSpec((None, D_MODEL, D_MODEL), lambda i: (w_layer, 0, 0),
                         pipeline_mode=pl.Buffered(1)),
            row,
            _mod_spec(l, 2, tm),
            _vec_spec(l, 0), _vec_spec(l, 0),
            _mod_spec(l, 4, tm),
            _mod_spec(l, 3, tm),
        ],
        out_specs=(row, row),
        compiler_params=_params("arbitrary"),
        name="oproj_ln",
    )(a_p, a_s, w_bf, x, mod, ln_g, ln_b, mod, mod)


def _mlp_kernel(u_ref, wu_ref, wd_ref, ff_ref):
    @pl.when(pl.program_id(1) == 0)
    def _():
        ff_ref[...] = jnp.zeros_like(ff_ref)

    wu = wu_ref[...].astype(BF16)
    wd = wd_ref[...].astype(BF16)
    rows = u_ref.shape[0] // ROW_GROUPS
    for r in range(ROW_GROUPS):
        sl = slice(r * rows, (r + 1) * rows)
        h = jnp.dot(u_ref[sl, :], wu, preferred_element_type=F32)
        h = jnp.maximum(h, 0.0)
        h = (h * h).astype(BF16)
        ff_ref[sl, :] += jnp.dot(h, wd, preferred_element_type=F32)


def _mlp(u, w_up, w_down, l, tm=1024, tf=512):
    return pl.pallas_call(
        _mlp_kernel,
        out_shape=jax.ShapeDtypeStruct((M_ALL, D_MODEL), F32),
        grid=(M_ALL // tm, D_FF // tf),
        in_specs=[
            pl.BlockSpec((tm, D_MODEL), lambda i, f: (i, 0), pipeline_mode=pl.Buffered(1)),
            pl.BlockSpec((None, D_MODEL, tf), lambda i, f: (l, 0, f)),
            pl.BlockSpec((None, tf, D_MODEL), lambda i, f: (l, f, 0)),
        ],
        out_specs=pl.BlockSpec((tm, D_MODEL), lambda i, f: (i, 0)),
        compiler_params=_params("arbitrary", "arbitrary"),
        name="mlp",
    )(u, w_up, w_down)


def _close_kernel(ff_ref, x_ref, gate_ref, g_ref, b_ref, sc_ref, sh_ref, x2_ref, un_ref):
    x2 = _post_norm(ALPHA * x_ref[...] + gate_ref[...] * ff_ref[...], g_ref[...], b_ref[...])
    x2_ref[...] = x2
    un_ref[...] = (x2 * (1.0 + sc_ref[...]) + sh_ref[...]).astype(BF16)


def _close(ff, x, mod, ln_g, ln_b, l):
    tm = ROW_TILE
    row = pl.BlockSpec((tm, D_MODEL), lambda i: (i, 0))
    return pl.pallas_call(
        _close_kernel,
        out_shape=(jax.ShapeDtypeStruct((M_ALL, D_MODEL), F32),
                   jax.ShapeDtypeStruct((M_ALL, D_MODEL), BF16)),
        grid=(M_ALL // tm,),
        in_specs=[row, row, _mod_spec(l, 5, tm), _vec_spec(l, 1), _vec_spec(l, 1),
                  _mod_spec(l + 1, 1, tm), _mod_spec(l + 1, 0, tm)],
        out_specs=(row, row),
        compiler_params=_params("arbitrary"),
        name="mlp_close",
    )(ff, x, mod, ln_g, ln_b, mod, mod)


def _final_kernel(ff_ref, x_ref, gate_ref, g_ref, b_ref, yp_ref, ys_ref):
    y = _post_norm(ALPHA * x_ref[...] + gate_ref[...] * ff_ref[...], g_ref[...], b_ref[...])
    is_prompt = pl.program_id(0) * ROW_TILE < M_P

    @pl.when(is_prompt)
    def _():
        yp_ref[...] = y

    @pl.when(jnp.logical_not(is_prompt))
    def _():
        ys_ref[...] = y


def _final(ff, x, mod, ln_g, ln_b, l):
    tm = ROW_TILE
    row = pl.BlockSpec((tm, D_MODEL), lambda i: (i, 0))
    return pl.pallas_call(
        _final_kernel,
        out_shape=(jax.ShapeDtypeStruct((M_P, D_MODEL), F32),
                   jax.ShapeDtypeStruct((M_S, D_MODEL), F32)),
        grid=(M_ALL // tm,),
        in_specs=[row, row, _mod_spec(l, 5, tm), _vec_spec(l, 1), _vec_spec(l, 1)],
        out_specs=(_prompt_rows_spec(tm), _latent_rows_spec(tm)),
        compiler_params=_params("arbitrary"),
        name="mlp_final",
    )(ff, x, mod, ln_g, ln_b)


def _rope_tables():
    n_freq = A_HEAD_DIM // 4
    pos = np.arange(DEC_SEQ)
    row = (pos // GRID_W).astype(np.float32)
    col = (pos % GRID_W).astype(np.float32)
    inv_freq = (ROPE_BASE ** (-np.arange(n_freq, dtype=np.float32) / n_freq)).astype(np.float32)
    ar = row[:, None] * inv_freq
    ac = col[:, None] * inv_freq
    cr, sr, cc, sc = np.cos(ar), np.sin(ar), np.cos(ac), np.sin(ac)
    z = np.zeros_like(sr)
    cos = np.concatenate([cr, cr, cc, cc], axis=1)
    sin_a = np.concatenate([-sr, z, -sc, z], axis=1)
    sin_b = np.concatenate([z, sr, z, sc], axis=1)
    return tuple(jnp.asarray(t.astype(np.float32)) for t in (cos, sin_a, sin_b))


def kernel(x_prompt, x_sample, cache_a_k, cache_a_v, cache_b_k, cache_b_v, c, c_ctx, w_mod, b_mod, ln_g, ln_b, w_up, w_down, a_w_qkv, a_w_o, a_lambda, a_subln_g, b_w_qkv, b_w_o, b_q_norm_g, b_k_norm_g, c_w_f):
    n_a = a_w_qkv.shape[0]
    n_b = b_w_qkv.shape[0]
    cond = jnp.zeros((COND_PAD, D_MODEL), F32).at[0].set(c_ctx).at[1:N_COND].set(c)
    mod = _modulation(cond, w_mod, b_mod).reshape(DEPTH, COND_PAD, N_MOD, 1, D_MODEL)
    ln_g4 = ln_g.reshape(DEPTH, 2, 1, D_MODEL)
    ln_b4 = ln_b.reshape(DEPTH, 2, 1, D_MODEL)
    rope = _rope_tables()
    w_o_bf = (_cast_bf16(a_w_o), _cast_bf16(b_w_o), _cast_bf16(c_w_f))

    ck_a = cache_a_k.reshape(DEC_BATCH, n_a, PAST_LEN, D_MODEL)
    cv_a = cache_a_v.reshape(DEC_BATCH, n_a, PAST_LEN, D_MODEL)
    ck_b = cache_b_k.reshape(DEC_BATCH, n_b, PAST_LEN, B_KV)
    cv_b = cache_b_v.reshape(DEC_BATCH, n_b, PAST_LEN, B_KV)
    subln = a_subln_g.reshape(n_a, 1, 2 * A_HEAD_DIM)
    b_gains = jnp.stack([b_q_norm_g, b_k_norm_g], axis=1).reshape(n_b, 2, 1, B_HEAD_DIM)

    x, u = _embed(x_prompt.reshape(M_P, D_MODEL), x_sample.reshape(M_S, D_MODEL), mod)

    a_kv, b_k, b_v = [], [], []
    for l in range(DEPTH):
        kind, j = l % N_MIXERS, l // N_MIXERS
        if kind == 0:
            lam_init = 0.8 - 0.6 * math.exp(-0.3 * l)
            q_p = _proj(u, a_w_qkv, j, 0, D_MODEL, 0, M_P, out_dtype=BF16, name="a_q_prompt")
            kv_p = _proj(u, a_w_qkv, j, D_MODEL, 2 * D_MODEL, 0, M_P, out_dtype=F32,
                         name="a_kv_prompt")
            qk_s = _proj(u, a_w_qkv, j, 0, 2 * D_MODEL, M_P, M_S, out_dtype=BF16, rope=rope,
                         name="a_qk_latent")
            v_s = _proj(u, a_w_qkv, j, 2 * D_MODEL, D_MODEL, M_P, M_S, out_dtype=BF16,
                        name="a_v_latent")
            a_kv.append(kv_p)
            att_p = _diff_prompt(a_lambda, subln, q_p, kv_p, j, lam_init)
            att_s = _diff_latent(a_lambda, subln, qk_s, v_s, ck_a, cv_a, j, lam_init)
            w_o = w_o_bf[0]
        elif kind == 1:
            gj = b_gains[j]
            q_p = _proj(u, b_w_qkv, j, 0, D_MODEL, 0, M_P, out_dtype=BF16, gains=gj,
                        gain_split=D_MODEL, name="b_q_prompt")
            k_p = _proj(u, b_w_qkv, j, D_MODEL, B_KV, 0, M_P, out_dtype=F32, tn=B_KV,
                        gains=gj[1:], gain_split=B_KV, name="b_k_prompt")
            v_p = _proj(u, b_w_qkv, j, D_MODEL + B_KV, B_KV, 0, M_P, out_dtype=F32, tn=B_KV,
                        name="b_v_prompt")
            qk_s = _proj(u, b_w_qkv, j, 0, D_MODEL + B_KV, M_P, M_S, out_dtype=BF16, tn=B_KV,
                         gains=gj, gain_split=D_MODEL, rope=rope, name="b_qk_latent")
            v_s = _proj(u, b_w_qkv, j, D_MODEL + B_KV, B_KV, M_P, M_S, out_dtype=BF16, tn=B_KV,
                        name="b_v_latent")
            b_k.append(k_p)
            b_v.append(v_p)
            att_p = _gqa_prompt(q_p, k_p, v_p)
            att_s = _gqa_latent(qk_s, v_s, ck_b, cv_b, j)
            w_o = w_o_bf[1]
        else:
            att_p = _dft(u, SEQ, BATCH, 0, C_GROUPS)
            att_s = _dft(u, DEC_SEQ, DEC_BATCH, M_P, 1)
            w_o = w_o_bf[2]
        x, u = _oproj(att_p, att_s, w_o, j, x, mod, ln_g4, ln_b4, l)
        ff = _mlp(u, w_up, w_down, l)
        if l + 1 < DEPTH:
            x, u = _close(ff, x, mod, ln_g4, ln_b4, l)
        else:
            y_p, y_s = _final(ff, x, mod, ln_g4, ln_b4, l)

    def per_layer(parts, cols, tail):
        stacked = jnp.stack([p[:, cols].reshape(BATCH, SEQ, *tail) for p in parts], axis=1)
        return stacked

    new_a_k = per_layer(a_kv, slice(0, D_MODEL), (A_HEADS, 2, A_HEAD_DIM))
    new_a_v = per_layer(a_kv, slice(D_MODEL, 2 * D_MODEL), (A_HEADS, 2 * A_HEAD_DIM))
    new_b_k = per_layer(b_k, slice(None), (B_KV_HEADS, B_HEAD_DIM))
    new_b_v = per_layer(b_v, slice(None), (B_KV_HEADS, B_HEAD_DIM))
    return (y_p.reshape(BATCH, SEQ, D_MODEL), y_s.reshape(DEC_BATCH, DEC_SEQ, D_MODEL),
            new_a_k, new_a_v, new_b_k, new_b_v)
```

```python
import functools
import math

import jax
import jax.numpy as jnp
import numpy as np
from jax import lax
from jax.experimental import pallas as pl
from jax.experimental.pallas import tpu as pltpu

D_MODEL = 2048
BATCH = 16
SEQ = 256
DEPTH = 4
DEC_BATCH = 2
DEC_SEQ = 1024
PAST_LEN = 512
GRID_W = 64
N_MIXERS = 3
A_HEAD_DIM = 128
A_HEADS = D_MODEL // (2 * A_HEAD_DIM)
B_HEAD_DIM = 128
B_HEADS = D_MODEL // B_HEAD_DIM
B_KV_HEADS = B_HEADS // 4
B_GROUP = B_HEADS // B_KV_HEADS
B_KV = B_KV_HEADS * B_HEAD_DIM
C_GROUPS = 4
C_GROUP_DIM = D_MODEL // C_GROUPS
D_FF = 4 * D_MODEL
ROPE_BASE = 10000.0
ALPHA = (2 * DEPTH) ** 0.25
LN_EPS = 1e-5
RMS_EPS = 1e-6
N_MOD = 6
LOG2E = 1.4426950408889634

M_P = BATCH * SEQ
M_S = DEC_BATCH * DEC_SEQ
M_ALL = M_P + M_S
N_COND = 1 + DEC_BATCH
COND_PAD = 8

LANE = 128
VMEM_LIMIT = 58 * 1024 * 1024
ROW_TILE = 512

BF16 = jnp.bfloat16
F32 = jnp.float32


def _params(*sem, vmem=VMEM_LIMIT):
    return pltpu.CompilerParams(dimension_semantics=sem, vmem_limit_bytes=vmem)


def _cond_row(i, tm):
    start = i * tm
    return jnp.where(start < M_P, 0, 1 + (start - M_P) // DEC_SEQ)


def _mod_spec(l, which, tm):
    def idx(*g):
        return (l, _cond_row(g[0], tm), which, 0, 0)
    return pl.BlockSpec((None, None, None, 1, D_MODEL), idx)


def _vec_spec(l, which):
    return pl.BlockSpec((None, None, 1, D_MODEL), lambda *g: (l, which, 0, 0))


def _prompt_rows_spec(tm):
    last = M_P // tm - 1
    return pl.BlockSpec((tm, D_MODEL), lambda i: (jnp.minimum(i, last), 0))


def _latent_rows_spec(tm):
    first = M_P // tm
    return pl.BlockSpec((tm, D_MODEL), lambda i: (jnp.maximum(i - first, 0), 0))


def _mod_kernel(c_ref, w_ref, b_ref, o_ref):
    c = c_ref[...]
    s = c * (1.0 / (1.0 + jnp.exp(-c)))
    o_ref[...] = jnp.dot(s.astype(BF16), w_ref[...].astype(BF16),
                         preferred_element_type=F32) + b_ref[...]


def _modulation(cond, w_mod, b_mod):
    tn = 1024
    n_out = N_MOD * D_MODEL
    return pl.pallas_call(
        _mod_kernel,
        out_shape=jax.ShapeDtypeStruct((DEPTH, COND_PAD, n_out), F32),
        grid=(DEPTH, n_out // tn),
        in_specs=[
            pl.BlockSpec((COND_PAD, D_MODEL), lambda l, n: (0, 0)),
            pl.BlockSpec((None, D_MODEL, tn), lambda l, n: (l, 0, n)),
            pl.BlockSpec((None, 1, tn), lambda l, n: (l, 0, n)),
        ],
        out_specs=pl.BlockSpec((None, COND_PAD, tn), lambda l, n: (l, 0, n)),
        compiler_params=_params("arbitrary", "arbitrary"),
        name="modulation",
    )(cond, w_mod, b_mod.reshape(DEPTH, 1, n_out))


def _cast_kernel(w_ref, o_ref):
    o_ref[...] = w_ref[...].astype(BF16)


def _cast_bf16(w, rows=512):
    n_l, k, n = w.shape
    spec = pl.BlockSpec((None, rows, n), lambda l, r: (l, r, 0))
    return pl.pallas_call(
        _cast_kernel,
        out_shape=jax.ShapeDtypeStruct(w.shape, BF16),
        grid=(n_l, k // rows),
        in_specs=[spec],
        out_specs=spec,
        compiler_params=_params("arbitrary", "arbitrary"),
        name="cast_weight",
    )(w)


def _embed_kernel(xp_ref, xs_ref, sc_ref, sh_ref, x_ref, u_ref):
    x = jnp.where(pl.program_id(0) * ROW_TILE < M_P, xp_ref[...], xs_ref[...])
    x_ref[...] = x
    u_ref[...] = (x * (1.0 + sc_ref[...]) + sh_ref[...]).astype(BF16)


def _embed(xp, xs, mod):
    tm = ROW_TILE
    row = pl.BlockSpec((tm, D_MODEL), lambda i: (i, 0))
    return pl.pallas_call(
        _embed_kernel,
        out_shape=(jax.ShapeDtypeStruct((M_ALL, D_MODEL), F32),
                   jax.ShapeDtypeStruct((M_ALL, D_MODEL), BF16)),
        grid=(M_ALL // tm,),
        in_specs=[_prompt_rows_spec(tm), _latent_rows_spec(tm),
                  _mod_spec(0, 1, tm), _mod_spec(0, 0, tm)],
        out_specs=(row, row),
        compiler_params=_params("arbitrary"),
        name="embed",
    )(xp, xs, mod, mod)


def _rms_chunk(x, g):
    ms = jnp.mean(x * x, axis=-1, keepdims=True)
    return x * lax.rsqrt(ms + RMS_EPS) * g


def _rope_chunk(x, cos, sin_a, sin_b):
    return x * cos + pltpu.roll(x, LANE - 32, 1) * sin_a + pltpu.roll(x, 32, 1) * sin_b


def _proj_kernel(*refs, tn, rms, rope):
    it = iter(refs)
    u_ref, w_ref = next(it), next(it)
    g_ref = next(it) if rms else None
    tabs = (next(it), next(it), next(it)) if rope else None
    o_ref, wbf = next(it), next(it)

    @pl.when(pl.program_id(1) == 0)
    def _():
        wbf[...] = w_ref[...].astype(BF16)

    acc = jnp.dot(u_ref[...], wbf[...], preferred_element_type=F32)
    if rms or rope:
        g = g_ref[...] if rms else None
        tab = tuple(t[...] for t in tabs) if rope else None
        for c in range(tn // LANE):
            x = acc[:, c * LANE:(c + 1) * LANE]
            if rms:
                x = _rms_chunk(x, g)
            if rope:
                x = _rope_chunk(x, *tab)
            o_ref[:, c * LANE:(c + 1) * LANE] = x.astype(o_ref.dtype)
    else:
        o_ref[...] = acc.astype(o_ref.dtype)


def _proj(u, w, w_layer, col0, ncols, m0, m_rows, *, out_dtype, tn=1024, tm=1024,
          gains=None, gain_split=None, rope=None, name="proj"):
    k = u.shape[1]
    assert col0 % tn == 0 and ncols % tn == 0 and m0 % tm == 0 and m_rows % tm == 0
    n_t, m_t, mt0, nt0 = ncols // tn, m_rows // tm, m0 // tm, col0 // tn
    ins = [u, w]
    in_specs = [
        pl.BlockSpec((tm, k), lambda n, m: (mt0 + m, 0)),
        pl.BlockSpec((None, k, tn), lambda n, m: (w_layer, 0, nt0 + n)),
    ]
    if gains is not None:
        ins.append(gains)
        in_specs.append(pl.BlockSpec(
            (None, 1, LANE), lambda n, m: (jnp.where(n * tn < gain_split, 0, 1), 0, 0)))
    if rope is not None:
        assert m0 >= M_P
        per = DEC_SEQ // tm
        for t in rope:
            ins.append(t)
            in_specs.append(pl.BlockSpec((tm, LANE), lambda n, m: ((mt0 + m) % per, 0)))
    kern = functools.partial(_proj_kernel, tn=tn, rms=gains is not None, rope=rope is not None)
    return pl.pallas_call(
        kern,
        out_shape=jax.ShapeDtypeStruct((m_rows, ncols), out_dtype),
        grid=(n_t, m_t),
        in_specs=in_specs,
        out_specs=pl.BlockSpec((tm, tn), lambda n, m: (m, n)),
        scratch_shapes=[pltpu.VMEM((k, tn), BF16)],
        compiler_params=_params("arbitrary", "arbitrary"),
        name=name,
    )(*ins)


def _qkt(q, k):
    return lax.dot_general(q, k, (((1,), (1,)), ((), ())), preferred_element_type=F32)


def _softmax_parts(scores, scale):
    m = scores[0].max(axis=-1, keepdims=True)
    for s in scores[1:]:
        m = jnp.maximum(m, s.max(axis=-1, keepdims=True))
    c = scale * LOG2E
    es = [jnp.exp2((s - m) * c) for s in scores]
    tot = es[0].sum(axis=-1, keepdims=True)
    for e in es[1:]:
        tot = tot + e.sum(axis=-1, keepdims=True)
    return es, 1.0 / tot


def _diff_lambda(lp_ref, lam_init):
    lp = lp_ref[...]
    s1 = jnp.sum(lp[0:1] * lp[1:2], axis=-1, keepdims=True)
    s2 = jnp.sum(lp[2:3] * lp[3:4], axis=-1, keepdims=True)
    return jnp.exp(s1) - jnp.exp(s2) + lam_init


def _diff_head(q, ks, vs, lam, gain):
    scale = A_HEAD_DIM ** -0.5
    q0, q1 = q[:, :A_HEAD_DIM], q[:, A_HEAD_DIM:]
    e0, r0 = _softmax_parts([_qkt(q0, k[:, :A_HEAD_DIM]) for k in ks], scale)
    e1, r1 = _softmax_parts([_qkt(q1, k[:, A_HEAD_DIM:]) for k in ks], scale)
    r1 = lam * r1
    o = None
    for a0, a1, v in zip(e0, e1, vs):
        a = (a0 * r0 - a1 * r1).astype(BF16)
        part = jnp.dot(a, v, preferred_element_type=F32)
        o = part if o is None else o + part
    ms = jnp.mean(o * o, axis=-1, keepdims=True)
    return (o * lax.rsqrt(ms + RMS_EPS) * gain).astype(BF16)


def _gqa_head(q, ks, vs):
    es, r = _softmax_parts([_qkt(q, k) for k in ks], B_HEAD_DIM ** -0.5)
    o = None
    for e, v in zip(es, vs):
        part = jnp.dot((e * r).astype(BF16), v, preferred_element_type=F32)
        o = part if o is None else o + part
    return o.astype(BF16)


A_CHUNKS = D_MODEL // LANE
B_CHUNKS = B_KV // LANE


def _a_v_row(c):
    return (c % 2) * A_HEADS + c // 2


def _a_v_rows(v, batch, seq):
    n_l = v.shape[1]
    v = v.reshape(batch, n_l, seq, A_HEADS, 2, LANE).transpose(0, 1, 2, 4, 3, 5)
    return v.reshape(batch, n_l, seq * A_CHUNKS, LANE)


def _a_v_from_rows(r, batch, seq):
    n_l = r.shape[1]
    v = r.reshape(batch, n_l, seq, 2, A_HEADS, LANE).transpose(0, 1, 2, 4, 3, 5)
    return v.reshape(batch, n_l, seq, A_HEADS, 2 * LANE)


def _diff_prompt_kernel(*refs, lam_init, n_prev, emit):
    lp_ref, g_ref, q_ref, k_ref, v_ref = refs[:5]
    prev = refs[5:5 + n_prev]
    o_ref = refs[5 + n_prev]
    lam = _diff_lambda(lp_ref, lam_init)
    gain = g_ref[...] * (1.0 - lam_init)
    w = 2 * A_HEAD_DIM
    for h in range(A_HEADS):
        sl = slice(h * w, (h + 1) * w)
        o_ref[:, sl] = _diff_head(q_ref[:, sl], [k_ref[:, sl].astype(BF16)],
                                  [v_ref[:, sl].astype(BF16)], lam, gain)
    if emit:
        nk_ref, nv_ref = refs[6 + n_prev], refs[7 + n_prev]
        layers = [(p, 0, p, D_MODEL) for p in prev] + [(k_ref, 0, v_ref, 0)]
        for jj, (kr, k0, vr, v0) in enumerate(layers):
            for c in range(A_CHUNKS):
                nk_ref[jj, pl.ds(c, SEQ, stride=A_CHUNKS), :] = kr[:, k0 + c * LANE:k0 + (c + 1) * LANE]
                nv_ref[jj, pl.ds(_a_v_row(c), SEQ, stride=A_CHUNKS), :] = (
                    vr[:, v0 + c * LANE:v0 + (c + 1) * LANE])


def _diff_prompt(lp, g, q, kv, j, lam_init, prev_kv=None):
    emit = prev_kv is not None
    prev_kv = list(prev_kv or [])
    n_layers = len(prev_kv) + 1
    att_shape = jax.ShapeDtypeStruct((M_P, D_MODEL), BF16)
    att_spec = pl.BlockSpec((SEQ, D_MODEL), lambda b: (b, 0))
    if emit:
        new_shape = jax.ShapeDtypeStruct((BATCH, n_layers, SEQ * A_CHUNKS, LANE), F32)
        new_spec = pl.BlockSpec((None, n_layers, SEQ * A_CHUNKS, LANE), lambda b: (b, 0, 0, 0))
        out_shape, out_specs = (att_shape, new_shape, new_shape), (att_spec, new_spec, new_spec)
    else:
        out_shape, out_specs = att_shape, att_spec
    return pl.pallas_call(
        functools.partial(_diff_prompt_kernel, lam_init=lam_init, n_prev=len(prev_kv), emit=emit),
        out_shape=out_shape,
        grid=(BATCH,),
        in_specs=[
            pl.BlockSpec((None, 4, A_HEAD_DIM), lambda b: (j, 0, 0)),
            pl.BlockSpec((None, 1, 2 * A_HEAD_DIM), lambda b: (j, 0, 0)),
            pl.BlockSpec((SEQ, D_MODEL), lambda b: (b, 0)),
            pl.BlockSpec((SEQ, D_MODEL), lambda b: (b, 0)),
            pl.BlockSpec((SEQ, D_MODEL), lambda b: (b, 1)),
        ] + [pl.BlockSpec((SEQ, 2 * D_MODEL), lambda b: (b, 0)) for _ in prev_kv],
        out_specs=out_specs,
        compiler_params=_params("arbitrary"),
        name="diff_attn_prompt",
    )(lp, g, q, kv, kv, *prev_kv)


def _diff_latent_kernel(lp_ref, g_ref, q_ref, k_ref, v_ref, ck_ref, cv_ref, o_ref, *, lam_init):
    lam = _diff_lambda(lp_ref, lam_init)
    gain = g_ref[...] * (1.0 - lam_init)
    h = pl.program_id(1)

    def cached(ref, r0, r1):
        halves = [ref[pl.ds(r, PAST_LEN, stride=A_CHUNKS), :] for r in (r0, r1)]
        return jnp.concatenate(halves, axis=1).astype(BF16)

    ck = cached(ck_ref, 2 * h, 2 * h + 1)
    cv = cached(cv_ref, h, A_HEADS + h)
    o_ref[...] = _diff_head(q_ref[...], [ck, k_ref[...]], [cv, v_ref[...]], lam, gain)


def _diff_latent(lp, g, qk, v, ck, cv, j, lam_init, tq=512):
    w = 2 * A_HEAD_DIM
    per = DEC_SEQ // tq
    ctx_spec = pl.BlockSpec((None, None, PAST_LEN * A_CHUNKS, LANE), lambda b, h, t: (b, j, 0, 0))
    return pl.pallas_call(
        functools.partial(_diff_latent_kernel, lam_init=lam_init),
        out_shape=jax.ShapeDtypeStruct((M_S, D_MODEL), BF16),
        grid=(DEC_BATCH, A_HEADS, per),
        in_specs=[
            pl.BlockSpec((None, 4, A_HEAD_DIM), lambda b, h, t: (j, 0, 0)),
            pl.BlockSpec((None, 1, w), lambda b, h, t: (j, 0, 0)),
            pl.BlockSpec((tq, w), lambda b, h, t: (b * per + t, h)),
            pl.BlockSpec((DEC_SEQ, w), lambda b, h, t: (b, A_HEADS + h)),
            pl.BlockSpec((DEC_SEQ, w), lambda b, h, t: (b, h)),
            ctx_spec, ctx_spec,
        ],
        out_specs=pl.BlockSpec((tq, w), lambda b, h, t: (b * per + t, h)),
        compiler_params=_params("arbitrary", "arbitrary", "arbitrary"),
        name="diff_attn_latent",
    )(lp, g, qk, qk, v, ck, cv)


def _gqa_prompt_kernel(*refs, n_prev, emit):
    q_ref, k_ref, v_ref = refs[:3]
    prev = refs[3:3 + 2 * n_prev]
    o_ref = refs[3 + 2 * n_prev]
    d = B_HEAD_DIM
    for n in range(B_KV_HEADS):
        k = [k_ref[:, n * d:(n + 1) * d].astype(BF16)]
        v = [v_ref[:, n * d:(n + 1) * d].astype(BF16)]
        for g in range(B_GROUP):
            sl = slice((n * B_GROUP + g) * d, (n * B_GROUP + g + 1) * d)
            o_ref[:, sl] = _gqa_head(q_ref[:, sl], k, v)
    if emit:
        nk_ref, nv_ref = refs[4 + 2 * n_prev], refs[5 + 2 * n_prev]
        layers = [(prev[2 * i], prev[2 * i + 1]) for i in range(n_prev)] + [(k_ref, v_ref)]
        for jj, (kr, vr) in enumerate(layers):
            for n in range(B_CHUNKS):
                rows = pl.ds(n, SEQ, stride=B_CHUNKS)
                nk_ref[jj, rows, :] = kr[:, n * LANE:(n + 1) * LANE]
                nv_ref[jj, rows, :] = vr[:, n * LANE:(n + 1) * LANE]


def _gqa_prompt(q, k, v, prev_kv=None):
    emit = prev_kv is not None
    prev = [a for pair in (prev_kv or []) for a in pair]
    n_layers = len(prev) // 2 + 1
    kv_spec = pl.BlockSpec((SEQ, B_KV), lambda b: (b, 0))
    att_shape = jax.ShapeDtypeStruct((M_P, D_MODEL), BF16)
    att_spec = pl.BlockSpec((SEQ, D_MODEL), lambda b: (b, 0))
    if emit:
        new_shape = jax.ShapeDtypeStruct((BATCH, n_layers, SEQ * B_CHUNKS, LANE), F32)
        new_spec = pl.BlockSpec((None, n_layers, SEQ * B_CHUNKS, LANE), lambda b: (b, 0, 0, 0))
        out_shape, out_specs = (att_shape, new_shape, new_shape), (att_spec, new_spec, new_spec)
    else:
        out_shape, out_specs = att_shape, att_spec
    return pl.pallas_call(
        functools.partial(_gqa_prompt_kernel, n_prev=len(prev) // 2, emit=emit),
        out_shape=out_shape,
        grid=(BATCH,),
        in_specs=[att_spec, kv_spec, kv_spec] + [kv_spec for _ in prev],
        out_specs=out_specs,
        compiler_params=_params("arbitrary"),
        name="gqa_attn_prompt",
    )(q, k, v, *prev)


def _gqa_latent_kernel(q_ref, k_ref, v_ref, ck_ref, cv_ref, o_ref):
    d = B_HEAD_DIM
    rows = pl.ds(pl.program_id(1), PAST_LEN, stride=B_CHUNKS)
    ks = [ck_ref[rows, :].astype(BF16), k_ref[...]]
    vs = [cv_ref[rows, :].astype(BF16), v_ref[...]]
    for g in range(B_GROUP):
        o_ref[:, g * d:(g + 1) * d] = _gqa_head(q_ref[:, g * d:(g + 1) * d], ks, vs)


def _gqa_latent(qk, v, ck, cv, j, tq=512):
    d = B_HEAD_DIM
    gw = B_GROUP * d
    per = DEC_SEQ // tq
    ctx_spec = pl.BlockSpec((None, None, PAST_LEN * B_CHUNKS, LANE), lambda b, n, t: (b, j, 0, 0))
    return pl.pallas_call(
        _gqa_latent_kernel,
        out_shape=jax.ShapeDtypeStruct((M_S, D_MODEL), BF16),
        grid=(DEC_BATCH, B_KV_HEADS, per),
        in_specs=[
            pl.BlockSpec((tq, gw), lambda b, n, t: (b * per + t, n)),
            pl.BlockSpec((DEC_SEQ, d), lambda b, n, t: (b, D_MODEL // d + n)),
            pl.BlockSpec((DEC_SEQ, d), lambda b, n, t: (b, n)),
            ctx_spec, ctx_spec,
        ],
        out_specs=pl.BlockSpec((tq, gw), lambda b, n, t: (b * per + t, n)),
        compiler_params=_params("arbitrary", "arbitrary", "arbitrary"),
        name="gqa_attn_latent",
    )(qk, qk, v, ck, cv)


def _dft_tables(s):
    def cs(n):
        idx = (np.arange(n)[:, None] * np.arange(n)[None, :]) % n
        ang = 2.0 * np.pi * idx / n
        return np.cos(ang), np.sin(ang)
    cc, sc = cs(C_GROUP_DIM)
    c_s, s_s = cs(s)
    t_chan = np.concatenate([cc, sc], axis=1).astype(np.float32)
    t_pos = np.concatenate([c_s, -s_s], axis=1).astype(np.float32)
    return jnp.asarray(t_chan).astype(BF16), jnp.asarray(t_pos).astype(BF16)


def _dft_group(x, t_chan, t_pos, norm):
    xcs = jnp.dot(x, t_chan, preferred_element_type=F32).astype(BF16)
    stacked = jnp.concatenate([xcs[:, :C_GROUP_DIM], xcs[:, C_GROUP_DIM:]], axis=0)
    y = jnp.dot(t_pos, stacked, preferred_element_type=F32)
    return (y * norm).astype(BF16)


def _dft_kernel(u_ref, tc_ref, tp_ref, o_ref, *, groups, norm):
    tc, tp = tc_ref[...], tp_ref[...]
    for g in range(groups):
        sl = slice(g * C_GROUP_DIM, (g + 1) * C_GROUP_DIM)
        o_ref[:, sl] = _dft_group(u_ref[:, sl], tc, tp, norm)


def _dft(u, s, batch, row0, groups_per_step):
    t_chan, t_pos = _dft_tables(s)
    gw = groups_per_step * C_GROUP_DIM
    rb0 = row0 // s
    return pl.pallas_call(
        functools.partial(_dft_kernel, groups=groups_per_step,
                          norm=1.0 / math.sqrt(s * C_GROUP_DIM)),
        out_shape=jax.ShapeDtypeStruct((batch * s, D_MODEL), BF16),
        grid=(batch, C_GROUPS // groups_per_step),
        in_specs=[
            pl.BlockSpec((s, gw), lambda b, g: (rb0 + b, g)),
            pl.BlockSpec(t_chan.shape, lambda b, g: (0, 0)),
            pl.BlockSpec(t_pos.shape, lambda b, g: (0, 0)),
        ],
        out_specs=pl.BlockSpec((s, gw), lambda b, g: (b, g)),
        compiler_params=_params("arbitrary", "arbitrary"),
        name="dft",
    )(u, t_chan, t_pos)


ROW_GROUPS = 2


def _post_norm(y, g, b):
    mu = jnp.mean(y, axis=-1, keepdims=True)
    yc = y - mu
    var = jnp.mean(yc * yc, axis=-1, keepdims=True)
    return yc * lax.rsqrt(var + LN_EPS) * g + b


def _oproj_kernel(ap_ref, as_ref, w_ref, x_ref, gate_ref, g_ref, b_ref, sc_ref, sh_ref,
                  x1_ref, u_ref):
    is_prompt = pl.program_id(0) * ROW_TILE < M_P
    gate, g, b = gate_ref[...], g_ref[...], b_ref[...]
    sc1, sh = 1.0 + sc_ref[...], sh_ref[...]
    rows = ROW_TILE // ROW_GROUPS
    for r in range(ROW_GROUPS):
        sl = slice(r * rows, (r + 1) * rows)
        a = jnp.where(is_prompt, ap_ref[sl, :], as_ref[sl, :])
        mixed = jnp.dot(a, w_ref[...], preferred_element_type=F32)
        x1 = _post_norm(ALPHA * x_ref[sl, :] + gate * mixed, g, b)
        x1_ref[sl, :] = x1
        u_ref[sl, :] = (x1 * sc1 + sh).astype(BF16)


def _oproj(a_p, a_s, w_bf, w_layer, x, mod, ln_g, ln_b, l):
    tm = ROW_TILE
    row = pl.BlockSpec((tm, D_MODEL), lambda i: (i, 0))
    return pl.pallas_call(
        _oproj_kernel,
        out_shape=(jax.ShapeDtypeStruct((M_ALL, D_MODEL), F32),
                   jax.ShapeDtypeStruct((M_ALL, D_MODEL), BF16)),
        grid=(M_ALL // tm,),
        in_specs=[
            _prompt_rows_spec(tm), _latent_rows_spec(tm),
            pl.BlockSpec((None, D_MODEL, D_MODEL), lambda i: (w_layer, 0, 0),
                         pipeline_mode=pl.Buffered(1)),
            row,
            _mod_spec(l, 2, tm),
            _vec_spec(l, 0), _vec_spec(l, 0),
            _mod_spec(l, 4, tm),
            _mod_spec(l, 3, tm),
        ],
        out_specs=(row, row),
        compiler_params=_params("arbitrary"),
        name="oproj_ln",
    )(a_p, a_s, w_bf, x, mod, ln_g, ln_b, mod, mod)


def _mlp_kernel(u_ref, wu_ref, wd_ref, ff_ref):
    @pl.when(pl.program_id(1) == 0)
    def _():
        ff_ref[...] = jnp.zeros_like(ff_ref)

    wu = wu_ref[...].astype(BF16)
    wd = wd_ref[...].astype(BF16)
    rows = u_ref.shape[0] // ROW_GROUPS
    for r in range(ROW_GROUPS):
        sl = slice(r * rows, (r + 1) * rows)
        h = jnp.dot(u_ref[sl, :], wu, preferred_element_type=F32)
        h = jnp.maximum(h, 0.0)
        h = (h * h).astype(BF16)
        ff_ref[sl, :] += jnp.dot(h, wd, preferred_element_type=F32)


def _mlp(u, w_up, w_down, l, tm=1024, tf=512):
    return pl.pallas_call(
        _mlp_kernel,
        out_shape=jax.ShapeDtypeStruct((M_ALL, D_MODEL), F32),
        grid=(M_ALL // tm, D_FF // tf),
        in_specs=[
            pl.BlockSpec((tm, D_MODEL), lambda i, f: (i, 0), pipeline_mode=pl.Buffered(1)),
            pl.BlockSpec((None, D_MODEL, tf), lambda i, f: (l, 0, f)),
            pl.BlockSpec((None, tf, D_MODEL), lambda i, f: (l, f, 0)),
        ],
        out_specs=pl.BlockSpec((tm, D_MODEL), lambda i, f: (i, 0)),
        compiler_params=_params("arbitrary", "arbitrary"),
        name="mlp",
    )(u, w_up, w_down)


def _close_kernel(ff_ref, x_ref, gate_ref, g_ref, b_ref, sc_ref, sh_ref, x2_ref, un_ref):
    x2 = _post_norm(ALPHA * x_ref[...] + gate_ref[...] * ff_ref[...], g_ref[...], b_ref[...])
    x2_ref[...] = x2
    un_ref[...] = (x2 * (1.0 + sc_ref[...]) + sh_ref[...]).astype(BF16)


def _close(ff, x, mod, ln_g, ln_b, l):
    tm = ROW_TILE
    row = pl.BlockSpec((tm, D_MODEL), lambda i: (i, 0))
    return pl.pallas_call(
        _close_kernel,
        out_shape=(jax.ShapeDtypeStruct((M_ALL, D_MODEL), F32),
                   jax.ShapeDtypeStruct((M_ALL, D_MODEL), BF16)),
        grid=(M_ALL // tm,),
        in_specs=[row, row, _mod_spec(l, 5, tm), _vec_spec(l, 1), _vec_spec(l, 1),
                  _mod_spec(l + 1, 1, tm), _mod_spec(l + 1, 0, tm)],
        out_specs=(row, row),
        compiler_params=_params("arbitrary"),
        name="mlp_close",
    )(ff, x, mod, ln_g, ln_b, mod, mod)


def _final_kernel(ff_ref, x_ref, gate_ref, g_ref, b_ref, yp_ref, ys_ref):
    y = _post_norm(ALPHA * x_ref[...] + gate_ref[...] * ff_ref[...], g_ref[...], b_ref[...])
    is_prompt = pl.program_id(0) * ROW_TILE < M_P

    @pl.when(is_prompt)
    def _():
        yp_ref[...] = y

    @pl.when(jnp.logical_not(is_prompt))
    def _():
        ys_ref[...] = y


def _final(ff, x, mod, ln_g, ln_b, l):
    tm = ROW_TILE
    row = pl.BlockSpec((tm, D_MODEL), lambda i: (i, 0))
    return pl.pallas_call(
        _final_kernel,
        out_shape=(jax.ShapeDtypeStruct((M_P, D_MODEL), F32),
                   jax.ShapeDtypeStruct((M_S, D_MODEL), F32)),
        grid=(M_ALL // tm,),
        in_specs=[row, row, _mod_spec(l, 5, tm), _vec_spec(l, 1), _vec_spec(l, 1)],
        out_specs=(_prompt_rows_spec(tm), _latent_rows_spec(tm)),
        compiler_params=_params("arbitrary"),
        name="mlp_final",
    )(ff, x, mod, ln_g, ln_b)


def _rope_tables():
    n_freq = A_HEAD_DIM // 4
    pos = np.arange(DEC_SEQ)
    row = (pos // GRID_W).astype(np.float32)
    col = (pos % GRID_W).astype(np.float32)
    inv_freq = (ROPE_BASE ** (-np.arange(n_freq, dtype=np.float32) / n_freq)).astype(np.float32)
    ar = row[:, None] * inv_freq
    ac = col[:, None] * inv_freq
    cr, sr, cc, sc = np.cos(ar), np.sin(ar), np.cos(ac), np.sin(ac)
    z = np.zeros_like(sr)
    cos = np.concatenate([cr, cr, cc, cc], axis=1)
    sin_a = np.concatenate([-sr, z, -sc, z], axis=1)
    sin_b = np.concatenate([z, sr, z, sc], axis=1)
    return tuple(jnp.asarray(t.astype(np.float32)) for t in (cos, sin_a, sin_b))


def kernel(x_prompt, x_sample, cache_a_k, cache_a_v, cache_b_k, cache_b_v, c, c_ctx, w_mod, b_mod, ln_g, ln_b, w_up, w_down, a_w_qkv, a_w_o, a_lambda, a_subln_g, b_w_qkv, b_w_o, b_q_norm_g, b_k_norm_g, c_w_f):
    n_a = a_w_qkv.shape[0]
    n_b = b_w_qkv.shape[0]
    cond = jnp.zeros((COND_PAD, D_MODEL), F32).at[0].set(c_ctx).at[1:N_COND].set(c)
    mod = _modulation(cond, w_mod, b_mod).reshape(DEPTH, COND_PAD, N_MOD, 1, D_MODEL)
    ln_g4 = ln_g.reshape(DEPTH, 2, 1, D_MODEL)
    ln_b4 = ln_b.reshape(DEPTH, 2, 1, D_MODEL)
    rope = _rope_tables()
    w_o_bf = (_cast_bf16(a_w_o), _cast_bf16(b_w_o), _cast_bf16(c_w_f))

    ck_a = cache_a_k.reshape(DEC_BATCH, n_a, PAST_LEN * A_CHUNKS, LANE)
    cv_a = _a_v_rows(cache_a_v, DEC_BATCH, PAST_LEN)
    ck_b = cache_b_k.reshape(DEC_BATCH, n_b, PAST_LEN * B_CHUNKS, LANE)
    cv_b = cache_b_v.reshape(DEC_BATCH, n_b, PAST_LEN * B_CHUNKS, LANE)
    subln = a_subln_g.reshape(n_a, 1, 2 * A_HEAD_DIM)
    b_gains = jnp.stack([b_q_norm_g, b_k_norm_g], axis=1).reshape(n_b, 2, 1, B_HEAD_DIM)

    x, u = _embed(x_prompt.reshape(M_P, D_MODEL), x_sample.reshape(M_S, D_MODEL), mod)

    a_kv, b_kv = [], []
    for l in range(DEPTH):
        kind, j = l % N_MIXERS, l // N_MIXERS
        if kind == 0:
            lam_init = 0.8 - 0.6 * math.exp(-0.3 * l)
            q_p = _proj(u, a_w_qkv, j, 0, D_MODEL, 0, M_P, out_dtype=BF16, name="a_q_prompt")
            kv_p = _proj(u, a_w_qkv, j, D_MODEL, 2 * D_MODEL, 0, M_P, out_dtype=F32,
                         name="a_kv_prompt")
            qk_s = _proj(u, a_w_qkv, j, 0, 2 * D_MODEL, M_P, M_S, out_dtype=BF16, rope=rope,
                         name="a_qk_latent")
            v_s = _proj(u, a_w_qkv, j, 2 * D_MODEL, D_MODEL, M_P, M_S, out_dtype=BF16,
                        name="a_v_latent")
            if j + 1 < n_a:
                att_p = _diff_prompt(a_lambda, subln, q_p, kv_p, j, lam_init)
                a_kv.append(kv_p)
            else:
                att_p, new_a_k, new_a_v = _diff_prompt(a_lambda, subln, q_p, kv_p, j, lam_init,
                                                       prev_kv=a_kv)
            att_s = _diff_latent(a_lambda, subln, qk_s, v_s, ck_a, cv_a, j, lam_init)
            w_o = w_o_bf[0]
        elif kind == 1:
            gj = b_gains[j]
            q_p = _proj(u, b_w_qkv, j, 0, D_MODEL, 0, M_P, out_dtype=BF16, gains=gj,
                        gain_split=D_MODEL, name="b_q_prompt")
            k_p = _proj(u, b_w_qkv, j, D_MODEL, B_KV, 0, M_P, out_dtype=F32, tn=B_KV,
                        gains=gj[1:], gain_split=B_KV, name="b_k_prompt")
            v_p = _proj(u, b_w_qkv, j, D_MODEL + B_KV, B_KV, 0, M_P, out_dtype=F32, tn=B_KV,
                        name="b_v_prompt")
            qk_s = _proj(u, b_w_qkv, j, 0, D_MODEL + B_KV, M_P, M_S, out_dtype=BF16, tn=B_KV,
                         gains=gj, gain_split=D_MODEL, rope=rope, name="b_qk_latent")
            v_s = _proj(u, b_w_qkv, j, D_MODEL + B_KV, B_KV, M_P, M_S, out_dtype=BF16, tn=B_KV,
                        name="b_v_latent")
            if j + 1 < n_b:
                att_p = _gqa_prompt(q_p, k_p, v_p)
                b_kv.append((k_p, v_p))
            else:
                att_p, new_b_k, new_b_v = _gqa_prompt(q_p, k_p, v_p, prev_kv=b_kv)
            att_s = _gqa_latent(qk_s, v_s, ck_b, cv_b, j)
            w_o = w_o_bf[1]
        else:
            att_p = _dft(u, SEQ, BATCH, 0, C_GROUPS)
            att_s = _dft(u, DEC_SEQ, DEC_BATCH, M_P, 1)
            w_o = w_o_bf[2]
        x, u = _oproj(att_p, att_s, w_o, j, x, mod, ln_g4, ln_b4, l)
        ff = _mlp(u, w_up, w_down, l)
        if l + 1 < DEPTH:
            x, u = _close(ff, x, mod, ln_g4, ln_b4, l)
        else:
            y_p, y_s = _final(ff, x, mod, ln_g4, ln_b4, l)

    return (y_p.reshape(BATCH, SEQ, D_MODEL), y_s.reshape(DEC_BATCH, DEC_SEQ, D_MODEL),
            new_a_k.reshape(BATCH, n_a, SEQ, A_HEADS, 2, A_HEAD_DIM),
            _a_v_from_rows(new_a_v, BATCH, SEQ),
            new_b_k.reshape(BATCH, n_b, SEQ, B_KV_HEADS, B_HEAD_DIM),
            new_b_v.reshape(BATCH, n_b, SEQ, B_KV_HEADS, B_HEAD_DIM))
```

```python
import functools
import math

import jax
import jax.numpy as jnp
import numpy as np
from jax import lax
from jax.experimental import pallas as pl
from jax.experimental.pallas import tpu as pltpu

D_MODEL = 2048
BATCH = 16
SEQ = 256
DEPTH = 4
DEC_BATCH = 2
DEC_SEQ = 1024
PAST_LEN = 512
GRID_W = 64
N_MIXERS = 3
A_HEAD_DIM = 128
A_HEADS = D_MODEL // (2 * A_HEAD_DIM)
B_HEAD_DIM = 128
B_HEADS = D_MODEL // B_HEAD_DIM
B_KV_HEADS = B_HEADS // 4
B_GROUP = B_HEADS // B_KV_HEADS
B_KV = B_KV_HEADS * B_HEAD_DIM
C_GROUPS = 4
C_GROUP_DIM = D_MODEL // C_GROUPS
D_FF = 4 * D_MODEL
ROPE_BASE = 10000.0
ALPHA = (2 * DEPTH) ** 0.25
LN_EPS = 1e-5
RMS_EPS = 1e-6
N_MOD = 6
LOG2E = 1.4426950408889634

M_P = BATCH * SEQ
M_S = DEC_BATCH * DEC_SEQ
M_ALL = M_P + M_S
N_COND = 1 + DEC_BATCH
COND_PAD = 8

LANE = 128
VMEM_LIMIT = 58 * 1024 * 1024
ROW_TILE = 512

BF16 = jnp.bfloat16
F32 = jnp.float32


def _params(*sem, vmem=VMEM_LIMIT):
    return pltpu.CompilerParams(dimension_semantics=sem, vmem_limit_bytes=vmem)


def _cond_row(i, tm):
    start = i * tm
    return jnp.where(start < M_P, 0, 1 + (start - M_P) // DEC_SEQ)


def _mod_spec(l, which, tm):
    def idx(*g):
        return (l, _cond_row(g[0], tm), which, 0, 0)
    return pl.BlockSpec((None, None, None, 1, D_MODEL), idx)


def _vec_spec(l, which):
    return pl.BlockSpec((None, None, 1, D_MODEL), lambda *g: (l, which, 0, 0))


def _prompt_rows_spec(tm):
    last = M_P // tm - 1
    return pl.BlockSpec((tm, D_MODEL), lambda i: (jnp.minimum(i, last), 0))


def _latent_rows_spec(tm):
    first = M_P // tm
    return pl.BlockSpec((tm, D_MODEL), lambda i: (jnp.maximum(i - first, 0), 0))


def _mod_kernel(c_ref, w_ref, b_ref, o_ref):
    c = c_ref[...]
    s = c * (1.0 / (1.0 + jnp.exp(-c)))
    o_ref[...] = jnp.dot(s.astype(BF16), w_ref[...].astype(BF16),
                         preferred_element_type=F32) + b_ref[...]


def _modulation(cond, w_mod, b_mod):
    tn = 1024
    n_out = N_MOD * D_MODEL
    return pl.pallas_call(
        _mod_kernel,
        out_shape=jax.ShapeDtypeStruct((DEPTH, COND_PAD, n_out), F32),
        grid=(DEPTH, n_out // tn),
        in_specs=[
            pl.BlockSpec((COND_PAD, D_MODEL), lambda l, n: (0, 0)),
            pl.BlockSpec((None, D_MODEL, tn), lambda l, n: (l, 0, n)),
            pl.BlockSpec((None, 1, tn), lambda l, n: (l, 0, n)),
        ],
        out_specs=pl.BlockSpec((None, COND_PAD, tn), lambda l, n: (l, 0, n)),
        compiler_params=_params("arbitrary", "arbitrary"),
        name="modulation",
    )(cond, w_mod, b_mod.reshape(DEPTH, 1, n_out))


def _cast_kernel(w_ref, o_ref):
    o_ref[...] = w_ref[...].astype(BF16)


def _cast_bf16(w, rows=512):
    n_l, k, n = w.shape
    spec = pl.BlockSpec((None, rows, n), lambda l, r: (l, r, 0))
    return pl.pallas_call(
        _cast_kernel,
        out_shape=jax.ShapeDtypeStruct(w.shape, BF16),
        grid=(n_l, k // rows),
        in_specs=[spec],
        out_specs=spec,
        compiler_params=_params("arbitrary", "arbitrary"),
        name="cast_weight",
    )(w)


def _embed_kernel(xp_ref, xs_ref, sc_ref, sh_ref, u_ref):
    x = jnp.where(pl.program_id(0) * ROW_TILE < M_P, xp_ref[...], xs_ref[...])
    u_ref[...] = (x * (1.0 + sc_ref[...]) + sh_ref[...]).astype(BF16)


def _embed(xp, xs, mod):
    tm = ROW_TILE
    return pl.pallas_call(
        _embed_kernel,
        out_shape=jax.ShapeDtypeStruct((M_ALL, D_MODEL), BF16),
        grid=(M_ALL // tm,),
        in_specs=[_prompt_rows_spec(tm), _latent_rows_spec(tm),
                  _mod_spec(0, 1, tm), _mod_spec(0, 0, tm)],
        out_specs=pl.BlockSpec((tm, D_MODEL), lambda i: (i, 0)),
        compiler_params=_params("arbitrary"),
        name="embed",
    )(xp, xs, mod, mod)


def _rms_chunk(x, g):
    ms = jnp.mean(x * x, axis=-1, keepdims=True)
    return x * lax.rsqrt(ms + RMS_EPS) * g


def _rope_chunk(x, cos, sin_a, sin_b):
    return x * cos + pltpu.roll(x, LANE - 32, 1) * sin_a + pltpu.roll(x, 32, 1) * sin_b


def _proj_kernel(*refs, tn, rms, rope):
    it = iter(refs)
    u_ref, w_ref = next(it), next(it)
    g_ref = next(it) if rms else None
    tabs = (next(it), next(it), next(it)) if rope else None
    o_ref, wbf = next(it), next(it)

    @pl.when(pl.program_id(1) == 0)
    def _():
        wbf[...] = w_ref[...].astype(BF16)

    acc = jnp.dot(u_ref[...], wbf[...], preferred_element_type=F32)
    if rms or rope:
        g = g_ref[...] if rms else None
        tab = tuple(t[...] for t in tabs) if rope else None
        for c in range(tn // LANE):
            x = acc[:, c * LANE:(c + 1) * LANE]
            if rms:
                x = _rms_chunk(x, g)
            if rope:
                x = _rope_chunk(x, *tab)
            o_ref[:, c * LANE:(c + 1) * LANE] = x.astype(o_ref.dtype)
    else:
        o_ref[...] = acc.astype(o_ref.dtype)


def _proj(u, w, w_layer, col0, ncols, m0, m_rows, *, out_dtype, tn=1024, tm=1024,
          gains=None, gain_split=None, rope=None, name="proj"):
    k = u.shape[1]
    assert col0 % tn == 0 and ncols % tn == 0 and m0 % tm == 0 and m_rows % tm == 0
    n_t, m_t, mt0, nt0 = ncols // tn, m_rows // tm, m0 // tm, col0 // tn
    ins = [u, w]
    in_specs = [
        pl.BlockSpec((tm, k), lambda n, m: (mt0 + m, 0)),
        pl.BlockSpec((None, k, tn), lambda n, m: (w_layer, 0, nt0 + n)),
    ]
    if gains is not None:
        ins.append(gains)
        in_specs.append(pl.BlockSpec(
            (None, 1, LANE), lambda n, m: (jnp.where(n * tn < gain_split, 0, 1), 0, 0)))
    if rope is not None:
        assert m0 >= M_P
        per = DEC_SEQ // tm
        for t in rope:
            ins.append(t)
            in_specs.append(pl.BlockSpec((tm, LANE), lambda n, m: ((mt0 + m) % per, 0)))
    kern = functools.partial(_proj_kernel, tn=tn, rms=gains is not None, rope=rope is not None)
    return pl.pallas_call(
        kern,
        out_shape=jax.ShapeDtypeStruct((m_rows, ncols), out_dtype),
        grid=(n_t, m_t),
        in_specs=in_specs,
        out_specs=pl.BlockSpec((tm, tn), lambda n, m: (m, n)),
        scratch_shapes=[pltpu.VMEM((k, tn), BF16)],
        compiler_params=_params("arbitrary", "arbitrary"),
        name=name,
    )(*ins)


def _qkt(q, k):
    return lax.dot_general(q, k, (((1,), (1,)), ((), ())), preferred_element_type=F32)


def _softmax_parts(scores, scale):
    m = scores[0].max(axis=-1, keepdims=True)
    for s in scores[1:]:
        m = jnp.maximum(m, s.max(axis=-1, keepdims=True))
    c = scale * LOG2E
    es = [jnp.exp2((s - m) * c) for s in scores]
    tot = es[0].sum(axis=-1, keepdims=True)
    for e in es[1:]:
        tot = tot + e.sum(axis=-1, keepdims=True)
    return es, 1.0 / tot


def _diff_lambda(lp_ref, lam_init):
    lp = lp_ref[...]
    s1 = jnp.sum(lp[0:1] * lp[1:2], axis=-1, keepdims=True)
    s2 = jnp.sum(lp[2:3] * lp[3:4], axis=-1, keepdims=True)
    return jnp.exp(s1) - jnp.exp(s2) + lam_init


def _diff_head(q, ks, vs, lam, gain):
    scale = A_HEAD_DIM ** -0.5
    q0, q1 = q[:, :A_HEAD_DIM], q[:, A_HEAD_DIM:]
    e0, r0 = _softmax_parts([_qkt(q0, k[:, :A_HEAD_DIM]) for k in ks], scale)
    e1, r1 = _softmax_parts([_qkt(q1, k[:, A_HEAD_DIM:]) for k in ks], scale)
    r1 = lam * r1
    o = None
    for a0, a1, v in zip(e0, e1, vs):
        a = (a0 * r0 - a1 * r1).astype(BF16)
        part = jnp.dot(a, v, preferred_element_type=F32)
        o = part if o is None else o + part
    ms = jnp.mean(o * o, axis=-1, keepdims=True)
    return (o * lax.rsqrt(ms + RMS_EPS) * gain).astype(BF16)


def _gqa_head(q, ks, vs):
    es, r = _softmax_parts([_qkt(q, k) for k in ks], B_HEAD_DIM ** -0.5)
    o = None
    for e, v in zip(es, vs):
        part = jnp.dot((e * r).astype(BF16), v, preferred_element_type=F32)
        o = part if o is None else o + part
    return o.astype(BF16)


A_CHUNKS = D_MODEL // LANE
B_CHUNKS = B_KV // LANE


def _a_v_row(c):
    return (c % 2) * A_HEADS + c // 2


def _a_v_rows(v, batch, seq):
    n_l = v.shape[1]
    v = v.reshape(batch, n_l, seq, A_HEADS, 2, LANE).transpose(0, 1, 2, 4, 3, 5)
    return v.reshape(batch, n_l, seq * A_CHUNKS, LANE)


def _a_v_from_rows(r, batch, seq):
    n_l = r.shape[1]
    v = r.reshape(batch, n_l, seq, 2, A_HEADS, LANE).transpose(0, 1, 2, 4, 3, 5)
    return v.reshape(batch, n_l, seq, A_HEADS, 2 * LANE)


def _diff_prompt_kernel(*refs, lam_init, n_prev, emit):
    lp_ref, g_ref, q_ref, k_ref, v_ref = refs[:5]
    prev = refs[5:5 + n_prev]
    o_ref = refs[5 + n_prev]
    lam = _diff_lambda(lp_ref, lam_init)
    gain = g_ref[...] * (1.0 - lam_init)
    w = 2 * A_HEAD_DIM
    for h in range(A_HEADS):
        sl = slice(h * w, (h + 1) * w)
        o_ref[:, sl] = _diff_head(q_ref[:, sl], [k_ref[:, sl].astype(BF16)],
                                  [v_ref[:, sl].astype(BF16)], lam, gain)
    if emit:
        nk_ref, nv_ref = refs[6 + n_prev], refs[7 + n_prev]
        layers = [(p, 0, p, D_MODEL) for p in prev] + [(k_ref, 0, v_ref, 0)]
        for jj, (kr, k0, vr, v0) in enumerate(layers):
            for c in range(A_CHUNKS):
                nk_ref[jj, pl.ds(c, SEQ, stride=A_CHUNKS), :] = kr[:, k0 + c * LANE:k0 + (c + 1) * LANE]
                nv_ref[jj, pl.ds(_a_v_row(c), SEQ, stride=A_CHUNKS), :] = (
                    vr[:, v0 + c * LANE:v0 + (c + 1) * LANE])


def _diff_prompt(lp, g, q, kv, j, lam_init, prev_kv=None):
    emit = prev_kv is not None
    prev_kv = list(prev_kv or [])
    n_layers = len(prev_kv) + 1
    att_shape = jax.ShapeDtypeStruct((M_P, D_MODEL), BF16)
    att_spec = pl.BlockSpec((SEQ, D_MODEL), lambda b: (b, 0))
    if emit:
        new_shape = jax.ShapeDtypeStruct((BATCH, n_layers, SEQ * A_CHUNKS, LANE), F32)
        new_spec = pl.BlockSpec((None, n_layers, SEQ * A_CHUNKS, LANE), lambda b: (b, 0, 0, 0))
        out_shape, out_specs = (att_shape, new_shape, new_shape), (att_spec, new_spec, new_spec)
    else:
        out_shape, out_specs = att_shape, att_spec
    return pl.pallas_call(
        functools.partial(_diff_prompt_kernel, lam_init=lam_init, n_prev=len(prev_kv), emit=emit),
        out_shape=out_shape,
        grid=(BATCH,),
        in_specs=[
            pl.BlockSpec((None, 4, A_HEAD_DIM), lambda b: (j, 0, 0)),
            pl.BlockSpec((None, 1, 2 * A_HEAD_DIM), lambda b: (j, 0, 0)),
            pl.BlockSpec((SEQ, D_MODEL), lambda b: (b, 0)),
            pl.BlockSpec((SEQ, D_MODEL), lambda b: (b, 0)),
            pl.BlockSpec((SEQ, D_MODEL), lambda b: (b, 1)),
        ] + [pl.BlockSpec((SEQ, 2 * D_MODEL), lambda b: (b, 0)) for _ in prev_kv],
        out_specs=out_specs,
        compiler_params=_params("arbitrary"),
        name="diff_attn_prompt",
    )(lp, g, q, kv, kv, *prev_kv)


def _diff_latent_kernel(lp_ref, g_ref, q_ref, k_ref, v_ref, ck_ref, cv_ref, o_ref, *, lam_init):
    lam = _diff_lambda(lp_ref, lam_init)
    gain = g_ref[...] * (1.0 - lam_init)
    h = pl.program_id(1)

    def cached(ref, r0, r1):
        halves = [ref[pl.ds(r, PAST_LEN, stride=A_CHUNKS), :] for r in (r0, r1)]
        return jnp.concatenate(halves, axis=1).astype(BF16)

    ck = cached(ck_ref, 2 * h, 2 * h + 1)
    cv = cached(cv_ref, h, A_HEADS + h)
    o_ref[...] = _diff_head(q_ref[...], [ck, k_ref[...]], [cv, v_ref[...]], lam, gain)


def _diff_latent(lp, g, qk, v, ck, cv, j, lam_init, tq=512):
    w = 2 * A_HEAD_DIM
    per = DEC_SEQ // tq
    ctx_spec = pl.BlockSpec((None, None, PAST_LEN * A_CHUNKS, LANE), lambda b, h, t: (b, j, 0, 0))
    return pl.pallas_call(
        functools.partial(_diff_latent_kernel, lam_init=lam_init),
        out_shape=jax.ShapeDtypeStruct((M_S, D_MODEL), BF16),
        grid=(DEC_BATCH, A_HEADS, per),
        in_specs=[
            pl.BlockSpec((None, 4, A_HEAD_DIM), lambda b, h, t: (j, 0, 0)),
            pl.BlockSpec((None, 1, w), lambda b, h, t: (j, 0, 0)),
            pl.BlockSpec((tq, w), lambda b, h, t: (b * per + t, h)),
            pl.BlockSpec((DEC_SEQ, w), lambda b, h, t: (b, A_HEADS + h)),
            pl.BlockSpec((DEC_SEQ, w), lambda b, h, t: (b, h)),
            ctx_spec, ctx_spec,
        ],
        out_specs=pl.BlockSpec((tq, w), lambda b, h, t: (b * per + t, h)),
        compiler_params=_params("arbitrary", "arbitrary", "arbitrary"),
        name="diff_attn_latent",
    )(lp, g, qk, qk, v, ck, cv)


def _gqa_prompt_kernel(*refs, n_prev, emit):
    q_ref, k_ref, v_ref = refs[:3]
    prev = refs[3:3 + 2 * n_prev]
    o_ref = refs[3 + 2 * n_prev]
    d = B_HEAD_DIM
    for n in range(B_KV_HEADS):
        k = [k_ref[:, n * d:(n + 1) * d].astype(BF16)]
        v = [v_ref[:, n * d:(n + 1) * d].astype(BF16)]
        for g in range(B_GROUP):
            sl = slice((n * B_GROUP + g) * d, (n * B_GROUP + g + 1) * d)
            o_ref[:, sl] = _gqa_head(q_ref[:, sl], k, v)
    if emit:
        nk_ref, nv_ref = refs[4 + 2 * n_prev], refs[5 + 2 * n_prev]
        layers = [(prev[2 * i], prev[2 * i + 1]) for i in range(n_prev)] + [(k_ref, v_ref)]
        for jj, (kr, vr) in enumerate(layers):
            for n in range(B_CHUNKS):
                rows = pl.ds(n, SEQ, stride=B_CHUNKS)
                nk_ref[jj, rows, :] = kr[:, n * LANE:(n + 1) * LANE]
                nv_ref[jj, rows, :] = vr[:, n * LANE:(n + 1) * LANE]


def _gqa_prompt(q, k, v, prev_kv=None):
    emit = prev_kv is not None
    prev = [a for pair in (prev_kv or []) for a in pair]
    n_layers = len(prev) // 2 + 1
    kv_spec = pl.BlockSpec((SEQ, B_KV), lambda b: (b, 0))
    att_shape = jax.ShapeDtypeStruct((M_P, D_MODEL), BF16)
    att_spec = pl.BlockSpec((SEQ, D_MODEL), lambda b: (b, 0))
    if emit:
        new_shape = jax.ShapeDtypeStruct((BATCH, n_layers, SEQ * B_CHUNKS, LANE), F32)
        new_spec = pl.BlockSpec((None, n_layers, SEQ * B_CHUNKS, LANE), lambda b: (b, 0, 0, 0))
        out_shape, out_specs = (att_shape, new_shape, new_shape), (att_spec, new_spec, new_spec)
    else:
        out_shape, out_specs = att_shape, att_spec
    return pl.pallas_call(
        functools.partial(_gqa_prompt_kernel, n_prev=len(prev) // 2, emit=emit),
        out_shape=out_shape,
        grid=(BATCH,),
        in_specs=[att_spec, kv_spec, kv_spec] + [kv_spec for _ in prev],
        out_specs=out_specs,
        compiler_params=_params("arbitrary"),
        name="gqa_attn_prompt",
    )(q, k, v, *prev)


def _gqa_latent_kernel(q_ref, k_ref, v_ref, ck_ref, cv_ref, o_ref):
    d = B_HEAD_DIM
    rows = pl.ds(pl.program_id(1), PAST_LEN, stride=B_CHUNKS)
    ks = [ck_ref[rows, :].astype(BF16), k_ref[...]]
    vs = [cv_ref[rows, :].astype(BF16), v_ref[...]]
    for g in range(B_GROUP):
        o_ref[:, g * d:(g + 1) * d] = _gqa_head(q_ref[:, g * d:(g + 1) * d], ks, vs)


def _gqa_latent(qk, v, ck, cv, j, tq=512):
    d = B_HEAD_DIM
    gw = B_GROUP * d
    per = DEC_SEQ // tq
    ctx_spec = pl.BlockSpec((None, None, PAST_LEN * B_CHUNKS, LANE), lambda b, n, t: (b, j, 0, 0))
    return pl.pallas_call(
        _gqa_latent_kernel,
        out_shape=jax.ShapeDtypeStruct((M_S, D_MODEL), BF16),
        grid=(DEC_BATCH, B_KV_HEADS, per),
        in_specs=[
            pl.BlockSpec((tq, gw), lambda b, n, t: (b * per + t, n)),
            pl.BlockSpec((DEC_SEQ, d), lambda b, n, t: (b, D_MODEL // d + n)),
            pl.BlockSpec((DEC_SEQ, d), lambda b, n, t: (b, n)),
            ctx_spec, ctx_spec,
        ],
        out_specs=pl.BlockSpec((tq, gw), lambda b, n, t: (b * per + t, n)),
        compiler_params=_params("arbitrary", "arbitrary", "arbitrary"),
        name="gqa_attn_latent",
    )(qk, qk, v, ck, cv)


def _dft_tables(s):
    def cs(n):
        idx = (np.arange(n)[:, None] * np.arange(n)[None, :]) % n
        ang = 2.0 * np.pi * idx / n
        return np.cos(ang), np.sin(ang)
    cc, sc = cs(C_GROUP_DIM)
    c_s, s_s = cs(s)
    t_chan = np.concatenate([cc, sc], axis=1).astype(np.float32)
    t_pos = np.concatenate([c_s, -s_s], axis=1).astype(np.float32)
    return jnp.asarray(t_chan).astype(BF16), jnp.asarray(t_pos).astype(BF16)


def _dft_group(x, t_chan, t_pos, norm):
    xcs = jnp.dot(x, t_chan, preferred_element_type=F32).astype(BF16)
    stacked = jnp.concatenate([xcs[:, :C_GROUP_DIM], xcs[:, C_GROUP_DIM:]], axis=0)
    y = jnp.dot(t_pos, stacked, preferred_element_type=F32)
    return (y * norm).astype(BF16)


def _dft_kernel(u_ref, tc_ref, tp_ref, o_ref, *, groups, norm):
    tc, tp = tc_ref[...], tp_ref[...]
    for g in range(groups):
        sl = slice(g * C_GROUP_DIM, (g + 1) * C_GROUP_DIM)
        o_ref[:, sl] = _dft_group(u_ref[:, sl], tc, tp, norm)


def _dft(u, s, batch, row0, groups_per_step):
    t_chan, t_pos = _dft_tables(s)
    gw = groups_per_step * C_GROUP_DIM
    rb0 = row0 // s
    return pl.pallas_call(
        functools.partial(_dft_kernel, groups=groups_per_step,
                          norm=1.0 / math.sqrt(s * C_GROUP_DIM)),
        out_shape=jax.ShapeDtypeStruct((batch * s, D_MODEL), BF16),
        grid=(batch, C_GROUPS // groups_per_step),
        in_specs=[
            pl.BlockSpec((s, gw), lambda b, g: (rb0 + b, g)),
            pl.BlockSpec(t_chan.shape, lambda b, g: (0, 0)),
            pl.BlockSpec(t_pos.shape, lambda b, g: (0, 0)),
        ],
        out_specs=pl.BlockSpec((s, gw), lambda b, g: (b, g)),
        compiler_params=_params("arbitrary", "arbitrary"),
        name="dft",
    )(u, t_chan, t_pos)


ROW_GROUPS = 2


def _post_norm(y, g, b):
    mu = jnp.mean(y, axis=-1, keepdims=True)
    yc = y - mu
    var = jnp.mean(yc * yc, axis=-1, keepdims=True)
    return yc * lax.rsqrt(var + LN_EPS) * g + b


def _oproj_kernel(*refs, split_x):
    ap_ref, as_ref, w_ref = refs[:3]
    x_refs = refs[3:5] if split_x else refs[3:4]
    gate_ref, g_ref, b_ref, sc_ref, sh_ref, x1_ref, u_ref = refs[3 + len(x_refs):]
    is_prompt = pl.program_id(0) * ROW_TILE < M_P
    gate, g, b = gate_ref[...], g_ref[...], b_ref[...]
    sc1, sh = 1.0 + sc_ref[...], sh_ref[...]
    rows = ROW_TILE // ROW_GROUPS
    for r in range(ROW_GROUPS):
        sl = slice(r * rows, (r + 1) * rows)
        a = jnp.where(is_prompt, ap_ref[sl, :], as_ref[sl, :])
        mixed = jnp.dot(a, w_ref[...], preferred_element_type=F32)
        if split_x:
            x = jnp.where(is_prompt, x_refs[0][sl, :], x_refs[1][sl, :])
        else:
            x = x_refs[0][sl, :]
        x1 = _post_norm(ALPHA * x + gate * mixed, g, b)
        x1_ref[sl, :] = x1
        u_ref[sl, :] = (x1 * sc1 + sh).astype(BF16)


def _oproj(a_p, a_s, w_bf, w_layer, x, mod, ln_g, ln_b, l):
    tm = ROW_TILE
    row = pl.BlockSpec((tm, D_MODEL), lambda i: (i, 0))
    split_x = isinstance(x, tuple)
    xs = list(x) if split_x else [x]
    x_specs = [_prompt_rows_spec(tm), _latent_rows_spec(tm)] if split_x else [row]
    return pl.pallas_call(
        functools.partial(_oproj_kernel, split_x=split_x),
        out_shape=(jax.ShapeDtypeStruct((M_ALL, D_MODEL), F32),
                   jax.ShapeDtypeStruct((M_ALL, D_MODEL), BF16)),
        grid=(M_ALL // tm,),
        in_specs=[
            _prompt_rows_spec(tm), _latent_rows_spec(tm),
            pl.BlockSpec((None, D_MODEL, D_MODEL), lambda i: (w_layer, 0, 0),
                         pipeline_mode=pl.Buffered(1)),
            *x_specs,
            _mod_spec(l, 2, tm),
            _vec_spec(l, 0), _vec_spec(l, 0),
            _mod_spec(l, 4, tm),
            _mod_spec(l, 3, tm),
        ],
        out_specs=(row, row),
        compiler_params=_params("arbitrary"),
        name="oproj_ln",
    )(a_p, a_s, w_bf, *xs, mod, ln_g, ln_b, mod, mod)


CLOSE_ROWS = 256


def _mlp_kernel(*refs, n_ff, last):
    u_ref, wu_ref, wd_ref, x_ref, gate_ref, g_ref, b_ref = refs[:7]
    acc = refs[-1]
    i, f = pl.program_id(0), pl.program_id(1)

    @pl.when(f == 0)
    def _():
        acc[...] = jnp.zeros_like(acc)

    @pl.when(f < n_ff)
    def _():
        wu = wu_ref[...].astype(BF16)
        wd = wd_ref[...].astype(BF16)
        rows = u_ref.shape[0] // ROW_GROUPS
        for r in range(ROW_GROUPS):
            sl = slice(r * rows, (r + 1) * rows)
            h = jnp.dot(u_ref[sl, :], wu, preferred_element_type=F32)
            h = jnp.maximum(h, 0.0)
            h = (h * h).astype(BF16)
            acc[sl, :] += jnp.dot(h, wd, preferred_element_type=F32)

    def closed():
        rows = pl.ds(pl.multiple_of((f - n_ff) * CLOSE_ROWS, CLOSE_ROWS), CLOSE_ROWS)
        return _post_norm(ALPHA * x_ref[...] + gate_ref[...] * acc[rows, :], g_ref[...], b_ref[...])

    closing = f >= n_ff
    if last:
        yp_ref, ys_ref = refs[7:9]
        is_prompt = i * u_ref.shape[0] < M_P

        @pl.when(jnp.logical_and(closing, is_prompt))
        def _():
            yp_ref[...] = closed()

        @pl.when(jnp.logical_and(closing, jnp.logical_not(is_prompt)))
        def _():
            ys_ref[...] = closed()
    else:
        sc_ref, sh_ref, x2_ref, un_ref = refs[7:11]

        @pl.when(closing)
        def _():
            x2 = closed()
            x2_ref[...] = x2
            un_ref[...] = (x2 * (1.0 + sc_ref[...]) + sh_ref[...]).astype(BF16)


def _mlp(u, w_up, w_down, x, mod, ln_g, ln_b, l, tm=1024, tf=512):
    last = l + 1 == DEPTH
    n_ff, n_close = D_FF // tf, tm // CLOSE_ROWS

    def close_block(i, f):
        return i * n_close + jnp.clip(f - n_ff, 0, n_close - 1)

    def ff_block(f):
        return jnp.minimum(f, n_ff - 1)

    row = pl.BlockSpec((CLOSE_ROWS, D_MODEL), lambda i, f: (close_block(i, f), 0))
    ins = [u, w_up, w_down, x, mod, ln_g, ln_b]
    in_specs = [
        pl.BlockSpec((tm, D_MODEL), lambda i, f: (i, 0)),
        pl.BlockSpec((None, D_MODEL, tf), lambda i, f: (l, 0, ff_block(f))),
        pl.BlockSpec((None, tf, D_MODEL), lambda i, f: (l, ff_block(f), 0)),
        row, _mod_spec(l, 5, tm), _vec_spec(l, 1), _vec_spec(l, 1),
    ]
    if last:
        n_p = M_P // CLOSE_ROWS
        out_shape = (jax.ShapeDtypeStruct((M_P, D_MODEL), F32),
                     jax.ShapeDtypeStruct((M_S, D_MODEL), F32))
        out_specs = (
            pl.BlockSpec((CLOSE_ROWS, D_MODEL),
                         lambda i, f: (jnp.minimum(close_block(i, f), n_p - 1), 0)),
            pl.BlockSpec((CLOSE_ROWS, D_MODEL),
                         lambda i, f: (jnp.maximum(close_block(i, f) - n_p, 0), 0)),
        )
    else:
        ins += [mod, mod]
        in_specs += [_mod_spec(l + 1, 1, tm), _mod_spec(l + 1, 0, tm)]
        out_shape = (jax.ShapeDtypeStruct((M_ALL, D_MODEL), F32),
                     jax.ShapeDtypeStruct((M_ALL, D_MODEL), BF16))
        out_specs = (row, row)
    return pl.pallas_call(
        functools.partial(_mlp_kernel, n_ff=n_ff, last=last),
        out_shape=out_shape,
        grid=(M_ALL // tm, n_ff + n_close),
        in_specs=in_specs,
        out_specs=out_specs,
        scratch_shapes=[pltpu.VMEM((tm, D_MODEL), F32)],
        compiler_params=_params("arbitrary", "arbitrary"),
        name="mlp",
    )(*ins)


def _rope_tables():
    n_freq = A_HEAD_DIM // 4
    pos = np.arange(DEC_SEQ)
    row = (pos // GRID_W).astype(np.float32)
    col = (pos % GRID_W).astype(np.float32)
    inv_freq = (ROPE_BASE ** (-np.arange(n_freq, dtype=np.float32) / n_freq)).astype(np.float32)
    ar = row[:, None] * inv_freq
    ac = col[:, None] * inv_freq
    cr, sr, cc, sc = np.cos(ar), np.sin(ar), np.cos(ac), np.sin(ac)
    z = np.zeros_like(sr)
    cos = np.concatenate([cr, cr, cc, cc], axis=1)
    sin_a = np.concatenate([-sr, z, -sc, z], axis=1)
    sin_b = np.concatenate([z, sr, z, sc], axis=1)
    return tuple(jnp.asarray(t.astype(np.float32)) for t in (cos, sin_a, sin_b))


def kernel(x_prompt, x_sample, cache_a_k, cache_a_v, cache_b_k, cache_b_v, c, c_ctx, w_mod, b_mod, ln_g, ln_b, w_up, w_down, a_w_qkv, a_w_o, a_lambda, a_subln_g, b_w_qkv, b_w_o, b_q_norm_g, b_k_norm_g, c_w_f):
    n_a = a_w_qkv.shape[0]
    n_b = b_w_qkv.shape[0]
    cond = jnp.zeros((COND_PAD, D_MODEL), F32).at[0].set(c_ctx).at[1:N_COND].set(c)
    mod = _modulation(cond, w_mod, b_mod).reshape(DEPTH, COND_PAD, N_MOD, 1, D_MODEL)
    ln_g4 = ln_g.reshape(DEPTH, 2, 1, D_MODEL)
    ln_b4 = ln_b.reshape(DEPTH, 2, 1, D_MODEL)
    rope = _rope_tables()
    w_o_bf = (_cast_bf16(a_w_o), _cast_bf16(b_w_o), _cast_bf16(c_w_f))

    ck_a = cache_a_k.reshape(DEC_BATCH, n_a, PAST_LEN * A_CHUNKS, LANE)
    cv_a = _a_v_rows(cache_a_v, DEC_BATCH, PAST_LEN)
    ck_b = cache_b_k.reshape(DEC_BATCH, n_b, PAST_LEN * B_CHUNKS, LANE)
    cv_b = cache_b_v.reshape(DEC_BATCH, n_b, PAST_LEN * B_CHUNKS, LANE)
    subln = a_subln_g.reshape(n_a, 1, 2 * A_HEAD_DIM)
    b_gains = jnp.stack([b_q_norm_g, b_k_norm_g], axis=1).reshape(n_b, 2, 1, B_HEAD_DIM)

    x = (x_prompt.reshape(M_P, D_MODEL), x_sample.reshape(M_S, D_MODEL))
    u = _embed(*x, mod)

    a_kv, b_kv = [], []
    for l in range(DEPTH):
        kind, j = l % N_MIXERS, l // N_MIXERS
        if kind == 0:
            lam_init = 0.8 - 0.6 * math.exp(-0.3 * l)
            q_p = _proj(u, a_w_qkv, j, 0, D_MODEL, 0, M_P, out_dtype=BF16, name="a_q_prompt")
            kv_p = _proj(u, a_w_qkv, j, D_MODEL, 2 * D_MODEL, 0, M_P, out_dtype=F32,
                         name="a_kv_prompt")
            qk_s = _proj(u, a_w_qkv, j, 0, 2 * D_MODEL, M_P, M_S, out_dtype=BF16, rope=rope,
                         name="a_qk_latent")
            v_s = _proj(u, a_w_qkv, j, 2 * D_MODEL, D_MODEL, M_P, M_S, out_dtype=BF16,
                        name="a_v_latent")
            if j + 1 < n_a:
                att_p = _diff_prompt(a_lambda, subln, q_p, kv_p, j, lam_init)
                a_kv.append(kv_p)
            else:
                att_p, new_a_k, new_a_v = _diff_prompt(a_lambda, subln, q_p, kv_p, j, lam_init,
                                                       prev_kv=a_kv)
            att_s = _diff_latent(a_lambda, subln, qk_s, v_s, ck_a, cv_a, j, lam_init)
            w_o = w_o_bf[0]
        elif kind == 1:
            gj = b_gains[j]
            q_p = _proj(u, b_w_qkv, j, 0, D_MODEL, 0, M_P, out_dtype=BF16, gains=gj,
                        gain_split=D_MODEL, name="b_q_prompt")
            k_p = _proj(u, b_w_qkv, j, D_MODEL, B_KV, 0, M_P, out_dtype=F32, tn=B_KV,
                        gains=gj[1:], gain_split=B_KV, name="b_k_prompt")
            v_p = _proj(u, b_w_qkv, j, D_MODEL + B_KV, B_KV, 0, M_P, out_dtype=F32, tn=B_KV,
                        name="b_v_prompt")
            qk_s = _proj(u, b_w_qkv, j, 0, D_MODEL + B_KV, M_P, M_S, out_dtype=BF16, tn=B_KV,
                         gains=gj, gain_split=D_MODEL, rope=rope, name="b_qk_latent")
            v_s = _proj(u, b_w_qkv, j, D_MODEL + B_KV, B_KV, M_P, M_S, out_dtype=BF16, tn=B_KV,
                        name="b_v_latent")
            if j + 1 < n_b:
                att_p = _gqa_prompt(q_p, k_p, v_p)
                b_kv.append((k_p, v_p))
            else:
                att_p, new_b_k, new_b_v = _gqa_prompt(q_p, k_p, v_p, prev_kv=b_kv)
            att_s = _gqa_latent(qk_s, v_s, ck_b, cv_b, j)
            w_o = w_o_bf[1]
        else:
            att_p = _dft(u, SEQ, BATCH, 0, C_GROUPS)
            att_s = _dft(u, DEC_SEQ, DEC_BATCH, M_P, 1)
            w_o = w_o_bf[2]
        x, u = _oproj(att_p, att_s, w_o, j, x, mod, ln_g4, ln_b4, l)
        x, u = _mlp(u, w_up, w_down, x, mod, ln_g4, ln_b4, l)

    y_p, y_s = x, u
    return (y_p.reshape(BATCH, SEQ, D_MODEL), y_s.reshape(DEC_BATCH, DEC_SEQ, D_MODEL),
            new_a_k.reshape(BATCH, n_a, SEQ, A_HEADS, 2, A_HEAD_DIM),
            _a_v_from_rows(new_a_v, BATCH, SEQ),
            new_b_k.reshape(BATCH, n_b, SEQ, B_KV_HEADS, B_HEAD_DIM),
            new_b_v.reshape(BATCH, n_b, SEQ, B_KV_HEADS, B_HEAD_DIM))
```

```python
import functools
import math

import jax
import jax.numpy as jnp
import numpy as np
from jax import lax
from jax.experimental import pallas as pl
from jax.experimental.pallas import tpu as pltpu

D_MODEL = 2048
BATCH = 16
SEQ = 256
DEPTH = 4
DEC_BATCH = 2
DEC_SEQ = 1024
PAST_LEN = 512
GRID_W = 64
N_MIXERS = 3
A_HEAD_DIM = 128
A_HEADS = D_MODEL // (2 * A_HEAD_DIM)
B_HEAD_DIM = 128
B_HEADS = D_MODEL // B_HEAD_DIM
B_KV_HEADS = B_HEADS // 4
B_GROUP = B_HEADS // B_KV_HEADS
B_KV = B_KV_HEADS * B_HEAD_DIM
C_GROUPS = 4
C_GROUP_DIM = D_MODEL // C_GROUPS
D_FF = 4 * D_MODEL
ROPE_BASE = 10000.0
ALPHA = (2 * DEPTH) ** 0.25
LN_EPS = 1e-5
RMS_EPS = 1e-6
N_MOD = 6
LOG2E = 1.4426950408889634

M_P = BATCH * SEQ
M_S = DEC_BATCH * DEC_SEQ
M_ALL = M_P + M_S
N_COND = 1 + DEC_BATCH
COND_PAD = 8

LANE = 128
VMEM_LIMIT = 58 * 1024 * 1024
ROW_TILE = 512

BF16 = jnp.bfloat16
F32 = jnp.float32


def _params(*sem, vmem=VMEM_LIMIT):
    return pltpu.CompilerParams(dimension_semantics=sem, vmem_limit_bytes=vmem)


def _cond_row(i, tm):
    start = i * tm
    return jnp.where(start < M_P, 0, 1 + (start - M_P) // DEC_SEQ)


def _mod_spec(l, which, tm):
    def idx(*g):
        return (l, _cond_row(g[0], tm), which, 0, 0)
    return pl.BlockSpec((None, None, None, 1, D_MODEL), idx)


def _vec_spec(l, which):
    return pl.BlockSpec((None, None, 1, D_MODEL), lambda *g: (l, which, 0, 0))


def _prompt_rows_spec(tm):
    last = M_P // tm - 1
    return pl.BlockSpec((tm, D_MODEL), lambda i: (jnp.minimum(i, last), 0))


def _latent_rows_spec(tm):
    first = M_P // tm
    return pl.BlockSpec((tm, D_MODEL), lambda i: (jnp.maximum(i - first, 0), 0))


def _mod_kernel(c_ref, w_ref, b_ref, o_ref):
    c = c_ref[...]
    s = c * (1.0 / (1.0 + jnp.exp(-c)))
    o_ref[...] = jnp.dot(s.astype(BF16), w_ref[...].astype(BF16),
                         preferred_element_type=F32) + b_ref[...]


def _modulation(cond, w_mod, b_mod):
    tn = 1024
    n_out = N_MOD * D_MODEL
    return pl.pallas_call(
        _mod_kernel,
        out_shape=jax.ShapeDtypeStruct((DEPTH, COND_PAD, n_out), F32),
        grid=(DEPTH, n_out // tn),
        in_specs=[
            pl.BlockSpec((COND_PAD, D_MODEL), lambda l, n: (0, 0)),
            pl.BlockSpec((None, D_MODEL, tn), lambda l, n: (l, 0, n)),
            pl.BlockSpec((None, 1, tn), lambda l, n: (l, 0, n)),
        ],
        out_specs=pl.BlockSpec((None, COND_PAD, tn), lambda l, n: (l, 0, n)),
        compiler_params=_params("arbitrary", "arbitrary"),
        name="modulation",
    )(cond, w_mod, b_mod.reshape(DEPTH, 1, n_out))


def _cast_kernel(w_ref, o_ref):
    o_ref[...] = w_ref[...].astype(BF16)


def _cast_bf16(w, rows=512):
    n_l, k, n = w.shape
    spec = pl.BlockSpec((None, rows, n), lambda l, r: (l, r, 0))
    return pl.pallas_call(
        _cast_kernel,
        out_shape=jax.ShapeDtypeStruct(w.shape, BF16),
        grid=(n_l, k // rows),
        in_specs=[spec],
        out_specs=spec,
        compiler_params=_params("arbitrary", "arbitrary"),
        name="cast_weight",
    )(w)


def _embed_kernel(xp_ref, xs_ref, sc_ref, sh_ref, u_ref):
    x = jnp.where(pl.program_id(0) * ROW_TILE < M_P, xp_ref[...], xs_ref[...])
    u_ref[...] = (x * (1.0 + sc_ref[...]) + sh_ref[...]).astype(BF16)


def _embed(xp, xs, mod):
    tm = ROW_TILE
    return pl.pallas_call(
        _embed_kernel,
        out_shape=jax.ShapeDtypeStruct((M_ALL, D_MODEL), BF16),
        grid=(M_ALL // tm,),
        in_specs=[_prompt_rows_spec(tm), _latent_rows_spec(tm),
                  _mod_spec(0, 1, tm), _mod_spec(0, 0, tm)],
        out_specs=pl.BlockSpec((tm, D_MODEL), lambda i: (i, 0)),
        compiler_params=_params("arbitrary"),
        name="embed",
    )(xp, xs, mod, mod)


def _rms_chunk(x, g):
    ms = jnp.mean(x * x, axis=-1, keepdims=True)
    return x * lax.rsqrt(ms + RMS_EPS) * g


def _rope_chunk(x, cos, sin_a, sin_b):
    return x * cos + pltpu.roll(x, LANE - 32, 1) * sin_a + pltpu.roll(x, 32, 1) * sin_b


def _proj_kernel(*refs, tn, rms, rope):
    it = iter(refs)
    u_ref, w_ref = next(it), next(it)
    g_ref = next(it) if rms else None
    tabs = (next(it), next(it), next(it)) if rope else None
    o_ref, wbf = next(it), next(it)

    @pl.when(pl.program_id(1) == 0)
    def _():
        wbf[...] = w_ref[...].astype(BF16)

    acc = jnp.dot(u_ref[...], wbf[...], preferred_element_type=F32)
    if rms or rope:
        g = g_ref[...] if rms else None
        tab = tuple(t[...] for t in tabs) if rope else None
        for c in range(tn // LANE):
            x = acc[:, c * LANE:(c + 1) * LANE]
            if rms:
                x = _rms_chunk(x, g)
            if rope:
                x = _rope_chunk(x, *tab)
            o_ref[:, c * LANE:(c + 1) * LANE] = x.astype(o_ref.dtype)
    else:
        o_ref[...] = acc.astype(o_ref.dtype)


def _proj(u, w, w_layer, col0, ncols, m0, m_rows, *, out_dtype, tn=1024, tm=1024,
          gains=None, gain_split=None, rope=None, name="proj"):
    k = u.shape[1]
    assert col0 % tn == 0 and ncols % tn == 0 and m0 % tm == 0 and m_rows % tm == 0
    n_t, m_t, mt0, nt0 = ncols // tn, m_rows // tm, m0 // tm, col0 // tn
    ins = [u, w]
    in_specs = [
        pl.BlockSpec((tm, k), lambda n, m: (mt0 + m, 0)),
        pl.BlockSpec((None, k, tn), lambda n, m: (w_layer, 0, nt0 + n)),
    ]
    if gains is not None:
        ins.append(gains)
        in_specs.append(pl.BlockSpec(
            (None, 1, LANE), lambda n, m: (jnp.where(n * tn < gain_split, 0, 1), 0, 0)))
    if rope is not None:
        assert m0 >= M_P
        per = DEC_SEQ // tm
        for t in rope:
            ins.append(t)
            in_specs.append(pl.BlockSpec((tm, LANE), lambda n, m: ((mt0 + m) % per, 0)))
    kern = functools.partial(_proj_kernel, tn=tn, rms=gains is not None, rope=rope is not None)
    return pl.pallas_call(
        kern,
        out_shape=jax.ShapeDtypeStruct((m_rows, ncols), out_dtype),
        grid=(n_t, m_t),
        in_specs=in_specs,
        out_specs=pl.BlockSpec((tm, tn), lambda n, m: (m, n)),
        scratch_shapes=[pltpu.VMEM((k, tn), BF16)],
        compiler_params=_params("arbitrary", "arbitrary"),
        name=name,
    )(*ins)


def _qkt(q, k):
    return lax.dot_general(q, k, (((1,), (1,)), ((), ())), preferred_element_type=F32)


def _softmax_parts(scores, scale):
    m = scores[0].max(axis=-1, keepdims=True)
    for s in scores[1:]:
        m = jnp.maximum(m, s.max(axis=-1, keepdims=True))
    c = scale * LOG2E
    es = [jnp.exp2((s - m) * c) for s in scores]
    tot = es[0].sum(axis=-1, keepdims=True)
    for e in es[1:]:
        tot = tot + e.sum(axis=-1, keepdims=True)
    return es, 1.0 / tot


def _diff_lambda(lp_ref, lam_init):
    lp = lp_ref[...]
    s1 = jnp.sum(lp[0:1] * lp[1:2], axis=-1, keepdims=True)
    s2 = jnp.sum(lp[2:3] * lp[3:4], axis=-1, keepdims=True)
    return jnp.exp(s1) - jnp.exp(s2) + lam_init


def _diff_head(q, ks, vs, lam, gain):
    scale = A_HEAD_DIM ** -0.5
    q0, q1 = q[:, :A_HEAD_DIM], q[:, A_HEAD_DIM:]
    e0, r0 = _softmax_parts([_qkt(q0, k[:, :A_HEAD_DIM]) for k in ks], scale)
    e1, r1 = _softmax_parts([_qkt(q1, k[:, A_HEAD_DIM:]) for k in ks], scale)
    r1 = lam * r1
    o = None
    for a0, a1, v in zip(e0, e1, vs):
        a = (a0 * r0 - a1 * r1).astype(BF16)
        part = jnp.dot(a, v, preferred_element_type=F32)
        o = part if o is None else o + part
    ms = jnp.mean(o * o, axis=-1, keepdims=True)
    return (o * lax.rsqrt(ms + RMS_EPS) * gain).astype(BF16)


def _gqa_head(q, ks, vs):
    es, r = _softmax_parts([_qkt(q, k) for k in ks], B_HEAD_DIM ** -0.5)
    o = None
    for e, v in zip(es, vs):
        part = jnp.dot((e * r).astype(BF16), v, preferred_element_type=F32)
        o = part if o is None else o + part
    return o.astype(BF16)


A_CHUNKS = D_MODEL // LANE
B_CHUNKS = B_KV // LANE


def _a_v_row(c):
    return (c % 2) * A_HEADS + c // 2


def _a_v_rows(v, batch, seq):
    n_l = v.shape[1]
    v = v.reshape(batch, n_l, seq, A_HEADS, 2, LANE).transpose(0, 1, 2, 4, 3, 5)
    return v.reshape(batch, n_l, seq * A_CHUNKS, LANE)


def _a_v_from_rows(r, batch, seq):
    n_l = r.shape[1]
    v = r.reshape(batch, n_l, seq, 2, A_HEADS, LANE).transpose(0, 1, 2, 4, 3, 5)
    return v.reshape(batch, n_l, seq, A_HEADS, 2 * LANE)


def _diff_prompt_kernel(*refs, lam_init, n_prev, emit):
    lp_ref, g_ref, q_ref, k_ref, v_ref = refs[:5]
    prev = refs[5:5 + n_prev]
    o_ref = refs[5 + n_prev]
    lam = _diff_lambda(lp_ref, lam_init)
    gain = g_ref[...] * (1.0 - lam_init)
    w = 2 * A_HEAD_DIM
    for h in range(A_HEADS):
        sl = slice(h * w, (h + 1) * w)
        o_ref[:, sl] = _diff_head(q_ref[:, sl], [k_ref[:, sl].astype(BF16)],
                                  [v_ref[:, sl].astype(BF16)], lam, gain)
    if emit:
        nk_ref, nv_ref = refs[6 + n_prev], refs[7 + n_prev]
        layers = [(p, 0, p, D_MODEL) for p in prev] + [(k_ref, 0, v_ref, 0)]
        for jj, (kr, k0, vr, v0) in enumerate(layers):
            for c in range(A_CHUNKS):
                nk_ref[jj, pl.ds(c, SEQ, stride=A_CHUNKS), :] = kr[:, k0 + c * LANE:k0 + (c + 1) * LANE]
                nv_ref[jj, pl.ds(_a_v_row(c), SEQ, stride=A_CHUNKS), :] = (
                    vr[:, v0 + c * LANE:v0 + (c + 1) * LANE])


def _diff_prompt(lp, g, q, kv, j, lam_init, prev_kv=None):
    emit = prev_kv is not None
    prev_kv = list(prev_kv or [])
    n_layers = len(prev_kv) + 1
    att_shape = jax.ShapeDtypeStruct((M_P, D_MODEL), BF16)
    att_spec = pl.BlockSpec((SEQ, D_MODEL), lambda b: (b, 0))
    if emit:
        new_shape = jax.ShapeDtypeStruct((BATCH, n_layers, SEQ * A_CHUNKS, LANE), F32)
        new_spec = pl.BlockSpec((None, n_layers, SEQ * A_CHUNKS, LANE), lambda b: (b, 0, 0, 0))
        out_shape, out_specs = (att_shape, new_shape, new_shape), (att_spec, new_spec, new_spec)
    else:
        out_shape, out_specs = att_shape, att_spec
    return pl.pallas_call(
        functools.partial(_diff_prompt_kernel, lam_init=lam_init, n_prev=len(prev_kv), emit=emit),
        out_shape=out_shape,
        grid=(BATCH,),
        in_specs=[
            pl.BlockSpec((None, 4, A_HEAD_DIM), lambda b: (j, 0, 0)),
            pl.BlockSpec((None, 1, 2 * A_HEAD_DIM), lambda b: (j, 0, 0)),
            pl.BlockSpec((SEQ, D_MODEL), lambda b: (b, 0)),
            pl.BlockSpec((SEQ, D_MODEL), lambda b: (b, 0)),
            pl.BlockSpec((SEQ, D_MODEL), lambda b: (b, 1)),
        ] + [pl.BlockSpec((SEQ, 2 * D_MODEL), lambda b: (b, 0)) for _ in prev_kv],
        out_specs=out_specs,
        compiler_params=_params("arbitrary"),
        name="diff_attn_prompt",
    )(lp, g, q, kv, kv, *prev_kv)


ATT_ROW_GROUPS = 2


def _diff_latent_kernel(lp_ref, g_ref, q_ref, k_ref, v_ref, ck_ref, cv_ref, o_ref, *, lam_init):
    lam = _diff_lambda(lp_ref, lam_init)
    gain = g_ref[...] * (1.0 - lam_init)
    h = pl.program_id(1)

    def cached(ref, r0, r1):
        halves = [ref[pl.ds(r, PAST_LEN, stride=A_CHUNKS), :] for r in (r0, r1)]
        return jnp.concatenate(halves, axis=1).astype(BF16)

    ck = cached(ck_ref, 2 * h, 2 * h + 1)
    cv = cached(cv_ref, h, A_HEADS + h)
    rows = q_ref.shape[0] // ATT_ROW_GROUPS
    for r in range(ATT_ROW_GROUPS):
        sl = slice(r * rows, (r + 1) * rows)
        o_ref[sl, :] = _diff_head(q_ref[sl, :], [ck, k_ref[...]], [cv, v_ref[...]], lam, gain)


def _diff_latent(lp, g, qk, v, ck, cv, j, lam_init, tq=512):
    w = 2 * A_HEAD_DIM
    per = DEC_SEQ // tq
    ctx_spec = pl.BlockSpec((None, None, PAST_LEN * A_CHUNKS, LANE), lambda b, h, t: (b, j, 0, 0))
    return pl.pallas_call(
        functools.partial(_diff_latent_kernel, lam_init=lam_init),
        out_shape=jax.ShapeDtypeStruct((M_S, D_MODEL), BF16),
        grid=(DEC_BATCH, A_HEADS, per),
        in_specs=[
            pl.BlockSpec((None, 4, A_HEAD_DIM), lambda b, h, t: (j, 0, 0)),
            pl.BlockSpec((None, 1, w), lambda b, h, t: (j, 0, 0)),
            pl.BlockSpec((tq, w), lambda b, h, t: (b * per + t, h)),
            pl.BlockSpec((DEC_SEQ, w), lambda b, h, t: (b, A_HEADS + h)),
            pl.BlockSpec((DEC_SEQ, w), lambda b, h, t: (b, h)),
            ctx_spec, ctx_spec,
        ],
        out_specs=pl.BlockSpec((tq, w), lambda b, h, t: (b * per + t, h)),
        compiler_params=_params("arbitrary", "arbitrary", "arbitrary"),
        name="diff_attn_latent",
    )(lp, g, qk, qk, v, ck, cv)


def _gqa_prompt_kernel(*refs, n_prev, emit):
    q_ref, k_ref, v_ref = refs[:3]
    prev = refs[3:3 + 2 * n_prev]
    o_ref = refs[3 + 2 * n_prev]
    d = B_HEAD_DIM
    for n in range(B_KV_HEADS):
        k = [k_ref[:, n * d:(n + 1) * d].astype(BF16)]
        v = [v_ref[:, n * d:(n + 1) * d].astype(BF16)]
        for g in range(B_GROUP):
            sl = slice((n * B_GROUP + g) * d, (n * B_GROUP + g + 1) * d)
            o_ref[:, sl] = _gqa_head(q_ref[:, sl], k, v)
    if emit:
        nk_ref, nv_ref = refs[4 + 2 * n_prev], refs[5 + 2 * n_prev]
        layers = [(prev[2 * i], prev[2 * i + 1]) for i in range(n_prev)] + [(k_ref, v_ref)]
        for jj, (kr, vr) in enumerate(layers):
            for n in range(B_CHUNKS):
                rows = pl.ds(n, SEQ, stride=B_CHUNKS)
                nk_ref[jj, rows, :] = kr[:, n * LANE:(n + 1) * LANE]
                nv_ref[jj, rows, :] = vr[:, n * LANE:(n + 1) * LANE]


def _gqa_prompt(q, k, v, prev_kv=None):
    emit = prev_kv is not None
    prev = [a for pair in (prev_kv or []) for a in pair]
    n_layers = len(prev) // 2 + 1
    kv_spec = pl.BlockSpec((SEQ, B_KV), lambda b: (b, 0))
    att_shape = jax.ShapeDtypeStruct((M_P, D_MODEL), BF16)
    att_spec = pl.BlockSpec((SEQ, D_MODEL), lambda b: (b, 0))
    if emit:
        new_shape = jax.ShapeDtypeStruct((BATCH, n_layers, SEQ * B_CHUNKS, LANE), F32)
        new_spec = pl.BlockSpec((None, n_layers, SEQ * B_CHUNKS, LANE), lambda b: (b, 0, 0, 0))
        out_shape, out_specs = (att_shape, new_shape, new_shape), (att_spec, new_spec, new_spec)
    else:
        out_shape, out_specs = att_shape, att_spec
    return pl.pallas_call(
        functools.partial(_gqa_prompt_kernel, n_prev=len(prev) // 2, emit=emit),
        out_shape=out_shape,
        grid=(BATCH,),
        in_specs=[att_spec, kv_spec, kv_spec] + [kv_spec for _ in prev],
        out_specs=out_specs,
        compiler_params=_params("arbitrary"),
        name="gqa_attn_prompt",
    )(q, k, v, *prev)


def _gqa_latent_kernel(q_ref, k_ref, v_ref, ck_ref, cv_ref, o_ref):
    d = B_HEAD_DIM
    rows = pl.ds(pl.program_id(1), PAST_LEN, stride=B_CHUNKS)
    ks = [ck_ref[rows, :].astype(BF16), k_ref[...]]
    vs = [cv_ref[rows, :].astype(BF16), v_ref[...]]
    for g in range(B_GROUP):
        o_ref[:, g * d:(g + 1) * d] = _gqa_head(q_ref[:, g * d:(g + 1) * d], ks, vs)


def _gqa_latent(qk, v, ck, cv, j, tq=512):
    d = B_HEAD_DIM
    gw = B_GROUP * d
    per = DEC_SEQ // tq
    ctx_spec = pl.BlockSpec((None, None, PAST_LEN * B_CHUNKS, LANE), lambda b, n, t: (b, j, 0, 0))
    return pl.pallas_call(
        _gqa_latent_kernel,
        out_shape=jax.ShapeDtypeStruct((M_S, D_MODEL), BF16),
        grid=(DEC_BATCH, B_KV_HEADS, per),
        in_specs=[
            pl.BlockSpec((tq, gw), lambda b, n, t: (b * per + t, n)),
            pl.BlockSpec((DEC_SEQ, d), lambda b, n, t: (b, D_MODEL // d + n)),
            pl.BlockSpec((DEC_SEQ, d), lambda b, n, t: (b, n)),
            ctx_spec, ctx_spec,
        ],
        out_specs=pl.BlockSpec((tq, gw), lambda b, n, t: (b * per + t, n)),
        compiler_params=_params("arbitrary", "arbitrary", "arbitrary"),
        name="gqa_attn_latent",
    )(qk, qk, v, ck, cv)


def _dft_tables(s):
    def cs(n):
        idx = (np.arange(n)[:, None] * np.arange(n)[None, :]) % n
        ang = 2.0 * np.pi * idx / n
        return np.cos(ang), np.sin(ang)
    cc, sc = cs(C_GROUP_DIM)
    c_s, s_s = cs(s)
    t_chan = np.concatenate([cc, sc], axis=1).astype(np.float32)
    t_pos = np.concatenate([c_s, -s_s], axis=1).astype(np.float32)
    return jnp.asarray(t_chan).astype(BF16), jnp.asarray(t_pos).astype(BF16)


def _dft_group(x, t_chan, t_pos, norm):
    xcs = jnp.dot(x, t_chan, preferred_element_type=F32).astype(BF16)
    stacked = jnp.concatenate([xcs[:, :C_GROUP_DIM], xcs[:, C_GROUP_DIM:]], axis=0)
    y = jnp.dot(t_pos, stacked, preferred_element_type=F32)
    return (y * norm).astype(BF16)


def _dft_kernel(u_ref, tc_ref, tp_ref, o_ref, *, groups, norm):
    tc, tp = tc_ref[...], tp_ref[...]
    for g in range(groups):
        sl = slice(g * C_GROUP_DIM, (g + 1) * C_GROUP_DIM)
        o_ref[:, sl] = _dft_group(u_ref[:, sl], tc, tp, norm)


def _dft(u, s, batch, row0, groups_per_step):
    t_chan, t_pos = _dft_tables(s)
    gw = groups_per_step * C_GROUP_DIM
    rb0 = row0 // s
    return pl.pallas_call(
        functools.partial(_dft_kernel, groups=groups_per_step,
                          norm=1.0 / math.sqrt(s * C_GROUP_DIM)),
        out_shape=jax.ShapeDtypeStruct((batch * s, D_MODEL), BF16),
        grid=(batch, C_GROUPS // groups_per_step),
        in_specs=[
            pl.BlockSpec((s, gw), lambda b, g: (rb0 + b, g)),
            pl.BlockSpec(t_chan.shape, lambda b, g: (0, 0)),
            pl.BlockSpec(t_pos.shape, lambda b, g: (0, 0)),
        ],
        out_specs=pl.BlockSpec((s, gw), lambda b, g: (b, g)),
        compiler_params=_params("arbitrary", "arbitrary"),
        name="dft",
    )(u, t_chan, t_pos)


ROW_GROUPS = 2


def _post_norm(y, g, b):
    mu = jnp.mean(y, axis=-1, keepdims=True)
    yc = y - mu
    var = jnp.mean(yc * yc, axis=-1, keepdims=True)
    return yc * lax.rsqrt(var + LN_EPS) * g + b


def _oproj_kernel(*refs, split_x):
    ap_ref, as_ref, w_ref = refs[:3]
    x_refs = refs[3:5] if split_x else refs[3:4]
    gate_ref, g_ref, b_ref, sc_ref, sh_ref, x1_ref, u_ref = refs[3 + len(x_refs):]
    is_prompt = pl.program_id(0) * ROW_TILE < M_P
    gate, g, b = gate_ref[...], g_ref[...], b_ref[...]
    sc1, sh = 1.0 + sc_ref[...], sh_ref[...]
    rows = ROW_TILE // ROW_GROUPS
    for r in range(ROW_GROUPS):
        sl = slice(r * rows, (r + 1) * rows)
        a = jnp.where(is_prompt, ap_ref[sl, :], as_ref[sl, :])
        mixed = jnp.dot(a, w_ref[...], preferred_element_type=F32)
        if split_x:
            x = jnp.where(is_prompt, x_refs[0][sl, :], x_refs[1][sl, :])
        else:
            x = x_refs[0][sl, :]
        x1 = _post_norm(ALPHA * x + gate * mixed, g, b)
        x1_ref[sl, :] = x1
        u_ref[sl, :] = (x1 * sc1 + sh).astype(BF16)


def _oproj(a_p, a_s, w_bf, w_layer, x, mod, ln_g, ln_b, l):
    tm = ROW_TILE
    row = pl.BlockSpec((tm, D_MODEL), lambda i: (i, 0))
    split_x = isinstance(x, tuple)
    xs = list(x) if split_x else [x]
    x_specs = [_prompt_rows_spec(tm), _latent_rows_spec(tm)] if split_x else [row]
    return pl.pallas_call(
        functools.partial(_oproj_kernel, split_x=split_x),
        out_shape=(jax.ShapeDtypeStruct((M_ALL, D_MODEL), F32),
                   jax.ShapeDtypeStruct((M_ALL, D_MODEL), BF16)),
        grid=(M_ALL // tm,),
        in_specs=[
            _prompt_rows_spec(tm), _latent_rows_spec(tm),
            pl.BlockSpec((None, D_MODEL, D_MODEL), lambda i: (w_layer, 0, 0),
                         pipeline_mode=pl.Buffered(1)),
            *x_specs,
            _mod_spec(l, 2, tm),
            _vec_spec(l, 0), _vec_spec(l, 0),
            _mod_spec(l, 4, tm),
            _mod_spec(l, 3, tm),
        ],
        out_specs=(row, row),
        compiler_params=_params("arbitrary"),
        name="oproj_ln",
    )(a_p, a_s, w_bf, *xs, mod, ln_g, ln_b, mod, mod)


def _mlp_kernel(*refs, n_tiles, n_ff, last):
    u_ref, wu_ref, wd_ref, x_ref, gate_ref, g_ref, b_ref = refs[:7]
    acc, prev = refs[-2:]
    i, f = pl.program_id(0), pl.program_id(1)
    tm = u_ref.shape[0]
    close_rows = tm // n_ff

    @pl.when(jnp.logical_and(f == 0, i > 0))
    def _():
        prev[...] = acc[...]

    @pl.when(f == 0)
    def _():
        acc[...] = jnp.zeros_like(acc)

    def matmuls():
        wu = wu_ref[...].astype(BF16)
        wd = wd_ref[...].astype(BF16)
        rows = tm // ROW_GROUPS
        for r in range(ROW_GROUPS):
            sl = slice(r * rows, (r + 1) * rows)
            h = jnp.dot(u_ref[sl, :], wu, preferred_element_type=F32)
            h = jnp.maximum(h, 0.0)
            h = (h * h).astype(BF16)
            acc[sl, :] += jnp.dot(h, wd, preferred_element_type=F32)

    def close():
        rows = pl.ds(pl.multiple_of(f * close_rows, close_rows), close_rows)
        y = _post_norm(ALPHA * x_ref[...] + gate_ref[...] * prev[rows, :], g_ref[...], b_ref[...])
        if last:
            yp_ref, ys_ref = refs[7:9]
            prev_is_prompt = (i - 1) * tm < M_P

            @pl.when(prev_is_prompt)
            def _():
                yp_ref[...] = y

            @pl.when(jnp.logical_not(prev_is_prompt))
            def _():
                ys_ref[...] = y
        else:
            sc_ref, sh_ref, x2_ref, un_ref = refs[7:11]
            x2_ref[...] = y
            un_ref[...] = (y * (1.0 + sc_ref[...]) + sh_ref[...]).astype(BF16)

    @pl.when(i == 0)
    def _():
        matmuls()

    @pl.when(jnp.logical_and(i > 0, i < n_tiles))
    def _():
        matmuls()
        close()

    @pl.when(i == n_tiles)
    def _():
        close()


def _mlp(u, w_up, w_down, x, mod, ln_g, ln_b, l, tm=1024, tf=512):
    last = l + 1 == DEPTH
    n_tiles, n_ff = M_ALL // tm, D_FF // tf
    close_rows = tm // n_ff

    def close_block(i, f):
        return jnp.where(i == 0, 0, (i - 1) * n_ff + f)

    def mod_prev(layer, which):
        def idx(i, f):
            return (layer, _cond_row(jnp.maximum(i - 1, 0), tm), which, 0, 0)
        return pl.BlockSpec((None, None, None, 1, D_MODEL), idx)

    def ff_block(i, f):
        return jnp.where(i < n_tiles, f, n_ff - 1)

    row = pl.BlockSpec((close_rows, D_MODEL), lambda i, f: (close_block(i, f), 0))
    ins = [u, w_up, w_down, x, mod, ln_g, ln_b]
    in_specs = [
        pl.BlockSpec((tm, D_MODEL), lambda i, f: (jnp.minimum(i, n_tiles - 1), 0)),
        pl.BlockSpec((None, D_MODEL, tf), lambda i, f: (l, 0, ff_block(i, f))),
        pl.BlockSpec((None, tf, D_MODEL), lambda i, f: (l, ff_block(i, f), 0)),
        row, mod_prev(l, 5), _vec_spec(l, 1), _vec_spec(l, 1),
    ]
    if last:
        n_p = M_P // close_rows
        out_shape = (jax.ShapeDtypeStruct((M_P, D_MODEL), F32),
                     jax.ShapeDtypeStruct((M_S, D_MODEL), F32))
        out_specs = (
            pl.BlockSpec((close_rows, D_MODEL),
                         lambda i, f: (jnp.minimum(close_block(i, f), n_p - 1), 0)),
            pl.BlockSpec((close_rows, D_MODEL),
                         lambda i, f: (jnp.maximum(close_block(i, f) - n_p, 0), 0)),
        )
    else:
        ins += [mod, mod]
        in_specs += [mod_prev(l + 1, 1), mod_prev(l + 1, 0)]
        out_shape = (jax.ShapeDtypeStruct((M_ALL, D_MODEL), F32),
                     jax.ShapeDtypeStruct((M_ALL, D_MODEL), BF16))
        out_specs = (row, row)
    return pl.pallas_call(
        functools.partial(_mlp_kernel, n_tiles=n_tiles, n_ff=n_ff, last=last),
        out_shape=out_shape,
        grid=(n_tiles + 1, n_ff),
        in_specs=in_specs,
        out_specs=out_specs,
        scratch_shapes=[pltpu.VMEM((tm, D_MODEL), F32), pltpu.VMEM((tm, D_MODEL), F32)],
        compiler_params=_params("arbitrary", "arbitrary"),
        name="mlp",
    )(*ins)


def _rope_tables():
    n_freq = A_HEAD_DIM // 4
    pos = np.arange(DEC_SEQ)
    row = (pos // GRID_W).astype(np.float32)
    col = (pos % GRID_W).astype(np.float32)
    inv_freq = (ROPE_BASE ** (-np.arange(n_freq, dtype=np.float32) / n_freq)).astype(np.float32)
    ar = row[:, None] * inv_freq
    ac = col[:, None] * inv_freq
    cr, sr, cc, sc = np.cos(ar), np.sin(ar), np.cos(ac), np.sin(ac)
    z = np.zeros_like(sr)
    cos = np.concatenate([cr, cr, cc, cc], axis=1)
    sin_a = np.concatenate([-sr, z, -sc, z], axis=1)
    sin_b = np.concatenate([z, sr, z, sc], axis=1)
    return tuple(jnp.asarray(t.astype(np.float32)) for t in (cos, sin_a, sin_b))


def kernel(x_prompt, x_sample, cache_a_k, cache_a_v, cache_b_k, cache_b_v, c, c_ctx, w_mod, b_mod, ln_g, ln_b, w_up, w_down, a_w_qkv, a_w_o, a_lambda, a_subln_g, b_w_qkv, b_w_o, b_q_norm_g, b_k_norm_g, c_w_f):
    n_a = a_w_qkv.shape[0]
    n_b = b_w_qkv.shape[0]
    cond = jnp.zeros((COND_PAD, D_MODEL), F32).at[0].set(c_ctx).at[1:N_COND].set(c)
    mod = _modulation(cond, w_mod, b_mod).reshape(DEPTH, COND_PAD, N_MOD, 1, D_MODEL)
    ln_g4 = ln_g.reshape(DEPTH, 2, 1, D_MODEL)
    ln_b4 = ln_b.reshape(DEPTH, 2, 1, D_MODEL)
    rope = _rope_tables()
    w_o_bf = (_cast_bf16(a_w_o), _cast_bf16(b_w_o), _cast_bf16(c_w_f))

    ck_a = cache_a_k.reshape(DEC_BATCH, n_a, PAST_LEN * A_CHUNKS, LANE)
    cv_a = _a_v_rows(cache_a_v, DEC_BATCH, PAST_LEN)
    ck_b = cache_b_k.reshape(DEC_BATCH, n_b, PAST_LEN * B_CHUNKS, LANE)
    cv_b = cache_b_v.reshape(DEC_BATCH, n_b, PAST_LEN * B_CHUNKS, LANE)
    subln = a_subln_g.reshape(n_a, 1, 2 * A_HEAD_DIM)
    b_gains = jnp.stack([b_q_norm_g, b_k_norm_g], axis=1).reshape(n_b, 2, 1, B_HEAD_DIM)

    x = (x_prompt.reshape(M_P, D_MODEL), x_sample.reshape(M_S, D_MODEL))
    u = _embed(*x, mod)

    a_kv, b_kv = [], []
    for l in range(DEPTH):
        kind, j = l % N_MIXERS, l // N_MIXERS
        if kind == 0:
            lam_init = 0.8 - 0.6 * math.exp(-0.3 * l)
            q_p = _proj(u, a_w_qkv, j, 0, D_MODEL, 0, M_P, out_dtype=BF16, name="a_q_prompt")
            kv_p = _proj(u, a_w_qkv, j, D_MODEL, 2 * D_MODEL, 0, M_P, out_dtype=F32,
                         name="a_kv_prompt")
            qk_s = _proj(u, a_w_qkv, j, 0, 2 * D_MODEL, M_P, M_S, out_dtype=BF16, rope=rope,
                         name="a_qk_latent")
            v_s = _proj(u, a_w_qkv, j, 2 * D_MODEL, D_MODEL, M_P, M_S, out_dtype=BF16,
                        name="a_v_latent")
            if j + 1 < n_a:
                att_p = _diff_prompt(a_lambda, subln, q_p, kv_p, j, lam_init)
                a_kv.append(kv_p)
            else:
                att_p, new_a_k, new_a_v = _diff_prompt(a_lambda, subln, q_p, kv_p, j, lam_init,
                                                       prev_kv=a_kv)
            att_s = _diff_latent(a_lambda, subln, qk_s, v_s, ck_a, cv_a, j, lam_init)
            w_o = w_o_bf[0]
        elif kind == 1:
            gj = b_gains[j]
            q_p = _proj(u, b_w_qkv, j, 0, D_MODEL, 0, M_P, out_dtype=BF16, gains=gj,
                        gain_split=D_MODEL, name="b_q_prompt")
            k_p = _proj(u, b_w_qkv, j, D_MODEL, B_KV, 0, M_P, out_dtype=F32, tn=B_KV,
                        gains=gj[1:], gain_split=B_KV, name="b_k_prompt")
            v_p = _proj(u, b_w_qkv, j, D_MODEL + B_KV, B_KV, 0, M_P, out_dtype=F32, tn=B_KV,
                        name="b_v_prompt")
            qk_s = _proj(u, b_w_qkv, j, 0, D_MODEL + B_KV, M_P, M_S, out_dtype=BF16, tn=B_KV,
                         gains=gj, gain_split=D_MODEL, rope=rope, name="b_qk_latent")
            v_s = _proj(u, b_w_qkv, j, D_MODEL + B_KV, B_KV, M_P, M_S, out_dtype=BF16, tn=B_KV,
                        name="b_v_latent")
            if j + 1 < n_b:
                att_p = _gqa_prompt(q_p, k_p, v_p)
                b_kv.append((k_p, v_p))
            else:
                att_p, new_b_k, new_b_v = _gqa_prompt(q_p, k_p, v_p, prev_kv=b_kv)
            att_s = _gqa_latent(qk_s, v_s, ck_b, cv_b, j)
            w_o = w_o_bf[1]
        else:
            att_p = _dft(u, SEQ, BATCH, 0, C_GROUPS)
            att_s = _dft(u, DEC_SEQ, DEC_BATCH, M_P, 1)
            w_o = w_o_bf[2]
        x, u = _oproj(att_p, att_s, w_o, j, x, mod, ln_g4, ln_b4, l)
        x, u = _mlp(u, w_up, w_down, x, mod, ln_g4, ln_b4, l)

    y_p, y_s = x, u
    return (y_p.reshape(BATCH, SEQ, D_MODEL), y_s.reshape(DEC_BATCH, DEC_SEQ, D_MODEL),
            new_a_k.reshape(BATCH, n_a, SEQ, A_HEADS, 2, A_HEAD_DIM),
            _a_v_from_rows(new_a_v, BATCH, SEQ),
            new_b_k.reshape(BATCH, n_b, SEQ, B_KV_HEADS, B_HEAD_DIM),
            new_b_v.reshape(BATCH, n_b, SEQ, B_KV_HEADS, B_HEAD_DIM))
```

```python
import functools
import math

import jax
import jax.numpy as jnp
import numpy as np
from jax import lax
from jax.experimental import pallas as pl
from jax.experimental.pallas import tpu as pltpu

D_MODEL = 2048
BATCH = 16
SEQ = 256
DEPTH = 4
DEC_BATCH = 2
DEC_SEQ = 1024
PAST_LEN = 512
GRID_W = 64
N_MIXERS = 3
A_HEAD_DIM = 128
A_HEADS = D_MODEL // (2 * A_HEAD_DIM)
B_HEAD_DIM = 128
B_HEADS = D_MODEL // B_HEAD_DIM
B_KV_HEADS = B_HEADS // 4
B_GROUP = B_HEADS // B_KV_HEADS
B_KV = B_KV_HEADS * B_HEAD_DIM
C_GROUPS = 4
C_GROUP_DIM = D_MODEL // C_GROUPS
D_FF = 4 * D_MODEL
ROPE_BASE = 10000.0
ALPHA = (2 * DEPTH) ** 0.25
LN_EPS = 1e-5
RMS_EPS = 1e-6
N_MOD = 6
LOG2E = 1.4426950408889634

M_P = BATCH * SEQ
M_S = DEC_BATCH * DEC_SEQ
M_ALL = M_P + M_S
N_COND = 1 + DEC_BATCH
COND_PAD = 8

LANE = 128
VMEM_LIMIT = 58 * 1024 * 1024
ROW_TILE = 512

BF16 = jnp.bfloat16
F32 = jnp.float32


def _params(*sem, vmem=VMEM_LIMIT):
    return pltpu.CompilerParams(dimension_semantics=sem, vmem_limit_bytes=vmem)


def _cond_row(i, tm):
    start = i * tm
    return jnp.where(start < M_P, 0, 1 + (start - M_P) // DEC_SEQ)


def _mod_spec(l, which, tm):
    def idx(*g):
        return (l, _cond_row(g[0], tm), which, 0, 0)
    return pl.BlockSpec((None, None, None, 1, D_MODEL), idx)


def _vec_spec(l, which):
    return pl.BlockSpec((None, None, 1, D_MODEL), lambda *g: (l, which, 0, 0))


def _prompt_rows_spec(tm):
    last = M_P // tm - 1
    return pl.BlockSpec((tm, D_MODEL), lambda i: (jnp.minimum(i, last), 0))


def _latent_rows_spec(tm):
    first = M_P // tm
    return pl.BlockSpec((tm, D_MODEL), lambda i: (jnp.maximum(i - first, 0), 0))


def _mod_kernel(c_ref, w_ref, b_ref, o_ref):
    c = c_ref[...]
    s = c * (1.0 / (1.0 + jnp.exp(-c)))
    o_ref[...] = jnp.dot(s.astype(BF16), w_ref[...].astype(BF16),
                         preferred_element_type=F32) + b_ref[...]


def _modulation(cond, w_mod, b_mod):
    tn = 1024
    n_out = N_MOD * D_MODEL
    return pl.pallas_call(
        _mod_kernel,
        out_shape=jax.ShapeDtypeStruct((DEPTH, COND_PAD, n_out), F32),
        grid=(DEPTH, n_out // tn),
        in_specs=[
            pl.BlockSpec((COND_PAD, D_MODEL), lambda l, n: (0, 0)),
            pl.BlockSpec((None, D_MODEL, tn), lambda l, n: (l, 0, n)),
            pl.BlockSpec((None, 1, tn), lambda l, n: (l, 0, n)),
        ],
        out_specs=pl.BlockSpec((None, COND_PAD, tn), lambda l, n: (l, 0, n)),
        compiler_params=_params("arbitrary", "arbitrary"),
        name="modulation",
    )(cond, w_mod, b_mod.reshape(DEPTH, 1, n_out))


def _cast_kernel(w_ref, o_ref):
    o_ref[...] = w_ref[...].astype(BF16)


def _cast_bf16(w, rows=512):
    n_l, k, n = w.shape
    spec = pl.BlockSpec((None, rows, n), lambda l, r: (l, r, 0))
    return pl.pallas_call(
        _cast_kernel,
        out_shape=jax.ShapeDtypeStruct(w.shape, BF16),
        grid=(n_l, k // rows),
        in_specs=[spec],
        out_specs=spec,
        compiler_params=_params("arbitrary", "arbitrary"),
        name="cast_weight",
    )(w)


def _embed_kernel(xp_ref, xs_ref, sc_ref, sh_ref, u_ref):
    x = jnp.where(pl.program_id(0) * ROW_TILE < M_P, xp_ref[...], xs_ref[...])
    u_ref[...] = (x * (1.0 + sc_ref[...]) + sh_ref[...]).astype(BF16)


def _embed(xp, xs, mod):
    tm = ROW_TILE
    return pl.pallas_call(
        _embed_kernel,
        out_shape=jax.ShapeDtypeStruct((M_ALL, D_MODEL), BF16),
        grid=(M_ALL // tm,),
        in_specs=[_prompt_rows_spec(tm), _latent_rows_spec(tm),
                  _mod_spec(0, 1, tm), _mod_spec(0, 0, tm)],
        out_specs=pl.BlockSpec((tm, D_MODEL), lambda i: (i, 0)),
        compiler_params=_params("arbitrary"),
        name="embed",
    )(xp, xs, mod, mod)


def _rms_chunk(x, g):
    ms = jnp.mean(x * x, axis=-1, keepdims=True)
    return x * lax.rsqrt(ms + RMS_EPS) * g


def _rope_chunk(x, cos, sin_a, sin_b):
    return x * cos + pltpu.roll(x, LANE - 32, 1) * sin_a + pltpu.roll(x, 32, 1) * sin_b


def _proj_kernel(*refs, tn, rms, rope):
    it = iter(refs)
    u_ref, w_ref = next(it), next(it)
    g_ref = next(it) if rms else None
    tabs = (next(it), next(it), next(it)) if rope else None
    o_ref, wbf = next(it), next(it)

    @pl.when(pl.program_id(1) == 0)
    def _():
        wbf[...] = w_ref[...].astype(BF16)

    acc = jnp.dot(u_ref[...], wbf[...], preferred_element_type=F32)
    chunks = [acc[:, c * LANE:(c + 1) * LANE] for c in range(tn // LANE)]
    if rms and rope:
        g = g_ref[...]
        tab = tuple(t[...] for t in tabs)
        rinv = [lax.rsqrt(jnp.mean(x * x, axis=-1, keepdims=True) + RMS_EPS) for x in chunks]

        @pl.when(pl.program_id(1) >= 0)
        def _():
            for c, (x, r) in enumerate(zip(chunks, rinv)):
                o_ref[:, c * LANE:(c + 1) * LANE] = (_rope_chunk(x * g, *tab) * r).astype(o_ref.dtype)
    elif rms:
        g = g_ref[...]
        for c, x in enumerate(chunks):
            o_ref[:, c * LANE:(c + 1) * LANE] = _rms_chunk(x, g).astype(o_ref.dtype)
    elif rope:
        tab = tuple(t[...] for t in tabs)
        for c, x in enumerate(chunks):
            o_ref[:, c * LANE:(c + 1) * LANE] = _rope_chunk(x, *tab).astype(o_ref.dtype)
    else:
        o_ref[...] = acc.astype(o_ref.dtype)


def _proj(u, w, w_layer, col0, ncols, m0, m_rows, *, out_dtype, tn=1024, tm=1024,
          gains=None, gain_split=None, rope=None, name="proj"):
    k = u.shape[1]
    assert col0 % tn == 0 and ncols % tn == 0 and m0 % tm == 0 and m_rows % tm == 0
    n_t, m_t, mt0, nt0 = ncols // tn, m_rows // tm, m0 // tm, col0 // tn
    ins = [u, w]
    in_specs = [
        pl.BlockSpec((tm, k), lambda n, m: (mt0 + m, 0)),
        pl.BlockSpec((None, k, tn), lambda n, m: (w_layer, 0, nt0 + n)),
    ]
    if gains is not None:
        ins.append(gains)
        in_specs.append(pl.BlockSpec(
            (None, 1, LANE), lambda n, m: (jnp.where(n * tn < gain_split, 0, 1), 0, 0)))
    if rope is not None:
        assert m0 >= M_P
        per = DEC_SEQ // tm
        for t in rope:
            ins.append(t)
            in_specs.append(pl.BlockSpec((tm, LANE), lambda n, m: ((mt0 + m) % per, 0)))
    kern = functools.partial(_proj_kernel, tn=tn, rms=gains is not None, rope=rope is not None)
    return pl.pallas_call(
        kern,
        out_shape=jax.ShapeDtypeStruct((m_rows, ncols), out_dtype),
        grid=(n_t, m_t),
        in_specs=in_specs,
        out_specs=pl.BlockSpec((tm, tn), lambda n, m: (m, n)),
        scratch_shapes=[pltpu.VMEM((k, tn), BF16)],
        compiler_params=_params("arbitrary", "arbitrary"),
        name=name,
    )(*ins)


def _qkt(q, k):
    return lax.dot_general(q, k, (((1,), (1,)), ((), ())), preferred_element_type=F32)


def _softmax_terms(s, scale):
    m = s.max(axis=-1, keepdims=True)
    e = jnp.exp2((s - m) * (scale * LOG2E))
    return e, 1.0 / e.sum(axis=-1, keepdims=True)


def _diff_lambda(lp_ref, lam_init):
    lp = lp_ref[...]
    s1 = jnp.sum(lp[0:1] * lp[1:2], axis=-1, keepdims=True)
    s2 = jnp.sum(lp[2:3] * lp[3:4], axis=-1, keepdims=True)
    return jnp.exp(s1) - jnp.exp(s2) + lam_init


def _diff_weights(s, lam):
    half = s.shape[0] // 2
    e, r = _softmax_terms(s, A_HEAD_DIM ** -0.5)
    return (e[:half] * r[:half] - e[half:] * (lam * r[half:])).astype(BF16)


def _sub_norm(o, gain):
    ms = jnp.mean(o * o, axis=-1, keepdims=True)
    return (o * lax.rsqrt(ms + RMS_EPS) * gain).astype(BF16)


A_CHUNKS = D_MODEL // LANE
B_CHUNKS = B_KV // LANE


def _a_v_row(c):
    return (c % 2) * A_HEADS + c // 2


def _a_v_rows(v, batch, seq):
    n_l = v.shape[1]
    v = v.reshape(batch, n_l, seq, A_HEADS, 2, LANE).transpose(0, 1, 2, 4, 3, 5)
    return v.reshape(batch, n_l, seq * A_CHUNKS, LANE)


def _a_v_from_rows(r, batch, seq):
    n_l = r.shape[1]
    v = r.reshape(batch, n_l, seq, 2, A_HEADS, LANE).transpose(0, 1, 2, 4, 3, 5)
    return v.reshape(batch, n_l, seq, A_HEADS, 2 * LANE)


def _diff_prompt_kernel(*refs, lam_init, n_prev, emit):
    lp_ref, g_ref, q_ref, k_ref, v_ref = refs[:5]
    prev = refs[5:5 + n_prev]
    o_ref = refs[5 + n_prev]
    s_scr, a_scr, o_scr = refs[-3:]
    lam = _diff_lambda(lp_ref, lam_init)
    gain = g_ref[...] * (1.0 - lam_init)
    w = 2 * A_HEAD_DIM
    for h in range(A_HEADS):
        for p in range(2):
            cols = slice(h * w + p * A_HEAD_DIM, h * w + (p + 1) * A_HEAD_DIM)
            s_scr[(p * A_HEADS + h) * SEQ:(p * A_HEADS + h + 1) * SEQ, :] = _qkt(
                q_ref[:, cols], k_ref[:, cols].astype(BF16))

    @pl.when(pl.program_id(0) >= 0)
    def _():
        a_scr[...] = _diff_weights(s_scr[...], lam)
        for h in range(A_HEADS):
            rows = slice(h * SEQ, (h + 1) * SEQ)
            o_scr[rows, :] = jnp.dot(a_scr[rows, :], v_ref[:, h * w:(h + 1) * w].astype(BF16),
                                     preferred_element_type=F32)
        y = _sub_norm(o_scr[...], gain)
        for h in range(A_HEADS):
            o_ref[:, h * w:(h + 1) * w] = y[h * SEQ:(h + 1) * SEQ, :]

    if emit:
        nk_ref, nv_ref = refs[6 + n_prev], refs[7 + n_prev]
        layers = [(p, 0, p, D_MODEL) for p in prev] + [(k_ref, 0, v_ref, 0)]
        for jj, (kr, k0, vr, v0) in enumerate(layers):
            for c in range(A_CHUNKS):
                nk_ref[jj, pl.ds(c, SEQ, stride=A_CHUNKS), :] = kr[:, k0 + c * LANE:k0 + (c + 1) * LANE]
                nv_ref[jj, pl.ds(_a_v_row(c), SEQ, stride=A_CHUNKS), :] = (
                    vr[:, v0 + c * LANE:v0 + (c + 1) * LANE])


def _diff_prompt(lp, g, q, kv, j, lam_init, prev_kv=None):
    emit = prev_kv is not None
    prev_kv = list(prev_kv or [])
    n_layers = len(prev_kv) + 1
    att_shape = jax.ShapeDtypeStruct((M_P, D_MODEL), BF16)
    att_spec = pl.BlockSpec((SEQ, D_MODEL), lambda b: (b, 0))
    if emit:
        new_shape = jax.ShapeDtypeStruct((BATCH, n_layers, SEQ * A_CHUNKS, LANE), F32)
        new_spec = pl.BlockSpec((None, n_layers, SEQ * A_CHUNKS, LANE), lambda b: (b, 0, 0, 0))
        out_shape, out_specs = (att_shape, new_shape, new_shape), (att_spec, new_spec, new_spec)
    else:
        out_shape, out_specs = att_shape, att_spec
    return pl.pallas_call(
        functools.partial(_diff_prompt_kernel, lam_init=lam_init, n_prev=len(prev_kv), emit=emit),
        out_shape=out_shape,
        grid=(BATCH,),
        in_specs=[
            pl.BlockSpec((None, 4, A_HEAD_DIM), lambda b: (j, 0, 0)),
            pl.BlockSpec((None, 1, 2 * A_HEAD_DIM), lambda b: (j, 0, 0)),
            pl.BlockSpec((SEQ, D_MODEL), lambda b: (b, 0)),
            pl.BlockSpec((SEQ, D_MODEL), lambda b: (b, 0)),
            pl.BlockSpec((SEQ, D_MODEL), lambda b: (b, 1)),
        ] + [pl.BlockSpec((SEQ, 2 * D_MODEL), lambda b: (b, 0)) for _ in prev_kv],
        out_specs=out_specs,
        scratch_shapes=[pltpu.VMEM((2 * A_HEADS * SEQ, SEQ), F32),
                        pltpu.VMEM((A_HEADS * SEQ, SEQ), BF16),
                        pltpu.VMEM((A_HEADS * SEQ, 2 * A_HEAD_DIM), F32)],
        compiler_params=_params("arbitrary"),
        name="diff_attn_prompt",
    )(lp, g, q, kv, kv, *prev_kv)


def _diff_latent_kernel(lp_ref, g_ref, q_ref, k_ref, v_ref, ck_ref, cv_ref, o_ref,
                        s_scr, a_scr, *, lam_init):
    lam = _diff_lambda(lp_ref, lam_init)
    gain = g_ref[...] * (1.0 - lam_init)
    h = pl.program_id(1)
    tq, d = q_ref.shape[0], A_HEAD_DIM

    def cached(ref, r0, r1):
        halves = [ref[pl.ds(r, PAST_LEN, stride=A_CHUNKS), :] for r in (r0, r1)]
        return jnp.concatenate(halves, axis=1).astype(BF16)

    ck = cached(ck_ref, 2 * h, 2 * h + 1)
    for p in range(2):
        q = q_ref[:, p * d:(p + 1) * d]
        s_scr[p * tq:(p + 1) * tq, :PAST_LEN] = _qkt(q, ck[:, p * d:(p + 1) * d])
        s_scr[p * tq:(p + 1) * tq, PAST_LEN:] = _qkt(q, k_ref[:, p * d:(p + 1) * d])

    @pl.when(h >= 0)
    def _():
        a_scr[...] = _diff_weights(s_scr[...], lam)
        cv = cached(cv_ref, h, A_HEADS + h)
        o = (jnp.dot(a_scr[:, :PAST_LEN], cv, preferred_element_type=F32)
             + jnp.dot(a_scr[:, PAST_LEN:], v_ref[...], preferred_element_type=F32))
        o_ref[...] = _sub_norm(o, gain)


def _diff_latent(lp, g, qk, v, ck, cv, j, lam_init, tq=512):
    w = 2 * A_HEAD_DIM
    per = DEC_SEQ // tq
    ctx_spec = pl.BlockSpec((None, None, PAST_LEN * A_CHUNKS, LANE), lambda b, h, t: (b, j, 0, 0))
    return pl.pallas_call(
        functools.partial(_diff_latent_kernel, lam_init=lam_init),
        out_shape=jax.ShapeDtypeStruct((M_S, D_MODEL), BF16),
        grid=(DEC_BATCH, A_HEADS, per),
        in_specs=[
            pl.BlockSpec((None, 4, A_HEAD_DIM), lambda b, h, t: (j, 0, 0)),
            pl.BlockSpec((None, 1, w), lambda b, h, t: (j, 0, 0)),
            pl.BlockSpec((tq, w), lambda b, h, t: (b * per + t, h)),
            pl.BlockSpec((DEC_SEQ, w), lambda b, h, t: (b, A_HEADS + h)),
            pl.BlockSpec((DEC_SEQ, w), lambda b, h, t: (b, h)),
            ctx_spec, ctx_spec,
        ],
        out_specs=pl.BlockSpec((tq, w), lambda b, h, t: (b * per + t, h)),
        scratch_shapes=[pltpu.VMEM((2 * tq, PAST_LEN + DEC_SEQ), F32),
                        pltpu.VMEM((tq, PAST_LEN + DEC_SEQ), BF16)],
        compiler_params=_params("arbitrary", "arbitrary", "arbitrary"),
        name="diff_attn_latent",
    )(lp, g, qk, qk, v, ck, cv)


def _gqa_prompt_kernel(*refs, n_prev, emit):
    q_ref, k_ref, v_ref = refs[:3]
    prev = refs[3:3 + 2 * n_prev]
    o_ref = refs[3 + 2 * n_prev]
    s_scr, a_scr, o_scr = refs[-3:]
    d = B_HEAD_DIM
    for n in range(B_KV_HEADS):
        k = k_ref[:, n * d:(n + 1) * d].astype(BF16)
        for g in range(B_GROUP):
            hd = n * B_GROUP + g
            s_scr[hd * SEQ:(hd + 1) * SEQ, :] = _qkt(q_ref[:, hd * d:(hd + 1) * d], k)

    @pl.when(pl.program_id(0) >= 0)
    def _():
        e, r = _softmax_terms(s_scr[...], B_HEAD_DIM ** -0.5)
        a_scr[...] = (e * r).astype(BF16)
        group_rows = B_GROUP * SEQ
        for n in range(B_KV_HEADS):
            rows = slice(n * group_rows, (n + 1) * group_rows)
            o_scr[rows, :] = jnp.dot(a_scr[rows, :], v_ref[:, n * d:(n + 1) * d].astype(BF16),
                                     preferred_element_type=F32)
        for hd in range(B_HEADS):
            o_ref[:, hd * d:(hd + 1) * d] = o_scr[hd * SEQ:(hd + 1) * SEQ, :].astype(BF16)

    if emit:
        nk_ref, nv_ref = refs[4 + 2 * n_prev], refs[5 + 2 * n_prev]
        layers = [(prev[2 * i], prev[2 * i + 1]) for i in range(n_prev)] + [(k_ref, v_ref)]
        for jj, (kr, vr) in enumerate(layers):
            for n in range(B_CHUNKS):
                rows = pl.ds(n, SEQ, stride=B_CHUNKS)
                nk_ref[jj, rows, :] = kr[:, n * LANE:(n + 1) * LANE]
                nv_ref[jj, rows, :] = vr[:, n * LANE:(n + 1) * LANE]


def _gqa_prompt(q, k, v, prev_kv=None):
    emit = prev_kv is not None
    prev = [a for pair in (prev_kv or []) for a in pair]
    n_layers = len(prev) // 2 + 1
    kv_spec = pl.BlockSpec((SEQ, B_KV), lambda b: (b, 0))
    att_shape = jax.ShapeDtypeStruct((M_P, D_MODEL), BF16)
    att_spec = pl.BlockSpec((SEQ, D_MODEL), lambda b: (b, 0))
    if emit:
        new_shape = jax.ShapeDtypeStruct((BATCH, n_layers, SEQ * B_CHUNKS, LANE), F32)
        new_spec = pl.BlockSpec((None, n_layers, SEQ * B_CHUNKS, LANE), lambda b: (b, 0, 0, 0))
        out_shape, out_specs = (att_shape, new_shape, new_shape), (att_spec, new_spec, new_spec)
    else:
        out_shape, out_specs = att_shape, att_spec
    return pl.pallas_call(
        functools.partial(_gqa_prompt_kernel, n_prev=len(prev) // 2, emit=emit),
        out_shape=out_shape,
        grid=(BATCH,),
        in_specs=[att_spec, kv_spec, kv_spec] + [kv_spec for _ in prev],
        out_specs=out_specs,
        scratch_shapes=[pltpu.VMEM((B_HEADS * SEQ, SEQ), F32),
                        pltpu.VMEM((B_HEADS * SEQ, SEQ), BF16),
                        pltpu.VMEM((B_HEADS * SEQ, B_HEAD_DIM), F32)],
        compiler_params=_params("arbitrary"),
        name="gqa_attn_prompt",
    )(q, k, v, *prev)


def _gqa_latent_kernel(q_ref, k_ref, v_ref, ck_ref, cv_ref, o_ref, s_scr, a_scr):
    d, tq = B_HEAD_DIM, q_ref.shape[0]
    n = pl.program_id(1)
    cached = pl.ds(n, PAST_LEN, stride=B_CHUNKS)
    ck = ck_ref[cached, :].astype(BF16)
    for g in range(B_GROUP):
        q = q_ref[:, g * d:(g + 1) * d]
        s_scr[g * tq:(g + 1) * tq, :PAST_LEN] = _qkt(q, ck)
        s_scr[g * tq:(g + 1) * tq, PAST_LEN:] = _qkt(q, k_ref[...])

    @pl.when(n >= 0)
    def _():
        e, r = _softmax_terms(s_scr[...], B_HEAD_DIM ** -0.5)
        a_scr[...] = (e * r).astype(BF16)
        o = (jnp.dot(a_scr[:, :PAST_LEN], cv_ref[cached, :].astype(BF16),
                     preferred_element_type=F32)
             + jnp.dot(a_scr[:, PAST_LEN:], v_ref[...], preferred_element_type=F32))
        for g in range(B_GROUP):
            o_ref[:, g * d:(g + 1) * d] = o[g * tq:(g + 1) * tq, :].astype(BF16)


def _gqa_latent(qk, v, ck, cv, j, tq=512):
    d = B_HEAD_DIM
    gw = B_GROUP * d
    per = DEC_SEQ // tq
    ctx_spec = pl.BlockSpec((None, None, PAST_LEN * B_CHUNKS, LANE), lambda b, n, t: (b, j, 0, 0))
    return pl.pallas_call(
        _gqa_latent_kernel,
        out_shape=jax.ShapeDtypeStruct((M_S, D_MODEL), BF16),
        grid=(DEC_BATCH, B_KV_HEADS, per),
        in_specs=[
            pl.BlockSpec((tq, gw), lambda b, n, t: (b * per + t, n)),
            pl.BlockSpec((DEC_SEQ, d), lambda b, n, t: (b, D_MODEL // d + n)),
            pl.BlockSpec((DEC_SEQ, d), lambda b, n, t: (b, n)),
            ctx_spec, ctx_spec,
        ],
        out_specs=pl.BlockSpec((tq, gw), lambda b, n, t: (b * per + t, n)),
        scratch_shapes=[pltpu.VMEM((B_GROUP * tq, PAST_LEN + DEC_SEQ), F32),
                        pltpu.VMEM((B_GROUP * tq, PAST_LEN + DEC_SEQ), BF16)],
        compiler_params=_params("arbitrary", "arbitrary", "arbitrary"),
        name="gqa_attn_latent",
    )(qk, qk, v, ck, cv)


def _dft_tables(s):
    def cs(n):
        idx = (np.arange(n)[:, None] * np.arange(n)[None, :]) % n
        ang = 2.0 * np.pi * idx / n
        return np.cos(ang), np.sin(ang)
    cc, sc = cs(C_GROUP_DIM)
    c_s, s_s = cs(s)
    t_chan = np.concatenate([cc, sc], axis=1).astype(np.float32)
    t_pos = np.concatenate([c_s, -s_s], axis=1).astype(np.float32)
    return jnp.asarray(t_chan).astype(BF16), jnp.asarray(t_pos).astype(BF16)


def _dft_group(x, t_chan, t_pos, norm):
    xcs = jnp.dot(x, t_chan, preferred_element_type=F32).astype(BF16)
    stacked = jnp.concatenate([xcs[:, :C_GROUP_DIM], xcs[:, C_GROUP_DIM:]], axis=0)
    y = jnp.dot(t_pos, stacked, preferred_element_type=F32)
    return (y * norm).astype(BF16)


def _dft_kernel(u_ref, tc_ref, tp_ref, o_ref, *, groups, norm):
    tc, tp = tc_ref[...], tp_ref[...]
    for g in range(groups):
        sl = slice(g * C_GROUP_DIM, (g + 1) * C_GROUP_DIM)
        o_ref[:, sl] = _dft_group(u_ref[:, sl], tc, tp, norm)


def _dft(u, s, batch, row0, groups_per_step):
    t_chan, t_pos = _dft_tables(s)
    gw = groups_per_step * C_GROUP_DIM
    rb0 = row0 // s
    return pl.pallas_call(
        functools.partial(_dft_kernel, groups=groups_per_step,
                          norm=1.0 / math.sqrt(s * C_GROUP_DIM)),
        out_shape=jax.ShapeDtypeStruct((batch * s, D_MODEL), BF16),
        grid=(batch, C_GROUPS // groups_per_step),
        in_specs=[
            pl.BlockSpec((s, gw), lambda b, g: (rb0 + b, g)),
            pl.BlockSpec(t_chan.shape, lambda b, g: (0, 0)),
            pl.BlockSpec(t_pos.shape, lambda b, g: (0, 0)),
        ],
        out_specs=pl.BlockSpec((s, gw), lambda b, g: (b, g)),
        compiler_params=_params("arbitrary", "arbitrary"),
        name="dft",
    )(u, t_chan, t_pos)


ROW_GROUPS = 2


def _post_norm(y, g, b):
    mu = jnp.mean(y, axis=-1, keepdims=True)
    yc = y - mu
    var = jnp.mean(yc * yc, axis=-1, keepdims=True)
    return yc * lax.rsqrt(var + LN_EPS) * g + b


def _oproj_kernel(*refs, split_x):
    ap_ref, as_ref, w_ref = refs[:3]
    x_refs = refs[3:5] if split_x else refs[3:4]
    gate_ref, g_ref, b_ref, sc_ref, sh_ref, x1_ref, u_ref = refs[3 + len(x_refs):]
    is_prompt = pl.program_id(0) * ROW_TILE < M_P
    gate, g, b = gate_ref[...], g_ref[...], b_ref[...]
    sc1, sh = 1.0 + sc_ref[...], sh_ref[...]
    rows = ROW_TILE // ROW_GROUPS
    for r in range(ROW_GROUPS):
        sl = slice(r * rows, (r + 1) * rows)
        a = jnp.where(is_prompt, ap_ref[sl, :], as_ref[sl, :])
        mixed = jnp.dot(a, w_ref[...], preferred_element_type=F32)
        if split_x:
            x = jnp.where(is_prompt, x_refs[0][sl, :], x_refs[1][sl, :])
        else:
            x = x_refs[0][sl, :]
        x1 = _post_norm(ALPHA * x + gate * mixed, g, b)
        x1_ref[sl, :] = x1
        u_ref[sl, :] = (x1 * sc1 + sh).astype(BF16)


def _oproj(a_p, a_s, w_bf, w_layer, x, mod, ln_g, ln_b, l):
    tm = ROW_TILE
    row = pl.BlockSpec((tm, D_MODEL), lambda i: (i, 0))
    split_x = isinstance(x, tuple)
    xs = list(x) if split_x else [x]
    x_specs = [_prompt_rows_spec(tm), _latent_rows_spec(tm)] if split_x else [row]
    return pl.pallas_call(
        functools.partial(_oproj_kernel, split_x=split_x),
        out_shape=(jax.ShapeDtypeStruct((M_ALL, D_MODEL), F32),
                   jax.ShapeDtypeStruct((M_ALL, D_MODEL), BF16)),
        grid=(M_ALL // tm,),
        in_specs=[
            _prompt_rows_spec(tm), _latent_rows_spec(tm),
            pl.BlockSpec((None, D_MODEL, D_MODEL), lambda i: (w_layer, 0, 0),
                         pipeline_mode=pl.Buffered(1)),
            *x_specs,
            _mod_spec(l, 2, tm),
            _vec_spec(l, 0), _vec_spec(l, 0),
            _mod_spec(l, 4, tm),
            _mod_spec(l, 3, tm),
        ],
        out_specs=(row, row),
        compiler_params=_params("arbitrary"),
        name="oproj_ln",
    )(a_p, a_s, w_bf, *xs, mod, ln_g, ln_b, mod, mod)


def _mlp_kernel(*refs, n_tiles, n_ff, last):
    u_ref, wu_ref, wd_ref, x_ref, gate_ref, g_ref, b_ref = refs[:7]
    acc, prev = refs[-2:]
    i, f = pl.program_id(0), pl.program_id(1)
    tm = u_ref.shape[0]
    close_rows = tm // n_ff

    @pl.when(jnp.logical_and(f == 0, i > 0))
    def _():
        prev[...] = acc[...]

    @pl.when(f == 0)
    def _():
        acc[...] = jnp.zeros_like(acc)

    def matmuls():
        wu = wu_ref[...].astype(BF16)
        wd = wd_ref[...].astype(BF16)
        rows = tm // ROW_GROUPS
        for r in range(ROW_GROUPS):
            sl = slice(r * rows, (r + 1) * rows)
            h = jnp.dot(u_ref[sl, :], wu, preferred_element_type=F32)
            h = jnp.maximum(h, 0.0)
            h = (h * h).astype(BF16)
            acc[sl, :] += jnp.dot(h, wd, preferred_element_type=F32)

    def close():
        rows = pl.ds(pl.multiple_of(f * close_rows, close_rows), close_rows)
        y = _post_norm(ALPHA * x_ref[...] + gate_ref[...] * prev[rows, :], g_ref[...], b_ref[...])
        if last:
            yp_ref, ys_ref = refs[7:9]
            prev_is_prompt = (i - 1) * tm < M_P

            @pl.when(prev_is_prompt)
            def _():
                yp_ref[...] = y

            @pl.when(jnp.logical_not(prev_is_prompt))
            def _():
                ys_ref[...] = y
        else:
            sc_ref, sh_ref, x2_ref, un_ref = refs[7:11]
            x2_ref[...] = y
            un_ref[...] = (y * (1.0 + sc_ref[...]) + sh_ref[...]).astype(BF16)

    @pl.when(i == 0)
    def _():
        matmuls()

    @pl.when(jnp.logical_and(i > 0, i < n_tiles))
    def _():
        matmuls()
        close()

    @pl.when(i == n_tiles)
    def _():
        close()


def _mlp(u, w_up, w_down, x, mod, ln_g, ln_b, l, tm=1024, tf=512):
    last = l + 1 == DEPTH
    n_tiles, n_ff = M_ALL // tm, D_FF // tf
    close_rows = tm // n_ff

    def close_block(i, f):
        return jnp.where(i == 0, 0, (i - 1) * n_ff + f)

    def mod_prev(layer, which):
        def idx(i, f):
            return (layer, _cond_row(jnp.maximum(i - 1, 0), tm), which, 0, 0)
        return pl.BlockSpec((None, None, None, 1, D_MODEL), idx)

    def ff_block(i, f):
        return jnp.where(i < n_tiles, f, n_ff - 1)

    row = pl.BlockSpec((close_rows, D_MODEL), lambda i, f: (close_block(i, f), 0))
    ins = [u, w_up, w_down, x, mod, ln_g, ln_b]
    in_specs = [
        pl.BlockSpec((tm, D_MODEL), lambda i, f: (jnp.minimum(i, n_tiles - 1), 0)),
        pl.BlockSpec((None, D_MODEL, tf), lambda i, f: (l, 0, ff_block(i, f))),
        pl.BlockSpec((None, tf, D_MODEL), lambda i, f: (l, ff_block(i, f), 0)),
        row, mod_prev(l, 5), _vec_spec(l, 1), _vec_spec(l, 1),
    ]
    if last:
        n_p = M_P // close_rows
        out_shape = (jax.ShapeDtypeStruct((M_P, D_MODEL), F32),
                     jax.ShapeDtypeStruct((M_S, D_MODEL), F32))
        out_specs = (
            pl.BlockSpec((close_rows, D_MODEL),
                         lambda i, f: (jnp.minimum(close_block(i, f), n_p - 1), 0)),
            pl.BlockSpec((close_rows, D_MODEL),
                         lambda i, f: (jnp.maximum(close_block(i, f) - n_p, 0), 0)),
        )
    else:
        ins += [mod, mod]
        in_specs += [mod_prev(l + 1, 1), mod_prev(l + 1, 0)]
        out_shape = (jax.ShapeDtypeStruct((M_ALL, D_MODEL), F32),
                     jax.ShapeDtypeStruct((M_ALL, D_MODEL), BF16))
        out_specs = (row, row)
    return pl.pallas_call(
        functools.partial(_mlp_kernel, n_tiles=n_tiles, n_ff=n_ff, last=last),
        out_shape=out_shape,
        grid=(n_tiles + 1, n_ff),
        in_specs=in_specs,
        out_specs=out_specs,
        scratch_shapes=[pltpu.VMEM((tm, D_MODEL), F32), pltpu.VMEM((tm, D_MODEL), F32)],
        compiler_params=_params("arbitrary", "arbitrary"),
        name="mlp",
    )(*ins)


def _rope_tables():
    n_freq = A_HEAD_DIM // 4
    pos = np.arange(DEC_SEQ)
    row = (pos // GRID_W).astype(np.float32)
    col = (pos % GRID_W).astype(np.float32)
    inv_freq = (ROPE_BASE ** (-np.arange(n_freq, dtype=np.float32) / n_freq)).astype(np.float32)
    ar = row[:, None] * inv_freq
    ac = col[:, None] * inv_freq
    cr, sr, cc, sc = np.cos(ar), np.sin(ar), np.cos(ac), np.sin(ac)
    z = np.zeros_like(sr)
    cos = np.concatenate([cr, cr, cc, cc], axis=1)
    sin_a = np.concatenate([-sr, z, -sc, z], axis=1)
    sin_b = np.concatenate([z, sr, z, sc], axis=1)
    return tuple(jnp.asarray(t.astype(np.float32)) for t in (cos, sin_a, sin_b))


def kernel(x_prompt, x_sample, cache_a_k, cache_a_v, cache_b_k, cache_b_v, c, c_ctx, w_mod, b_mod, ln_g, ln_b, w_up, w_down, a_w_qkv, a_w_o, a_lambda, a_subln_g, b_w_qkv, b_w_o, b_q_norm_g, b_k_norm_g, c_w_f):
    n_a = a_w_qkv.shape[0]
    n_b = b_w_qkv.shape[0]
    cond = jnp.zeros((COND_PAD, D_MODEL), F32).at[0].set(c_ctx).at[1:N_COND].set(c)
    mod = _modulation(cond, w_mod, b_mod).reshape(DEPTH, COND_PAD, N_MOD, 1, D_MODEL)
    ln_g4 = ln_g.reshape(DEPTH, 2, 1, D_MODEL)
    ln_b4 = ln_b.reshape(DEPTH, 2, 1, D_MODEL)
    rope = _rope_tables()
    w_o_bf = (_cast_bf16(a_w_o), _cast_bf16(b_w_o), _cast_bf16(c_w_f))

    ck_a = cache_a_k.reshape(DEC_BATCH, n_a, PAST_LEN * A_CHUNKS, LANE)
    cv_a = _a_v_rows(cache_a_v, DEC_BATCH, PAST_LEN)
    ck_b = cache_b_k.reshape(DEC_BATCH, n_b, PAST_LEN * B_CHUNKS, LANE)
    cv_b = cache_b_v.reshape(DEC_BATCH, n_b, PAST_LEN * B_CHUNKS, LANE)
    subln = a_subln_g.reshape(n_a, 1, 2 * A_HEAD_DIM)
    b_gains = jnp.stack([b_q_norm_g, b_k_norm_g], axis=1).reshape(n_b, 2, 1, B_HEAD_DIM)

    x = (x_prompt.reshape(M_P, D_MODEL), x_sample.reshape(M_S, D_MODEL))
    u = _embed(*x, mod)

    a_kv, b_kv = [], []
    for l in range(DEPTH):
        kind, j = l % N_MIXERS, l // N_MIXERS
        if kind == 0:
            lam_init = 0.8 - 0.6 * math.exp(-0.3 * l)
            q_p = _proj(u, a_w_qkv, j, 0, D_MODEL, 0, M_P, out_dtype=BF16, name="a_q_prompt")
            kv_p = _proj(u, a_w_qkv, j, D_MODEL, 2 * D_MODEL, 0, M_P, out_dtype=F32,
                         name="a_kv_prompt")
            qk_s = _proj(u, a_w_qkv, j, 0, 2 * D_MODEL, M_P, M_S, out_dtype=BF16, rope=rope,
                         name="a_qk_latent")
            v_s = _proj(u, a_w_qkv, j, 2 * D_MODEL, D_MODEL, M_P, M_S, out_dtype=BF16,
                        name="a_v_latent")
            if j + 1 < n_a:
                att_p = _diff_prompt(a_lambda, subln, q_p, kv_p, j, lam_init)
                a_kv.append(kv_p)
            else:
                att_p, new_a_k, new_a_v = _diff_prompt(a_lambda, subln, q_p, kv_p, j, lam_init,
                                                       prev_kv=a_kv)
            att_s = _diff_latent(a_lambda, subln, qk_s, v_s, ck_a, cv_a, j, lam_init)
            w_o = w_o_bf[0]
        elif kind == 1:
            gj = b_gains[j]
            q_p = _proj(u, b_w_qkv, j, 0, D_MODEL, 0, M_P, out_dtype=BF16, gains=gj,
                        gain_split=D_MODEL, name="b_q_prompt")
            k_p = _proj(u, b_w_qkv, j, D_MODEL, B_KV, 0, M_P, out_dtype=F32, tn=B_KV,
                        gains=gj[1:], gain_split=B_KV, name="b_k_prompt")
            v_p = _proj(u, b_w_qkv, j, D_MODEL + B_KV, B_KV, 0, M_P, out_dtype=F32, tn=B_KV,
                        name="b_v_prompt")
            qk_s = _proj(u, b_w_qkv, j, 0, D_MODEL + B_KV, M_P, M_S, out_dtype=BF16, tn=B_KV,
                         gains=gj, gain_split=D_MODEL, rope=rope, name="b_qk_latent")
            v_s = _proj(u, b_w_qkv, j, D_MODEL + B_KV, B_KV, M_P, M_S, out_dtype=BF16, tn=B_KV,
                        name="b_v_latent")
            if j + 1 < n_b:
                att_p = _gqa_prompt(q_p, k_p, v_p)
                b_kv.append((k_p, v_p))
            else:
                att_p, new_b_k, new_b_v = _gqa_prompt(q_p, k_p, v_p, prev_kv=b_kv)
            att_s = _gqa_latent(qk_s, v_s, ck_b, cv_b, j)
            w_o = w_o_bf[1]
        else:
            att_p = _dft(u, SEQ, BATCH, 0, C_GROUPS)
            att_s = _dft(u, DEC_SEQ, DEC_BATCH, M_P, 1)
            w_o = w_o_bf[2]
        x, u = _oproj(att_p, att_s, w_o, j, x, mod, ln_g4, ln_b4, l)
        x, u = _mlp(u, w_up, w_down, x, mod, ln_g4, ln_b4, l)

    y_p, y_s = x, u
    return (y_p.reshape(BATCH, SEQ, D_MODEL), y_s.reshape(DEC_BATCH, DEC_SEQ, D_MODEL),
            new_a_k.reshape(BATCH, n_a, SEQ, A_HEADS, 2, A_HEAD_DIM),
            _a_v_from_rows(new_a_v, BATCH, SEQ),
            new_b_k.reshape(BATCH, n_b, SEQ, B_KV_HEADS, B_HEAD_DIM),
            new_b_v.reshape(BATCH, n_b, SEQ, B_KV_HEADS, B_HEAD_DIM))
```

```python
import functools
import math

import jax
import jax.numpy as jnp
import numpy as np
from jax import lax
from jax.experimental import pallas as pl
from jax.experimental.pallas import tpu as pltpu

D_MODEL = 2048
BATCH = 16
SEQ = 256
DEPTH = 4
DEC_BATCH = 2
DEC_SEQ = 1024
PAST_LEN = 512
GRID_W = 64
N_MIXERS = 3
A_HEAD_DIM = 128
A_HEADS = D_MODEL // (2 * A_HEAD_DIM)
B_HEAD_DIM = 128
B_HEADS = D_MODEL // B_HEAD_DIM
B_KV_HEADS = B_HEADS // 4
B_GROUP = B_HEADS // B_KV_HEADS
B_KV = B_KV_HEADS * B_HEAD_DIM
C_GROUPS = 4
C_GROUP_DIM = D_MODEL // C_GROUPS
D_FF = 4 * D_MODEL
ROPE_BASE = 10000.0
ALPHA = (2 * DEPTH) ** 0.25
LN_EPS = 1e-5
RMS_EPS = 1e-6
N_MOD = 6
LOG2E = 1.4426950408889634

M_P = BATCH * SEQ
M_S = DEC_BATCH * DEC_SEQ
M_ALL = M_P + M_S
N_COND = 1 + DEC_BATCH
COND_PAD = 8

LANE = 128
VMEM_LIMIT = 58 * 1024 * 1024
ROW_TILE = 512

BF16 = jnp.bfloat16
F32 = jnp.float32


def _params(*sem, vmem=VMEM_LIMIT):
    return pltpu.CompilerParams(dimension_semantics=sem, vmem_limit_bytes=vmem)


def _cond_row(i, tm):
    start = i * tm
    return jnp.where(start < M_P, 0, 1 + (start - M_P) // DEC_SEQ)


def _mod_spec(l, which, tm):
    def idx(*g):
        return (l, _cond_row(g[0], tm), which, 0, 0)
    return pl.BlockSpec((None, None, None, 1, D_MODEL), idx)


def _vec_spec(l, which):
    return pl.BlockSpec((None, None, 1, D_MODEL), lambda *g: (l, which, 0, 0))


def _prompt_rows_spec(tm):
    last = M_P // tm - 1
    return pl.BlockSpec((tm, D_MODEL), lambda i: (jnp.minimum(i, last), 0))


def _latent_rows_spec(tm):
    first = M_P // tm
    return pl.BlockSpec((tm, D_MODEL), lambda i: (jnp.maximum(i - first, 0), 0))


def _mod_kernel(c_ref, w_ref, b_ref, o_ref):
    c = c_ref[...]
    s = c * (1.0 / (1.0 + jnp.exp(-c)))
    o_ref[...] = jnp.dot(s.astype(BF16), w_ref[...].astype(BF16),
                         preferred_element_type=F32) + b_ref[...]


def _modulation(cond, w_mod, b_mod):
    tn = 1024
    n_out = N_MOD * D_MODEL
    return pl.pallas_call(
        _mod_kernel,
        out_shape=jax.ShapeDtypeStruct((DEPTH, COND_PAD, n_out), F32),
        grid=(DEPTH, n_out // tn),
        in_specs=[
            pl.BlockSpec((COND_PAD, D_MODEL), lambda l, n: (0, 0)),
            pl.BlockSpec((None, D_MODEL, tn), lambda l, n: (l, 0, n)),
            pl.BlockSpec((None, 1, tn), lambda l, n: (l, 0, n)),
        ],
        out_specs=pl.BlockSpec((None, COND_PAD, tn), lambda l, n: (l, 0, n)),
        compiler_params=_params("arbitrary", "arbitrary"),
        name="modulation",
    )(cond, w_mod, b_mod.reshape(DEPTH, 1, n_out))


def _cast_kernel(w_ref, o_ref):
    o_ref[...] = w_ref[...].astype(BF16)


def _cast_bf16(w, rows=512):
    n_l, k, n = w.shape
    spec = pl.BlockSpec((None, rows, n), lambda l, r: (l, r, 0))
    return pl.pallas_call(
        _cast_kernel,
        out_shape=jax.ShapeDtypeStruct(w.shape, BF16),
        grid=(n_l, k // rows),
        in_specs=[spec],
        out_specs=spec,
        compiler_params=_params("arbitrary", "arbitrary"),
        name="cast_weight",
    )(w)


def _embed_kernel(xp_ref, xs_ref, sc_ref, sh_ref, u_ref):
    x = jnp.where(pl.program_id(0) * ROW_TILE < M_P, xp_ref[...], xs_ref[...])
    u_ref[...] = (x * (1.0 + sc_ref[...]) + sh_ref[...]).astype(BF16)


def _embed(xp, xs, mod):
    tm = ROW_TILE
    return pl.pallas_call(
        _embed_kernel,
        out_shape=jax.ShapeDtypeStruct((M_ALL, D_MODEL), BF16),
        grid=(M_ALL // tm,),
        in_specs=[_prompt_rows_spec(tm), _latent_rows_spec(tm),
                  _mod_spec(0, 1, tm), _mod_spec(0, 0, tm)],
        out_specs=pl.BlockSpec((tm, D_MODEL), lambda i: (i, 0)),
        compiler_params=_params("arbitrary"),
        name="embed",
    )(xp, xs, mod, mod)


def _rms_chunk(x, g):
    ms = jnp.mean(x * x, axis=-1, keepdims=True)
    return x * lax.rsqrt(ms + RMS_EPS) * g


def _rope_chunk(x, cos, sin_a, sin_b):
    return x * cos + pltpu.roll(x, LANE - 32, 1) * sin_a + pltpu.roll(x, 32, 1) * sin_b


def _proj_kernel(*refs, tn, rms, rope):
    it = iter(refs)
    u_ref, w_ref = next(it), next(it)
    g_ref = next(it) if rms else None
    tabs = (next(it), next(it), next(it)) if rope else None
    o_ref, wbf = next(it), next(it)

    @pl.when(pl.program_id(1) == 0)
    def _():
        wbf[...] = w_ref[...].astype(BF16)

    acc = jnp.dot(u_ref[...], wbf[...], preferred_element_type=F32)
    chunks = [acc[:, c * LANE:(c + 1) * LANE] for c in range(tn // LANE)]
    if rms and rope:
        g = g_ref[...]
        tab = tuple(t[...] for t in tabs)
        rinv = [lax.rsqrt(jnp.mean(x * x, axis=-1, keepdims=True) + RMS_EPS) for x in chunks]

        @pl.when(pl.program_id(1) >= 0)
        def _():
            for c, (x, r) in enumerate(zip(chunks, rinv)):
                o_ref[:, c * LANE:(c + 1) * LANE] = (_rope_chunk(x * g, *tab) * r).astype(o_ref.dtype)
    elif rms:
        g = g_ref[...]
        for c, x in enumerate(chunks):
            o_ref[:, c * LANE:(c + 1) * LANE] = _rms_chunk(x, g).astype(o_ref.dtype)
    elif rope:
        tab = tuple(t[...] for t in tabs)
        for c, x in enumerate(chunks):
            o_ref[:, c * LANE:(c + 1) * LANE] = _rope_chunk(x, *tab).astype(o_ref.dtype)
    else:
        o_ref[...] = acc.astype(o_ref.dtype)


def _proj(u, w, w_layer, col0, ncols, m0, m_rows, *, out_dtype, tn=1024, tm=1024,
          gains=None, gain_split=None, rope=None, name="proj"):
    k = u.shape[1]
    assert col0 % tn == 0 and ncols % tn == 0 and m0 % tm == 0 and m_rows % tm == 0
    n_t, m_t, mt0, nt0 = ncols // tn, m_rows // tm, m0 // tm, col0 // tn
    ins = [u, w]
    in_specs = [
        pl.BlockSpec((tm, k), lambda n, m: (mt0 + m, 0)),
        pl.BlockSpec((None, k, tn), lambda n, m: (w_layer, 0, nt0 + n)),
    ]
    if gains is not None:
        ins.append(gains)
        in_specs.append(pl.BlockSpec(
            (None, 1, LANE), lambda n, m: (jnp.where(n * tn < gain_split, 0, 1), 0, 0)))
    if rope is not None:
        assert m0 >= M_P
        per = DEC_SEQ // tm
        for t in rope:
            ins.append(t)
            in_specs.append(pl.BlockSpec((tm, LANE), lambda n, m: ((mt0 + m) % per, 0)))
    kern = functools.partial(_proj_kernel, tn=tn, rms=gains is not None, rope=rope is not None)
    return pl.pallas_call(
        kern,
        out_shape=jax.ShapeDtypeStruct((m_rows, ncols), out_dtype),
        grid=(n_t, m_t),
        in_specs=in_specs,
        out_specs=pl.BlockSpec((tm, tn), lambda n, m: (m, n)),
        scratch_shapes=[pltpu.VMEM((k, tn), BF16)],
        compiler_params=_params("arbitrary", "arbitrary"),
        name=name,
    )(*ins)


def _qkt(q, k):
    return lax.dot_general(q, k, (((1,), (1,)), ((), ())), preferred_element_type=F32)


def _softmax_terms(s, scale):
    m = s.max(axis=-1, keepdims=True)
    e = jnp.exp2((s - m) * (scale * LOG2E))
    return e, 1.0 / e.sum(axis=-1, keepdims=True)


def _diff_lambda(lp_ref, lam_init):
    lp = lp_ref[...]
    s1 = jnp.sum(lp[0:1] * lp[1:2], axis=-1, keepdims=True)
    s2 = jnp.sum(lp[2:3] * lp[3:4], axis=-1, keepdims=True)
    return jnp.exp(s1) - jnp.exp(s2) + lam_init


def _diff_weights(s, lam):
    half = s.shape[0] // 2
    e, r = _softmax_terms(s, A_HEAD_DIM ** -0.5)
    return (e[:half] * r[:half] - e[half:] * (lam * r[half:])).astype(BF16)


def _sub_norm(o, gain):
    ms = jnp.mean(o * o, axis=-1, keepdims=True)
    return (o * lax.rsqrt(ms + RMS_EPS) * gain).astype(BF16)


A_CHUNKS = D_MODEL // LANE
B_CHUNKS = B_KV // LANE


def _a_v_row(c):
    return (c % 2) * A_HEADS + c // 2


def _a_v_rows(v, batch, seq):
    n_l = v.shape[1]
    v = v.reshape(batch, n_l, seq, A_HEADS, 2, LANE).transpose(0, 1, 2, 4, 3, 5)
    return v.reshape(batch, n_l, seq * A_CHUNKS, LANE)


def _a_v_from_rows(r, batch, seq):
    n_l = r.shape[1]
    v = r.reshape(batch, n_l, seq, 2, A_HEADS, LANE).transpose(0, 1, 2, 4, 3, 5)
    return v.reshape(batch, n_l, seq, A_HEADS, 2 * LANE)


def _diff_prompt_kernel(*refs, lam_init, n_prev, emit):
    lp_ref, g_ref, q_ref, k_ref, v_ref = refs[:5]
    prev = refs[5:5 + n_prev]
    o_ref = refs[5 + n_prev]
    s_scr, a_scr, o_scr = refs[-3:]
    lam = _diff_lambda(lp_ref, lam_init)
    gain = g_ref[...] * (1.0 - lam_init)
    w = 2 * A_HEAD_DIM
    for h in range(A_HEADS):
        for p in range(2):
            cols = slice(h * w + p * A_HEAD_DIM, h * w + (p + 1) * A_HEAD_DIM)
            s_scr[(p * A_HEADS + h) * SEQ:(p * A_HEADS + h + 1) * SEQ, :] = _qkt(
                q_ref[:, cols], k_ref[:, cols].astype(BF16))

    @pl.when(pl.program_id(0) >= 0)
    def _():
        a_scr[...] = _diff_weights(s_scr[...], lam)
        for h in range(A_HEADS):
            rows = slice(h * SEQ, (h + 1) * SEQ)
            o_scr[rows, :] = jnp.dot(a_scr[rows, :], v_ref[:, h * w:(h + 1) * w].astype(BF16),
                                     preferred_element_type=F32)
        y = _sub_norm(o_scr[...], gain)
        for h in range(A_HEADS):
            o_ref[:, h * w:(h + 1) * w] = y[h * SEQ:(h + 1) * SEQ, :]

    if emit:
        nk_ref, nv_ref = refs[6 + n_prev], refs[7 + n_prev]
        layers = [(p, 0, p, D_MODEL) for p in prev] + [(k_ref, 0, v_ref, 0)]
        for jj, (kr, k0, vr, v0) in enumerate(layers):
            for c in range(A_CHUNKS):
                nk_ref[jj, pl.ds(c, SEQ, stride=A_CHUNKS), :] = kr[:, k0 + c * LANE:k0 + (c + 1) * LANE]
                nv_ref[jj, pl.ds(_a_v_row(c), SEQ, stride=A_CHUNKS), :] = (
                    vr[:, v0 + c * LANE:v0 + (c + 1) * LANE])


def _diff_prompt(lp, g, q, kv, j, lam_init, prev_kv=None):
    emit = prev_kv is not None
    prev_kv = list(prev_kv or [])
    n_layers = len(prev_kv) + 1
    att_shape = jax.ShapeDtypeStruct((M_P, D_MODEL), BF16)
    att_spec = pl.BlockSpec((SEQ, D_MODEL), lambda b: (b, 0))
    if emit:
        new_shape = jax.ShapeDtypeStruct((BATCH, n_layers, SEQ * A_CHUNKS, LANE), F32)
        new_spec = pl.BlockSpec((None, n_layers, SEQ * A_CHUNKS, LANE), lambda b: (b, 0, 0, 0))
        out_shape, out_specs = (att_shape, new_shape, new_shape), (att_spec, new_spec, new_spec)
    else:
        out_shape, out_specs = att_shape, att_spec
    return pl.pallas_call(
        functools.partial(_diff_prompt_kernel, lam_init=lam_init, n_prev=len(prev_kv), emit=emit),
        out_shape=out_shape,
        grid=(BATCH,),
        in_specs=[
            pl.BlockSpec((None, 4, A_HEAD_DIM), lambda b: (j, 0, 0)),
            pl.BlockSpec((None, 1, 2 * A_HEAD_DIM), lambda b: (j, 0, 0)),
            pl.BlockSpec((SEQ, D_MODEL), lambda b: (b, 0)),
            pl.BlockSpec((SEQ, D_MODEL), lambda b: (b, 0)),
            pl.BlockSpec((SEQ, D_MODEL), lambda b: (b, 1)),
        ] + [pl.BlockSpec((SEQ, 2 * D_MODEL), lambda b: (b, 0)) for _ in prev_kv],
        out_specs=out_specs,
        scratch_shapes=[pltpu.VMEM((2 * A_HEADS * SEQ, SEQ), F32),
                        pltpu.VMEM((A_HEADS * SEQ, SEQ), BF16),
                        pltpu.VMEM((A_HEADS * SEQ, 2 * A_HEAD_DIM), F32)],
        compiler_params=_params("arbitrary"),
        name="diff_attn_prompt",
    )(lp, g, q, kv, kv, *prev_kv)


def _diff_latent_kernel(lp_ref, g_ref, q_ref, k_ref, v_ref, ck_ref, cv_ref, o_ref,
                        s_scr, a_scr, *, lam_init):
    lam = _diff_lambda(lp_ref, lam_init)
    gain = g_ref[...] * (1.0 - lam_init)
    h = pl.program_id(1)
    tq, d = q_ref.shape[0], A_HEAD_DIM

    def cached(ref, r0, r1):
        halves = [ref[pl.ds(r, PAST_LEN, stride=A_CHUNKS), :] for r in (r0, r1)]
        return jnp.concatenate(halves, axis=1).astype(BF16)

    ck = cached(ck_ref, 2 * h, 2 * h + 1)
    for p in range(2):
        q = q_ref[:, p * d:(p + 1) * d]
        s_scr[p * tq:(p + 1) * tq, :PAST_LEN] = _qkt(q, ck[:, p * d:(p + 1) * d])
        s_scr[p * tq:(p + 1) * tq, PAST_LEN:] = _qkt(q, k_ref[:, p * d:(p + 1) * d])

    @pl.when(h >= 0)
    def _():
        a_scr[...] = _diff_weights(s_scr[...], lam)
        cv = cached(cv_ref, h, A_HEADS + h)
        o = (jnp.dot(a_scr[:, :PAST_LEN], cv, preferred_element_type=F32)
             + jnp.dot(a_scr[:, PAST_LEN:], v_ref[...], preferred_element_type=F32))
        o_ref[...] = _sub_norm(o, gain)


def _diff_latent(lp, g, qk, v, ck, cv, j, lam_init, tq=512):
    w = 2 * A_HEAD_DIM
    per = DEC_SEQ // tq
    ctx_spec = pl.BlockSpec((None, None, PAST_LEN * A_CHUNKS, LANE), lambda b, h, t: (b, j, 0, 0))
    return pl.pallas_call(
        functools.partial(_diff_latent_kernel, lam_init=lam_init),
        out_shape=jax.ShapeDtypeStruct((M_S, D_MODEL), BF16),
        grid=(DEC_BATCH, A_HEADS, per),
        in_specs=[
            pl.BlockSpec((None, 4, A_HEAD_DIM), lambda b, h, t: (j, 0, 0)),
            pl.BlockSpec((None, 1, w), lambda b, h, t: (j, 0, 0)),
            pl.BlockSpec((tq, w), lambda b, h, t: (b * per + t, h)),
            pl.BlockSpec((DEC_SEQ, w), lambda b, h, t: (b, A_HEADS + h)),
            pl.BlockSpec((DEC_SEQ, w), lambda b, h, t: (b, h)),
            ctx_spec, ctx_spec,
        ],
        out_specs=pl.BlockSpec((tq, w), lambda b, h, t: (b * per + t, h)),
        scratch_shapes=[pltpu.VMEM((2 * tq, PAST_LEN + DEC_SEQ), F32),
                        pltpu.VMEM((tq, PAST_LEN + DEC_SEQ), BF16)],
        compiler_params=_params("arbitrary", "arbitrary", "arbitrary"),
        name="diff_attn_latent",
    )(lp, g, qk, qk, v, ck, cv)


def _gqa_prompt_kernel(*refs, n_prev, emit):
    q_ref, k_ref, v_ref = refs[:3]
    prev = refs[3:3 + 2 * n_prev]
    o_ref = refs[3 + 2 * n_prev]
    s_scr, a_scr, o_scr = refs[-3:]
    d = B_HEAD_DIM
    for n in range(B_KV_HEADS):
        k = k_ref[:, n * d:(n + 1) * d].astype(BF16)
        for g in range(B_GROUP):
            hd = n * B_GROUP + g
            s_scr[hd * SEQ:(hd + 1) * SEQ, :] = _qkt(q_ref[:, hd * d:(hd + 1) * d], k)

    @pl.when(pl.program_id(0) >= 0)
    def _():
        e, r = _softmax_terms(s_scr[...], B_HEAD_DIM ** -0.5)
        a_scr[...] = (e * r).astype(BF16)
        group_rows = B_GROUP * SEQ
        for n in range(B_KV_HEADS):
            rows = slice(n * group_rows, (n + 1) * group_rows)
            o_scr[rows, :] = jnp.dot(a_scr[rows, :], v_ref[:, n * d:(n + 1) * d].astype(BF16),
                                     preferred_element_type=F32)
        for hd in range(B_HEADS):
            o_ref[:, hd * d:(hd + 1) * d] = o_scr[hd * SEQ:(hd + 1) * SEQ, :].astype(BF16)

    if emit:
        nk_ref, nv_ref = refs[4 + 2 * n_prev], refs[5 + 2 * n_prev]
        layers = [(prev[2 * i], prev[2 * i + 1]) for i in range(n_prev)] + [(k_ref, v_ref)]
        for jj, (kr, vr) in enumerate(layers):
            for n in range(B_CHUNKS):
                rows = pl.ds(n, SEQ, stride=B_CHUNKS)
                nk_ref[jj, rows, :] = kr[:, n * LANE:(n + 1) * LANE]
                nv_ref[jj, rows, :] = vr[:, n * LANE:(n + 1) * LANE]


def _gqa_prompt(q, k, v, prev_kv=None):
    emit = prev_kv is not None
    prev = [a for pair in (prev_kv or []) for a in pair]
    n_layers = len(prev) // 2 + 1
    kv_spec = pl.BlockSpec((SEQ, B_KV), lambda b: (b, 0))
    att_shape = jax.ShapeDtypeStruct((M_P, D_MODEL), BF16)
    att_spec = pl.BlockSpec((SEQ, D_MODEL), lambda b: (b, 0))
    if emit:
        new_shape = jax.ShapeDtypeStruct((BATCH, n_layers, SEQ * B_CHUNKS, LANE), F32)
        new_spec = pl.BlockSpec((None, n_layers, SEQ * B_CHUNKS, LANE), lambda b: (b, 0, 0, 0))
        out_shape, out_specs = (att_shape, new_shape, new_shape), (att_spec, new_spec, new_spec)
    else:
        out_shape, out_specs = att_shape, att_spec
    return pl.pallas_call(
        functools.partial(_gqa_prompt_kernel, n_prev=len(prev) // 2, emit=emit),
        out_shape=out_shape,
        grid=(BATCH,),
        in_specs=[att_spec, kv_spec, kv_spec] + [kv_spec for _ in prev],
        out_specs=out_specs,
        scratch_shapes=[pltpu.VMEM((B_HEADS * SEQ, SEQ), F32),
                        pltpu.VMEM((B_HEADS * SEQ, SEQ), BF16),
                        pltpu.VMEM((B_HEADS * SEQ, B_HEAD_DIM), F32)],
        compiler_params=_params("arbitrary"),
        name="gqa_attn_prompt",
    )(q, k, v, *prev)


def _gqa_latent_kernel(q_ref, k_ref, v_ref, ck_ref, cv_ref, o_ref, s_scr, a_scr):
    d, tq = B_HEAD_DIM, q_ref.shape[0]
    n = pl.program_id(1)
    cached = pl.ds(n, PAST_LEN, stride=B_CHUNKS)
    ck = ck_ref[cached, :].astype(BF16)
    for g in range(B_GROUP):
        q = q_ref[:, g * d:(g + 1) * d]
        s_scr[g * tq:(g + 1) * tq, :PAST_LEN] = _qkt(q, ck)
        s_scr[g * tq:(g + 1) * tq, PAST_LEN:] = _qkt(q, k_ref[...])

    @pl.when(n >= 0)
    def _():
        e, r = _softmax_terms(s_scr[...], B_HEAD_DIM ** -0.5)
        a_scr[...] = (e * r).astype(BF16)
        o = (jnp.dot(a_scr[:, :PAST_LEN], cv_ref[cached, :].astype(BF16),
                     preferred_element_type=F32)
             + jnp.dot(a_scr[:, PAST_LEN:], v_ref[...], preferred_element_type=F32))
        for g in range(B_GROUP):
            o_ref[:, g * d:(g + 1) * d] = o[g * tq:(g + 1) * tq, :].astype(BF16)


def _gqa_latent(qk, v, ck, cv, j, tq=512):
    d = B_HEAD_DIM
    gw = B_GROUP * d
    per = DEC_SEQ // tq
    ctx_spec = pl.BlockSpec((None, None, PAST_LEN * B_CHUNKS, LANE), lambda b, n, t: (b, j, 0, 0))
    return pl.pallas_call(
        _gqa_latent_kernel,
        out_shape=jax.ShapeDtypeStruct((M_S, D_MODEL), BF16),
        grid=(DEC_BATCH, B_KV_HEADS, per),
        in_specs=[
            pl.BlockSpec((tq, gw), lambda b, n, t: (b * per + t, n)),
            pl.BlockSpec((DEC_SEQ, d), lambda b, n, t: (b, D_MODEL // d + n)),
            pl.BlockSpec((DEC_SEQ, d), lambda b, n, t: (b, n)),
            ctx_spec, ctx_spec,
        ],
        out_specs=pl.BlockSpec((tq, gw), lambda b, n, t: (b * per + t, n)),
        scratch_shapes=[pltpu.VMEM((B_GROUP * tq, PAST_LEN + DEC_SEQ), F32),
                        pltpu.VMEM((B_GROUP * tq, PAST_LEN + DEC_SEQ), BF16)],
        compiler_params=_params("arbitrary", "arbitrary", "arbitrary"),
        name="gqa_attn_latent",
    )(qk, qk, v, ck, cv)


def _dft_tables(s):
    def cs(n):
        idx = (np.arange(n)[:, None] * np.arange(n)[None, :]) % n
        ang = 2.0 * np.pi * idx / n
        return np.cos(ang), np.sin(ang)
    cc, sc = cs(C_GROUP_DIM)
    c_s, s_s = cs(s)
    t_chan = np.concatenate([cc, sc], axis=1).astype(np.float32)
    t_pos = np.concatenate([c_s, -s_s], axis=1).astype(np.float32)
    return jnp.asarray(t_chan).astype(BF16), jnp.asarray(t_pos).astype(BF16)


def _dft_group(x, t_chan, t_pos, norm):
    xcs = jnp.dot(x, t_chan, preferred_element_type=F32).astype(BF16)
    stacked = jnp.concatenate([xcs[:, :C_GROUP_DIM], xcs[:, C_GROUP_DIM:]], axis=0)
    y = jnp.dot(t_pos, stacked, preferred_element_type=F32)
    return (y * norm).astype(BF16)


def _dft_kernel(u_ref, tc_ref, tp_ref, o_ref, *, groups, norm):
    tc, tp = tc_ref[...], tp_ref[...]
    for g in range(groups):
        sl = slice(g * C_GROUP_DIM, (g + 1) * C_GROUP_DIM)
        o_ref[:, sl] = _dft_group(u_ref[:, sl], tc, tp, norm)


def _dft(u, s, batch, row0, groups_per_step):
    t_chan, t_pos = _dft_tables(s)
    gw = groups_per_step * C_GROUP_DIM
    rb0 = row0 // s
    return pl.pallas_call(
        functools.partial(_dft_kernel, groups=groups_per_step,
                          norm=1.0 / math.sqrt(s * C_GROUP_DIM)),
        out_shape=jax.ShapeDtypeStruct((batch * s, D_MODEL), BF16),
        grid=(batch, C_GROUPS // groups_per_step),
        in_specs=[
            pl.BlockSpec((s, gw), lambda b, g: (rb0 + b, g)),
            pl.BlockSpec(t_chan.shape, lambda b, g: (0, 0)),
            pl.BlockSpec(t_pos.shape, lambda b, g: (0, 0)),
        ],
        out_specs=pl.BlockSpec((s, gw), lambda b, g: (b, g)),
        compiler_params=_params("arbitrary", "arbitrary"),
        name="dft",
    )(u, t_chan, t_pos)


ROW_GROUPS = 2


def _post_norm(y, g, b):
    mu = jnp.mean(y, axis=-1, keepdims=True)
    yc = y - mu
    var = jnp.mean(yc * yc, axis=-1, keepdims=True)
    return yc * lax.rsqrt(var + LN_EPS) * g + b


def _oproj_kernel(*refs, split_x):
    ap_ref, as_ref, w_ref = refs[:3]
    x_refs = refs[3:5] if split_x else refs[3:4]
    gate_ref, g_ref, b_ref, sc_ref, sh_ref, x1_ref, u_ref = refs[3 + len(x_refs):]
    is_prompt = pl.program_id(0) * ROW_TILE < M_P
    gate, g, b = gate_ref[...], g_ref[...], b_ref[...]
    sc1, sh = 1.0 + sc_ref[...], sh_ref[...]
    rows = ROW_TILE // ROW_GROUPS
    for r in range(ROW_GROUPS):
        sl = slice(r * rows, (r + 1) * rows)
        a = jnp.where(is_prompt, ap_ref[sl, :], as_ref[sl, :])
        mixed = jnp.dot(a, w_ref[...], preferred_element_type=F32)
        if split_x:
            x = jnp.where(is_prompt, x_refs[0][sl, :], x_refs[1][sl, :])
        else:
            x = x_refs[0][sl, :]
        x1 = _post_norm(ALPHA * x + gate * mixed, g, b)
        x1_ref[sl, :] = x1
        u_ref[sl, :] = (x1 * sc1 + sh).astype(BF16)


def _oproj(a_p, a_s, w_bf, w_layer, x, mod, ln_g, ln_b, l):
    tm = ROW_TILE
    row = pl.BlockSpec((tm, D_MODEL), lambda i: (i, 0))
    split_x = isinstance(x, tuple)
    xs = list(x) if split_x else [x]
    x_specs = [_prompt_rows_spec(tm), _latent_rows_spec(tm)] if split_x else [row]
    return pl.pallas_call(
        functools.partial(_oproj_kernel, split_x=split_x),
        out_shape=(jax.ShapeDtypeStruct((M_ALL, D_MODEL), F32),
                   jax.ShapeDtypeStruct((M_ALL, D_MODEL), BF16)),
        grid=(M_ALL // tm,),
        in_specs=[
            _prompt_rows_spec(tm), _latent_rows_spec(tm),
            pl.BlockSpec((None, D_MODEL, D_MODEL), lambda i: (w_layer, 0, 0),
                         pipeline_mode=pl.Buffered(1)),
            *x_specs,
            _mod_spec(l, 2, tm),
            _vec_spec(l, 0), _vec_spec(l, 0),
            _mod_spec(l, 4, tm),
            _mod_spec(l, 3, tm),
        ],
        out_specs=(row, row),
        compiler_params=_params("arbitrary"),
        name="oproj_ln",
    )(a_p, a_s, w_bf, *xs, mod, ln_g, ln_b, mod, mod)


MLP_ROW_GROUPS = 1


def _mlp_kernel(*refs, n_tiles, n_ff, last):
    u_ref, wu_ref, wd_ref, x_ref, gate_ref, g_ref, b_ref = refs[:7]
    acc, prev = refs[-2:]
    i, f = pl.program_id(0), pl.program_id(1)
    tm = u_ref.shape[0]
    close_rows = tm // n_ff

    @pl.when(jnp.logical_and(f == 0, i > 0))
    def _():
        prev[...] = acc[...]

    @pl.when(f == 0)
    def _():
        acc[...] = jnp.zeros_like(acc)

    def matmuls():
        wu = wu_ref[...].astype(BF16)
        wd = wd_ref[...].astype(BF16)
        rows = tm // MLP_ROW_GROUPS
        for r in range(MLP_ROW_GROUPS):
            sl = slice(r * rows, (r + 1) * rows)
            h = jnp.dot(u_ref[sl, :], wu, preferred_element_type=F32)
            h = jnp.maximum(h, 0.0)
            h = (h * h).astype(BF16)
            acc[sl, :] += jnp.dot(h, wd, preferred_element_type=F32)

    def close():
        rows = pl.ds(pl.multiple_of(f * close_rows, close_rows), close_rows)
        y = _post_norm(ALPHA * x_ref[...] + gate_ref[...] * prev[rows, :], g_ref[...], b_ref[...])
        if last:
            yp_ref, ys_ref = refs[7:9]
            prev_is_prompt = (i - 1) * tm < M_P

            @pl.when(prev_is_prompt)
            def _():
                yp_ref[...] = y

            @pl.when(jnp.logical_not(prev_is_prompt))
            def _():
                ys_ref[...] = y
        else:
            sc_ref, sh_ref, x2_ref, un_ref = refs[7:11]
            x2_ref[...] = y
            un_ref[...] = (y * (1.0 + sc_ref[...]) + sh_ref[...]).astype(BF16)

    @pl.when(i == 0)
    def _():
        matmuls()

    @pl.when(jnp.logical_and(i > 0, i < n_tiles))
    def _():
        matmuls()
        close()

    @pl.when(i == n_tiles)
    def _():
        close()


def _mlp(u, w_up, w_down, x, mod, ln_g, ln_b, l, tm=1024, tf=512):
    last = l + 1 == DEPTH
    n_tiles, n_ff = M_ALL // tm, D_FF // tf
    close_rows = tm // n_ff

    def close_block(i, f):
        return jnp.where(i == 0, 0, (i - 1) * n_ff + f)

    def mod_prev(layer, which):
        def idx(i, f):
            return (layer, _cond_row(jnp.maximum(i - 1, 0), tm), which, 0, 0)
        return pl.BlockSpec((None, None, None, 1, D_MODEL), idx)

    def ff_block(i, f):
        return jnp.where(i < n_tiles, f, n_ff - 1)

    row = pl.BlockSpec((close_rows, D_MODEL), lambda i, f: (close_block(i, f), 0))
    ins = [u, w_up, w_down, x, mod, ln_g, ln_b]
    in_specs = [
        pl.BlockSpec((tm, D_MODEL), lambda i, f: (jnp.minimum(i, n_tiles - 1), 0)),
        pl.BlockSpec((None, D_MODEL, tf), lambda i, f: (l, 0, ff_block(i, f))),
        pl.BlockSpec((None, tf, D_MODEL), lambda i, f: (l, ff_block(i, f), 0)),
        row, mod_prev(l, 5), _vec_spec(l, 1), _vec_spec(l, 1),
    ]
    if last:
        n_p = M_P // close_rows
        out_shape = (jax.ShapeDtypeStruct((M_P, D_MODEL), F32),
                     jax.ShapeDtypeStruct((M_S, D_MODEL), F32))
        out_specs = (
            pl.BlockSpec((close_rows, D_MODEL),
                         lambda i, f: (jnp.minimum(close_block(i, f), n_p - 1), 0)),
            pl.BlockSpec((close_rows, D_MODEL),
                         lambda i, f: (jnp.maximum(close_block(i, f) - n_p, 0), 0)),
        )
    else:
        ins += [mod, mod]
        in_specs += [mod_prev(l + 1, 1), mod_prev(l + 1, 0)]
        out_shape = (jax.ShapeDtypeStruct((M_ALL, D_MODEL), F32),
                     jax.ShapeDtypeStruct((M_ALL, D_MODEL), BF16))
        out_specs = (row, row)
    return pl.pallas_call(
        functools.partial(_mlp_kernel, n_tiles=n_tiles, n_ff=n_ff, last=last),
        out_shape=out_shape,
        grid=(n_tiles + 1, n_ff),
        in_specs=in_specs,
        out_specs=out_specs,
        scratch_shapes=[pltpu.VMEM((tm, D_MODEL), F32), pltpu.VMEM((tm, D_MODEL), F32)],
        compiler_params=_params("arbitrary", "arbitrary"),
        name="mlp",
    )(*ins)


def _rope_tables():
    n_freq = A_HEAD_DIM // 4
    pos = np.arange(DEC_SEQ)
    row = (pos // GRID_W).astype(np.float32)
    col = (pos % GRID_W).astype(np.float32)
    inv_freq = (ROPE_BASE ** (-np.arange(n_freq, dtype=np.float32) / n_freq)).astype(np.float32)
    ar = row[:, None] * inv_freq
    ac = col[:, None] * inv_freq
    cr, sr, cc, sc = np.cos(ar), np.sin(ar), np.cos(ac), np.sin(ac)
    z = np.zeros_like(sr)
    cos = np.concatenate([cr, cr, cc, cc], axis=1)
    sin_a = np.concatenate([-sr, z, -sc, z], axis=1)
    sin_b = np.concatenate([z, sr, z, sc], axis=1)
    return tuple(jnp.asarray(t.astype(np.float32)) for t in (cos, sin_a, sin_b))


def kernel(x_prompt, x_sample, cache_a_k, cache_a_v, cache_b_k, cache_b_v, c, c_ctx, w_mod, b_mod, ln_g, ln_b, w_up, w_down, a_w_qkv, a_w_o, a_lambda, a_subln_g, b_w_qkv, b_w_o, b_q_norm_g, b_k_norm_g, c_w_f):
    n_a = a_w_qkv.shape[0]
    n_b = b_w_qkv.shape[0]
    cond = jnp.zeros((COND_PAD, D_MODEL), F32).at[0].set(c_ctx).at[1:N_COND].set(c)
    mod = _modulation(cond, w_mod, b_mod).reshape(DEPTH, COND_PAD, N_MOD, 1, D_MODEL)
    ln_g4 = ln_g.reshape(DEPTH, 2, 1, D_MODEL)
    ln_b4 = ln_b.reshape(DEPTH, 2, 1, D_MODEL)
    rope = _rope_tables()
    w_o_bf = (_cast_bf16(a_w_o), _cast_bf16(b_w_o), _cast_bf16(c_w_f))

    ck_a = cache_a_k.reshape(DEC_BATCH, n_a, PAST_LEN * A_CHUNKS, LANE)
    cv_a = _a_v_rows(cache_a_v, DEC_BATCH, PAST_LEN)
    ck_b = cache_b_k.reshape(DEC_BATCH, n_b, PAST_LEN * B_CHUNKS, LANE)
    cv_b = cache_b_v.reshape(DEC_BATCH, n_b, PAST_LEN * B_CHUNKS, LANE)
    subln = a_subln_g.reshape(n_a, 1, 2 * A_HEAD_DIM)
    b_gains = jnp.stack([b_q_norm_g, b_k_norm_g], axis=1).reshape(n_b, 2, 1, B_HEAD_DIM)

    x = (x_prompt.reshape(M_P, D_MODEL), x_sample.reshape(M_S, D_MODEL))
    u = _embed(*x, mod)

    a_kv, b_kv = [], []
    for l in range(DEPTH):
        kind, j = l % N_MIXERS, l // N_MIXERS
        if kind == 0:
            lam_init = 0.8 - 0.6 * math.exp(-0.3 * l)
            q_p = _proj(u, a_w_qkv, j, 0, D_MODEL, 0, M_P, out_dtype=BF16, name="a_q_prompt")
            kv_p = _proj(u, a_w_qkv, j, D_MODEL, 2 * D_MODEL, 0, M_P, out_dtype=F32,
                         name="a_kv_prompt")
            qk_s = _proj(u, a_w_qkv, j, 0, 2 * D_MODEL, M_P, M_S, out_dtype=BF16, rope=rope,
                         name="a_qk_latent")
            v_s = _proj(u, a_w_qkv, j, 2 * D_MODEL, D_MODEL, M_P, M_S, out_dtype=BF16,
                        name="a_v_latent")
            if j + 1 < n_a:
                att_p = _diff_prompt(a_lambda, subln, q_p, kv_p, j, lam_init)
                a_kv.append(kv_p)
            else:
                att_p, new_a_k, new_a_v = _diff_prompt(a_lambda, subln, q_p, kv_p, j, lam_init,
                                                       prev_kv=a_kv)
            att_s = _diff_latent(a_lambda, subln, qk_s, v_s, ck_a, cv_a, j, lam_init)
            w_o = w_o_bf[0]
        elif kind == 1:
            gj = b_gains[j]
            q_p = _proj(u, b_w_qkv, j, 0, D_MODEL, 0, M_P, out_dtype=BF16, gains=gj,
                        gain_split=D_MODEL, name="b_q_prompt")
            k_p = _proj(u, b_w_qkv, j, D_MODEL, B_KV, 0, M_P, out_dtype=F32, tn=B_KV,
                        gains=gj[1:], gain_split=B_KV, name="b_k_prompt")
            v_p = _proj(u, b_w_qkv, j, D_MODEL + B_KV, B_KV, 0, M_P, out_dtype=F32, tn=B_KV,
                        name="b_v_prompt")
            qk_s = _proj(u, b_w_qkv, j, 0, D_MODEL + B_KV, M_P, M_S, out_dtype=BF16, tn=B_KV,
                         gains=gj, gain_split=D_MODEL, rope=rope, name="b_qk_latent")
            v_s = _proj(u, b_w_qkv, j, D_MODEL + B_KV, B_KV, M_P, M_S, out_dtype=BF16, tn=B_KV,
                        name="b_v_latent")
            if j + 1 < n_b:
                att_p = _gqa_prompt(q_p, k_p, v_p)
                b_kv.append((k_p, v_p))
            else:
                att_p, new_b_k, new_b_v = _gqa_prompt(q_p, k_p, v_p, prev_kv=b_kv)
            att_s = _gqa_latent(qk_s, v_s, ck_b, cv_b, j)
            w_o = w_o_bf[1]
        else:
            att_p = _dft(u, SEQ, BATCH, 0, C_GROUPS)
            att_s = _dft(u, DEC_SEQ, DEC_BATCH, M_P, 1)
            w_o = w_o_bf[2]
        x, u = _oproj(att_p, att_s, w_o, j, x, mod, ln_g4, ln_b4, l)
        x, u = _mlp(u, w_up, w_down, x, mod, ln_g4, ln_b4, l)

    y_p, y_s = x, u
    return (y_p.reshape(BATCH, SEQ, D_MODEL), y_s.reshape(DEC_BATCH, DEC_SEQ, D_MODEL),
            new_a_k.reshape(BATCH, n_a, SEQ, A_HEADS, 2, A_HEAD_DIM),
            _a_v_from_rows(new_a_v, BATCH, SEQ),
            new_b_k.reshape(BATCH, n_b, SEQ, B_KV_HEADS, B_HEAD_DIM),
            new_b_v.reshape(BATCH, n_b, SEQ, B_KV_HEADS, B_HEAD_DIM))
```

```python
import functools
import math

import jax
import jax.numpy as jnp
import numpy as np
from jax import lax
from jax.experimental import pallas as pl
from jax.experimental.pallas import tpu as pltpu

D_MODEL = 2048
BATCH = 16
SEQ = 256
DEPTH = 4
DEC_BATCH = 2
DEC_SEQ = 1024
PAST_LEN = 512
GRID_W = 64
N_MIXERS = 3
A_HEAD_DIM = 128
A_HEADS = D_MODEL // (2 * A_HEAD_DIM)
B_HEAD_DIM = 128
B_HEADS = D_MODEL // B_HEAD_DIM
B_KV_HEADS = B_HEADS // 4
B_GROUP = B_HEADS // B_KV_HEADS
B_KV = B_KV_HEADS * B_HEAD_DIM
C_GROUPS = 4
C_GROUP_DIM = D_MODEL // C_GROUPS
D_FF = 4 * D_MODEL
ROPE_BASE = 10000.0
ALPHA = (2 * DEPTH) ** 0.25
LN_EPS = 1e-5
RMS_EPS = 1e-6
N_MOD = 6
LOG2E = 1.4426950408889634

M_P = BATCH * SEQ
M_S = DEC_BATCH * DEC_SEQ
M_ALL = M_P + M_S
N_COND = 1 + DEC_BATCH
COND_PAD = 8

LANE = 128
VMEM_LIMIT = 58 * 1024 * 1024
ROW_TILE = 512

BF16 = jnp.bfloat16
F32 = jnp.float32


def _params(*sem, vmem=VMEM_LIMIT):
    return pltpu.CompilerParams(dimension_semantics=sem, vmem_limit_bytes=vmem)


def _cond_row(i, tm):
    start = i * tm
    return jnp.where(start < M_P, 0, 1 + (start - M_P) // DEC_SEQ)


def _mod_spec(l, which, tm):
    def idx(*g):
        return (l, _cond_row(g[0], tm), which, 0, 0)
    return pl.BlockSpec((None, None, None, 1, D_MODEL), idx)


def _vec_spec(l, which):
    return pl.BlockSpec((None, None, 1, D_MODEL), lambda *g: (l, which, 0, 0))


def _prompt_rows_spec(tm):
    last = M_P // tm - 1
    return pl.BlockSpec((tm, D_MODEL), lambda i: (jnp.minimum(i, last), 0))


def _latent_rows_spec(tm):
    first = M_P // tm
    return pl.BlockSpec((tm, D_MODEL), lambda i: (jnp.maximum(i - first, 0), 0))


def _mod_kernel(c_ref, w_ref, b_ref, o_ref):
    c = c_ref[...]
    s = c * (1.0 / (1.0 + jnp.exp(-c)))
    o_ref[...] = jnp.dot(s.astype(BF16), w_ref[...].astype(BF16),
                         preferred_element_type=F32) + b_ref[...]


def _modulation(cond, w_mod, b_mod):
    tn = 1024
    n_out = N_MOD * D_MODEL
    return pl.pallas_call(
        _mod_kernel,
        out_shape=jax.ShapeDtypeStruct((DEPTH, COND_PAD, n_out), F32),
        grid=(DEPTH, n_out // tn),
        in_specs=[
            pl.BlockSpec((COND_PAD, D_MODEL), lambda l, n: (0, 0)),
            pl.BlockSpec((None, D_MODEL, tn), lambda l, n: (l, 0, n)),
            pl.BlockSpec((None, 1, tn), lambda l, n: (l, 0, n)),
        ],
        out_specs=pl.BlockSpec((None, COND_PAD, tn), lambda l, n: (l, 0, n)),
        compiler_params=_params("arbitrary", "arbitrary"),
        name="modulation",
    )(cond, w_mod, b_mod.reshape(DEPTH, 1, n_out))


def _cast_kernel(w_ref, o_ref):
    o_ref[...] = w_ref[...].astype(BF16)


def _cast_bf16(w, rows=512):
    n_l, k, n = w.shape
    spec = pl.BlockSpec((None, rows, n), lambda l, r: (l, r, 0))
    return pl.pallas_call(
        _cast_kernel,
        out_shape=jax.ShapeDtypeStruct(w.shape, BF16),
        grid=(n_l, k // rows),
        in_specs=[spec],
        out_specs=spec,
        compiler_params=_params("arbitrary", "arbitrary"),
        name="cast_weight",
    )(w)


def _embed_kernel(xp_ref, xs_ref, sc_ref, sh_ref, u_ref):
    x = jnp.where(pl.program_id(0) * ROW_TILE < M_P, xp_ref[...], xs_ref[...])
    u_ref[...] = (x * (1.0 + sc_ref[...]) + sh_ref[...]).astype(BF16)


def _embed(xp, xs, mod):
    tm = ROW_TILE
    return pl.pallas_call(
        _embed_kernel,
        out_shape=jax.ShapeDtypeStruct((M_ALL, D_MODEL), BF16),
        grid=(M_ALL // tm,),
        in_specs=[_prompt_rows_spec(tm), _latent_rows_spec(tm),
                  _mod_spec(0, 1, tm), _mod_spec(0, 0, tm)],
        out_specs=pl.BlockSpec((tm, D_MODEL), lambda i: (i, 0)),
        compiler_params=_params("arbitrary"),
        name="embed",
    )(xp, xs, mod, mod)


def _rms_chunk(x, g):
    ms = jnp.mean(x * x, axis=-1, keepdims=True)
    return x * lax.rsqrt(ms + RMS_EPS) * g


def _rope_chunk(x, cos, sin_a, sin_b):
    return x * cos + pltpu.roll(x, LANE - 32, 1) * sin_a + pltpu.roll(x, 32, 1) * sin_b


def _proj_kernel(*refs, tn, rms, rope):
    it = iter(refs)
    u_ref, w_ref = next(it), next(it)
    g_ref = next(it) if rms else None
    tabs = (next(it), next(it), next(it)) if rope else None
    o_ref, wbf = next(it), next(it)

    @pl.when(pl.program_id(1) == 0)
    def _():
        wbf[...] = w_ref[...].astype(BF16)

    acc = jnp.dot(u_ref[...], wbf[...], preferred_element_type=F32)
    chunks = [acc[:, c * LANE:(c + 1) * LANE] for c in range(tn // LANE)]
    if rms and rope:
        g = g_ref[...]
        tab = tuple(t[...] for t in tabs)
        rinv = [lax.rsqrt(jnp.mean(x * x, axis=-1, keepdims=True) + RMS_EPS) for x in chunks]

        @pl.when(pl.program_id(1) >= 0)
        def _():
            for c, (x, r) in enumerate(zip(chunks, rinv)):
                o_ref[:, c * LANE:(c + 1) * LANE] = (_rope_chunk(x * g, *tab) * r).astype(o_ref.dtype)
    elif rms:
        g = g_ref[...]
        for c, x in enumerate(chunks):
            o_ref[:, c * LANE:(c + 1) * LANE] = _rms_chunk(x, g).astype(o_ref.dtype)
    elif rope:
        tab = tuple(t[...] for t in tabs)
        for c, x in enumerate(chunks):
            o_ref[:, c * LANE:(c + 1) * LANE] = _rope_chunk(x, *tab).astype(o_ref.dtype)
    else:
        o_ref[...] = acc.astype(o_ref.dtype)


def _proj(u, w, w_layer, col0, ncols, m0, m_rows, *, out_dtype, tn=1024, tm=1024,
          gains=None, gain_split=None, rope=None, name="proj"):
    k = u.shape[1]
    assert col0 % tn == 0 and ncols % tn == 0 and m0 % tm == 0 and m_rows % tm == 0
    n_t, m_t, mt0, nt0 = ncols // tn, m_rows // tm, m0 // tm, col0 // tn
    ins = [u, w]
    in_specs = [
        pl.BlockSpec((tm, k), lambda n, m: (mt0 + m, 0)),
        pl.BlockSpec((None, k, tn), lambda n, m: (w_layer, 0, nt0 + n)),
    ]
    if gains is not None:
        ins.append(gains)
        in_specs.append(pl.BlockSpec(
            (None, 1, LANE), lambda n, m: (jnp.where(n * tn < gain_split, 0, 1), 0, 0)))
    if rope is not None:
        assert m0 >= M_P
        per = DEC_SEQ // tm
        for t in rope:
            ins.append(t)
            in_specs.append(pl.BlockSpec((tm, LANE), lambda n, m: ((mt0 + m) % per, 0)))
    kern = functools.partial(_proj_kernel, tn=tn, rms=gains is not None, rope=rope is not None)
    return pl.pallas_call(
        kern,
        out_shape=jax.ShapeDtypeStruct((m_rows, ncols), out_dtype),
        grid=(n_t, m_t),
        in_specs=in_specs,
        out_specs=pl.BlockSpec((tm, tn), lambda n, m: (m, n)),
        scratch_shapes=[pltpu.VMEM((k, tn), BF16)],
        compiler_params=_params("arbitrary", "arbitrary"),
        name=name,
    )(*ins)


def _qkt(q, k):
    return lax.dot_general(q, k, (((1,), (1,)), ((), ())), preferred_element_type=F32)


def _softmax_terms(s, scale):
    m = s.max(axis=-1, keepdims=True)
    e = jnp.exp2((s - m) * (scale * LOG2E))
    return e, 1.0 / e.sum(axis=-1, keepdims=True)


def _diff_lambda(lp_ref, lam_init):
    lp = lp_ref[...]
    s1 = jnp.sum(lp[0:1] * lp[1:2], axis=-1, keepdims=True)
    s2 = jnp.sum(lp[2:3] * lp[3:4], axis=-1, keepdims=True)
    return jnp.exp(s1) - jnp.exp(s2) + lam_init


def _diff_weights(s, lam):
    half = s.shape[0] // 2
    e, r = _softmax_terms(s, A_HEAD_DIM ** -0.5)
    return (e[:half] * r[:half] - e[half:] * (lam * r[half:])).astype(BF16)


def _sub_norm(o, gain):
    ms = jnp.mean(o * o, axis=-1, keepdims=True)
    return (o * lax.rsqrt(ms + RMS_EPS) * gain).astype(BF16)


A_CHUNKS = D_MODEL // LANE
B_CHUNKS = B_KV // LANE


def _a_v_row(c):
    return (c % 2) * A_HEADS + c // 2


def _a_v_rows(v, batch, seq):
    n_l = v.shape[1]
    v = v.reshape(batch, n_l, seq, A_HEADS, 2, LANE).transpose(0, 1, 2, 4, 3, 5)
    return v.reshape(batch, n_l, seq * A_CHUNKS, LANE)


def _a_v_from_rows(r, batch, seq):
    n_l = r.shape[1]
    v = r.reshape(batch, n_l, seq, 2, A_HEADS, LANE).transpose(0, 1, 2, 4, 3, 5)
    return v.reshape(batch, n_l, seq, A_HEADS, 2 * LANE)


def _diff_prompt_kernel(*refs, lam_init, n_prev, emit):
    lp_ref, g_ref, q_ref, k_ref, v_ref = refs[:5]
    prev = refs[5:5 + n_prev]
    o_ref = refs[5 + n_prev]
    s_scr, a_scr, o_scr = refs[-3:]
    lam = _diff_lambda(lp_ref, lam_init)
    gain = g_ref[...] * (1.0 - lam_init)
    w = 2 * A_HEAD_DIM
    for h in range(A_HEADS):
        for p in range(2):
            cols = slice(h * w + p * A_HEAD_DIM, h * w + (p + 1) * A_HEAD_DIM)
            s_scr[(p * A_HEADS + h) * SEQ:(p * A_HEADS + h + 1) * SEQ, :] = _qkt(
                q_ref[:, cols], k_ref[:, cols].astype(BF16))

    @pl.when(pl.program_id(0) >= 0)
    def _():
        a_scr[...] = _diff_weights(s_scr[...], lam)
        for h in range(A_HEADS):
            rows = slice(h * SEQ, (h + 1) * SEQ)
            o_scr[rows, :] = jnp.dot(a_scr[rows, :], v_ref[:, h * w:(h + 1) * w].astype(BF16),
                                     preferred_element_type=F32)
        y = _sub_norm(o_scr[...], gain)
        for h in range(A_HEADS):
            o_ref[:, h * w:(h + 1) * w] = y[h * SEQ:(h + 1) * SEQ, :]

    if emit:
        nk_ref, nv_ref = refs[6 + n_prev], refs[7 + n_prev]
        layers = [(p, 0, p, D_MODEL) for p in prev] + [(k_ref, 0, v_ref, 0)]
        for jj, (kr, k0, vr, v0) in enumerate(layers):
            for c in range(A_CHUNKS):
                nk_ref[jj, pl.ds(c, SEQ, stride=A_CHUNKS), :] = kr[:, k0 + c * LANE:k0 + (c + 1) * LANE]
                nv_ref[jj, pl.ds(_a_v_row(c), SEQ, stride=A_CHUNKS), :] = (
                    vr[:, v0 + c * LANE:v0 + (c + 1) * LANE])


def _diff_prompt(lp, g, q, kv, j, lam_init, prev_kv=None):
    emit = prev_kv is not None
    prev_kv = list(prev_kv or [])
    n_layers = len(prev_kv) + 1
    att_shape = jax.ShapeDtypeStruct((M_P, D_MODEL), BF16)
    att_spec = pl.BlockSpec((SEQ, D_MODEL), lambda b: (b, 0))
    if emit:
        new_shape = jax.ShapeDtypeStruct((BATCH, n_layers, SEQ * A_CHUNKS, LANE), F32)
        new_spec = pl.BlockSpec((None, n_layers, SEQ * A_CHUNKS, LANE), lambda b: (b, 0, 0, 0))
        out_shape, out_specs = (att_shape, new_shape, new_shape), (att_spec, new_spec, new_spec)
    else:
        out_shape, out_specs = att_shape, att_spec
    return pl.pallas_call(
        functools.partial(_diff_prompt_kernel, lam_init=lam_init, n_prev=len(prev_kv), emit=emit),
        out_shape=out_shape,
        grid=(BATCH,),
        in_specs=[
            pl.BlockSpec((None, 4, A_HEAD_DIM), lambda b: (j, 0, 0)),
            pl.BlockSpec((None, 1, 2 * A_HEAD_DIM), lambda b: (j, 0, 0)),
            pl.BlockSpec((SEQ, D_MODEL), lambda b: (b, 0)),
            pl.BlockSpec((SEQ, D_MODEL), lambda b: (b, 0)),
            pl.BlockSpec((SEQ, D_MODEL), lambda b: (b, 1)),
        ] + [pl.BlockSpec((SEQ, 2 * D_MODEL), lambda b: (b, 0)) for _ in prev_kv],
        out_specs=out_specs,
        scratch_shapes=[pltpu.VMEM((2 * A_HEADS * SEQ, SEQ), F32),
                        pltpu.VMEM((A_HEADS * SEQ, SEQ), BF16),
                        pltpu.VMEM((A_HEADS * SEQ, 2 * A_HEAD_DIM), F32)],
        compiler_params=_params("arbitrary"),
        name="diff_attn_prompt",
    )(lp, g, q, kv, kv, *prev_kv)


def _diff_latent_kernel(lp_ref, g_ref, q_ref, k_ref, v_ref, ck_ref, cv_ref, o_ref,
                        s_scr, a_scr, *, lam_init):
    lam = _diff_lambda(lp_ref, lam_init)
    gain = g_ref[...] * (1.0 - lam_init)
    h = pl.program_id(1)
    tq, d = q_ref.shape[0], A_HEAD_DIM

    def cached(ref, r0, r1):
        halves = [ref[pl.ds(r, PAST_LEN, stride=A_CHUNKS), :] for r in (r0, r1)]
        return jnp.concatenate(halves, axis=1).astype(BF16)

    ck = cached(ck_ref, 2 * h, 2 * h + 1)
    for p in range(2):
        q = q_ref[:, p * d:(p + 1) * d]
        s_scr[p * tq:(p + 1) * tq, :PAST_LEN] = _qkt(q, ck[:, p * d:(p + 1) * d])
        s_scr[p * tq:(p + 1) * tq, PAST_LEN:] = _qkt(q, k_ref[:, p * d:(p + 1) * d])

    @pl.when(h >= 0)
    def _():
        a_scr[...] = _diff_weights(s_scr[...], lam)
        cv = cached(cv_ref, h, A_HEADS + h)
        o = (jnp.dot(a_scr[:, :PAST_LEN], cv, preferred_element_type=F32)
             + jnp.dot(a_scr[:, PAST_LEN:], v_ref[...], preferred_element_type=F32))
        o_ref[...] = _sub_norm(o, gain)


def _diff_latent(lp, g, qk, v, ck, cv, j, lam_init, tq=1024):
    w = 2 * A_HEAD_DIM
    per = DEC_SEQ // tq
    ctx_spec = pl.BlockSpec((None, None, PAST_LEN * A_CHUNKS, LANE), lambda b, h, t: (b, j, 0, 0))
    return pl.pallas_call(
        functools.partial(_diff_latent_kernel, lam_init=lam_init),
        out_shape=jax.ShapeDtypeStruct((M_S, D_MODEL), BF16),
        grid=(DEC_BATCH, A_HEADS, per),
        in_specs=[
            pl.BlockSpec((None, 4, A_HEAD_DIM), lambda b, h, t: (j, 0, 0)),
            pl.BlockSpec((None, 1, w), lambda b, h, t: (j, 0, 0)),
            pl.BlockSpec((tq, w), lambda b, h, t: (b * per + t, h)),
            pl.BlockSpec((DEC_SEQ, w), lambda b, h, t: (b, A_HEADS + h)),
            pl.BlockSpec((DEC_SEQ, w), lambda b, h, t: (b, h)),
            ctx_spec, ctx_spec,
        ],
        out_specs=pl.BlockSpec((tq, w), lambda b, h, t: (b * per + t, h)),
        scratch_shapes=[pltpu.VMEM((2 * tq, PAST_LEN + DEC_SEQ), F32),
                        pltpu.VMEM((tq, PAST_LEN + DEC_SEQ), BF16)],
        compiler_params=_params("arbitrary", "arbitrary", "arbitrary"),
        name="diff_attn_latent",
    )(lp, g, qk, qk, v, ck, cv)


def _gqa_prompt_kernel(*refs, n_prev, emit):
    q_ref, k_ref, v_ref = refs[:3]
    prev = refs[3:3 + 2 * n_prev]
    o_ref = refs[3 + 2 * n_prev]
    s_scr, a_scr, o_scr = refs[-3:]
    d = B_HEAD_DIM
    for n in range(B_KV_HEADS):
        k = k_ref[:, n * d:(n + 1) * d].astype(BF16)
        for g in range(B_GROUP):
            hd = n * B_GROUP + g
            s_scr[hd * SEQ:(hd + 1) * SEQ, :] = _qkt(q_ref[:, hd * d:(hd + 1) * d], k)

    @pl.when(pl.program_id(0) >= 0)
    def _():
        e, r = _softmax_terms(s_scr[...], B_HEAD_DIM ** -0.5)
        a_scr[...] = (e * r).astype(BF16)
        group_rows = B_GROUP * SEQ
        for n in range(B_KV_HEADS):
            rows = slice(n * group_rows, (n + 1) * group_rows)
            o_scr[rows, :] = jnp.dot(a_scr[rows, :], v_ref[:, n * d:(n + 1) * d].astype(BF16),
                                     preferred_element_type=F32)
        for hd in range(B_HEADS):
            o_ref[:, hd * d:(hd + 1) * d] = o_scr[hd * SEQ:(hd + 1) * SEQ, :].astype(BF16)

    if emit:
        nk_ref, nv_ref = refs[4 + 2 * n_prev], refs[5 + 2 * n_prev]
        layers = [(prev[2 * i], prev[2 * i + 1]) for i in range(n_prev)] + [(k_ref, v_ref)]
        for jj, (kr, vr) in enumerate(layers):
            for n in range(B_CHUNKS):
                rows = pl.ds(n, SEQ, stride=B_CHUNKS)
                nk_ref[jj, rows, :] = kr[:, n * LANE:(n + 1) * LANE]
                nv_ref[jj, rows, :] = vr[:, n * LANE:(n + 1) * LANE]


def _gqa_prompt(q, k, v, prev_kv=None):
    emit = prev_kv is not None
    prev = [a for pair in (prev_kv or []) for a in pair]
    n_layers = len(prev) // 2 + 1
    kv_spec = pl.BlockSpec((SEQ, B_KV), lambda b: (b, 0))
    att_shape = jax.ShapeDtypeStruct((M_P, D_MODEL), BF16)
    att_spec = pl.BlockSpec((SEQ, D_MODEL), lambda b: (b, 0))
    if emit:
        new_shape = jax.ShapeDtypeStruct((BATCH, n_layers, SEQ * B_CHUNKS, LANE), F32)
        new_spec = pl.BlockSpec((None, n_layers, SEQ * B_CHUNKS, LANE), lambda b: (b, 0, 0, 0))
        out_shape, out_specs = (att_shape, new_shape, new_shape), (att_spec, new_spec, new_spec)
    else:
        out_shape, out_specs = att_shape, att_spec
    return pl.pallas_call(
        functools.partial(_gqa_prompt_kernel, n_prev=len(prev) // 2, emit=emit),
        out_shape=out_shape,
        grid=(BATCH,),
        in_specs=[att_spec, kv_spec, kv_spec] + [kv_spec for _ in prev],
        out_specs=out_specs,
        scratch_shapes=[pltpu.VMEM((B_HEADS * SEQ, SEQ), F32),
                        pltpu.VMEM((B_HEADS * SEQ, SEQ), BF16),
                        pltpu.VMEM((B_HEADS * SEQ, B_HEAD_DIM), F32)],
        compiler_params=_params("arbitrary"),
        name="gqa_attn_prompt",
    )(q, k, v, *prev)


def _gqa_latent_kernel(q_ref, k_ref, v_ref, ck_ref, cv_ref, o_ref, s_scr, a_scr):
    d, tq = B_HEAD_DIM, q_ref.shape[0]
    n = pl.program_id(1)
    cached = pl.ds(n, PAST_LEN, stride=B_CHUNKS)
    ck = ck_ref[cached, :].astype(BF16)
    for g in range(B_GROUP):
        q = q_ref[:, g * d:(g + 1) * d]
        s_scr[g * tq:(g + 1) * tq, :PAST_LEN] = _qkt(q, ck)
        s_scr[g * tq:(g + 1) * tq, PAST_LEN:] = _qkt(q, k_ref[...])

    @pl.when(n >= 0)
    def _():
        e, r = _softmax_terms(s_scr[...], B_HEAD_DIM ** -0.5)
        a_scr[...] = (e * r).astype(BF16)
        o = (jnp.dot(a_scr[:, :PAST_LEN], cv_ref[cached, :].astype(BF16),
                     preferred_element_type=F32)
             + jnp.dot(a_scr[:, PAST_LEN:], v_ref[...], preferred_element_type=F32))
        for g in range(B_GROUP):
            o_ref[:, g * d:(g + 1) * d] = o[g * tq:(g + 1) * tq, :].astype(BF16)


def _gqa_latent(qk, v, ck, cv, j, tq=512):
    d = B_HEAD_DIM
    gw = B_GROUP * d
    per = DEC_SEQ // tq
    ctx_spec = pl.BlockSpec((None, None, PAST_LEN * B_CHUNKS, LANE), lambda b, n, t: (b, j, 0, 0))
    return pl.pallas_call(
        _gqa_latent_kernel,
        out_shape=jax.ShapeDtypeStruct((M_S, D_MODEL), BF16),
        grid=(DEC_BATCH, B_KV_HEADS, per),
        in_specs=[
            pl.BlockSpec((tq, gw), lambda b, n, t: (b * per + t, n)),
            pl.BlockSpec((DEC_SEQ, d), lambda b, n, t: (b, D_MODEL // d + n)),
            pl.BlockSpec((DEC_SEQ, d), lambda b, n, t: (b, n)),
            ctx_spec, ctx_spec,
        ],
        out_specs=pl.BlockSpec((tq, gw), lambda b, n, t: (b * per + t, n)),
        scratch_shapes=[pltpu.VMEM((B_GROUP * tq, PAST_LEN + DEC_SEQ), F32),
                        pltpu.VMEM((B_GROUP * tq, PAST_LEN + DEC_SEQ), BF16)],
        compiler_params=_params("arbitrary", "arbitrary", "arbitrary"),
        name="gqa_attn_latent",
    )(qk, qk, v, ck, cv)


def _dft_tables(s):
    def cs(n):
        idx = (np.arange(n)[:, None] * np.arange(n)[None, :]) % n
        ang = 2.0 * np.pi * idx / n
        return np.cos(ang), np.sin(ang)
    cc, sc = cs(C_GROUP_DIM)
    c_s, s_s = cs(s)
    t_chan = np.concatenate([cc, sc], axis=1).astype(np.float32)
    t_pos = np.concatenate([c_s, -s_s], axis=1).astype(np.float32)
    return jnp.asarray(t_chan).astype(BF16), jnp.asarray(t_pos).astype(BF16)


def _dft_group(x, t_chan, t_pos, norm):
    xcs = jnp.dot(x, t_chan, preferred_element_type=F32).astype(BF16)
    stacked = jnp.concatenate([xcs[:, :C_GROUP_DIM], xcs[:, C_GROUP_DIM:]], axis=0)
    y = jnp.dot(t_pos, stacked, preferred_element_type=F32)
    return (y * norm).astype(BF16)


def _dft_kernel(u_ref, tc_ref, tp_ref, o_ref, *, groups, norm):
    tc, tp = tc_ref[...], tp_ref[...]
    for g in range(groups):
        sl = slice(g * C_GROUP_DIM, (g + 1) * C_GROUP_DIM)
        o_ref[:, sl] = _dft_group(u_ref[:, sl], tc, tp, norm)


def _dft(u, s, batch, row0, groups_per_step):
    t_chan, t_pos = _dft_tables(s)
    gw = groups_per_step * C_GROUP_DIM
    rb0 = row0 // s
    return pl.pallas_call(
        functools.partial(_dft_kernel, groups=groups_per_step,
                          norm=1.0 / math.sqrt(s * C_GROUP_DIM)),
        out_shape=jax.ShapeDtypeStruct((batch * s, D_MODEL), BF16),
        grid=(batch, C_GROUPS // groups_per_step),
        in_specs=[
            pl.BlockSpec((s, gw), lambda b, g: (rb0 + b, g)),
            pl.BlockSpec(t_chan.shape, lambda b, g: (0, 0)),
            pl.BlockSpec(t_pos.shape, lambda b, g: (0, 0)),
        ],
        out_specs=pl.BlockSpec((s, gw), lambda b, g: (b, g)),
        compiler_params=_params("arbitrary", "arbitrary"),
        name="dft",
    )(u, t_chan, t_pos)


ROW_GROUPS = 2


def _post_norm(y, g, b):
    mu = jnp.mean(y, axis=-1, keepdims=True)
    yc = y - mu
    var = jnp.mean(yc * yc, axis=-1, keepdims=True)
    return yc * lax.rsqrt(var + LN_EPS) * g + b


def _oproj_kernel(*refs, split_x):
    ap_ref, as_ref, w_ref = refs[:3]
    x_refs = refs[3:5] if split_x else refs[3:4]
    gate_ref, g_ref, b_ref, sc_ref, sh_ref, x1_ref, u_ref = refs[3 + len(x_refs):]
    is_prompt = pl.program_id(0) * ROW_TILE < M_P
    gate, g, b = gate_ref[...], g_ref[...], b_ref[...]
    sc1, sh = 1.0 + sc_ref[...], sh_ref[...]
    rows = ROW_TILE // ROW_GROUPS
    for r in range(ROW_GROUPS):
        sl = slice(r * rows, (r + 1) * rows)
        a = jnp.where(is_prompt, ap_ref[sl, :], as_ref[sl, :])
        mixed = jnp.dot(a, w_ref[...], preferred_element_type=F32)
        if split_x:
            x = jnp.where(is_prompt, x_refs[0][sl, :], x_refs[1][sl, :])
        else:
            x = x_refs[0][sl, :]
        x1 = _post_norm(ALPHA * x + gate * mixed, g, b)
        x1_ref[sl, :] = x1
        u_ref[sl, :] = (x1 * sc1 + sh).astype(BF16)


def _oproj(a_p, a_s, w_bf, w_layer, x, mod, ln_g, ln_b, l):
    tm = ROW_TILE
    row = pl.BlockSpec((tm, D_MODEL), lambda i: (i, 0))
    split_x = isinstance(x, tuple)
    xs = list(x) if split_x else [x]
    x_specs = [_prompt_rows_spec(tm), _latent_rows_spec(tm)] if split_x else [row]
    return pl.pallas_call(
        functools.partial(_oproj_kernel, split_x=split_x),
        out_shape=(jax.ShapeDtypeStruct((M_ALL, D_MODEL), F32),
                   jax.ShapeDtypeStruct((M_ALL, D_MODEL), BF16)),
        grid=(M_ALL // tm,),
        in_specs=[
            _prompt_rows_spec(tm), _latent_rows_spec(tm),
            pl.BlockSpec((None, D_MODEL, D_MODEL), lambda i: (w_layer, 0, 0),
                         pipeline_mode=pl.Buffered(1)),
            *x_specs,
            _mod_spec(l, 2, tm),
            _vec_spec(l, 0), _vec_spec(l, 0),
            _mod_spec(l, 4, tm),
            _mod_spec(l, 3, tm),
        ],
        out_specs=(row, row),
        compiler_params=_params("arbitrary"),
        name="oproj_ln",
    )(a_p, a_s, w_bf, *xs, mod, ln_g, ln_b, mod, mod)


MLP_FF_PIECE = 512
MLP_VMEM_LIMIT = 63 * 1024 * 1024


def _mlp_kernel(*refs, n_tiles, n_ff, last):
    u_ref, wu_ref, wd_ref, x_ref, gate_ref, g_ref, b_ref = refs[:7]
    acc, prev = refs[-2:]
    i, f = pl.program_id(0), pl.program_id(1)
    tm = u_ref.shape[0]
    close_rows = tm // n_ff

    @pl.when(jnp.logical_and(f == 0, i > 0))
    def _():
        prev[...] = acc[...]

    @pl.when(f == 0)
    def _():
        acc[...] = jnp.zeros_like(acc)

    def matmuls():
        tf = wu_ref.shape[1]
        for c in range(tf // MLP_FF_PIECE):
            cols = slice(c * MLP_FF_PIECE, (c + 1) * MLP_FF_PIECE)
            h = jnp.dot(u_ref[...], wu_ref[:, cols].astype(BF16), preferred_element_type=F32)
            h = jnp.maximum(h, 0.0)
            h = (h * h).astype(BF16)
            acc[...] += jnp.dot(h, wd_ref[cols, :].astype(BF16), preferred_element_type=F32)

    def close():
        rows = pl.ds(pl.multiple_of(f * close_rows, close_rows), close_rows)
        y = _post_norm(ALPHA * x_ref[...] + gate_ref[...] * prev[rows, :], g_ref[...], b_ref[...])
        if last:
            yp_ref, ys_ref = refs[7:9]
            prev_is_prompt = (i - 1) * tm < M_P

            @pl.when(prev_is_prompt)
            def _():
                yp_ref[...] = y

            @pl.when(jnp.logical_not(prev_is_prompt))
            def _():
                ys_ref[...] = y
        else:
            sc_ref, sh_ref, x2_ref, un_ref = refs[7:11]
            x2_ref[...] = y
            un_ref[...] = (y * (1.0 + sc_ref[...]) + sh_ref[...]).astype(BF16)

    @pl.when(i == 0)
    def _():
        matmuls()

    @pl.when(jnp.logical_and(i > 0, i < n_tiles))
    def _():
        matmuls()
        close()

    @pl.when(i == n_tiles)
    def _():
        close()


def _mlp(u, w_up, w_down, x, mod, ln_g, ln_b, l, tm=1024, tf=1024):
    last = l + 1 == DEPTH
    n_tiles, n_ff = M_ALL // tm, D_FF // tf
    close_rows = tm // n_ff

    def close_block(i, f):
        return jnp.where(i == 0, 0, (i - 1) * n_ff + f)

    def mod_prev(layer, which):
        def idx(i, f):
            return (layer, _cond_row(jnp.maximum(i - 1, 0), tm), which, 0, 0)
        return pl.BlockSpec((None, None, None, 1, D_MODEL), idx)

    def ff_block(i, f):
        return jnp.where(i < n_tiles, f, n_ff - 1)

    row = pl.BlockSpec((close_rows, D_MODEL), lambda i, f: (close_block(i, f), 0))
    ins = [u, w_up, w_down, x, mod, ln_g, ln_b]
    in_specs = [
        pl.BlockSpec((tm, D_MODEL), lambda i, f: (jnp.minimum(i, n_tiles - 1), 0),
                     pipeline_mode=pl.Buffered(1)),
        pl.BlockSpec((None, D_MODEL, tf), lambda i, f: (l, 0, ff_block(i, f))),
        pl.BlockSpec((None, tf, D_MODEL), lambda i, f: (l, ff_block(i, f), 0)),
        row, mod_prev(l, 5), _vec_spec(l, 1), _vec_spec(l, 1),
    ]
    if last:
        n_p = M_P // close_rows
        out_shape = (jax.ShapeDtypeStruct((M_P, D_MODEL), F32),
                     jax.ShapeDtypeStruct((M_S, D_MODEL), F32))
        out_specs = (
            pl.BlockSpec((close_rows, D_MODEL),
                         lambda i, f: (jnp.minimum(close_block(i, f), n_p - 1), 0)),
            pl.BlockSpec((close_rows, D_MODEL),
                         lambda i, f: (jnp.maximum(close_block(i, f) - n_p, 0), 0)),
        )
    else:
        ins += [mod, mod]
        in_specs += [mod_prev(l + 1, 1), mod_prev(l + 1, 0)]
        out_shape = (jax.ShapeDtypeStruct((M_ALL, D_MODEL), F32),
                     jax.ShapeDtypeStruct((M_ALL, D_MODEL), BF16))
        out_specs = (row, row)
    return pl.pallas_call(
        functools.partial(_mlp_kernel, n_tiles=n_tiles, n_ff=n_ff, last=last),
        out_shape=out_shape,
        grid=(n_tiles + 1, n_ff),
        in_specs=in_specs,
        out_specs=out_specs,
        scratch_shapes=[pltpu.VMEM((tm, D_MODEL), F32), pltpu.VMEM((tm, D_MODEL), F32)],
        compiler_params=_params("arbitrary", "arbitrary", vmem=MLP_VMEM_LIMIT),
        name="mlp",
    )(*ins)


def _rope_tables():
    n_freq = A_HEAD_DIM // 4
    pos = np.arange(DEC_SEQ)
    row = (pos // GRID_W).astype(np.float32)
    col = (pos % GRID_W).astype(np.float32)
    inv_freq = (ROPE_BASE ** (-np.arange(n_freq, dtype=np.float32) / n_freq)).astype(np.float32)
    ar = row[:, None] * inv_freq
    ac = col[:, None] * inv_freq
    cr, sr, cc, sc = np.cos(ar), np.sin(ar), np.cos(ac), np.sin(ac)
    z = np.zeros_like(sr)
    cos = np.concatenate([cr, cr, cc, cc], axis=1)
    sin_a = np.concatenate([-sr, z, -sc, z], axis=1)
    sin_b = np.concatenate([z, sr, z, sc], axis=1)
    return tuple(jnp.asarray(t.astype(np.float32)) for t in (cos, sin_a, sin_b))


def kernel(x_prompt, x_sample, cache_a_k, cache_a_v, cache_b_k, cache_b_v, c, c_ctx, w_mod, b_mod, ln_g, ln_b, w_up, w_down, a_w_qkv, a_w_o, a_lambda, a_subln_g, b_w_qkv, b_w_o, b_q_norm_g, b_k_norm_g, c_w_f):
    n_a = a_w_qkv.shape[0]
    n_b = b_w_qkv.shape[0]
    cond = jnp.zeros((COND_PAD, D_MODEL), F32).at[0].set(c_ctx).at[1:N_COND].set(c)
    mod = _modulation(cond, w_mod, b_mod).reshape(DEPTH, COND_PAD, N_MOD, 1, D_MODEL)
    ln_g4 = ln_g.reshape(DEPTH, 2, 1, D_MODEL)
    ln_b4 = ln_b.reshape(DEPTH, 2, 1, D_MODEL)
    rope = _rope_tables()
    w_o_bf = (_cast_bf16(a_w_o), _cast_bf16(b_w_o), _cast_bf16(c_w_f))

    ck_a = cache_a_k.reshape(DEC_BATCH, n_a, PAST_LEN * A_CHUNKS, LANE)
    cv_a = _a_v_rows(cache_a_v, DEC_BATCH, PAST_LEN)
    ck_b = cache_b_k.reshape(DEC_BATCH, n_b, PAST_LEN * B_CHUNKS, LANE)
    cv_b = cache_b_v.reshape(DEC_BATCH, n_b, PAST_LEN * B_CHUNKS, LANE)
    subln = a_subln_g.reshape(n_a, 1, 2 * A_HEAD_DIM)
    b_gains = jnp.stack([b_q_norm_g, b_k_norm_g], axis=1).reshape(n_b, 2, 1, B_HEAD_DIM)

    x = (x_prompt.reshape(M_P, D_MODEL), x_sample.reshape(M_S, D_MODEL))
    u = _embed(*x, mod)

    a_kv, b_kv = [], []
    for l in range(DEPTH):
        kind, j = l % N_MIXERS, l // N_MIXERS
        if kind == 0:
            lam_init = 0.8 - 0.6 * math.exp(-0.3 * l)
            q_p = _proj(u, a_w_qkv, j, 0, D_MODEL, 0, M_P, out_dtype=BF16, name="a_q_prompt")
            kv_p = _proj(u, a_w_qkv, j, D_MODEL, 2 * D_MODEL, 0, M_P, out_dtype=F32,
                         name="a_kv_prompt")
            qk_s = _proj(u, a_w_qkv, j, 0, 2 * D_MODEL, M_P, M_S, out_dtype=BF16, rope=rope,
                         name="a_qk_latent")
            v_s = _proj(u, a_w_qkv, j, 2 * D_MODEL, D_MODEL, M_P, M_S, out_dtype=BF16,
                        name="a_v_latent")
            if j + 1 < n_a:
                att_p = _diff_prompt(a_lambda, subln, q_p, kv_p, j, lam_init)
                a_kv.append(kv_p)
            else:
                att_p, new_a_k, new_a_v = _diff_prompt(a_lambda, subln, q_p, kv_p, j, lam_init,
                                                       prev_kv=a_kv)
            att_s = _diff_latent(a_lambda, subln, qk_s, v_s, ck_a, cv_a, j, lam_init)
            w_o = w_o_bf[0]
        elif kind == 1:
            gj = b_gains[j]
            q_p = _proj(u, b_w_qkv, j, 0, D_MODEL, 0, M_P, out_dtype=BF16, gains=gj,
                        gain_split=D_MODEL, name="b_q_prompt")
            k_p = _proj(u, b_w_qkv, j, D_MODEL, B_KV, 0, M_P, out_dtype=F32, tn=B_KV,
                        gains=gj[1:], gain_split=B_KV, name="b_k_prompt")
            v_p = _proj(u, b_w_qkv, j, D_MODEL + B_KV, B_KV, 0, M_P, out_dtype=F32, tn=B_KV,
                        name="b_v_prompt")
            qk_s = _proj(u, b_w_qkv, j, 0, D_MODEL + B_KV, M_P, M_S, out_dtype=BF16, tn=B_KV,
                         gains=gj, gain_split=D_MODEL, rope=rope, name="b_qk_latent")
            v_s = _proj(u, b_w_qkv, j, D_MODEL + B_KV, B_KV, M_P, M_S, out_dtype=BF16, tn=B_KV,
                        name="b_v_latent")
            if j + 1 < n_b:
                att_p = _gqa_prompt(q_p, k_p, v_p)
                b_kv.append((k_p, v_p))
            else:
                att_p, new_b_k, new_b_v = _gqa_prompt(q_p, k_p, v_p, prev_kv=b_kv)
            att_s = _gqa_latent(qk_s, v_s, ck_b, cv_b, j)
            w_o = w_o_bf[1]
        else:
            att_p = _dft(u, SEQ, BATCH, 0, C_GROUPS)
            att_s = _dft(u, DEC_SEQ, DEC_BATCH, M_P, 1)
            w_o = w_o_bf[2]
        x, u = _oproj(att_p, att_s, w_o, j, x, mod, ln_g4, ln_b4, l)
        x, u = _mlp(u, w_up, w_down, x, mod, ln_g4, ln_b4, l)

    y_p, y_s = x, u
    return (y_p.reshape(BATCH, SEQ, D_MODEL), y_s.reshape(DEC_BATCH, DEC_SEQ, D_MODEL),
            new_a_k.reshape(BATCH, n_a, SEQ, A_HEADS, 2, A_HEAD_DIM),
            _a_v_from_rows(new_a_v, BATCH, SEQ),
            new_b_k.reshape(BATCH, n_b, SEQ, B_KV_HEADS, B_HEAD_DIM),
            new_b_v.reshape(BATCH, n_b, SEQ, B_KV_HEADS, B_HEAD_DIM))
```

```python
import functools
import math

import jax
import jax.numpy as jnp
import numpy as np
from jax import lax
from jax.experimental import pallas as pl
from jax.experimental.pallas import tpu as pltpu

D_MODEL = 2048
BATCH = 16
SEQ = 256
DEPTH = 4
DEC_BATCH = 2
DEC_SEQ = 1024
PAST_LEN = 512
GRID_W = 64
N_MIXERS = 3
A_HEAD_DIM = 128
A_HEADS = D_MODEL // (2 * A_HEAD_DIM)
B_HEAD_DIM = 128
B_HEADS = D_MODEL // B_HEAD_DIM
B_KV_HEADS = B_HEADS // 4
B_GROUP = B_HEADS // B_KV_HEADS
B_KV = B_KV_HEADS * B_HEAD_DIM
C_GROUPS = 4
C_GROUP_DIM = D_MODEL // C_GROUPS
D_FF = 4 * D_MODEL
ROPE_BASE = 10000.0
ALPHA = (2 * DEPTH) ** 0.25
LN_EPS = 1e-5
RMS_EPS = 1e-6
N_MOD = 6
LOG2E = 1.4426950408889634

M_P = BATCH * SEQ
M_S = DEC_BATCH * DEC_SEQ
M_ALL = M_P + M_S
N_COND = 1 + DEC_BATCH
COND_PAD = 8

LANE = 128
VMEM_LIMIT = 58 * 1024 * 1024
ROW_TILE = 512

BF16 = jnp.bfloat16
F32 = jnp.float32


def _params(*sem, vmem=VMEM_LIMIT):
    return pltpu.CompilerParams(dimension_semantics=sem, vmem_limit_bytes=vmem)


def _cond_row(i, tm):
    start = i * tm
    return jnp.where(start < M_P, 0, 1 + (start - M_P) // DEC_SEQ)


def _mod_spec(l, which, tm):
    def idx(*g):
        return (l, _cond_row(g[0], tm), which, 0, 0)
    return pl.BlockSpec((None, None, None, 1, D_MODEL), idx)


def _vec_spec(l, which):
    return pl.BlockSpec((None, None, 1, D_MODEL), lambda *g: (l, which, 0, 0))


def _prompt_rows_spec(tm):
    last = M_P // tm - 1
    return pl.BlockSpec((tm, D_MODEL), lambda i: (jnp.minimum(i, last), 0))


def _latent_rows_spec(tm):
    first = M_P // tm
    return pl.BlockSpec((tm, D_MODEL), lambda i: (jnp.maximum(i - first, 0), 0))


def _mod_kernel(c_ref, w_ref, b_ref, o_ref):
    c = c_ref[...]
    s = c * (1.0 / (1.0 + jnp.exp(-c)))
    o_ref[...] = jnp.dot(s.astype(BF16), w_ref[...].astype(BF16),
                         preferred_element_type=F32) + b_ref[...]


def _modulation(cond, w_mod, b_mod):
    tn = 1024
    n_out = N_MOD * D_MODEL
    return pl.pallas_call(
        _mod_kernel,
        out_shape=jax.ShapeDtypeStruct((DEPTH, COND_PAD, n_out), F32),
        grid=(DEPTH, n_out // tn),
        in_specs=[
            pl.BlockSpec((COND_PAD, D_MODEL), lambda l, n: (0, 0)),
            pl.BlockSpec((None, D_MODEL, tn), lambda l, n: (l, 0, n)),
            pl.BlockSpec((None, 1, tn), lambda l, n: (l, 0, n)),
        ],
        out_specs=pl.BlockSpec((None, COND_PAD, tn), lambda l, n: (l, 0, n)),
        compiler_params=_params("arbitrary", "arbitrary"),
        name="modulation",
    )(cond, w_mod, b_mod.reshape(DEPTH, 1, n_out))


def _cast_kernel(w_ref, o_ref):
    o_ref[...] = w_ref[...].astype(BF16)


def _cast_bf16(w, rows=512):
    n_l, k, n = w.shape
    spec = pl.BlockSpec((None, rows, n), lambda l, r: (l, r, 0))
    return pl.pallas_call(
        _cast_kernel,
        out_shape=jax.ShapeDtypeStruct(w.shape, BF16),
        grid=(n_l, k // rows),
        in_specs=[spec],
        out_specs=spec,
        compiler_params=_params("arbitrary", "arbitrary"),
        name="cast_weight",
    )(w)


def _embed_kernel(xp_ref, xs_ref, sc_ref, sh_ref, u_ref):
    x = jnp.where(pl.program_id(0) * ROW_TILE < M_P, xp_ref[...], xs_ref[...])
    u_ref[...] = (x * (1.0 + sc_ref[...]) + sh_ref[...]).astype(BF16)


def _embed(xp, xs, mod):
    tm = ROW_TILE
    return pl.pallas_call(
        _embed_kernel,
        out_shape=jax.ShapeDtypeStruct((M_ALL, D_MODEL), BF16),
        grid=(M_ALL // tm,),
        in_specs=[_prompt_rows_spec(tm), _latent_rows_spec(tm),
                  _mod_spec(0, 1, tm), _mod_spec(0, 0, tm)],
        out_specs=pl.BlockSpec((tm, D_MODEL), lambda i: (i, 0)),
        compiler_params=_params("arbitrary"),
        name="embed",
    )(xp, xs, mod, mod)


def _rms_chunk(x, g):
    ms = jnp.mean(x * x, axis=-1, keepdims=True)
    return x * lax.rsqrt(ms + RMS_EPS) * g


def _rope_chunk(x, cos, sin_a, sin_b):
    return x * cos + pltpu.roll(x, LANE - 32, 1) * sin_a + pltpu.roll(x, 32, 1) * sin_b


def _proj_kernel(*refs, tn, rms, rope):
    it = iter(refs)
    u_ref, w_ref = next(it), next(it)
    g_ref = next(it) if rms else None
    tabs = (next(it), next(it), next(it)) if rope else None
    o_ref, wbf = next(it), next(it)

    @pl.when(pl.program_id(1) == 0)
    def _():
        wbf[...] = w_ref[...].astype(BF16)

    acc = jnp.dot(u_ref[...], wbf[...], preferred_element_type=F32)
    chunks = [acc[:, c * LANE:(c + 1) * LANE] for c in range(tn // LANE)]
    if rms and rope:
        g = g_ref[...]
        tab = tuple(t[...] for t in tabs)
        rinv = [lax.rsqrt(jnp.mean(x * x, axis=-1, keepdims=True) + RMS_EPS) for x in chunks]

        @pl.when(pl.program_id(1) >= 0)
        def _():
            for c, (x, r) in enumerate(zip(chunks, rinv)):
                o_ref[:, c * LANE:(c + 1) * LANE] = (_rope_chunk(x * g, *tab) * r).astype(o_ref.dtype)
    elif rms:
        g = g_ref[...]
        for c, x in enumerate(chunks):
            o_ref[:, c * LANE:(c + 1) * LANE] = _rms_chunk(x, g).astype(o_ref.dtype)
    elif rope:
        tab = tuple(t[...] for t in tabs)
        for c, x in enumerate(chunks):
            o_ref[:, c * LANE:(c + 1) * LANE] = _rope_chunk(x, *tab).astype(o_ref.dtype)
    else:
        o_ref[...] = acc.astype(o_ref.dtype)


def _proj(u, w, w_layer, col0, ncols, m0, m_rows, *, out_dtype, tn=1024, tm=1024,
          gains=None, gain_split=None, rope=None, name="proj"):
    k = u.shape[1]
    assert col0 % tn == 0 and ncols % tn == 0 and m0 % tm == 0 and m_rows % tm == 0
    n_t, m_t, mt0, nt0 = ncols // tn, m_rows // tm, m0 // tm, col0 // tn
    ins = [u, w]
    in_specs = [
        pl.BlockSpec((tm, k), lambda n, m: (mt0 + m, 0)),
        pl.BlockSpec((None, k, tn), lambda n, m: (w_layer, 0, nt0 + n)),
    ]
    if gains is not None:
        ins.append(gains)
        in_specs.append(pl.BlockSpec(
            (None, 1, LANE), lambda n, m: (jnp.where(n * tn < gain_split, 0, 1), 0, 0)))
    if rope is not None:
        assert m0 >= M_P
        per = DEC_SEQ // tm
        for t in rope:
            ins.append(t)
            in_specs.append(pl.BlockSpec((tm, LANE), lambda n, m: ((mt0 + m) % per, 0)))
    kern = functools.partial(_proj_kernel, tn=tn, rms=gains is not None, rope=rope is not None)
    return pl.pallas_call(
        kern,
        out_shape=jax.ShapeDtypeStruct((m_rows, ncols), out_dtype),
        grid=(n_t, m_t),
        in_specs=in_specs,
        out_specs=pl.BlockSpec((tm, tn), lambda n, m: (m, n)),
        scratch_shapes=[pltpu.VMEM((k, tn), BF16)],
        compiler_params=_params("arbitrary", "arbitrary"),
        name=name,
    )(*ins)


def _qkt(q, k):
    return lax.dot_general(q, k, (((1,), (1,)), ((), ())), preferred_element_type=F32)


def _softmax_terms(s, scale):
    m = s.max(axis=-1, keepdims=True)
    e = jnp.exp2((s - m) * (scale * LOG2E))
    return e, 1.0 / e.sum(axis=-1, keepdims=True)


def _diff_lambda(lp_ref, lam_init):
    lp = lp_ref[...]
    s1 = jnp.sum(lp[0:1] * lp[1:2], axis=-1, keepdims=True)
    s2 = jnp.sum(lp[2:3] * lp[3:4], axis=-1, keepdims=True)
    return jnp.exp(s1) - jnp.exp(s2) + lam_init


def _diff_weights(s, lam):
    half = s.shape[0] // 2
    e, r = _softmax_terms(s, A_HEAD_DIM ** -0.5)
    return (e[:half] * r[:half] - e[half:] * (lam * r[half:])).astype(BF16)


def _sub_norm(o, gain):
    ms = jnp.mean(o * o, axis=-1, keepdims=True)
    return (o * lax.rsqrt(ms + RMS_EPS) * gain).astype(BF16)


A_CHUNKS = D_MODEL // LANE
B_CHUNKS = B_KV // LANE


def _a_v_row(c):
    return (c % 2) * A_HEADS + c // 2


def _a_v_rows(v, batch, seq):
    n_l = v.shape[1]
    v = v.reshape(batch, n_l, seq, A_HEADS, 2, LANE).transpose(0, 1, 2, 4, 3, 5)
    return v.reshape(batch, n_l, seq * A_CHUNKS, LANE)


def _a_v_from_rows(r, batch, seq):
    n_l = r.shape[1]
    v = r.reshape(batch, n_l, seq, 2, A_HEADS, LANE).transpose(0, 1, 2, 4, 3, 5)
    return v.reshape(batch, n_l, seq, A_HEADS, 2 * LANE)


def _diff_prompt_kernel(*refs, lam_init, n_prev, emit):
    lp_ref, g_ref, q_ref, k_ref, v_ref = refs[:5]
    prev = refs[5:5 + n_prev]
    o_ref = refs[5 + n_prev]
    s_scr, a_scr, o_scr = refs[-3:]
    lam = _diff_lambda(lp_ref, lam_init)
    gain = g_ref[...] * (1.0 - lam_init)
    w = 2 * A_HEAD_DIM
    for h in range(A_HEADS):
        for p in range(2):
            cols = slice(h * w + p * A_HEAD_DIM, h * w + (p + 1) * A_HEAD_DIM)
            s_scr[(p * A_HEADS + h) * SEQ:(p * A_HEADS + h + 1) * SEQ, :] = _qkt(
                q_ref[:, cols], k_ref[:, cols].astype(BF16))

    @pl.when(pl.program_id(0) >= 0)
    def _():
        a_scr[...] = _diff_weights(s_scr[...], lam)
        for h in range(A_HEADS):
            rows = slice(h * SEQ, (h + 1) * SEQ)
            o_scr[rows, :] = jnp.dot(a_scr[rows, :], v_ref[:, h * w:(h + 1) * w].astype(BF16),
                                     preferred_element_type=F32)
        y = _sub_norm(o_scr[...], gain)
        for h in range(A_HEADS):
            o_ref[:, h * w:(h + 1) * w] = y[h * SEQ:(h + 1) * SEQ, :]

    if emit:
        nk_ref, nv_ref = refs[6 + n_prev], refs[7 + n_prev]
        layers = [(p, 0, p, D_MODEL) for p in prev] + [(k_ref, 0, v_ref, 0)]
        for jj, (kr, k0, vr, v0) in enumerate(layers):
            for c in range(A_CHUNKS):
                nk_ref[jj, pl.ds(c, SEQ, stride=A_CHUNKS), :] = kr[:, k0 + c * LANE:k0 + (c + 1) * LANE]
                nv_ref[jj, pl.ds(_a_v_row(c), SEQ, stride=A_CHUNKS), :] = (
                    vr[:, v0 + c * LANE:v0 + (c + 1) * LANE])


def _diff_prompt(lp, g, q, kv, j, lam_init, prev_kv=None):
    emit = prev_kv is not None
    prev_kv = list(prev_kv or [])
    n_layers = len(prev_kv) + 1
    att_shape = jax.ShapeDtypeStruct((M_P, D_MODEL), BF16)
    att_spec = pl.BlockSpec((SEQ, D_MODEL), lambda b: (b, 0))
    if emit:
        new_shape = jax.ShapeDtypeStruct((BATCH, n_layers, SEQ * A_CHUNKS, LANE), F32)
        new_spec = pl.BlockSpec((None, n_layers, SEQ * A_CHUNKS, LANE), lambda b: (b, 0, 0, 0))
        out_shape, out_specs = (att_shape, new_shape, new_shape), (att_spec, new_spec, new_spec)
    else:
        out_shape, out_specs = att_shape, att_spec
    return pl.pallas_call(
        functools.partial(_diff_prompt_kernel, lam_init=lam_init, n_prev=len(prev_kv), emit=emit),
        out_shape=out_shape,
        grid=(BATCH,),
        in_specs=[
            pl.BlockSpec((None, 4, A_HEAD_DIM), lambda b: (j, 0, 0)),
            pl.BlockSpec((None, 1, 2 * A_HEAD_DIM), lambda b: (j, 0, 0)),
            pl.BlockSpec((SEQ, D_MODEL), lambda b: (b, 0)),
            pl.BlockSpec((SEQ, D_MODEL), lambda b: (b, 0)),
            pl.BlockSpec((SEQ, D_MODEL), lambda b: (b, 1)),
        ] + [pl.BlockSpec((SEQ, 2 * D_MODEL), lambda b: (b, 0)) for _ in prev_kv],
        out_specs=out_specs,
        scratch_shapes=[pltpu.VMEM((2 * A_HEADS * SEQ, SEQ), F32),
                        pltpu.VMEM((A_HEADS * SEQ, SEQ), BF16),
                        pltpu.VMEM((A_HEADS * SEQ, 2 * A_HEAD_DIM), F32)],
        compiler_params=_params("arbitrary"),
        name="diff_attn_prompt",
    )(lp, g, q, kv, kv, *prev_kv)


def _diff_latent_kernel(lp_ref, g_ref, q_ref, k_ref, v_ref, ck_ref, cv_ref, o_ref,
                        s_scr, a_scr, *, lam_init):
    lam = _diff_lambda(lp_ref, lam_init)
    gain = g_ref[...] * (1.0 - lam_init)
    h = pl.program_id(1)
    tq, d = q_ref.shape[0], A_HEAD_DIM

    def cached(ref, r0, r1):
        halves = [ref[pl.ds(r, PAST_LEN, stride=A_CHUNKS), :] for r in (r0, r1)]
        return jnp.concatenate(halves, axis=1).astype(BF16)

    ck = cached(ck_ref, 2 * h, 2 * h + 1)
    for p in range(2):
        q = q_ref[:, p * d:(p + 1) * d]
        s_scr[p * tq:(p + 1) * tq, :PAST_LEN] = _qkt(q, ck[:, p * d:(p + 1) * d])
        s_scr[p * tq:(p + 1) * tq, PAST_LEN:] = _qkt(q, k_ref[:, p * d:(p + 1) * d])

    @pl.when(h >= 0)
    def _():
        a_scr[...] = _diff_weights(s_scr[...], lam)
        cv = cached(cv_ref, h, A_HEADS + h)
        o = (jnp.dot(a_scr[:, :PAST_LEN], cv, preferred_element_type=F32)
             + jnp.dot(a_scr[:, PAST_LEN:], v_ref[...], preferred_element_type=F32))
        o_ref[...] = _sub_norm(o, gain)


def _diff_latent(lp, g, qk, v, ck, cv, j, lam_init, tq=1024):
    w = 2 * A_HEAD_DIM
    per = DEC_SEQ // tq
    ctx_spec = pl.BlockSpec((None, None, PAST_LEN * A_CHUNKS, LANE), lambda b, h, t: (b, j, 0, 0))
    return pl.pallas_call(
        functools.partial(_diff_latent_kernel, lam_init=lam_init),
        out_shape=jax.ShapeDtypeStruct((M_S, D_MODEL), BF16),
        grid=(DEC_BATCH, A_HEADS, per),
        in_specs=[
            pl.BlockSpec((None, 4, A_HEAD_DIM), lambda b, h, t: (j, 0, 0)),
            pl.BlockSpec((None, 1, w), lambda b, h, t: (j, 0, 0)),
            pl.BlockSpec((tq, w), lambda b, h, t: (b * per + t, h)),
            pl.BlockSpec((DEC_SEQ, w), lambda b, h, t: (b, A_HEADS + h)),
            pl.BlockSpec((DEC_SEQ, w), lambda b, h, t: (b, h)),
            ctx_spec, ctx_spec,
        ],
        out_specs=pl.BlockSpec((tq, w), lambda b, h, t: (b * per + t, h)),
        scratch_shapes=[pltpu.VMEM((2 * tq, PAST_LEN + DEC_SEQ), F32),
                        pltpu.VMEM((tq, PAST_LEN + DEC_SEQ), BF16)],
        compiler_params=_params("arbitrary", "arbitrary", "arbitrary"),
        name="diff_attn_latent",
    )(lp, g, qk, qk, v, ck, cv)


def _gqa_prompt_kernel(*refs, n_prev, emit):
    q_ref, k_ref, v_ref = refs[:3]
    prev = refs[3:3 + 2 * n_prev]
    o_ref = refs[3 + 2 * n_prev]
    s_scr, a_scr, o_scr = refs[-3:]
    d = B_HEAD_DIM
    for n in range(B_KV_HEADS):
        k = k_ref[:, n * d:(n + 1) * d].astype(BF16)
        for g in range(B_GROUP):
            hd = n * B_GROUP + g
            s_scr[hd * SEQ:(hd + 1) * SEQ, :] = _qkt(q_ref[:, hd * d:(hd + 1) * d], k)

    @pl.when(pl.program_id(0) >= 0)
    def _():
        e, r = _softmax_terms(s_scr[...], B_HEAD_DIM ** -0.5)
        a_scr[...] = (e * r).astype(BF16)
        group_rows = B_GROUP * SEQ
        for n in range(B_KV_HEADS):
            rows = slice(n * group_rows, (n + 1) * group_rows)
            o_scr[rows, :] = jnp.dot(a_scr[rows, :], v_ref[:, n * d:(n + 1) * d].astype(BF16),
                                     preferred_element_type=F32)
        for hd in range(B_HEADS):
            o_ref[:, hd * d:(hd + 1) * d] = o_scr[hd * SEQ:(hd + 1) * SEQ, :].astype(BF16)

    if emit:
        nk_ref, nv_ref = refs[4 + 2 * n_prev], refs[5 + 2 * n_prev]
        layers = [(prev[2 * i], prev[2 * i + 1]) for i in range(n_prev)] + [(k_ref, v_ref)]
        for jj, (kr, vr) in enumerate(layers):
            for n in range(B_CHUNKS):
                rows = pl.ds(n, SEQ, stride=B_CHUNKS)
                nk_ref[jj, rows, :] = kr[:, n * LANE:(n + 1) * LANE]
                nv_ref[jj, rows, :] = vr[:, n * LANE:(n + 1) * LANE]


def _gqa_prompt(q, k, v, prev_kv=None):
    emit = prev_kv is not None
    prev = [a for pair in (prev_kv or []) for a in pair]
    n_layers = len(prev) // 2 + 1
    kv_spec = pl.BlockSpec((SEQ, B_KV), lambda b: (b, 0))
    att_shape = jax.ShapeDtypeStruct((M_P, D_MODEL), BF16)
    att_spec = pl.BlockSpec((SEQ, D_MODEL), lambda b: (b, 0))
    if emit:
        new_shape = jax.ShapeDtypeStruct((BATCH, n_layers, SEQ * B_CHUNKS, LANE), F32)
        new_spec = pl.BlockSpec((None, n_layers, SEQ * B_CHUNKS, LANE), lambda b: (b, 0, 0, 0))
        out_shape, out_specs = (att_shape, new_shape, new_shape), (att_spec, new_spec, new_spec)
    else:
        out_shape, out_specs = att_shape, att_spec
    return pl.pallas_call(
        functools.partial(_gqa_prompt_kernel, n_prev=len(prev) // 2, emit=emit),
        out_shape=out_shape,
        grid=(BATCH,),
        in_specs=[att_spec, kv_spec, kv_spec] + [kv_spec for _ in prev],
        out_specs=out_specs,
        scratch_shapes=[pltpu.VMEM((B_HEADS * SEQ, SEQ), F32),
                        pltpu.VMEM((B_HEADS * SEQ, SEQ), BF16),
                        pltpu.VMEM((B_HEADS * SEQ, B_HEAD_DIM), F32)],
        compiler_params=_params("arbitrary"),
        name="gqa_attn_prompt",
    )(q, k, v, *prev)


def _gqa_latent_kernel(q_ref, k_ref, v_ref, ck_ref, cv_ref, o_ref, s_scr, a_scr):
    d, tq = B_HEAD_DIM, q_ref.shape[0]
    n = pl.program_id(1)
    cached = pl.ds(n, PAST_LEN, stride=B_CHUNKS)
    ck = ck_ref[cached, :].astype(BF16)
    for g in range(B_GROUP):
        q = q_ref[:, g * d:(g + 1) * d]
        s_scr[g * tq:(g + 1) * tq, :PAST_LEN] = _qkt(q, ck)
        s_scr[g * tq:(g + 1) * tq, PAST_LEN:] = _qkt(q, k_ref[...])

    @pl.when(n >= 0)
    def _():
        e, r = _softmax_terms(s_scr[...], B_HEAD_DIM ** -0.5)
        a_scr[...] = (e * r).astype(BF16)
        o = (jnp.dot(a_scr[:, :PAST_LEN], cv_ref[cached, :].astype(BF16),
                     preferred_element_type=F32)
             + jnp.dot(a_scr[:, PAST_LEN:], v_ref[...], preferred_element_type=F32))
        for g in range(B_GROUP):
            o_ref[:, g * d:(g + 1) * d] = o[g * tq:(g + 1) * tq, :].astype(BF16)


def _gqa_latent(qk, v, ck, cv, j, tq=1024):
    d = B_HEAD_DIM
    gw = B_GROUP * d
    per = DEC_SEQ // tq
    ctx_spec = pl.BlockSpec((None, None, PAST_LEN * B_CHUNKS, LANE), lambda b, n, t: (b, j, 0, 0))
    return pl.pallas_call(
        _gqa_latent_kernel,
        out_shape=jax.ShapeDtypeStruct((M_S, D_MODEL), BF16),
        grid=(DEC_BATCH, B_KV_HEADS, per),
        in_specs=[
            pl.BlockSpec((tq, gw), lambda b, n, t: (b * per + t, n)),
            pl.BlockSpec((DEC_SEQ, d), lambda b, n, t: (b, D_MODEL // d + n)),
            pl.BlockSpec((DEC_SEQ, d), lambda b, n, t: (b, n)),
            ctx_spec, ctx_spec,
        ],
        out_specs=pl.BlockSpec((tq, gw), lambda b, n, t: (b * per + t, n)),
        scratch_shapes=[pltpu.VMEM((B_GROUP * tq, PAST_LEN + DEC_SEQ), F32),
                        pltpu.VMEM((B_GROUP * tq, PAST_LEN + DEC_SEQ), BF16)],
        compiler_params=_params("arbitrary", "arbitrary", "arbitrary"),
        name="gqa_attn_latent",
    )(qk, qk, v, ck, cv)


def _dft_tables(s):
    def cs(n):
        idx = (np.arange(n)[:, None] * np.arange(n)[None, :]) % n
        ang = 2.0 * np.pi * idx / n
        return np.cos(ang), np.sin(ang)
    cc, sc = cs(C_GROUP_DIM)
    c_s, s_s = cs(s)
    t_chan = np.concatenate([cc, sc], axis=1).astype(np.float32)
    t_pos = np.concatenate([c_s, -s_s], axis=1).astype(np.float32)
    return jnp.asarray(t_chan).astype(BF16), jnp.asarray(t_pos).astype(BF16)


def _dft_group(x, t_chan, t_pos, norm):
    xcs = jnp.dot(x, t_chan, preferred_element_type=F32).astype(BF16)
    stacked = jnp.concatenate([xcs[:, :C_GROUP_DIM], xcs[:, C_GROUP_DIM:]], axis=0)
    y = jnp.dot(t_pos, stacked, preferred_element_type=F32)
    return (y * norm).astype(BF16)


def _dft_kernel(u_ref, tc_ref, tp_ref, o_ref, *, groups, norm):
    tc, tp = tc_ref[...], tp_ref[...]
    for g in range(groups):
        sl = slice(g * C_GROUP_DIM, (g + 1) * C_GROUP_DIM)
        o_ref[:, sl] = _dft_group(u_ref[:, sl], tc, tp, norm)


def _dft(u, s, batch, row0, groups_per_step):
    t_chan, t_pos = _dft_tables(s)
    gw = groups_per_step * C_GROUP_DIM
    rb0 = row0 // s
    return pl.pallas_call(
        functools.partial(_dft_kernel, groups=groups_per_step,
                          norm=1.0 / math.sqrt(s * C_GROUP_DIM)),
        out_shape=jax.ShapeDtypeStruct((batch * s, D_MODEL), BF16),
        grid=(batch, C_GROUPS // groups_per_step),
        in_specs=[
            pl.BlockSpec((s, gw), lambda b, g: (rb0 + b, g)),
            pl.BlockSpec(t_chan.shape, lambda b, g: (0, 0)),
            pl.BlockSpec(t_pos.shape, lambda b, g: (0, 0)),
        ],
        out_specs=pl.BlockSpec((s, gw), lambda b, g: (b, g)),
        compiler_params=_params("arbitrary", "arbitrary"),
        name="dft",
    )(u, t_chan, t_pos)


ROW_GROUPS = 2


def _post_norm(y, g, b):
    mu = jnp.mean(y, axis=-1, keepdims=True)
    yc = y - mu
    var = jnp.mean(yc * yc, axis=-1, keepdims=True)
    return yc * lax.rsqrt(var + LN_EPS) * g + b


def _oproj_kernel(*refs, split_x):
    ap_ref, as_ref, w_ref = refs[:3]
    x_refs = refs[3:5] if split_x else refs[3:4]
    gate_ref, g_ref, b_ref, sc_ref, sh_ref, x1_ref, u_ref = refs[3 + len(x_refs):]
    is_prompt = pl.program_id(0) * ROW_TILE < M_P
    gate, g, b = gate_ref[...], g_ref[...], b_ref[...]
    sc1, sh = 1.0 + sc_ref[...], sh_ref[...]
    rows = ROW_TILE // ROW_GROUPS
    for r in range(ROW_GROUPS):
        sl = slice(r * rows, (r + 1) * rows)
        a = jnp.where(is_prompt, ap_ref[sl, :], as_ref[sl, :])
        mixed = jnp.dot(a, w_ref[...], preferred_element_type=F32)
        if split_x:
            x = jnp.where(is_prompt, x_refs[0][sl, :], x_refs[1][sl, :])
        else:
            x = x_refs[0][sl, :]
        x1 = _post_norm(ALPHA * x + gate * mixed, g, b)
        x1_ref[sl, :] = x1
        u_ref[sl, :] = (x1 * sc1 + sh).astype(BF16)


def _oproj(a_p, a_s, w_bf, w_layer, x, mod, ln_g, ln_b, l):
    tm = ROW_TILE
    row = pl.BlockSpec((tm, D_MODEL), lambda i: (i, 0))
    split_x = isinstance(x, tuple)
    xs = list(x) if split_x else [x]
    x_specs = [_prompt_rows_spec(tm), _latent_rows_spec(tm)] if split_x else [row]
    return pl.pallas_call(
        functools.partial(_oproj_kernel, split_x=split_x),
        out_shape=(jax.ShapeDtypeStruct((M_ALL, D_MODEL), F32),
                   jax.ShapeDtypeStruct((M_ALL, D_MODEL), BF16)),
        grid=(M_ALL // tm,),
        in_specs=[
            _prompt_rows_spec(tm), _latent_rows_spec(tm),
            pl.BlockSpec((None, D_MODEL, D_MODEL), lambda i: (w_layer, 0, 0),
                         pipeline_mode=pl.Buffered(1)),
            *x_specs,
            _mod_spec(l, 2, tm),
            _vec_spec(l, 0), _vec_spec(l, 0),
            _mod_spec(l, 4, tm),
            _mod_spec(l, 3, tm),
        ],
        out_specs=(row, row),
        compiler_params=_params("arbitrary"),
        name="oproj_ln",
    )(a_p, a_s, w_bf, *xs, mod, ln_g, ln_b, mod, mod)


MLP_FF_PIECE = 512


def _mlp_kernel(*refs, n_tiles, n_ff, last):
    u_ref, wu_ref, wd_ref, x_ref, gate_ref, g_ref, b_ref = refs[:7]
    acc, prev = refs[-2:]
    i, f = pl.program_id(0), pl.program_id(1)
    tm = u_ref.shape[0]
    close_rows = tm // n_ff

    @pl.when(jnp.logical_and(f == 0, i > 0))
    def _():
        prev[...] = acc[...]

    @pl.when(f == 0)
    def _():
        acc[...] = jnp.zeros_like(acc)

    def matmuls():
        tf = wu_ref.shape[1]
        for c in range(tf // MLP_FF_PIECE):
            cols = slice(c * MLP_FF_PIECE, (c + 1) * MLP_FF_PIECE)
            h = jnp.dot(u_ref[...], wu_ref[:, cols].astype(BF16), preferred_element_type=F32)
            h = jnp.maximum(h, 0.0)
            h = (h * h).astype(BF16)
            acc[...] += jnp.dot(h, wd_ref[cols, :].astype(BF16), preferred_element_type=F32)

    def close():
        rows = pl.ds(pl.multiple_of(f * close_rows, close_rows), close_rows)
        y = _post_norm(ALPHA * x_ref[...] + gate_ref[...] * prev[rows, :], g_ref[...], b_ref[...])
        if last:
            yp_ref, ys_ref = refs[7:9]
            prev_is_prompt = (i - 1) * tm < M_P

            @pl.when(prev_is_prompt)
            def _():
                yp_ref[...] = y

            @pl.when(jnp.logical_not(prev_is_prompt))
            def _():
                ys_ref[...] = y
        else:
            sc_ref, sh_ref, x2_ref, un_ref = refs[7:11]
            x2_ref[...] = y
            un_ref[...] = (y * (1.0 + sc_ref[...]) + sh_ref[...]).astype(BF16)

    @pl.when(i == 0)
    def _():
        matmuls()

    @pl.when(jnp.logical_and(i > 0, i < n_tiles))
    def _():
        matmuls()
        close()

    @pl.when(i == n_tiles)
    def _():
        close()


def _mlp(u, w_up, w_down, x, mod, ln_g, ln_b, l, tm=1024, tf=512):
    last = l + 1 == DEPTH
    n_tiles, n_ff = M_ALL // tm, D_FF // tf
    close_rows = tm // n_ff

    def close_block(i, f):
        return jnp.where(i == 0, 0, (i - 1) * n_ff + f)

    def mod_prev(layer, which):
        def idx(i, f):
            return (layer, _cond_row(jnp.maximum(i - 1, 0), tm), which, 0, 0)
        return pl.BlockSpec((None, None, None, 1, D_MODEL), idx)

    def ff_block(i, f):
        return jnp.where(i < n_tiles, f, n_ff - 1)

    row = pl.BlockSpec((close_rows, D_MODEL), lambda i, f: (close_block(i, f), 0))
    ins = [u, w_up, w_down, x, mod, ln_g, ln_b]
    in_specs = [
        pl.BlockSpec((tm, D_MODEL), lambda i, f: (jnp.minimum(i, n_tiles - 1), 0)),
        pl.BlockSpec((None, D_MODEL, tf), lambda i, f: (l, 0, ff_block(i, f))),
        pl.BlockSpec((None, tf, D_MODEL), lambda i, f: (l, ff_block(i, f), 0)),
        row, mod_prev(l, 5), _vec_spec(l, 1), _vec_spec(l, 1),
    ]
    if last:
        n_p = M_P // close_rows
        out_shape = (jax.ShapeDtypeStruct((M_P, D_MODEL), F32),
                     jax.ShapeDtypeStruct((M_S, D_MODEL), F32))
        out_specs = (
            pl.BlockSpec((close_rows, D_MODEL),
                         lambda i, f: (jnp.minimum(close_block(i, f), n_p - 1), 0)),
            pl.BlockSpec((close_rows, D_MODEL),
                         lambda i, f: (jnp.maximum(close_block(i, f) - n_p, 0), 0)),
        )
    else:
        ins += [mod, mod]
        in_specs += [mod_prev(l + 1, 1), mod_prev(l + 1, 0)]
        out_shape = (jax.ShapeDtypeStruct((M_ALL, D_MODEL), F32),
                     jax.ShapeDtypeStruct((M_ALL, D_MODEL), BF16))
        out_specs = (row, row)
    return pl.pallas_call(
        functools.partial(_mlp_kernel, n_tiles=n_tiles, n_ff=n_ff, last=last),
        out_shape=out_shape,
        grid=(n_tiles + 1, n_ff),
        in_specs=in_specs,
        out_specs=out_specs,
        scratch_shapes=[pltpu.VMEM((tm, D_MODEL), F32), pltpu.VMEM((tm, D_MODEL), F32)],
        compiler_params=_params("arbitrary", "arbitrary"),
        name="mlp",
    )(*ins)


def _rope_tables():
    n_freq = A_HEAD_DIM // 4
    pos = np.arange(DEC_SEQ)
    row = (pos // GRID_W).astype(np.float32)
    col = (pos % GRID_W).astype(np.float32)
    inv_freq = (ROPE_BASE ** (-np.arange(n_freq, dtype=np.float32) / n_freq)).astype(np.float32)
    ar = row[:, None] * inv_freq
    ac = col[:, None] * inv_freq
    cr, sr, cc, sc = np.cos(ar), np.sin(ar), np.cos(ac), np.sin(ac)
    z = np.zeros_like(sr)
    cos = np.concatenate([cr, cr, cc, cc], axis=1)
    sin_a = np.concatenate([-sr, z, -sc, z], axis=1)
    sin_b = np.concatenate([z, sr, z, sc], axis=1)
    return tuple(jnp.asarray(t.astype(np.float32)) for t in (cos, sin_a, sin_b))


def kernel(x_prompt, x_sample, cache_a_k, cache_a_v, cache_b_k, cache_b_v, c, c_ctx, w_mod, b_mod, ln_g, ln_b, w_up, w_down, a_w_qkv, a_w_o, a_lambda, a_subln_g, b_w_qkv, b_w_o, b_q_norm_g, b_k_norm_g, c_w_f):
    n_a = a_w_qkv.shape[0]
    n_b = b_w_qkv.shape[0]
    cond = jnp.zeros((COND_PAD, D_MODEL), F32).at[0].set(c_ctx).at[1:N_COND].set(c)
    mod = _modulation(cond, w_mod, b_mod).reshape(DEPTH, COND_PAD, N_MOD, 1, D_MODEL)
    ln_g4 = ln_g.reshape(DEPTH, 2, 1, D_MODEL)
    ln_b4 = ln_b.reshape(DEPTH, 2, 1, D_MODEL)
    rope = _rope_tables()
    w_o_bf = (_cast_bf16(a_w_o), _cast_bf16(b_w_o), _cast_bf16(c_w_f))

    ck_a = cache_a_k.reshape(DEC_BATCH, n_a, PAST_LEN * A_CHUNKS, LANE)
    cv_a = _a_v_rows(cache_a_v, DEC_BATCH, PAST_LEN)
    ck_b = cache_b_k.reshape(DEC_BATCH, n_b, PAST_LEN * B_CHUNKS, LANE)
    cv_b = cache_b_v.reshape(DEC_BATCH, n_b, PAST_LEN * B_CHUNKS, LANE)
    subln = a_subln_g.reshape(n_a, 1, 2 * A_HEAD_DIM)
    b_gains = jnp.stack([b_q_norm_g, b_k_norm_g], axis=1).reshape(n_b, 2, 1, B_HEAD_DIM)

    x = (x_prompt.reshape(M_P, D_MODEL), x_sample.reshape(M_S, D_MODEL))
    u = _embed(*x, mod)

    a_kv, b_kv = [], []
    for l in range(DEPTH):
        kind, j = l % N_MIXERS, l // N_MIXERS
        if kind == 0:
            lam_init = 0.8 - 0.6 * math.exp(-0.3 * l)
            q_p = _proj(u, a_w_qkv, j, 0, D_MODEL, 0, M_P, out_dtype=BF16, name="a_q_prompt")
            kv_p = _proj(u, a_w_qkv, j, D_MODEL, 2 * D_MODEL, 0, M_P, out_dtype=F32,
                         name="a_kv_prompt")
            qk_s = _proj(u, a_w_qkv, j, 0, 2 * D_MODEL, M_P, M_S, out_dtype=BF16, rope=rope,
                         name="a_qk_latent")
            v_s = _proj(u, a_w_qkv, j, 2 * D_MODEL, D_MODEL, M_P, M_S, out_dtype=BF16,
                        name="a_v_latent")
            if j + 1 < n_a:
                att_p = _diff_prompt(a_lambda, subln, q_p, kv_p, j, lam_init)
                a_kv.append(kv_p)
            else:
                att_p, new_a_k, new_a_v = _diff_prompt(a_lambda, subln, q_p, kv_p, j, lam_init,
                                                       prev_kv=a_kv)
            att_s = _diff_latent(a_lambda, subln, qk_s, v_s, ck_a, cv_a, j, lam_init)
            w_o = w_o_bf[0]
        elif kind == 1:
            gj = b_gains[j]
            q_p = _proj(u, b_w_qkv, j, 0, D_MODEL, 0, M_P, out_dtype=BF16, gains=gj,
                        gain_split=D_MODEL, name="b_q_prompt")
            k_p = _proj(u, b_w_qkv, j, D_MODEL, B_KV, 0, M_P, out_dtype=F32, tn=B_KV,
                        gains=gj[1:], gain_split=B_KV, name="b_k_prompt")
            v_p = _proj(u, b_w_qkv, j, D_MODEL + B_KV, B_KV, 0, M_P, out_dtype=F32, tn=B_KV,
                        name="b_v_prompt")
            qk_s = _proj(u, b_w_qkv, j, 0, D_MODEL + B_KV, M_P, M_S, out_dtype=BF16, tn=B_KV,
                         gains=gj, gain_split=D_MODEL, rope=rope, name="b_qk_latent")
            v_s = _proj(u, b_w_qkv, j, D_MODEL + B_KV, B_KV, M_P, M_S, out_dtype=BF16, tn=B_KV,
                        name="b_v_latent")
            if j + 1 < n_b:
                att_p = _gqa_prompt(q_p, k_p, v_p)
                b_kv.append((k_p, v_p))
            else:
                att_p, new_b_k, new_b_v = _gqa_prompt(q_p, k_p, v_p, prev_kv=b_kv)
            att_s = _gqa_latent(qk_s, v_s, ck_b, cv_b, j)
            w_o = w_o_bf[1]
        else:
            att_p = _dft(u, SEQ, BATCH, 0, C_GROUPS)
            att_s = _dft(u, DEC_SEQ, DEC_BATCH, M_P, 1)
            w_o = w_o_bf[2]
        x, u = _oproj(att_p, att_s, w_o, j, x, mod, ln_g4, ln_b4, l)
        x, u = _mlp(u, w_up, w_down, x, mod, ln_g4, ln_b4, l)

    y_p, y_s = x, u
    return (y_p.reshape(BATCH, SEQ, D_MODEL), y_s.reshape(DEC_BATCH, DEC_SEQ, D_MODEL),
            new_a_k.reshape(BATCH, n_a, SEQ, A_HEADS, 2, A_HEAD_DIM),
            _a_v_from_rows(new_a_v, BATCH, SEQ),
            new_b_k.reshape(BATCH, n_b, SEQ, B_KV_HEADS, B_HEAD_DIM),
            new_b_v.reshape(BATCH, n_b, SEQ, B_KV_HEADS, B_HEAD_DIM))
```

```python
import functools
import math

import jax
import jax.numpy as jnp
import numpy as np
from jax import lax
from jax.experimental import pallas as pl
from jax.experimental.pallas import tpu as pltpu

D_MODEL = 2048
BATCH = 16
SEQ = 256
DEPTH = 4
DEC_BATCH = 2
DEC_SEQ = 1024
PAST_LEN = 512
GRID_W = 64
N_MIXERS = 3
A_HEAD_DIM = 128
A_HEADS = D_MODEL // (2 * A_HEAD_DIM)
B_HEAD_DIM = 128
B_HEADS = D_MODEL // B_HEAD_DIM
B_KV_HEADS = B_HEADS // 4
B_GROUP = B_HEADS // B_KV_HEADS
B_KV = B_KV_HEADS * B_HEAD_DIM
C_GROUPS = 4
C_GROUP_DIM = D_MODEL // C_GROUPS
D_FF = 4 * D_MODEL
ROPE_BASE = 10000.0
ALPHA = (2 * DEPTH) ** 0.25
LN_EPS = 1e-5
RMS_EPS = 1e-6
N_MOD = 6
LOG2E = 1.4426950408889634

M_P = BATCH * SEQ
M_S = DEC_BATCH * DEC_SEQ
M_ALL = M_P + M_S
N_COND = 1 + DEC_BATCH
COND_PAD = 8

LANE = 128
VMEM_LIMIT = 58 * 1024 * 1024
ROW_TILE = 512

BF16 = jnp.bfloat16
F32 = jnp.float32


def _params(*sem, vmem=VMEM_LIMIT):
    return pltpu.CompilerParams(dimension_semantics=sem, vmem_limit_bytes=vmem)


def _cond_row(i, tm):
    start = i * tm
    return jnp.where(start < M_P, 0, 1 + (start - M_P) // DEC_SEQ)


def _mod_spec(l, which, tm):
    def idx(*g):
        return (l, _cond_row(g[0], tm), which, 0, 0)
    return pl.BlockSpec((None, None, None, 1, D_MODEL), idx)


def _vec_spec(l, which):
    return pl.BlockSpec((None, None, 1, D_MODEL), lambda *g: (l, which, 0, 0))


def _prompt_rows_spec(tm):
    last = M_P // tm - 1
    return pl.BlockSpec((tm, D_MODEL), lambda i: (jnp.minimum(i, last), 0))


def _latent_rows_spec(tm):
    first = M_P // tm
    return pl.BlockSpec((tm, D_MODEL), lambda i: (jnp.maximum(i - first, 0), 0))


def _mod_kernel(c_ref, w_ref, b_ref, o_ref):
    c = c_ref[...]
    s = c * (1.0 / (1.0 + jnp.exp(-c)))
    o_ref[...] = jnp.dot(s.astype(BF16), w_ref[...].astype(BF16),
                         preferred_element_type=F32) + b_ref[...]


def _modulation(cond, w_mod, b_mod):
    tn = 1024
    n_out = N_MOD * D_MODEL
    return pl.pallas_call(
        _mod_kernel,
        out_shape=jax.ShapeDtypeStruct((DEPTH, COND_PAD, n_out), F32),
        grid=(DEPTH, n_out // tn),
        in_specs=[
            pl.BlockSpec((COND_PAD, D_MODEL), lambda l, n: (0, 0)),
            pl.BlockSpec((None, D_MODEL, tn), lambda l, n: (l, 0, n)),
            pl.BlockSpec((None, 1, tn), lambda l, n: (l, 0, n)),
        ],
        out_specs=pl.BlockSpec((None, COND_PAD, tn), lambda l, n: (l, 0, n)),
        compiler_params=_params("arbitrary", "arbitrary"),
        name="modulation",
    )(cond, w_mod, b_mod.reshape(DEPTH, 1, n_out))


def _cast_kernel(w_ref, o_ref):
    o_ref[...] = w_ref[...].astype(BF16)


def _cast_bf16(w, rows=512):
    n_l, k, n = w.shape
    spec = pl.BlockSpec((None, rows, n), lambda l, r: (l, r, 0))
    return pl.pallas_call(
        _cast_kernel,
        out_shape=jax.ShapeDtypeStruct(w.shape, BF16),
        grid=(n_l, k // rows),
        in_specs=[spec],
        out_specs=spec,
        compiler_params=_params("arbitrary", "arbitrary"),
        name="cast_weight",
    )(w)


def _embed_kernel(xp_ref, xs_ref, sc_ref, sh_ref, u_ref):
    x = jnp.where(pl.program_id(0) * ROW_TILE < M_P, xp_ref[...], xs_ref[...])
    u_ref[...] = (x * (1.0 + sc_ref[...]) + sh_ref[...]).astype(BF16)


def _embed(xp, xs, mod):
    tm = ROW_TILE
    return pl.pallas_call(
        _embed_kernel,
        out_shape=jax.ShapeDtypeStruct((M_ALL, D_MODEL), BF16),
        grid=(M_ALL // tm,),
        in_specs=[_prompt_rows_spec(tm), _latent_rows_spec(tm),
                  _mod_spec(0, 1, tm), _mod_spec(0, 0, tm)],
        out_specs=pl.BlockSpec((tm, D_MODEL), lambda i: (i, 0)),
        compiler_params=_params("arbitrary"),
        name="embed",
    )(xp, xs, mod, mod)


def _rms_chunk(x, g):
    ms = jnp.mean(x * x, axis=-1, keepdims=True)
    return x * lax.rsqrt(ms + RMS_EPS) * g


def _rope_chunk(x, cos, sin_a, sin_b):
    return x * cos + pltpu.roll(x, LANE - 32, 1) * sin_a + pltpu.roll(x, 32, 1) * sin_b


def _proj_kernel(*refs, tn, rms, rope):
    it = iter(refs)
    u_ref, w_ref = next(it), next(it)
    g_ref = next(it) if rms else None
    tabs = (next(it), next(it), next(it)) if rope else None
    o_ref, wbf = next(it), next(it)

    @pl.when(pl.program_id(1) == 0)
    def _():
        wbf[...] = w_ref[...].astype(BF16)

    acc = jnp.dot(u_ref[...], wbf[...], preferred_element_type=F32)
    chunks = [acc[:, c * LANE:(c + 1) * LANE] for c in range(tn // LANE)]
    if rms and rope:
        g = g_ref[...]
        tab = tuple(t[...] for t in tabs)
        rinv = [lax.rsqrt(jnp.mean(x * x, axis=-1, keepdims=True) + RMS_EPS) for x in chunks]

        @pl.when(pl.program_id(1) >= 0)
        def _():
            for c, (x, r) in enumerate(zip(chunks, rinv)):
                o_ref[:, c * LANE:(c + 1) * LANE] = (_rope_chunk(x * g, *tab) * r).astype(o_ref.dtype)
    elif rms:
        g = g_ref[...]
        for c, x in enumerate(chunks):
            o_ref[:, c * LANE:(c + 1) * LANE] = _rms_chunk(x, g).astype(o_ref.dtype)
    elif rope:
        tab = tuple(t[...] for t in tabs)
        for c, x in enumerate(chunks):
            o_ref[:, c * LANE:(c + 1) * LANE] = _rope_chunk(x, *tab).astype(o_ref.dtype)
    else:
        o_ref[...] = acc.astype(o_ref.dtype)


def _proj(u, w, w_layer, col0, ncols, m0, m_rows, *, out_dtype, tn=1024, tm=1024,
          gains=None, gain_split=None, rope=None, name="proj"):
    k = u.shape[1]
    assert col0 % tn == 0 and ncols % tn == 0 and m0 % tm == 0 and m_rows % tm == 0
    n_t, m_t, mt0, nt0 = ncols // tn, m_rows // tm, m0 // tm, col0 // tn
    ins = [u, w]
    in_specs = [
        pl.BlockSpec((tm, k), lambda n, m: (mt0 + m, 0)),
        pl.BlockSpec((None, k, tn), lambda n, m: (w_layer, 0, nt0 + n)),
    ]
    if gains is not None:
        ins.append(gains)
        in_specs.append(pl.BlockSpec(
            (None, 1, LANE), lambda n, m: (jnp.where(n * tn < gain_split, 0, 1), 0, 0)))
    if rope is not None:
        assert m0 >= M_P
        per = DEC_SEQ // tm
        for t in rope:
            ins.append(t)
            in_specs.append(pl.BlockSpec((tm, LANE), lambda n, m: ((mt0 + m) % per, 0)))
    kern = functools.partial(_proj_kernel, tn=tn, rms=gains is not None, rope=rope is not None)
    return pl.pallas_call(
        kern,
        out_shape=jax.ShapeDtypeStruct((m_rows, ncols), out_dtype),
        grid=(n_t, m_t),
        in_specs=in_specs,
        out_specs=pl.BlockSpec((tm, tn), lambda n, m: (m, n)),
        scratch_shapes=[pltpu.VMEM((k, tn), BF16)],
        compiler_params=_params("arbitrary", "arbitrary"),
        name=name,
    )(*ins)


def _qkt(q, k):
    return lax.dot_general(q, k, (((1,), (1,)), ((), ())), preferred_element_type=F32)


def _softmax_terms(s, scale):
    m = s.max(axis=-1, keepdims=True)
    e = jnp.exp2((s - m) * (scale * LOG2E))
    return e, 1.0 / e.sum(axis=-1, keepdims=True)


def _diff_lambda(lp_ref, lam_init):
    lp = lp_ref[...]
    s1 = jnp.sum(lp[0:1] * lp[1:2], axis=-1, keepdims=True)
    s2 = jnp.sum(lp[2:3] * lp[3:4], axis=-1, keepdims=True)
    return jnp.exp(s1) - jnp.exp(s2) + lam_init


def _diff_weights(s, lam):
    half = s.shape[0] // 2
    e, r = _softmax_terms(s, A_HEAD_DIM ** -0.5)
    return (e[:half] * r[:half] - e[half:] * (lam * r[half:])).astype(BF16)


def _sub_norm(o, gain):
    ms = jnp.mean(o * o, axis=-1, keepdims=True)
    return (o * lax.rsqrt(ms + RMS_EPS) * gain).astype(BF16)


A_CHUNKS = D_MODEL // LANE
B_CHUNKS = B_KV // LANE


def _a_v_row(c):
    return (c % 2) * A_HEADS + c // 2


def _a_v_rows(v, batch, seq):
    n_l = v.shape[1]
    v = v.reshape(batch, n_l, seq, A_HEADS, 2, LANE).transpose(0, 1, 2, 4, 3, 5)
    return v.reshape(batch, n_l, seq * A_CHUNKS, LANE)


def _a_v_from_rows(r, batch, seq):
    n_l = r.shape[1]
    v = r.reshape(batch, n_l, seq, 2, A_HEADS, LANE).transpose(0, 1, 2, 4, 3, 5)
    return v.reshape(batch, n_l, seq, A_HEADS, 2 * LANE)


def _diff_prompt_kernel(*refs, lam_init, n_prev, emit):
    lp_ref, g_ref, q_ref, k_ref, v_ref = refs[:5]
    prev = refs[5:5 + n_prev]
    o_ref = refs[5 + n_prev]
    s_scr, a_scr, o_scr = refs[-3:]
    lam = _diff_lambda(lp_ref, lam_init)
    gain = g_ref[...] * (1.0 - lam_init)
    w = 2 * A_HEAD_DIM
    for h in range(A_HEADS):
        for p in range(2):
            cols = slice(h * w + p * A_HEAD_DIM, h * w + (p + 1) * A_HEAD_DIM)
            s_scr[(p * A_HEADS + h) * SEQ:(p * A_HEADS + h + 1) * SEQ, :] = _qkt(
                q_ref[:, cols], k_ref[:, cols].astype(BF16))

    @pl.when(pl.program_id(0) >= 0)
    def _():
        a_scr[...] = _diff_weights(s_scr[...], lam)
        for h in range(A_HEADS):
            rows = slice(h * SEQ, (h + 1) * SEQ)
            o_scr[rows, :] = jnp.dot(a_scr[rows, :], v_ref[:, h * w:(h + 1) * w].astype(BF16),
                                     preferred_element_type=F32)
        y = _sub_norm(o_scr[...], gain)
        for h in range(A_HEADS):
            o_ref[:, h * w:(h + 1) * w] = y[h * SEQ:(h + 1) * SEQ, :]

    if emit:
        nk_ref, nv_ref = refs[6 + n_prev], refs[7 + n_prev]
        layers = [(p, 0, p, D_MODEL) for p in prev] + [(k_ref, 0, v_ref, 0)]
        for jj, (kr, k0, vr, v0) in enumerate(layers):
            for c in range(A_CHUNKS):
                nk_ref[jj, pl.ds(c, SEQ, stride=A_CHUNKS), :] = kr[:, k0 + c * LANE:k0 + (c + 1) * LANE]
                nv_ref[jj, pl.ds(_a_v_row(c), SEQ, stride=A_CHUNKS), :] = (
                    vr[:, v0 + c * LANE:v0 + (c + 1) * LANE])


def _diff_prompt(lp, g, q, kv, j, lam_init, prev_kv=None):
    emit = prev_kv is not None
    prev_kv = list(prev_kv or [])
    n_layers = len(prev_kv) + 1
    att_shape = jax.ShapeDtypeStruct((M_P, D_MODEL), BF16)
    att_spec = pl.BlockSpec((SEQ, D_MODEL), lambda b: (b, 0))
    if emit:
        new_shape = jax.ShapeDtypeStruct((BATCH, n_layers, SEQ * A_CHUNKS, LANE), F32)
        new_spec = pl.BlockSpec((None, n_layers, SEQ * A_CHUNKS, LANE), lambda b: (b, 0, 0, 0))
        out_shape, out_specs = (att_shape, new_shape, new_shape), (att_spec, new_spec, new_spec)
    else:
        out_shape, out_specs = att_shape, att_spec
    return pl.pallas_call(
        functools.partial(_diff_prompt_kernel, lam_init=lam_init, n_prev=len(prev_kv), emit=emit),
        out_shape=out_shape,
        grid=(BATCH,),
        in_specs=[
            pl.BlockSpec((None, 4, A_HEAD_DIM), lambda b: (j, 0, 0)),
            pl.BlockSpec((None, 1, 2 * A_HEAD_DIM), lambda b: (j, 0, 0)),
            pl.BlockSpec((SEQ, D_MODEL), lambda b: (b, 0)),
            pl.BlockSpec((SEQ, D_MODEL), lambda b: (b, 0)),
            pl.BlockSpec((SEQ, D_MODEL), lambda b: (b, 1)),
        ] + [pl.BlockSpec((SEQ, 2 * D_MODEL), lambda b: (b, 0)) for _ in prev_kv],
        out_specs=out_specs,
        scratch_shapes=[pltpu.VMEM((2 * A_HEADS * SEQ, SEQ), F32),
                        pltpu.VMEM((A_HEADS * SEQ, SEQ), BF16),
                        pltpu.VMEM((A_HEADS * SEQ, 2 * A_HEAD_DIM), F32)],
        compiler_params=_params("arbitrary"),
        name="diff_attn_prompt",
    )(lp, g, q, kv, kv, *prev_kv)


def _diff_latent_kernel(lp_ref, g_ref, q_ref, k_ref, ck_ref, v_ref, cv_ref, o_ref,
                        s_even, s_odd, a_scr, *, lam_init, n_items):
    lam = _diff_lambda(lp_ref, lam_init)
    gain = g_ref[...] * (1.0 - lam_init)
    t = pl.program_id(0)
    tq, d = q_ref.shape[0], A_HEAD_DIM

    def cached(ref, r0, r1):
        halves = [ref[pl.ds(r, PAST_LEN, stride=A_CHUNKS), :] for r in (r0, r1)]
        return jnp.concatenate(halves, axis=1).astype(BF16)

    def scores(dst):
        h = t % A_HEADS
        ck = cached(ck_ref, 2 * h, 2 * h + 1)
        for p in range(2):
            q = q_ref[:, p * d:(p + 1) * d]
            dst[p * tq:(p + 1) * tq, :PAST_LEN] = _qkt(q, ck[:, p * d:(p + 1) * d])
            dst[p * tq:(p + 1) * tq, PAST_LEN:] = _qkt(q, k_ref[:, p * d:(p + 1) * d])

    def finish(src):
        h = (t - 1) % A_HEADS
        a_scr[...] = _diff_weights(src[...], lam)
        cv = cached(cv_ref, h, A_HEADS + h)
        o = (jnp.dot(a_scr[:, :PAST_LEN], cv, preferred_element_type=F32)
             + jnp.dot(a_scr[:, PAST_LEN:], v_ref[...], preferred_element_type=F32))
        o_ref[...] = _sub_norm(o, gain)

    bufs = (s_even, s_odd)

    @pl.when(t == 0)
    def _():
        scores(s_even)

    middle = jnp.logical_and(t > 0, t < n_items)
    for parity in (0, 1):
        @pl.when(jnp.logical_and(middle, t % 2 == parity))
        def _():
            scores(bufs[parity])
            finish(bufs[1 - parity])

    @pl.when(t == n_items)
    def _():
        finish(bufs[(n_items - 1) % 2])


def _diff_latent(lp, g, qk, v, ck, cv, j, lam_init):
    w = 2 * A_HEAD_DIM
    n_items = DEC_BATCH * A_HEADS

    def cur(t):
        it = jnp.minimum(t, n_items - 1)
        return it // A_HEADS, it % A_HEADS

    def done(t):
        it = jnp.maximum(t - 1, 0)
        return it // A_HEADS, it % A_HEADS

    ctx = (None, None, PAST_LEN * A_CHUNKS, LANE)
    return pl.pallas_call(
        functools.partial(_diff_latent_kernel, lam_init=lam_init, n_items=n_items),
        out_shape=jax.ShapeDtypeStruct((M_S, D_MODEL), BF16),
        grid=(n_items + 1,),
        in_specs=[
            pl.BlockSpec((None, 4, A_HEAD_DIM), lambda t: (j, 0, 0)),
            pl.BlockSpec((None, 1, w), lambda t: (j, 0, 0)),
            pl.BlockSpec((DEC_SEQ, w), lambda t: cur(t)),
            pl.BlockSpec((DEC_SEQ, w), lambda t: (cur(t)[0], A_HEADS + cur(t)[1])),
            pl.BlockSpec(ctx, lambda t: (cur(t)[0], j, 0, 0), pipeline_mode=pl.Buffered(1)),
            pl.BlockSpec((DEC_SEQ, w), lambda t: done(t)),
            pl.BlockSpec(ctx, lambda t: (done(t)[0], j, 0, 0), pipeline_mode=pl.Buffered(1)),
        ],
        out_specs=pl.BlockSpec((DEC_SEQ, w), lambda t: done(t)),
        scratch_shapes=[pltpu.VMEM((2 * DEC_SEQ, PAST_LEN + DEC_SEQ), F32),
                        pltpu.VMEM((2 * DEC_SEQ, PAST_LEN + DEC_SEQ), F32),
                        pltpu.VMEM((DEC_SEQ, PAST_LEN + DEC_SEQ), BF16)],
        compiler_params=_params("arbitrary"),
        name="diff_attn_latent",
    )(lp, g, qk, qk, ck, v, cv)


def _gqa_prompt_kernel(*refs, n_prev, emit):
    q_ref, k_ref, v_ref = refs[:3]
    prev = refs[3:3 + 2 * n_prev]
    o_ref = refs[3 + 2 * n_prev]
    s_scr, a_scr, o_scr = refs[-3:]
    d = B_HEAD_DIM
    for n in range(B_KV_HEADS):
        k = k_ref[:, n * d:(n + 1) * d].astype(BF16)
        for g in range(B_GROUP):
            hd = n * B_GROUP + g
            s_scr[hd * SEQ:(hd + 1) * SEQ, :] = _qkt(q_ref[:, hd * d:(hd + 1) * d], k)

    @pl.when(pl.program_id(0) >= 0)
    def _():
        e, r = _softmax_terms(s_scr[...], B_HEAD_DIM ** -0.5)
        a_scr[...] = (e * r).astype(BF16)
        group_rows = B_GROUP * SEQ
        for n in range(B_KV_HEADS):
            rows = slice(n * group_rows, (n + 1) * group_rows)
            o_scr[rows, :] = jnp.dot(a_scr[rows, :], v_ref[:, n * d:(n + 1) * d].astype(BF16),
                                     preferred_element_type=F32)
        for hd in range(B_HEADS):
            o_ref[:, hd * d:(hd + 1) * d] = o_scr[hd * SEQ:(hd + 1) * SEQ, :].astype(BF16)

    if emit:
        nk_ref, nv_ref = refs[4 + 2 * n_prev], refs[5 + 2 * n_prev]
        layers = [(prev[2 * i], prev[2 * i + 1]) for i in range(n_prev)] + [(k_ref, v_ref)]
        for jj, (kr, vr) in enumerate(layers):
            for n in range(B_CHUNKS):
                rows = pl.ds(n, SEQ, stride=B_CHUNKS)
                nk_ref[jj, rows, :] = kr[:, n * LANE:(n + 1) * LANE]
                nv_ref[jj, rows, :] = vr[:, n * LANE:(n + 1) * LANE]


def _gqa_prompt(q, k, v, prev_kv=None):
    emit = prev_kv is not None
    prev = [a for pair in (prev_kv or []) for a in pair]
    n_layers = len(prev) // 2 + 1
    kv_spec = pl.BlockSpec((SEQ, B_KV), lambda b: (b, 0))
    att_shape = jax.ShapeDtypeStruct((M_P, D_MODEL), BF16)
    att_spec = pl.BlockSpec((SEQ, D_MODEL), lambda b: (b, 0))
    if emit:
        new_shape = jax.ShapeDtypeStruct((BATCH, n_layers, SEQ * B_CHUNKS, LANE), F32)
        new_spec = pl.BlockSpec((None, n_layers, SEQ * B_CHUNKS, LANE), lambda b: (b, 0, 0, 0))
        out_shape, out_specs = (att_shape, new_shape, new_shape), (att_spec, new_spec, new_spec)
    else:
        out_shape, out_specs = att_shape, att_spec
    return pl.pallas_call(
        functools.partial(_gqa_prompt_kernel, n_prev=len(prev) // 2, emit=emit),
        out_shape=out_shape,
        grid=(BATCH,),
        in_specs=[att_spec, kv_spec, kv_spec] + [kv_spec for _ in prev],
        out_specs=out_specs,
        scratch_shapes=[pltpu.VMEM((B_HEADS * SEQ, SEQ), F32),
                        pltpu.VMEM((B_HEADS * SEQ, SEQ), BF16),
                        pltpu.VMEM((B_HEADS * SEQ, B_HEAD_DIM), F32)],
        compiler_params=_params("arbitrary"),
        name="gqa_attn_prompt",
    )(q, k, v, *prev)


def _gqa_latent_kernel(q_ref, k_ref, v_ref, ck_ref, cv_ref, o_ref, s_scr, a_scr):
    d, tq = B_HEAD_DIM, q_ref.shape[0]
    n = pl.program_id(1)
    cached = pl.ds(n, PAST_LEN, stride=B_CHUNKS)
    ck = ck_ref[cached, :].astype(BF16)
    for g in range(B_GROUP):
        q = q_ref[:, g * d:(g + 1) * d]
        s_scr[g * tq:(g + 1) * tq, :PAST_LEN] = _qkt(q, ck)
        s_scr[g * tq:(g + 1) * tq, PAST_LEN:] = _qkt(q, k_ref[...])

    @pl.when(n >= 0)
    def _():
        e, r = _softmax_terms(s_scr[...], B_HEAD_DIM ** -0.5)
        a_scr[...] = (e * r).astype(BF16)
        o = (jnp.dot(a_scr[:, :PAST_LEN], cv_ref[cached, :].astype(BF16),
                     preferred_element_type=F32)
             + jnp.dot(a_scr[:, PAST_LEN:], v_ref[...], preferred_element_type=F32))
        for g in range(B_GROUP):
            o_ref[:, g * d:(g + 1) * d] = o[g * tq:(g + 1) * tq, :].astype(BF16)


def _gqa_latent(qk, v, ck, cv, j, tq=1024):
    d = B_HEAD_DIM
    gw = B_GROUP * d
    per = DEC_SEQ // tq
    ctx_spec = pl.BlockSpec((None, None, PAST_LEN * B_CHUNKS, LANE), lambda b, n, t: (b, j, 0, 0))
    return pl.pallas_call(
        _gqa_latent_kernel,
        out_shape=jax.ShapeDtypeStruct((M_S, D_MODEL), BF16),
        grid=(DEC_BATCH, B_KV_HEADS, per),
        in_specs=[
            pl.BlockSpec((tq, gw), lambda b, n, t: (b * per + t, n)),
            pl.BlockSpec((DEC_SEQ, d), lambda b, n, t: (b, D_MODEL // d + n)),
            pl.BlockSpec((DEC_SEQ, d), lambda b, n, t: (b, n)),
            ctx_spec, ctx_spec,
        ],
        out_specs=pl.BlockSpec((tq, gw), lambda b, n, t: (b * per + t, n)),
        scratch_shapes=[pltpu.VMEM((B_GROUP * tq, PAST_LEN + DEC_SEQ), F32),
                        pltpu.VMEM((B_GROUP * tq, PAST_LEN + DEC_SEQ), BF16)],
        compiler_params=_params("arbitrary", "arbitrary", "arbitrary"),
        name="gqa_attn_latent",
    )(qk, qk, v, ck, cv)


def _dft_tables(s):
    def cs(n):
        idx = (np.arange(n)[:, None] * np.arange(n)[None, :]) % n
        ang = 2.0 * np.pi * idx / n
        return np.cos(ang), np.sin(ang)
    cc, sc = cs(C_GROUP_DIM)
    c_s, s_s = cs(s)
    t_chan = np.concatenate([cc, sc], axis=1).astype(np.float32)
    t_pos = np.concatenate([c_s, -s_s], axis=1).astype(np.float32)
    return jnp.asarray(t_chan).astype(BF16), jnp.asarray(t_pos).astype(BF16)


def _dft_group(x, t_chan, t_pos, norm):
    xcs = jnp.dot(x, t_chan, preferred_element_type=F32).astype(BF16)
    stacked = jnp.concatenate([xcs[:, :C_GROUP_DIM], xcs[:, C_GROUP_DIM:]], axis=0)
    y = jnp.dot(t_pos, stacked, preferred_element_type=F32)
    return (y * norm).astype(BF16)


def _dft_kernel(u_ref, tc_ref, tp_ref, o_ref, *, groups, norm):
    tc, tp = tc_ref[...], tp_ref[...]
    for g in range(groups):
        sl = slice(g * C_GROUP_DIM, (g + 1) * C_GROUP_DIM)
        o_ref[:, sl] = _dft_group(u_ref[:, sl], tc, tp, norm)


def _dft(u, s, batch, row0, groups_per_step):
    t_chan, t_pos = _dft_tables(s)
    gw = groups_per_step * C_GROUP_DIM
    rb0 = row0 // s
    return pl.pallas_call(
        functools.partial(_dft_kernel, groups=groups_per_step,
                          norm=1.0 / math.sqrt(s * C_GROUP_DIM)),
        out_shape=jax.ShapeDtypeStruct((batch * s, D_MODEL), BF16),
        grid=(batch, C_GROUPS // groups_per_step),
        in_specs=[
            pl.BlockSpec((s, gw), lambda b, g: (rb0 + b, g)),
            pl.BlockSpec(t_chan.shape, lambda b, g: (0, 0)),
            pl.BlockSpec(t_pos.shape, lambda b, g: (0, 0)),
        ],
        out_specs=pl.BlockSpec((s, gw), lambda b, g: (b, g)),
        compiler_params=_params("arbitrary", "arbitrary"),
        name="dft",
    )(u, t_chan, t_pos)


ROW_GROUPS = 2


def _post_norm(y, g, b):
    mu = jnp.mean(y, axis=-1, keepdims=True)
    yc = y - mu
    var = jnp.mean(yc * yc, axis=-1, keepdims=True)
    return yc * lax.rsqrt(var + LN_EPS) * g + b


def _oproj_kernel(*refs, split_x):
    ap_ref, as_ref, w_ref = refs[:3]
    x_refs = refs[3:5] if split_x else refs[3:4]
    gate_ref, g_ref, b_ref, sc_ref, sh_ref, x1_ref, u_ref = refs[3 + len(x_refs):]
    is_prompt = pl.program_id(0) * ROW_TILE < M_P
    gate, g, b = gate_ref[...], g_ref[...], b_ref[...]
    sc1, sh = 1.0 + sc_ref[...], sh_ref[...]
    rows = ROW_TILE // ROW_GROUPS
    for r in range(ROW_GROUPS):
        sl = slice(r * rows, (r + 1) * rows)
        a = jnp.where(is_prompt, ap_ref[sl, :], as_ref[sl, :])
        mixed = jnp.dot(a, w_ref[...], preferred_element_type=F32)
        if split_x:
            x = jnp.where(is_prompt, x_refs[0][sl, :], x_refs[1][sl, :])
        else:
            x = x_refs[0][sl, :]
        x1 = _post_norm(ALPHA * x + gate * mixed, g, b)
        x1_ref[sl, :] = x1
        u_ref[sl, :] = (x1 * sc1 + sh).astype(BF16)


def _oproj(a_p, a_s, w_bf, w_layer, x, mod, ln_g, ln_b, l):
    tm = ROW_TILE
    row = pl.BlockSpec((tm, D_MODEL), lambda i: (i, 0))
    split_x = isinstance(x, tuple)
    xs = list(x) if split_x else [x]
    x_specs = [_prompt_rows_spec(tm), _latent_rows_spec(tm)] if split_x else [row]
    return pl.pallas_call(
        functools.partial(_oproj_kernel, split_x=split_x),
        out_shape=(jax.ShapeDtypeStruct((M_ALL, D_MODEL), F32),
                   jax.ShapeDtypeStruct((M_ALL, D_MODEL), BF16)),
        grid=(M_ALL // tm,),
        in_specs=[
            _prompt_rows_spec(tm), _latent_rows_spec(tm),
            pl.BlockSpec((None, D_MODEL, D_MODEL), lambda i: (w_layer, 0, 0),
                         pipeline_mode=pl.Buffered(1)),
            *x_specs,
            _mod_spec(l, 2, tm),
            _vec_spec(l, 0), _vec_spec(l, 0),
            _mod_spec(l, 4, tm),
            _mod_spec(l, 3, tm),
        ],
        out_specs=(row, row),
        compiler_params=_params("arbitrary"),
        name="oproj_ln",
    )(a_p, a_s, w_bf, *xs, mod, ln_g, ln_b, mod, mod)


MLP_FF_PIECE = 512


def _mlp_kernel(*refs, n_tiles, n_ff, last):
    u_ref, wu_ref, wd_ref, x_ref, gate_ref, g_ref, b_ref = refs[:7]
    acc, prev = refs[-2:]
    i, f = pl.program_id(0), pl.program_id(1)
    tm = u_ref.shape[0]
    close_rows = tm // n_ff

    @pl.when(jnp.logical_and(f == 0, i > 0))
    def _():
        prev[...] = acc[...]

    @pl.when(f == 0)
    def _():
        acc[...] = jnp.zeros_like(acc)

    def matmuls():
        tf = wu_ref.shape[1]
        for c in range(tf // MLP_FF_PIECE):
            cols = slice(c * MLP_FF_PIECE, (c + 1) * MLP_FF_PIECE)
            h = jnp.dot(u_ref[...], wu_ref[:, cols].astype(BF16), preferred_element_type=F32)
            h = jnp.maximum(h, 0.0)
            h = (h * h).astype(BF16)
            acc[...] += jnp.dot(h, wd_ref[cols, :].astype(BF16), preferred_element_type=F32)

    def close():
        rows = pl.ds(pl.multiple_of(f * close_rows, close_rows), close_rows)
        y = _post_norm(ALPHA * x_ref[...] + gate_ref[...] * prev[rows, :], g_ref[...], b_ref[...])
        if last:
            yp_ref, ys_ref = refs[7:9]
            prev_is_prompt = (i - 1) * tm < M_P

            @pl.when(prev_is_prompt)
            def _():
                yp_ref[...] = y

            @pl.when(jnp.logical_not(prev_is_prompt))
            def _():
                ys_ref[...] = y
        else:
            sc_ref, sh_ref, x2_ref, un_ref = refs[7:11]
            x2_ref[...] = y
            un_ref[...] = (y * (1.0 + sc_ref[...]) + sh_ref[...]).astype(BF16)

    @pl.when(i == 0)
    def _():
        matmuls()

    @pl.when(jnp.logical_and(i > 0, i < n_tiles))
    def _():
        matmuls()
        close()

    @pl.when(i == n_tiles)
    def _():
        close()


def _mlp(u, w_up, w_down, x, mod, ln_g, ln_b, l, tm=1024, tf=512):
    last = l + 1 == DEPTH
    n_tiles, n_ff = M_ALL // tm, D_FF // tf
    close_rows = tm // n_ff

    def close_block(i, f):
        return jnp.where(i == 0, 0, (i - 1) * n_ff + f)

    def mod_prev(layer, which):
        def idx(i, f):
            return (layer, _cond_row(jnp.maximum(i - 1, 0), tm), which, 0, 0)
        return pl.BlockSpec((None, None, None, 1, D_MODEL), idx)

    def ff_block(i, f):
        return jnp.where(i < n_tiles, f, n_ff - 1)

    row = pl.BlockSpec((close_rows, D_MODEL), lambda i, f: (close_block(i, f), 0))
    ins = [u, w_up, w_down, x, mod, ln_g, ln_b]
    in_specs = [
        pl.BlockSpec((tm, D_MODEL), lambda i, f: (jnp.minimum(i, n_tiles - 1), 0)),
        pl.BlockSpec((None, D_MODEL, tf), lambda i, f: (l, 0, ff_block(i, f))),
        pl.BlockSpec((None, tf, D_MODEL), lambda i, f: (l, ff_block(i, f), 0)),
        row, mod_prev(l, 5), _vec_spec(l, 1), _vec_spec(l, 1),
    ]
    if last:
        n_p = M_P // close_rows
        out_shape = (jax.ShapeDtypeStruct((M_P, D_MODEL), F32),
                     jax.ShapeDtypeStruct((M_S, D_MODEL), F32))
        out_specs = (
            pl.BlockSpec((close_rows, D_MODEL),
                         lambda i, f: (jnp.minimum(close_block(i, f), n_p - 1), 0)),
            pl.BlockSpec((close_rows, D_MODEL),
                         lambda i, f: (jnp.maximum(close_block(i, f) - n_p, 0), 0)),
        )
    else:
        ins += [mod, mod]
        in_specs += [mod_prev(l + 1, 1), mod_prev(l + 1, 0)]
        out_shape = (jax.ShapeDtypeStruct((M_ALL, D_MODEL), F32),
                     jax.ShapeDtypeStruct((M_ALL, D_MODEL), BF16))
        out_specs = (row, row)
    return pl.pallas_call(
        functools.partial(_mlp_kernel, n_tiles=n_tiles, n_ff=n_ff, last=last),
        out_shape=out_shape,
        grid=(n_tiles + 1, n_ff),
        in_specs=in_specs,
        out_specs=out_specs,
        scratch_shapes=[pltpu.VMEM((tm, D_MODEL), F32), pltpu.VMEM((tm, D_MODEL), F32)],
        compiler_params=_params("arbitrary", "arbitrary"),
        name="mlp",
    )(*ins)


def _rope_tables():
    n_freq = A_HEAD_DIM // 4
    pos = np.arange(DEC_SEQ)
    row = (pos // GRID_W).astype(np.float32)
    col = (pos % GRID_W).astype(np.float32)
    inv_freq = (ROPE_BASE ** (-np.arange(n_freq, dtype=np.float32) / n_freq)).astype(np.float32)
    ar = row[:, None] * inv_freq
    ac = col[:, None] * inv_freq
    cr, sr, cc, sc = np.cos(ar), np.sin(ar), np.cos(ac), np.sin(ac)
    z = np.zeros_like(sr)
    cos = np.concatenate([cr, cr, cc, cc], axis=1)
    sin_a = np.concatenate([-sr, z, -sc, z], axis=1)
    sin_b = np.concatenate([z, sr, z, sc], axis=1)
    return tuple(jnp.asarray(t.astype(np.float32)) for t in (cos, sin_a, sin_b))


def kernel(x_prompt, x_sample, cache_a_k, cache_a_v, cache_b_k, cache_b_v, c, c_ctx, w_mod, b_mod, ln_g, ln_b, w_up, w_down, a_w_qkv, a_w_o, a_lambda, a_subln_g, b_w_qkv, b_w_o, b_q_norm_g, b_k_norm_g, c_w_f):
    n_a = a_w_qkv.shape[0]
    n_b = b_w_qkv.shape[0]
    cond = jnp.zeros((COND_PAD, D_MODEL), F32).at[0].set(c_ctx).at[1:N_COND].set(c)
    mod = _modulation(cond, w_mod, b_mod).reshape(DEPTH, COND_PAD, N_MOD, 1, D_MODEL)
    ln_g4 = ln_g.reshape(DEPTH, 2, 1, D_MODEL)
    ln_b4 = ln_b.reshape(DEPTH, 2, 1, D_MODEL)
    rope = _rope_tables()
    w_o_bf = (_cast_bf16(a_w_o), _cast_bf16(b_w_o), _cast_bf16(c_w_f))

    ck_a = cache_a_k.reshape(DEC_BATCH, n_a, PAST_LEN * A_CHUNKS, LANE)
    cv_a = _a_v_rows(cache_a_v, DEC_BATCH, PAST_LEN)
    ck_b = cache_b_k.reshape(DEC_BATCH, n_b, PAST_LEN * B_CHUNKS, LANE)
    cv_b = cache_b_v.reshape(DEC_BATCH, n_b, PAST_LEN * B_CHUNKS, LANE)
    subln = a_subln_g.reshape(n_a, 1, 2 * A_HEAD_DIM)
    b_gains = jnp.stack([b_q_norm_g, b_k_norm_g], axis=1).reshape(n_b, 2, 1, B_HEAD_DIM)

    x = (x_prompt.reshape(M_P, D_MODEL), x_sample.reshape(M_S, D_MODEL))
    u = _embed(*x, mod)

    a_kv, b_kv = [], []
    for l in range(DEPTH):
        kind, j = l % N_MIXERS, l // N_MIXERS
        if kind == 0:
            lam_init = 0.8 - 0.6 * math.exp(-0.3 * l)
            q_p = _proj(u, a_w_qkv, j, 0, D_MODEL, 0, M_P, out_dtype=BF16, name="a_q_prompt")
            kv_p = _proj(u, a_w_qkv, j, D_MODEL, 2 * D_MODEL, 0, M_P, out_dtype=F32,
                         name="a_kv_prompt")
            qk_s = _proj(u, a_w_qkv, j, 0, 2 * D_MODEL, M_P, M_S, out_dtype=BF16, rope=rope,
                         name="a_qk_latent")
            v_s = _proj(u, a_w_qkv, j, 2 * D_MODEL, D_MODEL, M_P, M_S, out_dtype=BF16,
                        name="a_v_latent")
            if j + 1 < n_a:
                att_p = _diff_prompt(a_lambda, subln, q_p, kv_p, j, lam_init)
                a_kv.append(kv_p)
            else:
                att_p, new_a_k, new_a_v = _diff_prompt(a_lambda, subln, q_p, kv_p, j, lam_init,
                                                       prev_kv=a_kv)
            att_s = _diff_latent(a_lambda, subln, qk_s, v_s, ck_a, cv_a, j, lam_init)
            w_o = w_o_bf[0]
        elif kind == 1:
            gj = b_gains[j]
            q_p = _proj(u, b_w_qkv, j, 0, D_MODEL, 0, M_P, out_dtype=BF16, gains=gj,
                        gain_split=D_MODEL, name="b_q_prompt")
            k_p = _proj(u, b_w_qkv, j, D_MODEL, B_KV, 0, M_P, out_dtype=F32, tn=B_KV,
                        gains=gj[1:], gain_split=B_KV, name="b_k_prompt")
            v_p = _proj(u, b_w_qkv, j, D_MODEL + B_KV, B_KV, 0, M_P, out_dtype=F32, tn=B_KV,
                        name="b_v_prompt")
            qk_s = _proj(u, b_w_qkv, j, 0, D_MODEL + B_KV, M_P, M_S, out_dtype=BF16, tn=B_KV,
                         gains=gj, gain_split=D_MODEL, rope=rope, name="b_qk_latent")
            v_s = _proj(u, b_w_qkv, j, D_MODEL + B_KV, B_KV, M_P, M_S, out_dtype=BF16, tn=B_KV,
                        name="b_v_latent")
            if j + 1 < n_b:
                att_p = _gqa_prompt(q_p, k_p, v_p)
                b_kv.append((k_p, v_p))
            else:
                att_p, new_b_k, new_b_v = _gqa_prompt(q_p, k_p, v_p, prev_kv=b_kv)
            att_s = _gqa_latent(qk_s, v_s, ck_b, cv_b, j)
            w_o = w_o_bf[1]
        else:
            att_p = _dft(u, SEQ, BATCH, 0, C_GROUPS)
            att_s = _dft(u, DEC_SEQ, DEC_BATCH, M_P, 1)
            w_o = w_o_bf[2]
        x, u = _oproj(att_p, att_s, w_o, j, x, mod, ln_g4, ln_b4, l)
        x, u = _mlp(u, w_up, w_down, x, mod, ln_g4, ln_b4, l)

    y_p, y_s = x, u
    return (y_p.reshape(BATCH, SEQ, D_MODEL), y_s.reshape(DEC_BATCH, DEC_SEQ, D_MODEL),
            new_a_k.reshape(BATCH, n_a, SEQ, A_HEADS, 2, A_HEAD_DIM),
            _a_v_from_rows(new_a_v, BATCH, SEQ),
            new_b_k.reshape(BATCH, n_b, SEQ, B_KV_HEADS, B_HEAD_DIM),
            new_b_v.reshape(BATCH, n_b, SEQ, B_KV_HEADS, B_HEAD_DIM))
```

```python
import functools
import math

import jax
import jax.numpy as jnp
import numpy as np
from jax import lax
from jax.experimental import pallas as pl
from jax.experimental.pallas import tpu as pltpu

D_MODEL = 2048
BATCH = 16
SEQ = 256
DEPTH = 4
DEC_BATCH = 2
DEC_SEQ = 1024
PAST_LEN = 512
GRID_W = 64
N_MIXERS = 3
A_HEAD_DIM = 128
A_HEADS = D_MODEL // (2 * A_HEAD_DIM)
B_HEAD_DIM = 128
B_HEADS = D_MODEL // B_HEAD_DIM
B_KV_HEADS = B_HEADS // 4
B_GROUP = B_HEADS // B_KV_HEADS
B_KV = B_KV_HEADS * B_HEAD_DIM
C_GROUPS = 4
C_GROUP_DIM = D_MODEL // C_GROUPS
D_FF = 4 * D_MODEL
ROPE_BASE = 10000.0
ALPHA = (2 * DEPTH) ** 0.25
LN_EPS = 1e-5
RMS_EPS = 1e-6
N_MOD = 6
LOG2E = 1.4426950408889634

M_P = BATCH * SEQ
M_S = DEC_BATCH * DEC_SEQ
M_ALL = M_P + M_S
N_COND = 1 + DEC_BATCH
COND_PAD = 8

LANE = 128
VMEM_LIMIT = 58 * 1024 * 1024
ROW_TILE = 512

BF16 = jnp.bfloat16
F32 = jnp.float32


def _params(*sem, vmem=VMEM_LIMIT):
    return pltpu.CompilerParams(dimension_semantics=sem, vmem_limit_bytes=vmem)


def _cond_row(i, tm):
    start = i * tm
    return jnp.where(start < M_P, 0, 1 + (start - M_P) // DEC_SEQ)


def _mod_spec(l, which, tm):
    def idx(*g):
        return (l, _cond_row(g[0], tm), which, 0, 0)
    return pl.BlockSpec((None, None, None, 1, D_MODEL), idx)


def _vec_spec(l, which):
    return pl.BlockSpec((None, None, 1, D_MODEL), lambda *g: (l, which, 0, 0))


def _prompt_rows_spec(tm):
    last = M_P // tm - 1
    return pl.BlockSpec((tm, D_MODEL), lambda i: (jnp.minimum(i, last), 0))


def _latent_rows_spec(tm):
    first = M_P // tm
    return pl.BlockSpec((tm, D_MODEL), lambda i: (jnp.maximum(i - first, 0), 0))


def _mod_kernel(c_ref, w_ref, b_ref, o_ref):
    c = c_ref[...]
    s = c * (1.0 / (1.0 + jnp.exp(-c)))
    o_ref[...] = jnp.dot(s.astype(BF16), w_ref[...].astype(BF16),
                         preferred_element_type=F32) + b_ref[...]


def _modulation(cond, w_mod, b_mod):
    tn = 1024
    n_out = N_MOD * D_MODEL
    return pl.pallas_call(
        _mod_kernel,
        out_shape=jax.ShapeDtypeStruct((DEPTH, COND_PAD, n_out), F32),
        grid=(DEPTH, n_out // tn),
        in_specs=[
            pl.BlockSpec((COND_PAD, D_MODEL), lambda l, n: (0, 0)),
            pl.BlockSpec((None, D_MODEL, tn), lambda l, n: (l, 0, n)),
            pl.BlockSpec((None, 1, tn), lambda l, n: (l, 0, n)),
        ],
        out_specs=pl.BlockSpec((None, COND_PAD, tn), lambda l, n: (l, 0, n)),
        compiler_params=_params("arbitrary", "arbitrary"),
        name="modulation",
    )(cond, w_mod, b_mod.reshape(DEPTH, 1, n_out))


def _cast_kernel(w_ref, o_ref):
    o_ref[...] = w_ref[...].astype(BF16)


def _cast_bf16(w, rows=512):
    n_l, k, n = w.shape
    spec = pl.BlockSpec((None, rows, n), lambda l, r: (l, r, 0))
    return pl.pallas_call(
        _cast_kernel,
        out_shape=jax.ShapeDtypeStruct(w.shape, BF16),
        grid=(n_l, k // rows),
        in_specs=[spec],
        out_specs=spec,
        compiler_params=_params("arbitrary", "arbitrary"),
        name="cast_weight",
    )(w)


def _embed_kernel(xp_ref, xs_ref, sc_ref, sh_ref, u_ref):
    x = jnp.where(pl.program_id(0) * ROW_TILE < M_P, xp_ref[...], xs_ref[...])
    u_ref[...] = (x * (1.0 + sc_ref[...]) + sh_ref[...]).astype(BF16)


def _embed(xp, xs, mod):
    tm = ROW_TILE
    return pl.pallas_call(
        _embed_kernel,
        out_shape=jax.ShapeDtypeStruct((M_ALL, D_MODEL), BF16),
        grid=(M_ALL // tm,),
        in_specs=[_prompt_rows_spec(tm), _latent_rows_spec(tm),
                  _mod_spec(0, 1, tm), _mod_spec(0, 0, tm)],
        out_specs=pl.BlockSpec((tm, D_MODEL), lambda i: (i, 0)),
        compiler_params=_params("arbitrary"),
        name="embed",
    )(xp, xs, mod, mod)


def _rms_chunk(x, g):
    ms = jnp.mean(x * x, axis=-1, keepdims=True)
    return x * lax.rsqrt(ms + RMS_EPS) * g


def _rope_chunk(x, cos, sin_a, sin_b):
    return x * cos + pltpu.roll(x, LANE - 32, 1) * sin_a + pltpu.roll(x, 32, 1) * sin_b


def _proj_kernel(*refs, tn, rms, rope):
    it = iter(refs)
    u_ref, w_ref = next(it), next(it)
    g_ref = next(it) if rms else None
    tabs = (next(it), next(it), next(it)) if rope else None
    o_ref, wbf = next(it), next(it)

    @pl.when(pl.program_id(1) == 0)
    def _():
        wbf[...] = w_ref[...].astype(BF16)

    if rms and rope:
        acc = jnp.dot(u_ref[...], wbf[...], preferred_element_type=F32)
        chunks = [acc[:, c * LANE:(c + 1) * LANE] for c in range(tn // LANE)]
        g = g_ref[...]
        tab = tuple(t[...] for t in tabs)
        rinv = [lax.rsqrt(jnp.mean(x * x, axis=-1, keepdims=True) + RMS_EPS) for x in chunks]

        @pl.when(pl.program_id(1) >= 0)
        def _():
            for c, (x, r) in enumerate(zip(chunks, rinv)):
                o_ref[:, c * LANE:(c + 1) * LANE] = (_rope_chunk(x * g, *tab) * r).astype(o_ref.dtype)
    elif rms or rope:
        rows = u_ref.shape[0] // 2
        g = g_ref[...] if rms else None
        for r in range(2):
            sl = slice(r * rows, (r + 1) * rows)
            acc = jnp.dot(u_ref[sl, :], wbf[...], preferred_element_type=F32)
            for c in range(tn // LANE):
                x = acc[:, c * LANE:(c + 1) * LANE]
                if rms:
                    x = _rms_chunk(x, g)
                else:
                    x = _rope_chunk(x, *(t[sl, :] for t in tabs))
                o_ref[sl, c * LANE:(c + 1) * LANE] = x.astype(o_ref.dtype)
    else:
        acc = jnp.dot(u_ref[...], wbf[...], preferred_element_type=F32)
        o_ref[...] = acc.astype(o_ref.dtype)


def _proj(u, w, w_layer, col0, ncols, m0, m_rows, *, out_dtype, tn=1024, tm=1024,
          gains=None, gain_split=None, rope=None, name="proj"):
    k = u.shape[1]
    assert col0 % tn == 0 and ncols % tn == 0 and m0 % tm == 0 and m_rows % tm == 0
    n_t, m_t, mt0, nt0 = ncols // tn, m_rows // tm, m0 // tm, col0 // tn
    ins = [u, w]
    in_specs = [
        pl.BlockSpec((tm, k), lambda n, m: (mt0 + m, 0)),
        pl.BlockSpec((None, k, tn), lambda n, m: (w_layer, 0, nt0 + n)),
    ]
    if gains is not None:
        ins.append(gains)
        in_specs.append(pl.BlockSpec(
            (None, 1, LANE), lambda n, m: (jnp.where(n * tn < gain_split, 0, 1), 0, 0)))
    if rope is not None:
        assert m0 >= M_P
        per = DEC_SEQ // tm
        for t in rope:
            ins.append(t)
            in_specs.append(pl.BlockSpec((tm, LANE), lambda n, m: ((mt0 + m) % per, 0)))
    kern = functools.partial(_proj_kernel, tn=tn, rms=gains is not None, rope=rope is not None)
    return pl.pallas_call(
        kern,
        out_shape=jax.ShapeDtypeStruct((m_rows, ncols), out_dtype),
        grid=(n_t, m_t),
        in_specs=in_specs,
        out_specs=pl.BlockSpec((tm, tn), lambda n, m: (m, n)),
        scratch_shapes=[pltpu.VMEM((k, tn), BF16)],
        compiler_params=_params("arbitrary", "arbitrary"),
        name=name,
    )(*ins)


def _qkt(q, k):
    return lax.dot_general(q, k, (((1,), (1,)), ((), ())), preferred_element_type=F32)


def _softmax_terms(s, scale):
    m = s.max(axis=-1, keepdims=True)
    e = jnp.exp2((s - m) * (scale * LOG2E))
    return e, 1.0 / e.sum(axis=-1, keepdims=True)


def _diff_lambda(lp_ref, lam_init):
    lp = lp_ref[...]
    s1 = jnp.sum(lp[0:1] * lp[1:2], axis=-1, keepdims=True)
    s2 = jnp.sum(lp[2:3] * lp[3:4], axis=-1, keepdims=True)
    return jnp.exp(s1) - jnp.exp(s2) + lam_init


def _diff_weights(s, lam):
    half = s.shape[0] // 2
    e, r = _softmax_terms(s, A_HEAD_DIM ** -0.5)
    return (e[:half] * r[:half] - e[half:] * (lam * r[half:])).astype(BF16)


def _sub_norm(o, gain):
    ms = jnp.mean(o * o, axis=-1, keepdims=True)
    return (o * lax.rsqrt(ms + RMS_EPS) * gain).astype(BF16)


A_CHUNKS = D_MODEL // LANE
B_CHUNKS = B_KV // LANE


def _a_v_row(c):
    return (c % 2) * A_HEADS + c // 2


def _a_v_rows(v, batch, seq):
    n_l = v.shape[1]
    v = v.reshape(batch, n_l, seq, A_HEADS, 2, LANE).transpose(0, 1, 2, 4, 3, 5)
    return v.reshape(batch, n_l, seq * A_CHUNKS, LANE)


def _a_v_from_rows(r, batch, seq):
    n_l = r.shape[1]
    v = r.reshape(batch, n_l, seq, 2, A_HEADS, LANE).transpose(0, 1, 2, 4, 3, 5)
    return v.reshape(batch, n_l, seq, A_HEADS, 2 * LANE)


def _diff_prompt_kernel(*refs, lam_init, n_prev, emit):
    lp_ref, g_ref, q_ref, k_ref, v_ref = refs[:5]
    prev = refs[5:5 + n_prev]
    o_ref = refs[5 + n_prev]
    s_scr, a_scr, o_scr = refs[-3:]
    lam = _diff_lambda(lp_ref, lam_init)
    gain = g_ref[...] * (1.0 - lam_init)
    w = 2 * A_HEAD_DIM
    for h in range(A_HEADS):
        for p in range(2):
            cols = slice(h * w + p * A_HEAD_DIM, h * w + (p + 1) * A_HEAD_DIM)
            s_scr[(p * A_HEADS + h) * SEQ:(p * A_HEADS + h + 1) * SEQ, :] = _qkt(
                q_ref[:, cols], k_ref[:, cols].astype(BF16))

    @pl.when(pl.program_id(0) >= 0)
    def _():
        a_scr[...] = _diff_weights(s_scr[...], lam)
        for h in range(A_HEADS):
            rows = slice(h * SEQ, (h + 1) * SEQ)
            o_scr[rows, :] = jnp.dot(a_scr[rows, :], v_ref[:, h * w:(h + 1) * w].astype(BF16),
                                     preferred_element_type=F32)
        y = _sub_norm(o_scr[...], gain)
        for h in range(A_HEADS):
            o_ref[:, h * w:(h + 1) * w] = y[h * SEQ:(h + 1) * SEQ, :]

    if emit:
        nk_ref, nv_ref = refs[6 + n_prev], refs[7 + n_prev]
        layers = [(p, 0, p, D_MODEL) for p in prev] + [(k_ref, 0, v_ref, 0)]
        for jj, (kr, k0, vr, v0) in enumerate(layers):
            for c in range(A_CHUNKS):
                nk_ref[jj, pl.ds(c, SEQ, stride=A_CHUNKS), :] = kr[:, k0 + c * LANE:k0 + (c + 1) * LANE]
                nv_ref[jj, pl.ds(_a_v_row(c), SEQ, stride=A_CHUNKS), :] = (
                    vr[:, v0 + c * LANE:v0 + (c + 1) * LANE])


def _diff_prompt(lp, g, q, kv, j, lam_init, prev_kv=None):
    emit = prev_kv is not None
    prev_kv = list(prev_kv or [])
    n_layers = len(prev_kv) + 1
    att_shape = jax.ShapeDtypeStruct((M_P, D_MODEL), BF16)
    att_spec = pl.BlockSpec((SEQ, D_MODEL), lambda b: (b, 0))
    if emit:
        new_shape = jax.ShapeDtypeStruct((BATCH, n_layers, SEQ * A_CHUNKS, LANE), F32)
        new_spec = pl.BlockSpec((None, n_layers, SEQ * A_CHUNKS, LANE), lambda b: (b, 0, 0, 0))
        out_shape, out_specs = (att_shape, new_shape, new_shape), (att_spec, new_spec, new_spec)
    else:
        out_shape, out_specs = att_shape, att_spec
    return pl.pallas_call(
        functools.partial(_diff_prompt_kernel, lam_init=lam_init, n_prev=len(prev_kv), emit=emit),
        out_shape=out_shape,
        grid=(BATCH,),
        in_specs=[
            pl.BlockSpec((None, 4, A_HEAD_DIM), lambda b: (j, 0, 0)),
            pl.BlockSpec((None, 1, 2 * A_HEAD_DIM), lambda b: (j, 0, 0)),
            pl.BlockSpec((SEQ, D_MODEL), lambda b: (b, 0)),
            pl.BlockSpec((SEQ, D_MODEL), lambda b: (b, 0)),
            pl.BlockSpec((SEQ, D_MODEL), lambda b: (b, 1)),
        ] + [pl.BlockSpec((SEQ, 2 * D_MODEL), lambda b: (b, 0)) for _ in prev_kv],
        out_specs=out_specs,
        scratch_shapes=[pltpu.VMEM((2 * A_HEADS * SEQ, SEQ), F32),
                        pltpu.VMEM((A_HEADS * SEQ, SEQ), BF16),
                        pltpu.VMEM((A_HEADS * SEQ, 2 * A_HEAD_DIM), F32)],
        compiler_params=_params("arbitrary"),
        name="diff_attn_prompt",
    )(lp, g, q, kv, kv, *prev_kv)


def _diff_latent_kernel(lp_ref, g_ref, q_ref, k_ref, v_ref, ck_ref, cv_ref, o_ref,
                        s_scr, a_scr, *, lam_init):
    lam = _diff_lambda(lp_ref, lam_init)
    gain = g_ref[...] * (1.0 - lam_init)
    h = pl.program_id(1)
    tq, d = q_ref.shape[0], A_HEAD_DIM

    def cached(ref, r0, r1):
        halves = [ref[pl.ds(r, PAST_LEN, stride=A_CHUNKS), :] for r in (r0, r1)]
        return jnp.concatenate(halves, axis=1).astype(BF16)

    ck = cached(ck_ref, 2 * h, 2 * h + 1)
    for p in range(2):
        q = q_ref[:, p * d:(p + 1) * d]
        s_scr[p * tq:(p + 1) * tq, :PAST_LEN] = _qkt(q, ck[:, p * d:(p + 1) * d])
        s_scr[p * tq:(p + 1) * tq, PAST_LEN:] = _qkt(q, k_ref[:, p * d:(p + 1) * d])

    @pl.when(h >= 0)
    def _():
        a_scr[...] = _diff_weights(s_scr[...], lam)
        cv = cached(cv_ref, h, A_HEADS + h)
        o = (jnp.dot(a_scr[:, :PAST_LEN], cv, preferred_element_type=F32)
             + jnp.dot(a_scr[:, PAST_LEN:], v_ref[...], preferred_element_type=F32))
        o_ref[...] = _sub_norm(o, gain)


def _diff_latent(lp, g, qk, v, ck, cv, j, lam_init, tq=1024):
    w = 2 * A_HEAD_DIM
    per = DEC_SEQ // tq
    ctx_spec = pl.BlockSpec((None, None, PAST_LEN * A_CHUNKS, LANE), lambda b, h, t: (b, j, 0, 0))
    return pl.pallas_call(
        functools.partial(_diff_latent_kernel, lam_init=lam_init),
        out_shape=jax.ShapeDtypeStruct((M_S, D_MODEL), BF16),
        grid=(DEC_BATCH, A_HEADS, per),
        in_specs=[
            pl.BlockSpec((None, 4, A_HEAD_DIM), lambda b, h, t: (j, 0, 0)),
            pl.BlockSpec((None, 1, w), lambda b, h, t: (j, 0, 0)),
            pl.BlockSpec((tq, w), lambda b, h, t: (b * per + t, h)),
            pl.BlockSpec((DEC_SEQ, w), lambda b, h, t: (b, A_HEADS + h)),
            pl.BlockSpec((DEC_SEQ, w), lambda b, h, t: (b, h)),
            ctx_spec, ctx_spec,
        ],
        out_specs=pl.BlockSpec((tq, w), lambda b, h, t: (b * per + t, h)),
        scratch_shapes=[pltpu.VMEM((2 * tq, PAST_LEN + DEC_SEQ), F32),
                        pltpu.VMEM((tq, PAST_LEN + DEC_SEQ), BF16)],
        compiler_params=_params("arbitrary", "arbitrary", "arbitrary"),
        name="diff_attn_latent",
    )(lp, g, qk, qk, v, ck, cv)


def _gqa_prompt_kernel(*refs, n_prev, emit):
    q_ref, k_ref, v_ref = refs[:3]
    prev = refs[3:3 + 2 * n_prev]
    o_ref = refs[3 + 2 * n_prev]
    s_scr, a_scr, o_scr = refs[-3:]
    d = B_HEAD_DIM
    for n in range(B_KV_HEADS):
        k = k_ref[:, n * d:(n + 1) * d].astype(BF16)
        for g in range(B_GROUP):
            hd = n * B_GROUP + g
            s_scr[hd * SEQ:(hd + 1) * SEQ, :] = _qkt(q_ref[:, hd * d:(hd + 1) * d], k)

    @pl.when(pl.program_id(0) >= 0)
    def _():
        e, r = _softmax_terms(s_scr[...], B_HEAD_DIM ** -0.5)
        a_scr[...] = (e * r).astype(BF16)
        group_rows = B_GROUP * SEQ
        for n in range(B_KV_HEADS):
            rows = slice(n * group_rows, (n + 1) * group_rows)
            o_scr[rows, :] = jnp.dot(a_scr[rows, :], v_ref[:, n * d:(n + 1) * d].astype(BF16),
                                     preferred_element_type=F32)
        for hd in range(B_HEADS):
            o_ref[:, hd * d:(hd + 1) * d] = o_scr[hd * SEQ:(hd + 1) * SEQ, :].astype(BF16)

    if emit:
        nk_ref, nv_ref = refs[4 + 2 * n_prev], refs[5 + 2 * n_prev]
        layers = [(prev[2 * i], prev[2 * i + 1]) for i in range(n_prev)] + [(k_ref, v_ref)]
        for jj, (kr, vr) in enumerate(layers):
            for n in range(B_CHUNKS):
                rows = pl.ds(n, SEQ, stride=B_CHUNKS)
                nk_ref[jj, rows, :] = kr[:, n * LANE:(n + 1) * LANE]
                nv_ref[jj, rows, :] = vr[:, n * LANE:(n + 1) * LANE]


def _gqa_prompt(q, k, v, prev_kv=None):
    emit = prev_kv is not None
    prev = [a for pair in (prev_kv or []) for a in pair]
    n_layers = len(prev) // 2 + 1
    kv_spec = pl.BlockSpec((SEQ, B_KV), lambda b: (b, 0))
    att_shape = jax.ShapeDtypeStruct((M_P, D_MODEL), BF16)
    att_spec = pl.BlockSpec((SEQ, D_MODEL), lambda b: (b, 0))
    if emit:
        new_shape = jax.ShapeDtypeStruct((BATCH, n_layers, SEQ * B_CHUNKS, LANE), F32)
        new_spec = pl.BlockSpec((None, n_layers, SEQ * B_CHUNKS, LANE), lambda b: (b, 0, 0, 0))
        out_shape, out_specs = (att_shape, new_shape, new_shape), (att_spec, new_spec, new_spec)
    else:
        out_shape, out_specs = att_shape, att_spec
    return pl.pallas_call(
        functools.partial(_gqa_prompt_kernel, n_prev=len(prev) // 2, emit=emit),
        out_shape=out_shape,
        grid=(BATCH,),
        in_specs=[att_spec, kv_spec, kv_spec] + [kv_spec for _ in prev],
        out_specs=out_specs,
        scratch_shapes=[pltpu.VMEM((B_HEADS * SEQ, SEQ), F32),
                        pltpu.VMEM((B_HEADS * SEQ, SEQ), BF16),
                        pltpu.VMEM((B_HEADS * SEQ, B_HEAD_DIM), F32)],
        compiler_params=_params("arbitrary"),
        name="gqa_attn_prompt",
    )(q, k, v, *prev)


def _gqa_latent_kernel(q_ref, k_ref, v_ref, ck_ref, cv_ref, o_ref, s_scr, a_scr):
    d, tq = B_HEAD_DIM, q_ref.shape[0]
    n = pl.program_id(1)
    cached = pl.ds(n, PAST_LEN, stride=B_CHUNKS)
    ck = ck_ref[cached, :].astype(BF16)
    for g in range(B_GROUP):
        q = q_ref[:, g * d:(g + 1) * d]
        s_scr[g * tq:(g + 1) * tq, :PAST_LEN] = _qkt(q, ck)
        s_scr[g * tq:(g + 1) * tq, PAST_LEN:] = _qkt(q, k_ref[...])

    @pl.when(n >= 0)
    def _():
        e, r = _softmax_terms(s_scr[...], B_HEAD_DIM ** -0.5)
        a_scr[...] = (e * r).astype(BF16)
        o = (jnp.dot(a_scr[:, :PAST_LEN], cv_ref[cached, :].astype(BF16),
                     preferred_element_type=F32)
             + jnp.dot(a_scr[:, PAST_LEN:], v_ref[...], preferred_element_type=F32))
        for g in range(B_GROUP):
            o_ref[:, g * d:(g + 1) * d] = o[g * tq:(g + 1) * tq, :].astype(BF16)


def _gqa_latent(qk, v, ck, cv, j, tq=1024):
    d = B_HEAD_DIM
    gw = B_GROUP * d
    per = DEC_SEQ // tq
    ctx_spec = pl.BlockSpec((None, None, PAST_LEN * B_CHUNKS, LANE), lambda b, n, t: (b, j, 0, 0))
    return pl.pallas_call(
        _gqa_latent_kernel,
        out_shape=jax.ShapeDtypeStruct((M_S, D_MODEL), BF16),
        grid=(DEC_BATCH, B_KV_HEADS, per),
        in_specs=[
            pl.BlockSpec((tq, gw), lambda b, n, t: (b * per + t, n)),
            pl.BlockSpec((DEC_SEQ, d), lambda b, n, t: (b, D_MODEL // d + n)),
            pl.BlockSpec((DEC_SEQ, d), lambda b, n, t: (b, n)),
            ctx_spec, ctx_spec,
        ],
        out_specs=pl.BlockSpec((tq, gw), lambda b, n, t: (b * per + t, n)),
        scratch_shapes=[pltpu.VMEM((B_GROUP * tq, PAST_LEN + DEC_SEQ), F32),
                        pltpu.VMEM((B_GROUP * tq, PAST_LEN + DEC_SEQ), BF16)],
        compiler_params=_params("arbitrary", "arbitrary", "arbitrary"),
        name="gqa_attn_latent",
    )(qk, qk, v, ck, cv)


def _dft_tables(s):
    def cs(n):
        idx = (np.arange(n)[:, None] * np.arange(n)[None, :]) % n
        ang = 2.0 * np.pi * idx / n
        return np.cos(ang), np.sin(ang)
    cc, sc = cs(C_GROUP_DIM)
    c_s, s_s = cs(s)
    t_chan = np.concatenate([cc, sc], axis=1).astype(np.float32)
    t_pos = np.concatenate([c_s, -s_s], axis=1).astype(np.float32)
    return jnp.asarray(t_chan).astype(BF16), jnp.asarray(t_pos).astype(BF16)


def _dft_group(x, t_chan, t_pos, norm):
    xcs = jnp.dot(x, t_chan, preferred_element_type=F32).astype(BF16)
    stacked = jnp.concatenate([xcs[:, :C_GROUP_DIM], xcs[:, C_GROUP_DIM:]], axis=0)
    y = jnp.dot(t_pos, stacked, preferred_element_type=F32)
    return (y * norm).astype(BF16)


def _dft_kernel(u_ref, tc_ref, tp_ref, o_ref, *, groups, norm):
    tc, tp = tc_ref[...], tp_ref[...]
    for g in range(groups):
        sl = slice(g * C_GROUP_DIM, (g + 1) * C_GROUP_DIM)
        o_ref[:, sl] = _dft_group(u_ref[:, sl], tc, tp, norm)


def _dft(u, s, batch, row0, groups_per_step):
    t_chan, t_pos = _dft_tables(s)
    gw = groups_per_step * C_GROUP_DIM
    rb0 = row0 // s
    return pl.pallas_call(
        functools.partial(_dft_kernel, groups=groups_per_step,
                          norm=1.0 / math.sqrt(s * C_GROUP_DIM)),
        out_shape=jax.ShapeDtypeStruct((batch * s, D_MODEL), BF16),
        grid=(batch, C_GROUPS // groups_per_step),
        in_specs=[
            pl.BlockSpec((s, gw), lambda b, g: (rb0 + b, g)),
            pl.BlockSpec(t_chan.shape, lambda b, g: (0, 0)),
            pl.BlockSpec(t_pos.shape, lambda b, g: (0, 0)),
        ],
        out_specs=pl.BlockSpec((s, gw), lambda b, g: (b, g)),
        compiler_params=_params("arbitrary", "arbitrary"),
        name="dft",
    )(u, t_chan, t_pos)


ROW_GROUPS = 2


def _post_norm(y, g, b):
    mu = jnp.mean(y, axis=-1, keepdims=True)
    yc = y - mu
    var = jnp.mean(yc * yc, axis=-1, keepdims=True)
    return yc * lax.rsqrt(var + LN_EPS) * g + b


def _oproj_kernel(*refs, split_x):
    ap_ref, as_ref, w_ref = refs[:3]
    x_refs = refs[3:5] if split_x else refs[3:4]
    gate_ref, g_ref, b_ref, sc_ref, sh_ref, x1_ref, u_ref = refs[3 + len(x_refs):]
    is_prompt = pl.program_id(0) * ROW_TILE < M_P
    gate, g, b = gate_ref[...], g_ref[...], b_ref[...]
    sc1, sh = 1.0 + sc_ref[...], sh_ref[...]
    rows = ROW_TILE // ROW_GROUPS
    for r in range(ROW_GROUPS):
        sl = slice(r * rows, (r + 1) * rows)
        a = jnp.where(is_prompt, ap_ref[sl, :], as_ref[sl, :])
        mixed = jnp.dot(a, w_ref[...], preferred_element_type=F32)
        if split_x:
            x = jnp.where(is_prompt, x_refs[0][sl, :], x_refs[1][sl, :])
        else:
            x = x_refs[0][sl, :]
        x1 = _post_norm(ALPHA * x + gate * mixed, g, b)
        x1_ref[sl, :] = x1
        u_ref[sl, :] = (x1 * sc1 + sh).astype(BF16)


def _oproj(a_p, a_s, w_bf, w_layer, x, mod, ln_g, ln_b, l):
    tm = ROW_TILE
    row = pl.BlockSpec((tm, D_MODEL), lambda i: (i, 0))
    split_x = isinstance(x, tuple)
    xs = list(x) if split_x else [x]
    x_specs = [_prompt_rows_spec(tm), _latent_rows_spec(tm)] if split_x else [row]
    return pl.pallas_call(
        functools.partial(_oproj_kernel, split_x=split_x),
        out_shape=(jax.ShapeDtypeStruct((M_ALL, D_MODEL), F32),
                   jax.ShapeDtypeStruct((M_ALL, D_MODEL), BF16)),
        grid=(M_ALL // tm,),
        in_specs=[
            _prompt_rows_spec(tm), _latent_rows_spec(tm),
            pl.BlockSpec((None, D_MODEL, D_MODEL), lambda i: (w_layer, 0, 0),
                         pipeline_mode=pl.Buffered(1)),
            *x_specs,
            _mod_spec(l, 2, tm),
            _vec_spec(l, 0), _vec_spec(l, 0),
            _mod_spec(l, 4, tm),
            _mod_spec(l, 3, tm),
        ],
        out_specs=(row, row),
        compiler_params=_params("arbitrary"),
        name="oproj_ln",
    )(a_p, a_s, w_bf, *xs, mod, ln_g, ln_b, mod, mod)


MLP_FF_PIECE = 512


def _mlp_kernel(*refs, n_tiles, n_ff, last):
    u_ref, wu_ref, wd_ref, x_ref, gate_ref, g_ref, b_ref = refs[:7]
    accs = refs[-2:]
    i, f = pl.program_id(0), pl.program_id(1)
    tm = u_ref.shape[0]
    close_rows = tm // n_ff

    for parity in (0, 1):
        @pl.when(jnp.logical_and(f == 0, i % 2 == parity))
        def _():
            accs[parity][...] = jnp.zeros_like(accs[parity])

    def matmuls(acc):
        tf = wu_ref.shape[1]
        for c in range(tf // MLP_FF_PIECE):
            cols = slice(c * MLP_FF_PIECE, (c + 1) * MLP_FF_PIECE)
            h = jnp.dot(u_ref[...], wu_ref[:, cols].astype(BF16), preferred_element_type=F32)
            h = jnp.maximum(h, 0.0)
            h = (h * h).astype(BF16)
            acc[...] += jnp.dot(h, wd_ref[cols, :].astype(BF16), preferred_element_type=F32)

    def close(prev):
        rows = pl.ds(pl.multiple_of(f * close_rows, close_rows), close_rows)
        y = _post_norm(ALPHA * x_ref[...] + gate_ref[...] * prev[rows, :], g_ref[...], b_ref[...])
        if last:
            yp_ref, ys_ref = refs[7:9]
            prev_is_prompt = (i - 1) * tm < M_P

            @pl.when(prev_is_prompt)
            def _():
                yp_ref[...] = y

            @pl.when(jnp.logical_not(prev_is_prompt))
            def _():
                ys_ref[...] = y
        else:
            sc_ref, sh_ref, x2_ref, un_ref = refs[7:11]
            x2_ref[...] = y
            un_ref[...] = (y * (1.0 + sc_ref[...]) + sh_ref[...]).astype(BF16)

    @pl.when(i == 0)
    def _():
        matmuls(accs[0])

    middle = jnp.logical_and(i > 0, i < n_tiles)
    for parity in (0, 1):
        @pl.when(jnp.logical_and(middle, i % 2 == parity))
        def _():
            matmuls(accs[parity])
            close(accs[1 - parity])

    @pl.when(i == n_tiles)
    def _():
        close(accs[(n_tiles - 1) % 2])


def _mlp(u, w_up, w_down, x, mod, ln_g, ln_b, l, tm=1024, tf=512):
    last = l + 1 == DEPTH
    n_tiles, n_ff = M_ALL // tm, D_FF // tf
    close_rows = tm // n_ff

    def close_block(i, f):
        return jnp.where(i == 0, 0, (i - 1) * n_ff + f)

    def mod_prev(layer, which):
        def idx(i, f):
            return (layer, _cond_row(jnp.maximum(i - 1, 0), tm), which, 0, 0)
        return pl.BlockSpec((None, None, None, 1, D_MODEL), idx)

    def ff_block(i, f):
        return jnp.where(i < n_tiles, f, n_ff - 1)

    row = pl.BlockSpec((close_rows, D_MODEL), lambda i, f: (close_block(i, f), 0))
    ins = [u, w_up, w_down, x, mod, ln_g, ln_b]
    in_specs = [
        pl.BlockSpec((tm, D_MODEL), lambda i, f: (jnp.minimum(i, n_tiles - 1), 0)),
        pl.BlockSpec((None, D_MODEL, tf), lambda i, f: (l, 0, ff_block(i, f))),
        pl.BlockSpec((None, tf, D_MODEL), lambda i, f: (l, ff_block(i, f), 0)),
        row, mod_prev(l, 5), _vec_spec(l, 1), _vec_spec(l, 1),
    ]
    if last:
        n_p = M_P // close_rows
        out_shape = (jax.ShapeDtypeStruct((M_P, D_MODEL), F32),
                     jax.ShapeDtypeStruct((M_S, D_MODEL), F32))
        out_specs = (
            pl.BlockSpec((close_rows, D_MODEL),
                         lambda i, f: (jnp.minimum(close_block(i, f), n_p - 1), 0)),
            pl.BlockSpec((close_rows, D_MODEL),
                         lambda i, f: (jnp.maximum(close_block(i, f) - n_p, 0), 0)),
        )
    else:
        ins += [mod, mod]
        in_specs += [mod_prev(l + 1, 1), mod_prev(l + 1, 0)]
        out_shape = (jax.ShapeDtypeStruct((M_ALL, D_MODEL), F32),
                     jax.ShapeDtypeStruct((M_ALL, D_MODEL), BF16))
        out_specs = (row, row)
    return pl.pallas_call(
        functools.partial(_mlp_kernel, n_tiles=n_tiles, n_ff=n_ff, last=last),
        out_shape=out_shape,
        grid=(n_tiles + 1, n_ff),
        in_specs=in_specs,
        out_specs=out_specs,
        scratch_shapes=[pltpu.VMEM((tm, D_MODEL), F32), pltpu.VMEM((tm, D_MODEL), F32)],
        compiler_params=_params("arbitrary", "arbitrary"),
        name="mlp",
    )(*ins)


def _rope_tables():
    n_freq = A_HEAD_DIM // 4
    pos = np.arange(DEC_SEQ)
    row = (pos // GRID_W).astype(np.float32)
    col = (pos % GRID_W).astype(np.float32)
    inv_freq = (ROPE_BASE ** (-np.arange(n_freq, dtype=np.float32) / n_freq)).astype(np.float32)
    ar = row[:, None] * inv_freq
    ac = col[:, None] * inv_freq
    cr, sr, cc, sc = np.cos(ar), np.sin(ar), np.cos(ac), np.sin(ac)
    z = np.zeros_like(sr)
    cos = np.concatenate([cr, cr, cc, cc], axis=1)
    sin_a = np.concatenate([-sr, z, -sc, z], axis=1)
    sin_b = np.concatenate([z, sr, z, sc], axis=1)
    return tuple(jnp.asarray(t.astype(np.float32)) for t in (cos, sin_a, sin_b))


def kernel(x_prompt, x_sample, cache_a_k, cache_a_v, cache_b_k, cache_b_v, c, c_ctx, w_mod, b_mod, ln_g, ln_b, w_up, w_down, a_w_qkv, a_w_o, a_lambda, a_subln_g, b_w_qkv, b_w_o, b_q_norm_g, b_k_norm_g, c_w_f):
    n_a = a_w_qkv.shape[0]
    n_b = b_w_qkv.shape[0]
    cond = jnp.zeros((COND_PAD, D_MODEL), F32).at[0].set(c_ctx).at[1:N_COND].set(c)
    mod = _modulation(cond, w_mod, b_mod).reshape(DEPTH, COND_PAD, N_MOD, 1, D_MODEL)
    ln_g4 = ln_g.reshape(DEPTH, 2, 1, D_MODEL)
    ln_b4 = ln_b.reshape(DEPTH, 2, 1, D_MODEL)
    rope = _rope_tables()
    w_o_bf = (_cast_bf16(a_w_o), _cast_bf16(b_w_o), _cast_bf16(c_w_f))

    ck_a = cache_a_k.reshape(DEC_BATCH, n_a, PAST_LEN * A_CHUNKS, LANE)
    cv_a = _a_v_rows(cache_a_v, DEC_BATCH, PAST_LEN)
    ck_b = cache_b_k.reshape(DEC_BATCH, n_b, PAST_LEN * B_CHUNKS, LANE)
    cv_b = cache_b_v.reshape(DEC_BATCH, n_b, PAST_LEN * B_CHUNKS, LANE)
    subln = a_subln_g.reshape(n_a, 1, 2 * A_HEAD_DIM)
    b_gains = jnp.stack([b_q_norm_g, b_k_norm_g], axis=1).reshape(n_b, 2, 1, B_HEAD_DIM)

    x = (x_prompt.reshape(M_P, D_MODEL), x_sample.reshape(M_S, D_MODEL))
    u = _embed(*x, mod)

    a_kv, b_kv = [], []
    for l in range(DEPTH):
        kind, j = l % N_MIXERS, l // N_MIXERS
        if kind == 0:
            lam_init = 0.8 - 0.6 * math.exp(-0.3 * l)
            q_p = _proj(u, a_w_qkv, j, 0, D_MODEL, 0, M_P, out_dtype=BF16, name="a_q_prompt")
            kv_p = _proj(u, a_w_qkv, j, D_MODEL, 2 * D_MODEL, 0, M_P, out_dtype=F32,
                         name="a_kv_prompt")
            qk_s = _proj(u, a_w_qkv, j, 0, 2 * D_MODEL, M_P, M_S, out_dtype=BF16, rope=rope,
                         name="a_qk_latent")
            v_s = _proj(u, a_w_qkv, j, 2 * D_MODEL, D_MODEL, M_P, M_S, out_dtype=BF16,
                        name="a_v_latent")
            if j + 1 < n_a:
                att_p = _diff_prompt(a_lambda, subln, q_p, kv_p, j, lam_init)
                a_kv.append(kv_p)
            else:
                att_p, new_a_k, new_a_v = _diff_prompt(a_lambda, subln, q_p, kv_p, j, lam_init,
                                                       prev_kv=a_kv)
            att_s = _diff_latent(a_lambda, subln, qk_s, v_s, ck_a, cv_a, j, lam_init)
            w_o = w_o_bf[0]
        elif kind == 1:
            gj = b_gains[j]
            q_p = _proj(u, b_w_qkv, j, 0, D_MODEL, 0, M_P, out_dtype=BF16, gains=gj,
                        gain_split=D_MODEL, name="b_q_prompt")
            k_p = _proj(u, b_w_qkv, j, D_MODEL, B_KV, 0, M_P, out_dtype=F32, tn=B_KV,
                        gains=gj[1:], gain_split=B_KV, name="b_k_prompt")
            v_p = _proj(u, b_w_qkv, j, D_MODEL + B_KV, B_KV, 0, M_P, out_dtype=F32, tn=B_KV,
                        name="b_v_prompt")
            qk_s = _proj(u, b_w_qkv, j, 0, D_MODEL + B_KV, M_P, M_S, out_dtype=BF16, tn=B_KV,
                         gains=gj, gain_split=D_MODEL, rope=rope, name="b_qk_latent")
            v_s = _proj(u, b_w_qkv, j, D_MODEL + B_KV, B_KV, M_P, M_S, out_dtype=BF16, tn=B_KV,
                        name="b_v_latent")
            if j + 1 < n_b:
                att_p = _gqa_prompt(q_p, k_p, v_p)
                b_kv.append((k_p, v_p))
            else:
                att_p, new_b_k, new_b_v = _gqa_prompt(q_p, k_p, v_p, prev_kv=b_kv)
            att_s = _gqa_latent(qk_s, v_s, ck_b, cv_b, j)
            w_o = w_o_bf[1]
        else:
            att_p = _dft(u, SEQ, BATCH, 0, C_GROUPS)
            att_s = _dft(u, DEC_SEQ, DEC_BATCH, M_P, 1)
            w_o = w_o_bf[2]
        x, u = _oproj(att_p, att_s, w_o, j, x, mod, ln_g4, ln_b4, l)
        x, u = _mlp(u, w_up, w_down, x, mod, ln_g4, ln_b4, l)

    y_p, y_s = x, u
    return (y_p.reshape(BATCH, SEQ, D_MODEL), y_s.reshape(DEC_BATCH, DEC_SEQ, D_MODEL),
            new_a_k.reshape(BATCH, n_a, SEQ, A_HEADS, 2, A_HEAD_DIM),
            _a_v_from_rows(new_a_v, BATCH, SEQ),
            new_b_k.reshape(BATCH, n_b, SEQ, B_KV_HEADS, B_HEAD_DIM),
            new_b_v.reshape(BATCH, n_b, SEQ, B_KV_HEADS, B_HEAD_DIM))
```

```python
import functools
import math

import jax
import jax.numpy as jnp
import numpy as np
from jax import lax
from jax.experimental import pallas as pl
from jax.experimental.pallas import tpu as pltpu

D_MODEL = 2048
BATCH = 16
SEQ = 256
DEPTH = 4
DEC_BATCH = 2
DEC_SEQ = 1024
PAST_LEN = 512
GRID_W = 64
N_MIXERS = 3
A_HEAD_DIM = 128
A_HEADS = D_MODEL // (2 * A_HEAD_DIM)
B_HEAD_DIM = 128
B_HEADS = D_MODEL // B_HEAD_DIM
B_KV_HEADS = B_HEADS // 4
B_GROUP = B_HEADS // B_KV_HEADS
B_KV = B_KV_HEADS * B_HEAD_DIM
C_GROUPS = 4
C_GROUP_DIM = D_MODEL // C_GROUPS
D_FF = 4 * D_MODEL
ROPE_BASE = 10000.0
ALPHA = (2 * DEPTH) ** 0.25
LN_EPS = 1e-5
RMS_EPS = 1e-6
N_MOD = 6
LOG2E = 1.4426950408889634

M_P = BATCH * SEQ
M_S = DEC_BATCH * DEC_SEQ
M_ALL = M_P + M_S
N_COND = 1 + DEC_BATCH
COND_PAD = 8

LANE = 128
VMEM_LIMIT = 58 * 1024 * 1024
ROW_TILE = 512

BF16 = jnp.bfloat16
F32 = jnp.float32


def _params(*sem, vmem=VMEM_LIMIT):
    return pltpu.CompilerParams(dimension_semantics=sem, vmem_limit_bytes=vmem)


def _cond_row(i, tm):
    start = i * tm
    return jnp.where(start < M_P, 0, 1 + (start - M_P) // DEC_SEQ)


def _mod_spec(l, which, tm):
    def idx(*g):
        return (l, _cond_row(g[0], tm), which, 0, 0)
    return pl.BlockSpec((None, None, None, 1, D_MODEL), idx)


def _vec_spec(l, which):
    return pl.BlockSpec((None, None, 1, D_MODEL), lambda *g: (l, which, 0, 0))


def _prompt_rows_spec(tm):
    last = M_P // tm - 1
    return pl.BlockSpec((tm, D_MODEL), lambda i: (jnp.minimum(i, last), 0))


def _latent_rows_spec(tm):
    first = M_P // tm
    return pl.BlockSpec((tm, D_MODEL), lambda i: (jnp.maximum(i - first, 0), 0))


def _mod_kernel(c_ref, w_ref, b_ref, o_ref):
    c = c_ref[...]
    s = c * (1.0 / (1.0 + jnp.exp(-c)))
    o_ref[...] = jnp.dot(s.astype(BF16), w_ref[...].astype(BF16),
                         preferred_element_type=F32) + b_ref[...]


def _modulation(cond, w_mod, b_mod):
    tn = 1024
    n_out = N_MOD * D_MODEL
    return pl.pallas_call(
        _mod_kernel,
        out_shape=jax.ShapeDtypeStruct((DEPTH, COND_PAD, n_out), F32),
        grid=(DEPTH, n_out // tn),
        in_specs=[
            pl.BlockSpec((COND_PAD, D_MODEL), lambda l, n: (0, 0)),
            pl.BlockSpec((None, D_MODEL, tn), lambda l, n: (l, 0, n)),
            pl.BlockSpec((None, 1, tn), lambda l, n: (l, 0, n)),
        ],
        out_specs=pl.BlockSpec((None, COND_PAD, tn), lambda l, n: (l, 0, n)),
        compiler_params=_params("arbitrary", "arbitrary"),
        name="modulation",
    )(cond, w_mod, b_mod.reshape(DEPTH, 1, n_out))


def _cast_kernel(w_ref, o_ref):
    o_ref[...] = w_ref[...].astype(BF16)


def _cast_bf16(w, rows=512):
    n_l, k, n = w.shape
    spec = pl.BlockSpec((None, rows, n), lambda l, r: (l, r, 0))
    return pl.pallas_call(
        _cast_kernel,
        out_shape=jax.ShapeDtypeStruct(w.shape, BF16),
        grid=(n_l, k // rows),
        in_specs=[spec],
        out_specs=spec,
        compiler_params=_params("arbitrary", "arbitrary"),
        name="cast_weight",
    )(w)


def _embed_kernel(xp_ref, xs_ref, sc_ref, sh_ref, u_ref):
    x = jnp.where(pl.program_id(0) * ROW_TILE < M_P, xp_ref[...], xs_ref[...])
    u_ref[...] = (x * (1.0 + sc_ref[...]) + sh_ref[...]).astype(BF16)


def _embed(xp, xs, mod):
    tm = ROW_TILE
    return pl.pallas_call(
        _embed_kernel,
        out_shape=jax.ShapeDtypeStruct((M_ALL, D_MODEL), BF16),
        grid=(M_ALL // tm,),
        in_specs=[_prompt_rows_spec(tm), _latent_rows_spec(tm),
                  _mod_spec(0, 1, tm), _mod_spec(0, 0, tm)],
        out_specs=pl.BlockSpec((tm, D_MODEL), lambda i: (i, 0)),
        compiler_params=_params("arbitrary"),
        name="embed",
    )(xp, xs, mod, mod)


def _rms_chunk(x, g):
    ms = jnp.mean(x * x, axis=-1, keepdims=True)
    return x * lax.rsqrt(ms + RMS_EPS) * g


def _rope_chunk(x, cos, sin_a, sin_b):
    return x * cos + pltpu.roll(x, LANE - 32, 1) * sin_a + pltpu.roll(x, 32, 1) * sin_b


def _proj_kernel(*refs, tn, rms, rope):
    it = iter(refs)
    u_ref, w_ref = next(it), next(it)
    g_ref = next(it) if rms else None
    tabs = (next(it), next(it), next(it)) if rope else None
    o_ref, wbf = next(it), next(it)

    @pl.when(pl.program_id(1) == 0)
    def _():
        wbf[...] = w_ref[...].astype(BF16)

    if rms and rope:
        acc = jnp.dot(u_ref[...], wbf[...], preferred_element_type=F32)
        chunks = [acc[:, c * LANE:(c + 1) * LANE] for c in range(tn // LANE)]
        g = g_ref[...]
        tab = tuple(t[...] for t in tabs)
        rinv = [lax.rsqrt(jnp.mean(x * x, axis=-1, keepdims=True) + RMS_EPS) for x in chunks]

        @pl.when(pl.program_id(1) >= 0)
        def _():
            for c, (x, r) in enumerate(zip(chunks, rinv)):
                o_ref[:, c * LANE:(c + 1) * LANE] = (_rope_chunk(x * g, *tab) * r).astype(o_ref.dtype)
    elif rms or rope:
        rows = u_ref.shape[0] // 2
        g = g_ref[...] if rms else None
        for r in range(2):
            sl = slice(r * rows, (r + 1) * rows)
            acc = jnp.dot(u_ref[sl, :], wbf[...], preferred_element_type=F32)
            for c in range(tn // LANE):
                x = acc[:, c * LANE:(c + 1) * LANE]
                if rms:
                    x = _rms_chunk(x, g)
                else:
                    x = _rope_chunk(x, *(t[sl, :] for t in tabs))
                o_ref[sl, c * LANE:(c + 1) * LANE] = x.astype(o_ref.dtype)
    else:
        acc = jnp.dot(u_ref[...], wbf[...], preferred_element_type=F32)
        o_ref[...] = acc.astype(o_ref.dtype)


def _proj(u, w, w_layer, col0, ncols, m0, m_rows, *, out_dtype, tn=1024, tm=1024,
          gains=None, gain_split=None, rope=None, name="proj"):
    k = u.shape[1]
    assert col0 % tn == 0 and ncols % tn == 0 and m0 % tm == 0 and m_rows % tm == 0
    n_t, m_t, mt0, nt0 = ncols // tn, m_rows // tm, m0 // tm, col0 // tn
    ins = [u, w]
    in_specs = [
        pl.BlockSpec((tm, k), lambda n, m: (mt0 + m, 0)),
        pl.BlockSpec((None, k, tn), lambda n, m: (w_layer, 0, nt0 + n)),
    ]
    if gains is not None:
        ins.append(gains)
        in_specs.append(pl.BlockSpec(
            (None, 1, LANE), lambda n, m: (jnp.where(n * tn < gain_split, 0, 1), 0, 0)))
    if rope is not None:
        assert m0 >= M_P
        per = DEC_SEQ // tm
        for t in rope:
            ins.append(t)
            in_specs.append(pl.BlockSpec((tm, LANE), lambda n, m: ((mt0 + m) % per, 0)))
    kern = functools.partial(_proj_kernel, tn=tn, rms=gains is not None, rope=rope is not None)
    return pl.pallas_call(
        kern,
        out_shape=jax.ShapeDtypeStruct((m_rows, ncols), out_dtype),
        grid=(n_t, m_t),
        in_specs=in_specs,
        out_specs=pl.BlockSpec((tm, tn), lambda n, m: (m, n)),
        scratch_shapes=[pltpu.VMEM((k, tn), BF16)],
        compiler_params=_params("arbitrary", "arbitrary"),
        name=name,
    )(*ins)


def _qkt(q, k):
    return lax.dot_general(q, k, (((1,), (1,)), ((), ())), preferred_element_type=F32)


def _softmax_terms(s, scale):
    m = s.max(axis=-1, keepdims=True)
    e = jnp.exp2((s - m) * (scale * LOG2E))
    return e, 1.0 / e.sum(axis=-1, keepdims=True)


def _diff_lambda(lp_ref, lam_init):
    lp = lp_ref[...]
    s1 = jnp.sum(lp[0:1] * lp[1:2], axis=-1, keepdims=True)
    s2 = jnp.sum(lp[2:3] * lp[3:4], axis=-1, keepdims=True)
    return jnp.exp(s1) - jnp.exp(s2) + lam_init


def _diff_weights(s, lam):
    half = s.shape[0] // 2
    e, r = _softmax_terms(s, A_HEAD_DIM ** -0.5)
    return (e[:half] * r[:half] - e[half:] * (lam * r[half:])).astype(BF16)


def _sub_norm(o, gain):
    ms = jnp.mean(o * o, axis=-1, keepdims=True)
    return (o * lax.rsqrt(ms + RMS_EPS) * gain).astype(BF16)


A_CHUNKS = D_MODEL // LANE
B_CHUNKS = B_KV // LANE


def _a_v_rows(v, batch, seq):
    n_l = v.shape[1]
    v = v.reshape(batch, n_l, seq, A_HEADS, 2, LANE).transpose(0, 1, 2, 4, 3, 5)
    return v.reshape(batch, n_l, seq * A_CHUNKS, LANE)


def _a_v_from_rows(r, batch, seq):
    n_l = r.shape[1]
    v = r.reshape(batch, n_l, seq, 2, A_HEADS, LANE).transpose(0, 1, 2, 4, 3, 5)
    return v.reshape(batch, n_l, seq, A_HEADS, 2 * LANE)


def _diff_prompt_kernel(*refs, lam_init, n_prev, emit):
    lp_ref, g_ref, q_ref, k_ref, v_ref = refs[:5]
    prev = refs[5:5 + n_prev]
    o_ref = refs[5 + n_prev]
    s_scr, a_scr, o_scr = refs[-3:]
    lam = _diff_lambda(lp_ref, lam_init)
    gain = g_ref[...] * (1.0 - lam_init)
    w = 2 * A_HEAD_DIM
    for h in range(A_HEADS):
        for p in range(2):
            cols = slice(h * w + p * A_HEAD_DIM, h * w + (p + 1) * A_HEAD_DIM)
            s_scr[(p * A_HEADS + h) * SEQ:(p * A_HEADS + h + 1) * SEQ, :] = _qkt(
                q_ref[:, cols], k_ref[:, cols].astype(BF16))

    @pl.when(pl.program_id(0) >= 0)
    def _():
        a_scr[...] = _diff_weights(s_scr[...], lam)
        for h in range(A_HEADS):
            rows = slice(h * SEQ, (h + 1) * SEQ)
            o_scr[rows, :] = jnp.dot(a_scr[rows, :], v_ref[:, h * w:(h + 1) * w].astype(BF16),
                                     preferred_element_type=F32)
        y = _sub_norm(o_scr[...], gain)
        for h in range(A_HEADS):
            o_ref[:, h * w:(h + 1) * w] = y[h * SEQ:(h + 1) * SEQ, :]

    if emit:
        nk_ref, nv_ref = refs[6 + n_prev], refs[7 + n_prev]
        layers = [(p, 0, p, D_MODEL) for p in prev] + [(k_ref, 0, v_ref, 0)]
        for jj, (kr, k0, vr, v0) in enumerate(layers):
            nk_ref[jj] = kr[:, k0:k0 + D_MODEL].reshape(SEQ, A_CHUNKS, LANE)
            for t in range(2):
                halves = [vr[:, v0 + (2 * h + t) * LANE:v0 + (2 * h + t + 1) * LANE]
                          for h in range(A_HEADS)]
                nv_ref[jj, :, t * A_HEADS:(t + 1) * A_HEADS, :] = jnp.concatenate(
                    halves, axis=1).reshape(SEQ, A_HEADS, LANE)


def _diff_prompt(lp, g, q, kv, j, lam_init, prev_kv=None):
    emit = prev_kv is not None
    prev_kv = list(prev_kv or [])
    n_layers = len(prev_kv) + 1
    att_shape = jax.ShapeDtypeStruct((M_P, D_MODEL), BF16)
    att_spec = pl.BlockSpec((SEQ, D_MODEL), lambda b: (b, 0))
    if emit:
        new_shape = jax.ShapeDtypeStruct((BATCH, n_layers, SEQ, A_CHUNKS, LANE), F32)
        new_spec = pl.BlockSpec((None, n_layers, SEQ, A_CHUNKS, LANE), lambda b: (b, 0, 0, 0, 0))
        out_shape, out_specs = (att_shape, new_shape, new_shape), (att_spec, new_spec, new_spec)
    else:
        out_shape, out_specs = att_shape, att_spec
    return pl.pallas_call(
        functools.partial(_diff_prompt_kernel, lam_init=lam_init, n_prev=len(prev_kv), emit=emit),
        out_shape=out_shape,
        grid=(BATCH,),
        in_specs=[
            pl.BlockSpec((None, 4, A_HEAD_DIM), lambda b: (j, 0, 0)),
            pl.BlockSpec((None, 1, 2 * A_HEAD_DIM), lambda b: (j, 0, 0)),
            pl.BlockSpec((SEQ, D_MODEL), lambda b: (b, 0)),
            pl.BlockSpec((SEQ, D_MODEL), lambda b: (b, 0)),
            pl.BlockSpec((SEQ, D_MODEL), lambda b: (b, 1)),
        ] + [pl.BlockSpec((SEQ, 2 * D_MODEL), lambda b: (b, 0)) for _ in prev_kv],
        out_specs=out_specs,
        scratch_shapes=[pltpu.VMEM((2 * A_HEADS * SEQ, SEQ), F32),
                        pltpu.VMEM((A_HEADS * SEQ, SEQ), BF16),
                        pltpu.VMEM((A_HEADS * SEQ, 2 * A_HEAD_DIM), F32)],
        compiler_params=_params("arbitrary"),
        name="diff_attn_prompt",
    )(lp, g, q, kv, kv, *prev_kv)


def _diff_latent_kernel(lp_ref, g_ref, q_ref, k_ref, v_ref, ck_ref, cv_ref, o_ref,
                        s_scr, a_scr, *, lam_init):
    lam = _diff_lambda(lp_ref, lam_init)
    gain = g_ref[...] * (1.0 - lam_init)
    h = pl.program_id(1)
    tq, d = q_ref.shape[0], A_HEAD_DIM

    def cached(ref, r0, r1):
        halves = [ref[pl.ds(r, PAST_LEN, stride=A_CHUNKS), :] for r in (r0, r1)]
        return jnp.concatenate(halves, axis=1).astype(BF16)

    ck = cached(ck_ref, 2 * h, 2 * h + 1)
    for p in range(2):
        q = q_ref[:, p * d:(p + 1) * d]
        s_scr[p * tq:(p + 1) * tq, :PAST_LEN] = _qkt(q, ck[:, p * d:(p + 1) * d])
        s_scr[p * tq:(p + 1) * tq, PAST_LEN:] = _qkt(q, k_ref[:, p * d:(p + 1) * d])

    @pl.when(h >= 0)
    def _():
        a_scr[...] = _diff_weights(s_scr[...], lam)
        cv = cached(cv_ref, h, A_HEADS + h)
        o = (jnp.dot(a_scr[:, :PAST_LEN], cv, preferred_element_type=F32)
             + jnp.dot(a_scr[:, PAST_LEN:], v_ref[...], preferred_element_type=F32))
        o_ref[...] = _sub_norm(o, gain)


def _diff_latent(lp, g, qk, v, ck, cv, j, lam_init, tq=1024):
    w = 2 * A_HEAD_DIM
    per = DEC_SEQ // tq
    ctx_spec = pl.BlockSpec((None, None, PAST_LEN * A_CHUNKS, LANE), lambda b, h, t: (b, j, 0, 0))
    return pl.pallas_call(
        functools.partial(_diff_latent_kernel, lam_init=lam_init),
        out_shape=jax.ShapeDtypeStruct((M_S, D_MODEL), BF16),
        grid=(DEC_BATCH, A_HEADS, per),
        in_specs=[
            pl.BlockSpec((None, 4, A_HEAD_DIM), lambda b, h, t: (j, 0, 0)),
            pl.BlockSpec((None, 1, w), lambda b, h, t: (j, 0, 0)),
            pl.BlockSpec((tq, w), lambda b, h, t: (b * per + t, h)),
            pl.BlockSpec((DEC_SEQ, w), lambda b, h, t: (b, A_HEADS + h)),
            pl.BlockSpec((DEC_SEQ, w), lambda b, h, t: (b, h)),
            ctx_spec, ctx_spec,
        ],
        out_specs=pl.BlockSpec((tq, w), lambda b, h, t: (b * per + t, h)),
        scratch_shapes=[pltpu.VMEM((2 * tq, PAST_LEN + DEC_SEQ), F32),
                        pltpu.VMEM((tq, PAST_LEN + DEC_SEQ), BF16)],
        compiler_params=_params("arbitrary", "arbitrary", "arbitrary"),
        name="diff_attn_latent",
    )(lp, g, qk, qk, v, ck, cv)


def _gqa_prompt_kernel(*refs, n_prev, emit):
    q_ref, k_ref, v_ref = refs[:3]
    prev = refs[3:3 + 2 * n_prev]
    o_ref = refs[3 + 2 * n_prev]
    s_scr, a_scr, o_scr = refs[-3:]
    d = B_HEAD_DIM
    for n in range(B_KV_HEADS):
        k = k_ref[:, n * d:(n + 1) * d].astype(BF16)
        for g in range(B_GROUP):
            hd = n * B_GROUP + g
            s_scr[hd * SEQ:(hd + 1) * SEQ, :] = _qkt(q_ref[:, hd * d:(hd + 1) * d], k)

    @pl.when(pl.program_id(0) >= 0)
    def _():
        e, r = _softmax_terms(s_scr[...], B_HEAD_DIM ** -0.5)
        a_scr[...] = (e * r).astype(BF16)
        group_rows = B_GROUP * SEQ
        for n in range(B_KV_HEADS):
            rows = slice(n * group_rows, (n + 1) * group_rows)
            o_scr[rows, :] = jnp.dot(a_scr[rows, :], v_ref[:, n * d:(n + 1) * d].astype(BF16),
                                     preferred_element_type=F32)
        for hd in range(B_HEADS):
            o_ref[:, hd * d:(hd + 1) * d] = o_scr[hd * SEQ:(hd + 1) * SEQ, :].astype(BF16)

    if emit:
        nk_ref, nv_ref = refs[4 + 2 * n_prev], refs[5 + 2 * n_prev]
        layers = [(prev[2 * i], prev[2 * i + 1]) for i in range(n_prev)] + [(k_ref, v_ref)]
        for jj, (kr, vr) in enumerate(layers):
            for n in range(B_CHUNKS):
                rows = pl.ds(n, SEQ, stride=B_CHUNKS)
                nk_ref[jj, rows, :] = kr[:, n * LANE:(n + 1) * LANE]
                nv_ref[jj, rows, :] = vr[:, n * LANE:(n + 1) * LANE]


def _gqa_prompt(q, k, v, prev_kv=None):
    emit = prev_kv is not None
    prev = [a for pair in (prev_kv or []) for a in pair]
    n_layers = len(prev) // 2 + 1
    kv_spec = pl.BlockSpec((SEQ, B_KV), lambda b: (b, 0))
    att_shape = jax.ShapeDtypeStruct((M_P, D_MODEL), BF16)
    att_spec = pl.BlockSpec((SEQ, D_MODEL), lambda b: (b, 0))
    if emit:
        new_shape = jax.ShapeDtypeStruct((BATCH, n_layers, SEQ * B_CHUNKS, LANE), F32)
        new_spec = pl.BlockSpec((None, n_layers, SEQ * B_CHUNKS, LANE), lambda b: (b, 0, 0, 0))
        out_shape, out_specs = (att_shape, new_shape, new_shape), (att_spec, new_spec, new_spec)
    else:
        out_shape, out_specs = att_shape, att_spec
    return pl.pallas_call(
        functools.partial(_gqa_prompt_kernel, n_prev=len(prev) // 2, emit=emit),
        out_shape=out_shape,
        grid=(BATCH,),
        in_specs=[att_spec, kv_spec, kv_spec] + [kv_spec for _ in prev],
        out_specs=out_specs,
        scratch_shapes=[pltpu.VMEM((B_HEADS * SEQ, SEQ), F32),
                        pltpu.VMEM((B_HEADS * SEQ, SEQ), BF16),
                        pltpu.VMEM((B_HEADS * SEQ, B_HEAD_DIM), F32)],
        compiler_params=_params("arbitrary"),
        name="gqa_attn_prompt",
    )(q, k, v, *prev)


def _gqa_latent_kernel(q_ref, k_ref, v_ref, ck_ref, cv_ref, o_ref, s_scr, a_scr):
    d, tq = B_HEAD_DIM, q_ref.shape[0]
    n = pl.program_id(1)
    cached = pl.ds(n, PAST_LEN, stride=B_CHUNKS)
    ck = ck_ref[cached, :].astype(BF16)
    for g in range(B_GROUP):
        q = q_ref[:, g * d:(g + 1) * d]
        s_scr[g * tq:(g + 1) * tq, :PAST_LEN] = _qkt(q, ck)
        s_scr[g * tq:(g + 1) * tq, PAST_LEN:] = _qkt(q, k_ref[...])

    @pl.when(n >= 0)
    def _():
        e, r = _softmax_terms(s_scr[...], B_HEAD_DIM ** -0.5)
        a_scr[...] = (e * r).astype(BF16)
        o = (jnp.dot(a_scr[:, :PAST_LEN], cv_ref[cached, :].astype(BF16),
                     preferred_element_type=F32)
             + jnp.dot(a_scr[:, PAST_LEN:], v_ref[...], preferred_element_type=F32))
        for g in range(B_GROUP):
            o_ref[:, g * d:(g + 1) * d] = o[g * tq:(g + 1) * tq, :].astype(BF16)


def _gqa_latent(qk, v, ck, cv, j, tq=1024):
    d = B_HEAD_DIM
    gw = B_GROUP * d
    per = DEC_SEQ // tq
    ctx_spec = pl.BlockSpec((None, None, PAST_LEN * B_CHUNKS, LANE), lambda b, n, t: (b, j, 0, 0))
    return pl.pallas_call(
        _gqa_latent_kernel,
        out_shape=jax.ShapeDtypeStruct((M_S, D_MODEL), BF16),
        grid=(DEC_BATCH, B_KV_HEADS, per),
        in_specs=[
            pl.BlockSpec((tq, gw), lambda b, n, t: (b * per + t, n)),
            pl.BlockSpec((DEC_SEQ, d), lambda b, n, t: (b, D_MODEL // d + n)),
            pl.BlockSpec((DEC_SEQ, d), lambda b, n, t: (b, n)),
            ctx_spec, ctx_spec,
        ],
        out_specs=pl.BlockSpec((tq, gw), lambda b, n, t: (b * per + t, n)),
        scratch_shapes=[pltpu.VMEM((B_GROUP * tq, PAST_LEN + DEC_SEQ), F32),
                        pltpu.VMEM((B_GROUP * tq, PAST_LEN + DEC_SEQ), BF16)],
        compiler_params=_params("arbitrary", "arbitrary", "arbitrary"),
        name="gqa_attn_latent",
    )(qk, qk, v, ck, cv)


def _dft_tables(s):
    def cs(n):
        idx = (np.arange(n)[:, None] * np.arange(n)[None, :]) % n
        ang = 2.0 * np.pi * idx / n
        return np.cos(ang), np.sin(ang)
    cc, sc = cs(C_GROUP_DIM)
    c_s, s_s = cs(s)
    t_chan = np.concatenate([cc, sc], axis=1).astype(np.float32)
    t_pos = np.concatenate([c_s, -s_s], axis=1).astype(np.float32)
    return jnp.asarray(t_chan).astype(BF16), jnp.asarray(t_pos).astype(BF16)


def _dft_group(x, t_chan, t_pos, norm):
    xcs = jnp.dot(x, t_chan, preferred_element_type=F32).astype(BF16)
    stacked = jnp.concatenate([xcs[:, :C_GROUP_DIM], xcs[:, C_GROUP_DIM:]], axis=0)
    y = jnp.dot(t_pos, stacked, preferred_element_type=F32)
    return (y * norm).astype(BF16)


def _dft_kernel(u_ref, tc_ref, tp_ref, o_ref, *, groups, norm):
    tc, tp = tc_ref[...], tp_ref[...]
    for g in range(groups):
        sl = slice(g * C_GROUP_DIM, (g + 1) * C_GROUP_DIM)
        o_ref[:, sl] = _dft_group(u_ref[:, sl], tc, tp, norm)


def _dft(u, s, batch, row0, groups_per_step):
    t_chan, t_pos = _dft_tables(s)
    gw = groups_per_step * C_GROUP_DIM
    rb0 = row0 // s
    return pl.pallas_call(
        functools.partial(_dft_kernel, groups=groups_per_step,
                          norm=1.0 / math.sqrt(s * C_GROUP_DIM)),
        out_shape=jax.ShapeDtypeStruct((batch * s, D_MODEL), BF16),
        grid=(batch, C_GROUPS // groups_per_step),
        in_specs=[
            pl.BlockSpec((s, gw), lambda b, g: (rb0 + b, g)),
            pl.BlockSpec(t_chan.shape, lambda b, g: (0, 0)),
            pl.BlockSpec(t_pos.shape, lambda b, g: (0, 0)),
        ],
        out_specs=pl.BlockSpec((s, gw), lambda b, g: (b, g)),
        compiler_params=_params("arbitrary", "arbitrary"),
        name="dft",
    )(u, t_chan, t_pos)


ROW_GROUPS = 2


def _post_norm(y, g, b):
    mu = jnp.mean(y, axis=-1, keepdims=True)
    yc = y - mu
    var = jnp.mean(yc * yc, axis=-1, keepdims=True)
    return yc * lax.rsqrt(var + LN_EPS) * g + b


def _oproj_kernel(*refs, split_x):
    ap_ref, as_ref, w_ref = refs[:3]
    x_refs = refs[3:5] if split_x else refs[3:4]
    gate_ref, g_ref, b_ref, sc_ref, sh_ref, x1_ref, u_ref = refs[3 + len(x_refs):]
    is_prompt = pl.program_id(0) * ROW_TILE < M_P
    gate, g, b = gate_ref[...], g_ref[...], b_ref[...]
    sc1, sh = 1.0 + sc_ref[...], sh_ref[...]
    rows = ROW_TILE // ROW_GROUPS
    for r in range(ROW_GROUPS):
        sl = slice(r * rows, (r + 1) * rows)
        a = jnp.where(is_prompt, ap_ref[sl, :], as_ref[sl, :])
        mixed = jnp.dot(a, w_ref[...], preferred_element_type=F32)
        if split_x:
            x = jnp.where(is_prompt, x_refs[0][sl, :], x_refs[1][sl, :])
        else:
            x = x_refs[0][sl, :]
        x1 = _post_norm(ALPHA * x + gate * mixed, g, b)
        x1_ref[sl, :] = x1
        u_ref[sl, :] = (x1 * sc1 + sh).astype(BF16)


def _oproj(a_p, a_s, w_bf, w_layer, x, mod, ln_g, ln_b, l):
    tm = ROW_TILE
    row = pl.BlockSpec((tm, D_MODEL), lambda i: (i, 0))
    split_x = isinstance(x, tuple)
    xs = list(x) if split_x else [x]
    x_specs = [_prompt_rows_spec(tm), _latent_rows_spec(tm)] if split_x else [row]
    return pl.pallas_call(
        functools.partial(_oproj_kernel, split_x=split_x),
        out_shape=(jax.ShapeDtypeStruct((M_ALL, D_MODEL), F32),
                   jax.ShapeDtypeStruct((M_ALL, D_MODEL), BF16)),
        grid=(M_ALL // tm,),
        in_specs=[
            _prompt_rows_spec(tm), _latent_rows_spec(tm),
            pl.BlockSpec((None, D_MODEL, D_MODEL), lambda i: (w_layer, 0, 0),
                         pipeline_mode=pl.Buffered(1)),
            *x_specs,
            _mod_spec(l, 2, tm),
            _vec_spec(l, 0), _vec_spec(l, 0),
            _mod_spec(l, 4, tm),
            _mod_spec(l, 3, tm),
        ],
        out_specs=(row, row),
        compiler_params=_params("arbitrary"),
        name="oproj_ln",
    )(a_p, a_s, w_bf, *xs, mod, ln_g, ln_b, mod, mod)


MLP_FF_PIECE = 512


def _mlp_kernel(*refs, n_tiles, n_ff, last):
    u_ref, wu_ref, wd_ref, x_ref, gate_ref, g_ref, b_ref = refs[:7]
    accs = refs[-2:]
    i, f = pl.program_id(0), pl.program_id(1)
    tm = u_ref.shape[0]
    close_rows = tm // n_ff

    for parity in (0, 1):
        @pl.when(jnp.logical_and(f == 0, i % 2 == parity))
        def _():
            accs[parity][...] = jnp.zeros_like(accs[parity])

    def matmuls(acc):
        tf = wu_ref.shape[1]
        for c in range(tf // MLP_FF_PIECE):
            cols = slice(c * MLP_FF_PIECE, (c + 1) * MLP_FF_PIECE)
            h = jnp.dot(u_ref[...], wu_ref[:, cols].astype(BF16), preferred_element_type=F32)
            h = jnp.maximum(h, 0.0)
            h = (h * h).astype(BF16)
            acc[...] += jnp.dot(h, wd_ref[cols, :].astype(BF16), preferred_element_type=F32)

    def close(prev):
        rows = pl.ds(pl.multiple_of(f * close_rows, close_rows), close_rows)
        y = _post_norm(ALPHA * x_ref[...] + gate_ref[...] * prev[rows, :], g_ref[...], b_ref[...])
        if last:
            yp_ref, ys_ref = refs[7:9]
            prev_is_prompt = (i - 1) * tm < M_P

            @pl.when(prev_is_prompt)
            def _():
                yp_ref[...] = y

            @pl.when(jnp.logical_not(prev_is_prompt))
            def _():
                ys_ref[...] = y
        else:
            sc_ref, sh_ref, x2_ref, un_ref = refs[7:11]
            x2_ref[...] = y
            un_ref[...] = (y * (1.0 + sc_ref[...]) + sh_ref[...]).astype(BF16)

    @pl.when(i == 0)
    def _():
        matmuls(accs[0])

    middle = jnp.logical_and(i > 0, i < n_tiles)
    for parity in (0, 1):
        @pl.when(jnp.logical_and(middle, i % 2 == parity))
        def _():
            matmuls(accs[parity])
            close(accs[1 - parity])

    @pl.when(i == n_tiles)
    def _():
        close(accs[(n_tiles - 1) % 2])


def _mlp(u, w_up, w_down, x, mod, ln_g, ln_b, l, tm=1024, tf=512):
    last = l + 1 == DEPTH
    n_tiles, n_ff = M_ALL // tm, D_FF // tf
    close_rows = tm // n_ff

    def close_block(i, f):
        return jnp.where(i == 0, 0, (i - 1) * n_ff + f)

    def mod_prev(layer, which):
        def idx(i, f):
            return (layer, _cond_row(jnp.maximum(i - 1, 0), tm), which, 0, 0)
        return pl.BlockSpec((None, None, None, 1, D_MODEL), idx)

    def ff_block(i, f):
        return jnp.where(i < n_tiles, f, n_ff - 1)

    row = pl.BlockSpec((close_rows, D_MODEL), lambda i, f: (close_block(i, f), 0))
    ins = [u, w_up, w_down, x, mod, ln_g, ln_b]
    in_specs = [
        pl.BlockSpec((tm, D_MODEL), lambda i, f: (jnp.minimum(i, n_tiles - 1), 0)),
        pl.BlockSpec((None, D_MODEL, tf), lambda i, f: (l, 0, ff_block(i, f))),
        pl.BlockSpec((None, tf, D_MODEL), lambda i, f: (l, ff_block(i, f), 0)),
        row, mod_prev(l, 5), _vec_spec(l, 1), _vec_spec(l, 1),
    ]
    if last:
        n_p = M_P // close_rows
        out_shape = (jax.ShapeDtypeStruct((M_P, D_MODEL), F32),
                     jax.ShapeDtypeStruct((M_S, D_MODEL), F32))
        out_specs = (
            pl.BlockSpec((close_rows, D_MODEL),
                         lambda i, f: (jnp.minimum(close_block(i, f), n_p - 1), 0)),
            pl.BlockSpec((close_rows, D_MODEL),
                         lambda i, f: (jnp.maximum(close_block(i, f) - n_p, 0), 0)),
        )
    else:
        ins += [mod, mod]
        in_specs += [mod_prev(l + 1, 1), mod_prev(l + 1, 0)]
        out_shape = (jax.ShapeDtypeStruct((M_ALL, D_MODEL), F32),
                     jax.ShapeDtypeStruct((M_ALL, D_MODEL), BF16))
        out_specs = (row, row)
    return pl.pallas_call(
        functools.partial(_mlp_kernel, n_tiles=n_tiles, n_ff=n_ff, last=last),
        out_shape=out_shape,
        grid=(n_tiles + 1, n_ff),
        in_specs=in_specs,
        out_specs=out_specs,
        scratch_shapes=[pltpu.VMEM((tm, D_MODEL), F32), pltpu.VMEM((tm, D_MODEL), F32)],
        compiler_params=_params("arbitrary", "arbitrary"),
        name="mlp",
    )(*ins)


def _rope_tables():
    n_freq = A_HEAD_DIM // 4
    pos = np.arange(DEC_SEQ)
    row = (pos // GRID_W).astype(np.float32)
    col = (pos % GRID_W).astype(np.float32)
    inv_freq = (ROPE_BASE ** (-np.arange(n_freq, dtype=np.float32) / n_freq)).astype(np.float32)
    ar = row[:, None] * inv_freq
    ac = col[:, None] * inv_freq
    cr, sr, cc, sc = np.cos(ar), np.sin(ar), np.cos(ac), np.sin(ac)
    z = np.zeros_like(sr)
    cos = np.concatenate([cr, cr, cc, cc], axis=1)
    sin_a = np.concatenate([-sr, z, -sc, z], axis=1)
    sin_b = np.concatenate([z, sr, z, sc], axis=1)
    return tuple(jnp.asarray(t.astype(np.float32)) for t in (cos, sin_a, sin_b))


def kernel(x_prompt, x_sample, cache_a_k, cache_a_v, cache_b_k, cache_b_v, c, c_ctx, w_mod, b_mod, ln_g, ln_b, w_up, w_down, a_w_qkv, a_w_o, a_lambda, a_subln_g, b_w_qkv, b_w_o, b_q_norm_g, b_k_norm_g, c_w_f):
    n_a = a_w_qkv.shape[0]
    n_b = b_w_qkv.shape[0]
    cond = jnp.zeros((COND_PAD, D_MODEL), F32).at[0].set(c_ctx).at[1:N_COND].set(c)
    mod = _modulation(cond, w_mod, b_mod).reshape(DEPTH, COND_PAD, N_MOD, 1, D_MODEL)
    ln_g4 = ln_g.reshape(DEPTH, 2, 1, D_MODEL)
    ln_b4 = ln_b.reshape(DEPTH, 2, 1, D_MODEL)
    rope = _rope_tables()
    w_o_bf = (_cast_bf16(a_w_o), _cast_bf16(b_w_o), _cast_bf16(c_w_f))

    ck_a = cache_a_k.reshape(DEC_BATCH, n_a, PAST_LEN * A_CHUNKS, LANE)
    cv_a = _a_v_rows(cache_a_v, DEC_BATCH, PAST_LEN)
    ck_b = cache_b_k.reshape(DEC_BATCH, n_b, PAST_LEN * B_CHUNKS, LANE)
    cv_b = cache_b_v.reshape(DEC_BATCH, n_b, PAST_LEN * B_CHUNKS, LANE)
    subln = a_subln_g.reshape(n_a, 1, 2 * A_HEAD_DIM)
    b_gains = jnp.stack([b_q_norm_g, b_k_norm_g], axis=1).reshape(n_b, 2, 1, B_HEAD_DIM)

    x = (x_prompt.reshape(M_P, D_MODEL), x_sample.reshape(M_S, D_MODEL))
    u = _embed(*x, mod)

    a_kv, b_kv = [], []
    for l in range(DEPTH):
        kind, j = l % N_MIXERS, l // N_MIXERS
        if kind == 0:
            lam_init = 0.8 - 0.6 * math.exp(-0.3 * l)
            q_p = _proj(u, a_w_qkv, j, 0, D_MODEL, 0, M_P, out_dtype=BF16, name="a_q_prompt")
            kv_p = _proj(u, a_w_qkv, j, D_MODEL, 2 * D_MODEL, 0, M_P, out_dtype=F32,
                         name="a_kv_prompt")
            qk_s = _proj(u, a_w_qkv, j, 0, 2 * D_MODEL, M_P, M_S, out_dtype=BF16, rope=rope,
                         name="a_qk_latent")
            v_s = _proj(u, a_w_qkv, j, 2 * D_MODEL, D_MODEL, M_P, M_S, out_dtype=BF16,
                        name="a_v_latent")
            if j + 1 < n_a:
                att_p = _diff_prompt(a_lambda, subln, q_p, kv_p, j, lam_init)
                a_kv.append(kv_p)
            else:
                att_p, new_a_k, new_a_v = _diff_prompt(a_lambda, subln, q_p, kv_p, j, lam_init,
                                                       prev_kv=a_kv)
            att_s = _diff_latent(a_lambda, subln, qk_s, v_s, ck_a, cv_a, j, lam_init)
            w_o = w_o_bf[0]
        elif kind == 1:
            gj = b_gains[j]
            q_p = _proj(u, b_w_qkv, j, 0, D_MODEL, 0, M_P, out_dtype=BF16, gains=gj,
                        gain_split=D_MODEL, name="b_q_prompt")
            k_p = _proj(u, b_w_qkv, j, D_MODEL, B_KV, 0, M_P, out_dtype=F32, tn=B_KV,
                        gains=gj[1:], gain_split=B_KV, name="b_k_prompt")
            v_p = _proj(u, b_w_qkv, j, D_MODEL + B_KV, B_KV, 0, M_P, out_dtype=F32, tn=B_KV,
                        name="b_v_prompt")
            qk_s = _proj(u, b_w_qkv, j, 0, D_MODEL + B_KV, M_P, M_S, out_dtype=BF16, tn=B_KV,
                         gains=gj, gain_split=D_MODEL, rope=rope, name="b_qk_latent")
            v_s = _proj(u, b_w_qkv, j, D_MODEL + B_KV, B_KV, M_P, M_S, out_dtype=BF16, tn=B_KV,
                        name="b_v_latent")
            if j + 1 < n_b:
                att_p = _gqa_prompt(q_p, k_p, v_p)
                b_kv.append((k_p, v_p))
            else:
                att_p, new_b_k, new_b_v = _gqa_prompt(q_p, k_p, v_p, prev_kv=b_kv)
            att_s = _gqa_latent(qk_s, v_s, ck_b, cv_b, j)
            w_o = w_o_bf[1]
        else:
            att_p = _dft(u, SEQ, BATCH, 0, C_GROUPS)
            att_s = _dft(u, DEC_SEQ, DEC_BATCH, M_P, 1)
            w_o = w_o_bf[2]
        x, u = _oproj(att_p, att_s, w_o, j, x, mod, ln_g4, ln_b4, l)
        x, u = _mlp(u, w_up, w_down, x, mod, ln_g4, ln_b4, l)

    y_p, y_s = x, u
    return (y_p.reshape(BATCH, SEQ, D_MODEL), y_s.reshape(DEC_BATCH, DEC_SEQ, D_MODEL),
            new_a_k.reshape(BATCH, n_a, SEQ, A_HEADS, 2, A_HEAD_DIM),
            _a_v_from_rows(new_a_v, BATCH, SEQ),
            new_b_k.reshape(BATCH, n_b, SEQ, B_KV_HEADS, B_HEAD_DIM),
            new_b_v.reshape(BATCH, n_b, SEQ, B_KV_HEADS, B_HEAD_DIM))
```

```python
import functools
import math

import jax
import jax.numpy as jnp
import numpy as np
from jax import lax
from jax.experimental import pallas as pl
from jax.experimental.pallas import tpu as pltpu

D_MODEL = 2048
BATCH = 16
SEQ = 256
DEPTH = 4
DEC_BATCH = 2
DEC_SEQ = 1024
PAST_LEN = 512
GRID_W = 64
N_MIXERS = 3
A_HEAD_DIM = 128
A_HEADS = D_MODEL // (2 * A_HEAD_DIM)
B_HEAD_DIM = 128
B_HEADS = D_MODEL // B_HEAD_DIM
B_KV_HEADS = B_HEADS // 4
B_GROUP = B_HEADS // B_KV_HEADS
B_KV = B_KV_HEADS * B_HEAD_DIM
C_GROUPS = 4
C_GROUP_DIM = D_MODEL // C_GROUPS
D_FF = 4 * D_MODEL
ROPE_BASE = 10000.0
ALPHA = (2 * DEPTH) ** 0.25
LN_EPS = 1e-5
RMS_EPS = 1e-6
N_MOD = 6
LOG2E = 1.4426950408889634

M_P = BATCH * SEQ
M_S = DEC_BATCH * DEC_SEQ
M_ALL = M_P + M_S
N_COND = 1 + DEC_BATCH
COND_PAD = 8

LANE = 128
VMEM_LIMIT = 58 * 1024 * 1024
ROW_TILE = 512

BF16 = jnp.bfloat16
F32 = jnp.float32


def _params(*sem, vmem=VMEM_LIMIT):
    return pltpu.CompilerParams(dimension_semantics=sem, vmem_limit_bytes=vmem)


def _cond_row(i, tm):
    start = i * tm
    return jnp.where(start < M_P, 0, 1 + (start - M_P) // DEC_SEQ)


def _mod_spec(l, which, tm):
    def idx(*g):
        return (l, _cond_row(g[0], tm), which, 0, 0)
    return pl.BlockSpec((None, None, None, 1, D_MODEL), idx)


def _vec_spec(l, which):
    return pl.BlockSpec((None, None, 1, D_MODEL), lambda *g: (l, which, 0, 0))


def _prompt_rows_spec(tm):
    last = M_P // tm - 1
    return pl.BlockSpec((tm, D_MODEL), lambda i: (jnp.minimum(i, last), 0))


def _latent_rows_spec(tm):
    first = M_P // tm
    return pl.BlockSpec((tm, D_MODEL), lambda i: (jnp.maximum(i - first, 0), 0))


def _mod_kernel(c_ref, w_ref, b_ref, o_ref):
    c = c_ref[...]
    s = c * (1.0 / (1.0 + jnp.exp(-c)))
    o_ref[...] = jnp.dot(s.astype(BF16), w_ref[...].astype(BF16),
                         preferred_element_type=F32) + b_ref[...]


def _modulation(cond, w_mod, b_mod):
    tn = 1024
    n_out = N_MOD * D_MODEL
    return pl.pallas_call(
        _mod_kernel,
        out_shape=jax.ShapeDtypeStruct((DEPTH, COND_PAD, n_out), F32),
        grid=(DEPTH, n_out // tn),
        in_specs=[
            pl.BlockSpec((COND_PAD, D_MODEL), lambda l, n: (0, 0)),
            pl.BlockSpec((None, D_MODEL, tn), lambda l, n: (l, 0, n)),
            pl.BlockSpec((None, 1, tn), lambda l, n: (l, 0, n)),
        ],
        out_specs=pl.BlockSpec((None, COND_PAD, tn), lambda l, n: (l, 0, n)),
        compiler_params=_params("arbitrary", "arbitrary"),
        name="modulation",
    )(cond, w_mod, b_mod.reshape(DEPTH, 1, n_out))


def _cast_kernel(w_ref, o_ref):
    o_ref[...] = w_ref[...].astype(BF16)


def _cast_bf16(w, rows=512):
    n_l, k, n = w.shape
    spec = pl.BlockSpec((None, rows, n), lambda l, r: (l, r, 0))
    return pl.pallas_call(
        _cast_kernel,
        out_shape=jax.ShapeDtypeStruct(w.shape, BF16),
        grid=(n_l, k // rows),
        in_specs=[spec],
        out_specs=spec,
        compiler_params=_params("arbitrary", "arbitrary"),
        name="cast_weight",
    )(w)


def _embed_kernel(xp_ref, xs_ref, sc_ref, sh_ref, u_ref):
    x = jnp.where(pl.program_id(0) * ROW_TILE < M_P, xp_ref[...], xs_ref[...])
    u_ref[...] = (x * (1.0 + sc_ref[...]) + sh_ref[...]).astype(BF16)


def _embed(xp, xs, mod):
    tm = ROW_TILE
    return pl.pallas_call(
        _embed_kernel,
        out_shape=jax.ShapeDtypeStruct((M_ALL, D_MODEL), BF16),
        grid=(M_ALL // tm,),
        in_specs=[_prompt_rows_spec(tm), _latent_rows_spec(tm),
                  _mod_spec(0, 1, tm), _mod_spec(0, 0, tm)],
        out_specs=pl.BlockSpec((tm, D_MODEL), lambda i: (i, 0)),
        compiler_params=_params("arbitrary"),
        name="embed",
    )(xp, xs, mod, mod)


def _rms_chunk(x, g):
    ms = jnp.mean(x * x, axis=-1, keepdims=True)
    return x * lax.rsqrt(ms + RMS_EPS) * g


def _rope_chunk(x, cos, sin_a, sin_b):
    return x * cos + pltpu.roll(x, LANE - 32, 1) * sin_a + pltpu.roll(x, 32, 1) * sin_b


def _proj_kernel(*refs, tn, rms, rope):
    it = iter(refs)
    u_ref, w_ref = next(it), next(it)
    g_ref = next(it) if rms else None
    tabs = (next(it), next(it), next(it)) if rope else None
    o_ref, wbf = next(it), next(it)

    @pl.when(pl.program_id(1) == 0)
    def _():
        wbf[...] = w_ref[...].astype(BF16)

    if rms and rope:
        acc = jnp.dot(u_ref[...], wbf[...], preferred_element_type=F32)
        chunks = [acc[:, c * LANE:(c + 1) * LANE] for c in range(tn // LANE)]
        g = g_ref[...]
        tab = tuple(t[...] for t in tabs)
        rinv = [lax.rsqrt(jnp.mean(x * x, axis=-1, keepdims=True) + RMS_EPS) for x in chunks]

        @pl.when(pl.program_id(1) >= 0)
        def _():
            for c, (x, r) in enumerate(zip(chunks, rinv)):
                o_ref[:, c * LANE:(c + 1) * LANE] = (_rope_chunk(x * g, *tab) * r).astype(o_ref.dtype)
    elif rms or rope:
        rows = u_ref.shape[0] // 2
        g = g_ref[...] if rms else None
        for r in range(2):
            sl = slice(r * rows, (r + 1) * rows)
            acc = jnp.dot(u_ref[sl, :], wbf[...], preferred_element_type=F32)
            for c in range(tn // LANE):
                x = acc[:, c * LANE:(c + 1) * LANE]
                if rms:
                    x = _rms_chunk(x, g)
                else:
                    x = _rope_chunk(x, *(t[sl, :] for t in tabs))
                o_ref[sl, c * LANE:(c + 1) * LANE] = x.astype(o_ref.dtype)
    else:
        acc = jnp.dot(u_ref[...], wbf[...], preferred_element_type=F32)
        o_ref[...] = acc.astype(o_ref.dtype)


def _proj(u, w, w_layer, col0, ncols, m0, m_rows, *, out_dtype, tn=1024, tm=1024,
          gains=None, gain_split=None, rope=None, name="proj"):
    k = u.shape[1]
    assert col0 % tn == 0 and ncols % tn == 0 and m0 % tm == 0 and m_rows % tm == 0
    n_t, m_t, mt0, nt0 = ncols // tn, m_rows // tm, m0 // tm, col0 // tn
    ins = [u, w]
    in_specs = [
        pl.BlockSpec((tm, k), lambda n, m: (mt0 + m, 0)),
        pl.BlockSpec((None, k, tn), lambda n, m: (w_layer, 0, nt0 + n)),
    ]
    if gains is not None:
        ins.append(gains)
        in_specs.append(pl.BlockSpec(
            (None, 1, LANE), lambda n, m: (jnp.where(n * tn < gain_split, 0, 1), 0, 0)))
    if rope is not None:
        assert m0 >= M_P
        per = DEC_SEQ // tm
        for t in rope:
            ins.append(t)
            in_specs.append(pl.BlockSpec((tm, LANE), lambda n, m: ((mt0 + m) % per, 0)))
    kern = functools.partial(_proj_kernel, tn=tn, rms=gains is not None, rope=rope is not None)
    return pl.pallas_call(
        kern,
        out_shape=jax.ShapeDtypeStruct((m_rows, ncols), out_dtype),
        grid=(n_t, m_t),
        in_specs=in_specs,
        out_specs=pl.BlockSpec((tm, tn), lambda n, m: (m, n)),
        scratch_shapes=[pltpu.VMEM((k, tn), BF16)],
        compiler_params=_params("arbitrary", "arbitrary"),
        name=name,
    )(*ins)


def _qkt(q, k):
    return lax.dot_general(q, k, (((1,), (1,)), ((), ())), preferred_element_type=F32)


def _softmax_terms(s, scale):
    m = s.max(axis=-1, keepdims=True)
    e = jnp.exp2((s - m) * (scale * LOG2E))
    return e, 1.0 / e.sum(axis=-1, keepdims=True)


def _diff_lambda(lp_ref, lam_init):
    lp = lp_ref[...]
    s1 = jnp.sum(lp[0:1] * lp[1:2], axis=-1, keepdims=True)
    s2 = jnp.sum(lp[2:3] * lp[3:4], axis=-1, keepdims=True)
    return jnp.exp(s1) - jnp.exp(s2) + lam_init


def _diff_weights(s, lam):
    half = s.shape[0] // 2
    e, r = _softmax_terms(s, A_HEAD_DIM ** -0.5)
    return (e[:half] * r[:half] - e[half:] * (lam * r[half:])).astype(BF16)


def _sub_norm(o, gain):
    ms = jnp.mean(o * o, axis=-1, keepdims=True)
    return (o * lax.rsqrt(ms + RMS_EPS) * gain).astype(BF16)


A_CHUNKS = D_MODEL // LANE
B_CHUNKS = B_KV // LANE


def _a_v_rows(v, batch, seq):
    n_l = v.shape[1]
    v = v.reshape(batch, n_l, seq, A_HEADS, 2, LANE).transpose(0, 1, 2, 4, 3, 5)
    return v.reshape(batch, n_l, seq * A_CHUNKS, LANE)


def _a_v_from_rows(r, batch, seq):
    n_l = r.shape[1]
    v = r.reshape(batch, n_l, seq, 2, A_HEADS, LANE).transpose(0, 1, 2, 4, 3, 5)
    return v.reshape(batch, n_l, seq, A_HEADS, 2 * LANE)


def _diff_prompt_kernel(*refs, lam_init, n_prev, emit):
    lp_ref, g_ref, q_ref, k_ref, v_ref = refs[:5]
    prev = refs[5:5 + n_prev]
    o_ref = refs[5 + n_prev]
    s_scr, a_scr, o_scr = refs[-3:]
    lam = _diff_lambda(lp_ref, lam_init)
    gain = g_ref[...] * (1.0 - lam_init)
    w = 2 * A_HEAD_DIM
    for h in range(A_HEADS):
        for p in range(2):
            cols = slice(h * w + p * A_HEAD_DIM, h * w + (p + 1) * A_HEAD_DIM)
            s_scr[(p * A_HEADS + h) * SEQ:(p * A_HEADS + h + 1) * SEQ, :] = _qkt(
                q_ref[:, cols], k_ref[:, cols].astype(BF16))

    @pl.when(pl.program_id(0) >= 0)
    def _():
        a_scr[...] = _diff_weights(s_scr[...], lam)
        for h in range(A_HEADS):
            rows = slice(h * SEQ, (h + 1) * SEQ)
            o_scr[rows, :] = jnp.dot(a_scr[rows, :], v_ref[:, h * w:(h + 1) * w].astype(BF16),
                                     preferred_element_type=F32)
        y = _sub_norm(o_scr[...], gain)
        for h in range(A_HEADS):
            o_ref[:, h * w:(h + 1) * w] = y[h * SEQ:(h + 1) * SEQ, :]

    if emit:
        nk_ref, nv_ref = refs[6 + n_prev], refs[7 + n_prev]
        layers = [(p, 0, p, D_MODEL) for p in prev] + [(k_ref, 0, v_ref, 0)]
        for jj, (kr, k0, vr, v0) in enumerate(layers):
            nk_ref[jj] = kr[:, k0:k0 + D_MODEL].reshape(SEQ, A_CHUNKS, LANE)
            for t in range(2):
                halves = [vr[:, v0 + (2 * h + t) * LANE:v0 + (2 * h + t + 1) * LANE]
                          for h in range(A_HEADS)]
                nv_ref[jj, :, t * A_HEADS:(t + 1) * A_HEADS, :] = jnp.concatenate(
                    halves, axis=1).reshape(SEQ, A_HEADS, LANE)


def _diff_prompt(lp, g, q, kv, j, lam_init, prev_kv=None):
    emit = prev_kv is not None
    prev_kv = list(prev_kv or [])
    n_layers = len(prev_kv) + 1
    att_shape = jax.ShapeDtypeStruct((M_P, D_MODEL), BF16)
    att_spec = pl.BlockSpec((SEQ, D_MODEL), lambda b: (b, 0))
    if emit:
        new_shape = jax.ShapeDtypeStruct((BATCH, n_layers, SEQ, A_CHUNKS, LANE), F32)
        new_spec = pl.BlockSpec((None, n_layers, SEQ, A_CHUNKS, LANE), lambda b: (b, 0, 0, 0, 0))
        out_shape, out_specs = (att_shape, new_shape, new_shape), (att_spec, new_spec, new_spec)
    else:
        out_shape, out_specs = att_shape, att_spec
    return pl.pallas_call(
        functools.partial(_diff_prompt_kernel, lam_init=lam_init, n_prev=len(prev_kv), emit=emit),
        out_shape=out_shape,
        grid=(BATCH,),
        in_specs=[
            pl.BlockSpec((None, 4, A_HEAD_DIM), lambda b: (j, 0, 0)),
            pl.BlockSpec((None, 1, 2 * A_HEAD_DIM), lambda b: (j, 0, 0)),
            pl.BlockSpec((SEQ, D_MODEL), lambda b: (b, 0)),
            pl.BlockSpec((SEQ, D_MODEL), lambda b: (b, 0)),
            pl.BlockSpec((SEQ, D_MODEL), lambda b: (b, 1)),
        ] + [pl.BlockSpec((SEQ, 2 * D_MODEL), lambda b: (b, 0)) for _ in prev_kv],
        out_specs=out_specs,
        scratch_shapes=[pltpu.VMEM((2 * A_HEADS * SEQ, SEQ), F32),
                        pltpu.VMEM((A_HEADS * SEQ, SEQ), BF16),
                        pltpu.VMEM((A_HEADS * SEQ, 2 * A_HEAD_DIM), F32)],
        compiler_params=_params("arbitrary"),
        name="diff_attn_prompt",
    )(lp, g, q, kv, kv, *prev_kv)


def _diff_latent_kernel(lp_ref, g_ref, q_ref, k_ref, v_ref, ck_ref, cv_ref, o_ref,
                        s_scr, a_scr, *, lam_init):
    lam = _diff_lambda(lp_ref, lam_init)
    gain = g_ref[...] * (1.0 - lam_init)
    h = pl.program_id(1)
    tq, d = q_ref.shape[0], A_HEAD_DIM

    def cached(ref, r0, r1):
        halves = [ref[pl.ds(r, PAST_LEN, stride=A_CHUNKS), :] for r in (r0, r1)]
        return jnp.concatenate(halves, axis=1).astype(BF16)

    ck = cached(ck_ref, 2 * h, 2 * h + 1)
    for p in range(2):
        q = q_ref[:, p * d:(p + 1) * d]
        s_scr[p * tq:(p + 1) * tq, :PAST_LEN] = _qkt(q, ck[:, p * d:(p + 1) * d])
        s_scr[p * tq:(p + 1) * tq, PAST_LEN:] = _qkt(q, k_ref[:, p * d:(p + 1) * d])

    @pl.when(h >= 0)
    def _():
        a_scr[...] = _diff_weights(s_scr[...], lam)
        cv = cached(cv_ref, h, A_HEADS + h)
        o = (jnp.dot(a_scr[:, :PAST_LEN], cv, preferred_element_type=F32)
             + jnp.dot(a_scr[:, PAST_LEN:], v_ref[...], preferred_element_type=F32))
        o_ref[...] = _sub_norm(o, gain)


def _diff_latent(lp, g, qk, v, ck, cv, j, lam_init, tq=1024):
    w = 2 * A_HEAD_DIM
    per = DEC_SEQ // tq
    ctx_spec = pl.BlockSpec((None, None, PAST_LEN * A_CHUNKS, LANE), lambda b, h, t: (b, j, 0, 0))
    return pl.pallas_call(
        functools.partial(_diff_latent_kernel, lam_init=lam_init),
        out_shape=jax.ShapeDtypeStruct((M_S, D_MODEL), BF16),
        grid=(DEC_BATCH, A_HEADS, per),
        in_specs=[
            pl.BlockSpec((None, 4, A_HEAD_DIM), lambda b, h, t: (j, 0, 0)),
            pl.BlockSpec((None, 1, w), lambda b, h, t: (j, 0, 0)),
            pl.BlockSpec((tq, w), lambda b, h, t: (b * per + t, h)),
            pl.BlockSpec((DEC_SEQ, w), lambda b, h, t: (b, A_HEADS + h)),
            pl.BlockSpec((DEC_SEQ, w), lambda b, h, t: (b, h)),
            ctx_spec, ctx_spec,
        ],
        out_specs=pl.BlockSpec((tq, w), lambda b, h, t: (b * per + t, h)),
        scratch_shapes=[pltpu.VMEM((2 * tq, PAST_LEN + DEC_SEQ), F32),
                        pltpu.VMEM((tq, PAST_LEN + DEC_SEQ), BF16)],
        compiler_params=_params("arbitrary", "arbitrary", "arbitrary"),
        name="diff_attn_latent",
    )(lp, g, qk, qk, v, ck, cv)


def _gqa_prompt_kernel(*refs, n_prev, emit):
    q_ref, k_ref, v_ref = refs[:3]
    prev = refs[3:3 + 2 * n_prev]
    o_ref = refs[3 + 2 * n_prev]
    s_scr, a_scr, o_scr = refs[-3:]
    d = B_HEAD_DIM
    for n in range(B_KV_HEADS):
        k = k_ref[:, n * d:(n + 1) * d].astype(BF16)
        for g in range(B_GROUP):
            hd = n * B_GROUP + g
            s_scr[hd * SEQ:(hd + 1) * SEQ, :] = _qkt(q_ref[:, hd * d:(hd + 1) * d], k)

    @pl.when(pl.program_id(0) >= 0)
    def _():
        e, r = _softmax_terms(s_scr[...], B_HEAD_DIM ** -0.5)
        a_scr[...] = (e * r).astype(BF16)
        group_rows = B_GROUP * SEQ
        for n in range(B_KV_HEADS):
            rows = slice(n * group_rows, (n + 1) * group_rows)
            o_scr[rows, :] = jnp.dot(a_scr[rows, :], v_ref[:, n * d:(n + 1) * d].astype(BF16),
                                     preferred_element_type=F32)
        for hd in range(B_HEADS):
            o_ref[:, hd * d:(hd + 1) * d] = o_scr[hd * SEQ:(hd + 1) * SEQ, :].astype(BF16)

    if emit:
        nk_ref, nv_ref = refs[4 + 2 * n_prev], refs[5 + 2 * n_prev]
        layers = [(prev[2 * i], prev[2 * i + 1]) for i in range(n_prev)] + [(k_ref, v_ref)]
        for jj, (kr, vr) in enumerate(layers):
            for n in range(B_CHUNKS):
                rows = pl.ds(n, SEQ, stride=B_CHUNKS)
                nk_ref[jj, rows, :] = kr[:, n * LANE:(n + 1) * LANE]
                nv_ref[jj, rows, :] = vr[:, n * LANE:(n + 1) * LANE]


def _gqa_prompt(q, k, v, prev_kv=None):
    emit = prev_kv is not None
    prev = [a for pair in (prev_kv or []) for a in pair]
    n_layers = len(prev) // 2 + 1
    kv_spec = pl.BlockSpec((SEQ, B_KV), lambda b: (b, 0))
    att_shape = jax.ShapeDtypeStruct((M_P, D_MODEL), BF16)
    att_spec = pl.BlockSpec((SEQ, D_MODEL), lambda b: (b, 0))
    if emit:
        new_shape = jax.ShapeDtypeStruct((BATCH, n_layers, SEQ * B_CHUNKS, LANE), F32)
        new_spec = pl.BlockSpec((None, n_layers, SEQ * B_CHUNKS, LANE), lambda b: (b, 0, 0, 0))
        out_shape, out_specs = (att_shape, new_shape, new_shape), (att_spec, new_spec, new_spec)
    else:
        out_shape, out_specs = att_shape, att_spec
    return pl.pallas_call(
        functools.partial(_gqa_prompt_kernel, n_prev=len(prev) // 2, emit=emit),
        out_shape=out_shape,
        grid=(BATCH,),
        in_specs=[att_spec, kv_spec, kv_spec] + [kv_spec for _ in prev],
        out_specs=out_specs,
        scratch_shapes=[pltpu.VMEM((B_HEADS * SEQ, SEQ), F32),
                        pltpu.VMEM((B_HEADS * SEQ, SEQ), BF16),
                        pltpu.VMEM((B_HEADS * SEQ, B_HEAD_DIM), F32)],
        compiler_params=_params("arbitrary"),
        name="gqa_attn_prompt",
    )(q, k, v, *prev)


def _gqa_latent_kernel(q_ref, k_ref, v_ref, ck_ref, cv_ref, o_ref, s_scr, a_scr):
    d, tq = B_HEAD_DIM, q_ref.shape[0]
    n = pl.program_id(1)
    cached = pl.ds(n, PAST_LEN, stride=B_CHUNKS)
    ck = ck_ref[cached, :].astype(BF16)
    for g in range(B_GROUP):
        q = q_ref[:, g * d:(g + 1) * d]
        s_scr[g * tq:(g + 1) * tq, :PAST_LEN] = _qkt(q, ck)
        s_scr[g * tq:(g + 1) * tq, PAST_LEN:] = _qkt(q, k_ref[...])

    @pl.when(n >= 0)
    def _():
        e, r = _softmax_terms(s_scr[...], B_HEAD_DIM ** -0.5)
        a_scr[...] = (e * r).astype(BF16)
        o = (jnp.dot(a_scr[:, :PAST_LEN], cv_ref[cached, :].astype(BF16),
                     preferred_element_type=F32)
             + jnp.dot(a_scr[:, PAST_LEN:], v_ref[...], preferred_element_type=F32))
        for g in range(B_GROUP):
            o_ref[:, g * d:(g + 1) * d] = o[g * tq:(g + 1) * tq, :].astype(BF16)


def _gqa_latent(qk, v, ck, cv, j, tq=1024):
    d = B_HEAD_DIM
    gw = B_GROUP * d
    per = DEC_SEQ // tq
    ctx_spec = pl.BlockSpec((None, None, PAST_LEN * B_CHUNKS, LANE), lambda b, n, t: (b, j, 0, 0))
    return pl.pallas_call(
        _gqa_latent_kernel,
        out_shape=jax.ShapeDtypeStruct((M_S, D_MODEL), BF16),
        grid=(DEC_BATCH, B_KV_HEADS, per),
        in_specs=[
            pl.BlockSpec((tq, gw), lambda b, n, t: (b * per + t, n)),
            pl.BlockSpec((DEC_SEQ, d), lambda b, n, t: (b, D_MODEL // d + n)),
            pl.BlockSpec((DEC_SEQ, d), lambda b, n, t: (b, n)),
            ctx_spec, ctx_spec,
        ],
        out_specs=pl.BlockSpec((tq, gw), lambda b, n, t: (b * per + t, n)),
        scratch_shapes=[pltpu.VMEM((B_GROUP * tq, PAST_LEN + DEC_SEQ), F32),
                        pltpu.VMEM((B_GROUP * tq, PAST_LEN + DEC_SEQ), BF16)],
        compiler_params=_params("arbitrary", "arbitrary", "arbitrary"),
        name="gqa_attn_latent",
    )(qk, qk, v, ck, cv)


def _dft_tables(s):
    def cs(n):
        idx = (np.arange(n)[:, None] * np.arange(n)[None, :]) % n
        ang = 2.0 * np.pi * idx / n
        return np.cos(ang), np.sin(ang)
    cc, sc = cs(C_GROUP_DIM)
    c_s, s_s = cs(s)
    t_chan = np.concatenate([cc, sc], axis=1).astype(np.float32)
    t_pos = np.concatenate([c_s, -s_s], axis=1).astype(np.float32)
    return jnp.asarray(t_chan).astype(BF16), jnp.asarray(t_pos).astype(BF16)


def _dft_group(x, t_chan, t_pos, norm):
    xcs = jnp.dot(x, t_chan, preferred_element_type=F32).astype(BF16)
    stacked = jnp.concatenate([xcs[:, :C_GROUP_DIM], xcs[:, C_GROUP_DIM:]], axis=0)
    y = jnp.dot(t_pos, stacked, preferred_element_type=F32)
    return (y * norm).astype(BF16)


def _dft_kernel(u_ref, tc_ref, tp_ref, o_ref, *, groups, norm):
    tc, tp = tc_ref[...], tp_ref[...]
    for g in range(groups):
        sl = slice(g * C_GROUP_DIM, (g + 1) * C_GROUP_DIM)
        o_ref[:, sl] = _dft_group(u_ref[:, sl], tc, tp, norm)


def _dft(u, s, batch, row0, groups_per_step):
    t_chan, t_pos = _dft_tables(s)
    gw = groups_per_step * C_GROUP_DIM
    rb0 = row0 // s
    return pl.pallas_call(
        functools.partial(_dft_kernel, groups=groups_per_step,
                          norm=1.0 / math.sqrt(s * C_GROUP_DIM)),
        out_shape=jax.ShapeDtypeStruct((batch * s, D_MODEL), BF16),
        grid=(batch, C_GROUPS // groups_per_step),
        in_specs=[
            pl.BlockSpec((s, gw), lambda b, g: (rb0 + b, g)),
            pl.BlockSpec(t_chan.shape, lambda b, g: (0, 0)),
            pl.BlockSpec(t_pos.shape, lambda b, g: (0, 0)),
        ],
        out_specs=pl.BlockSpec((s, gw), lambda b, g: (b, g)),
        compiler_params=_params("arbitrary", "arbitrary"),
        name="dft",
    )(u, t_chan, t_pos)


ROW_GROUPS = 2


def _post_norm(y, g, b):
    mu = jnp.mean(y, axis=-1, keepdims=True)
    yc = y - mu
    var = jnp.mean(yc * yc, axis=-1, keepdims=True)
    return yc * lax.rsqrt(var + LN_EPS) * g + b


def _oproj_kernel(*refs, split_x):
    ap_ref, as_ref, w_ref = refs[:3]
    x_refs = refs[3:5] if split_x else refs[3:4]
    gate_ref, g_ref, b_ref, sc_ref, sh_ref, x1_ref, u_ref = refs[3 + len(x_refs):]
    is_prompt = pl.program_id(0) * ROW_TILE < M_P
    gate, g, b = gate_ref[...], g_ref[...], b_ref[...]
    sc1, sh = 1.0 + sc_ref[...], sh_ref[...]
    rows = ROW_TILE // ROW_GROUPS
    for r in range(ROW_GROUPS):
        sl = slice(r * rows, (r + 1) * rows)
        a = jnp.where(is_prompt, ap_ref[sl, :], as_ref[sl, :])
        mixed = jnp.dot(a, w_ref[...], preferred_element_type=F32)
        if split_x:
            x = jnp.where(is_prompt, x_refs[0][sl, :], x_refs[1][sl, :])
        else:
            x = x_refs[0][sl, :]
        x1 = _post_norm(ALPHA * x + gate * mixed, g, b)
        x1_ref[sl, :] = x1
        u_ref[sl, :] = (x1 * sc1 + sh).astype(BF16)


def _oproj(a_p, a_s, w_bf, w_layer, x, mod, ln_g, ln_b, l):
    tm = ROW_TILE
    row = pl.BlockSpec((tm, D_MODEL), lambda i: (i, 0))
    split_x = isinstance(x, tuple)
    xs = list(x) if split_x else [x]
    x_specs = [_prompt_rows_spec(tm), _latent_rows_spec(tm)] if split_x else [row]
    return pl.pallas_call(
        functools.partial(_oproj_kernel, split_x=split_x),
        out_shape=(jax.ShapeDtypeStruct((M_ALL, D_MODEL), F32),
                   jax.ShapeDtypeStruct((M_ALL, D_MODEL), BF16)),
        grid=(M_ALL // tm,),
        in_specs=[
            _prompt_rows_spec(tm), _latent_rows_spec(tm),
            pl.BlockSpec((None, D_MODEL, D_MODEL), lambda i: (w_layer, 0, 0),
                         pipeline_mode=pl.Buffered(1)),
            *x_specs,
            _mod_spec(l, 2, tm),
            _vec_spec(l, 0), _vec_spec(l, 0),
            _mod_spec(l, 4, tm),
            _mod_spec(l, 3, tm),
        ],
        out_specs=(row, row),
        compiler_params=_params("arbitrary"),
        name="oproj_ln",
    )(a_p, a_s, w_bf, *xs, mod, ln_g, ln_b, mod, mod)


MLP_FF_PIECE = 512


def _mlp_kernel(*refs, n_tiles, n_ff, last):
    u_ref, wu_ref, wd_ref, x_ref, gate_ref, g_ref, b_ref = refs[:7]
    accs = refs[-2:]
    i, f = pl.program_id(0), pl.program_id(1)
    tm = u_ref.shape[0]
    close_rows = tm // n_ff

    for parity in (0, 1):
        @pl.when(jnp.logical_and(f == 0, i == parity))
        def _():
            accs[parity][...] = jnp.zeros_like(accs[parity])

    def matmuls(acc):
        tf = wu_ref.shape[1]
        for c in range(tf // MLP_FF_PIECE):
            cols = slice(c * MLP_FF_PIECE, (c + 1) * MLP_FF_PIECE)
            h = jnp.dot(u_ref[...], wu_ref[:, cols].astype(BF16), preferred_element_type=F32)
            h = jnp.maximum(h, 0.0)
            h = (h * h).astype(BF16)
            acc[...] += jnp.dot(h, wd_ref[cols, :].astype(BF16), preferred_element_type=F32)

    def close(prev):
        rows = pl.ds(pl.multiple_of(f * close_rows, close_rows), close_rows)
        y = _post_norm(ALPHA * x_ref[...] + gate_ref[...] * prev[rows, :], g_ref[...], b_ref[...])
        prev[rows, :] = jnp.zeros((close_rows, prev.shape[1]), F32)
        if last:
            yp_ref, ys_ref = refs[7:9]
            prev_is_prompt = (i - 1) * tm < M_P

            @pl.when(prev_is_prompt)
            def _():
                yp_ref[...] = y

            @pl.when(jnp.logical_not(prev_is_prompt))
            def _():
                ys_ref[...] = y
        else:
            sc_ref, sh_ref, x2_ref, un_ref = refs[7:11]
            x2_ref[...] = y
            un_ref[...] = (y * (1.0 + sc_ref[...]) + sh_ref[...]).astype(BF16)

    @pl.when(i == 0)
    def _():
        matmuls(accs[0])

    middle = jnp.logical_and(i > 0, i < n_tiles)
    for parity in (0, 1):
        @pl.when(jnp.logical_and(middle, i % 2 == parity))
        def _():
            matmuls(accs[parity])
            close(accs[1 - parity])

    @pl.when(i == n_tiles)
    def _():
        close(accs[(n_tiles - 1) % 2])


def _mlp(u, w_up, w_down, x, mod, ln_g, ln_b, l, tm=1024, tf=512):
    last = l + 1 == DEPTH
    n_tiles, n_ff = M_ALL // tm, D_FF // tf
    close_rows = tm // n_ff

    def close_block(i, f):
        return jnp.where(i == 0, 0, (i - 1) * n_ff + f)

    def mod_prev(layer, which):
        def idx(i, f):
            return (layer, _cond_row(jnp.maximum(i - 1, 0), tm), which, 0, 0)
        return pl.BlockSpec((None, None, None, 1, D_MODEL), idx)

    def ff_block(i, f):
        return jnp.where(i < n_tiles, f, n_ff - 1)

    row = pl.BlockSpec((close_rows, D_MODEL), lambda i, f: (close_block(i, f), 0))
    ins = [u, w_up, w_down, x, mod, ln_g, ln_b]
    in_specs = [
        pl.BlockSpec((tm, D_MODEL), lambda i, f: (jnp.minimum(i, n_tiles - 1), 0)),
        pl.BlockSpec((None, D_MODEL, tf), lambda i, f: (l, 0, ff_block(i, f))),
        pl.BlockSpec((None, tf, D_MODEL), lambda i, f: (l, ff_block(i, f), 0)),
        row, mod_prev(l, 5), _vec_spec(l, 1), _vec_spec(l, 1),
    ]
    if last:
        n_p = M_P // close_rows
        out_shape = (jax.ShapeDtypeStruct((M_P, D_MODEL), F32),
                     jax.ShapeDtypeStruct((M_S, D_MODEL), F32))
        out_specs = (
            pl.BlockSpec((close_rows, D_MODEL),
                         lambda i, f: (jnp.minimum(close_block(i, f), n_p - 1), 0)),
            pl.BlockSpec((close_rows, D_MODEL),
                         lambda i, f: (jnp.maximum(close_block(i, f) - n_p, 0), 0)),
        )
    else:
        ins += [mod, mod]
        in_specs += [mod_prev(l + 1, 1), mod_prev(l + 1, 0)]
        out_shape = (jax.ShapeDtypeStruct((M_ALL, D_MODEL), F32),
                     jax.ShapeDtypeStruct((M_ALL, D_MODEL), BF16))
        out_specs = (row, row)
    return pl.pallas_call(
        functools.partial(_mlp_kernel, n_tiles=n_tiles, n_ff=n_ff, last=last),
        out_shape=out_shape,
        grid=(n_tiles + 1, n_ff),
        in_specs=in_specs,
        out_specs=out_specs,
        scratch_shapes=[pltpu.VMEM((tm, D_MODEL), F32), pltpu.VMEM((tm, D_MODEL), F32)],
        compiler_params=_params("arbitrary", "arbitrary"),
        name="mlp",
    )(*ins)


def _rope_tables():
    n_freq = A_HEAD_DIM // 4
    pos = np.arange(DEC_SEQ)
    row = (pos // GRID_W).astype(np.float32)
    col = (pos % GRID_W).astype(np.float32)
    inv_freq = (ROPE_BASE ** (-np.arange(n_freq, dtype=np.float32) / n_freq)).astype(np.float32)
    ar = row[:, None] * inv_freq
    ac = col[:, None] * inv_freq
    cr, sr, cc, sc = np.cos(ar), np.sin(ar), np.cos(ac), np.sin(ac)
    z = np.zeros_like(sr)
    cos = np.concatenate([cr, cr, cc, cc], axis=1)
    sin_a = np.concatenate([-sr, z, -sc, z], axis=1)
    sin_b = np.concatenate([z, sr, z, sc], axis=1)
    return tuple(jnp.asarray(t.astype(np.float32)) for t in (cos, sin_a, sin_b))


def kernel(x_prompt, x_sample, cache_a_k, cache_a_v, cache_b_k, cache_b_v, c, c_ctx, w_mod, b_mod, ln_g, ln_b, w_up, w_down, a_w_qkv, a_w_o, a_lambda, a_subln_g, b_w_qkv, b_w_o, b_q_norm_g, b_k_norm_g, c_w_f):
    n_a = a_w_qkv.shape[0]
    n_b = b_w_qkv.shape[0]
    cond = jnp.zeros((COND_PAD, D_MODEL), F32).at[0].set(c_ctx).at[1:N_COND].set(c)
    mod = _modulation(cond, w_mod, b_mod).reshape(DEPTH, COND_PAD, N_MOD, 1, D_MODEL)
    ln_g4 = ln_g.reshape(DEPTH, 2, 1, D_MODEL)
    ln_b4 = ln_b.reshape(DEPTH, 2, 1, D_MODEL)
    rope = _rope_tables()
    w_o_bf = (_cast_bf16(a_w_o), _cast_bf16(b_w_o), _cast_bf16(c_w_f))

    ck_a = cache_a_k.reshape(DEC_BATCH, n_a, PAST_LEN * A_CHUNKS, LANE)
    cv_a = _a_v_rows(cache_a_v, DEC_BATCH, PAST_LEN)
    ck_b = cache_b_k.reshape(DEC_BATCH, n_b, PAST_LEN * B_CHUNKS, LANE)
    cv_b = cache_b_v.reshape(DEC_BATCH, n_b, PAST_LEN * B_CHUNKS, LANE)
    subln = a_subln_g.reshape(n_a, 1, 2 * A_HEAD_DIM)
    b_gains = jnp.stack([b_q_norm_g, b_k_norm_g], axis=1).reshape(n_b, 2, 1, B_HEAD_DIM)

    x = (x_prompt.reshape(M_P, D_MODEL), x_sample.reshape(M_S, D_MODEL))
    u = _embed(*x, mod)

    a_kv, b_kv = [], []
    for l in range(DEPTH):
        kind, j = l % N_MIXERS, l // N_MIXERS
        if kind == 0:
            lam_init = 0.8 - 0.6 * math.exp(-0.3 * l)
            q_p = _proj(u, a_w_qkv, j, 0, D_MODEL, 0, M_P, out_dtype=BF16, name="a_q_prompt")
            kv_p = _proj(u, a_w_qkv, j, D_MODEL, 2 * D_MODEL, 0, M_P, out_dtype=F32,
                         name="a_kv_prompt")
            qk_s = _proj(u, a_w_qkv, j, 0, 2 * D_MODEL, M_P, M_S, out_dtype=BF16, rope=rope,
                         name="a_qk_latent")
            v_s = _proj(u, a_w_qkv, j, 2 * D_MODEL, D_MODEL, M_P, M_S, out_dtype=BF16,
                        name="a_v_latent")
            if j + 1 < n_a:
                att_p = _diff_prompt(a_lambda, subln, q_p, kv_p, j, lam_init)
                a_kv.append(kv_p)
            else:
                att_p, new_a_k, new_a_v = _diff_prompt(a_lambda, subln, q_p, kv_p, j, lam_init,
                                                       prev_kv=a_kv)
            att_s = _diff_latent(a_lambda, subln, qk_s, v_s, ck_a, cv_a, j, lam_init)
            w_o = w_o_bf[0]
        elif kind == 1:
            gj = b_gains[j]
            q_p = _proj(u, b_w_qkv, j, 0, D_MODEL, 0, M_P, out_dtype=BF16, gains=gj,
                        gain_split=D_MODEL, name="b_q_prompt")
            k_p = _proj(u, b_w_qkv, j, D_MODEL, B_KV, 0, M_P, out_dtype=F32, tn=B_KV,
                        gains=gj[1:], gain_split=B_KV, name="b_k_prompt")
            v_p = _proj(u, b_w_qkv, j, D_MODEL + B_KV, B_KV, 0, M_P, out_dtype=F32, tn=B_KV,
                        name="b_v_prompt")
            qk_s = _proj(u, b_w_qkv, j, 0, D_MODEL + B_KV, M_P, M_S, out_dtype=BF16, tn=B_KV,
                         gains=gj, gain_split=D_MODEL, rope=rope, name="b_qk_latent")
            v_s = _proj(u, b_w_qkv, j, D_MODEL + B_KV, B_KV, M_P, M_S, out_dtype=BF16, tn=B_KV,
                        name="b_v_latent")
            if j + 1 < n_b:
                att_p = _gqa_prompt(q_p, k_p, v_p)
                b_kv.append((k_p, v_p))
            else:
                att_p, new_b_k, new_b_v = _gqa_prompt(q_p, k_p, v_p, prev_kv=b_kv)
            att_s = _gqa_latent(qk_s, v_s, ck_b, cv_b, j)
            w_o = w_o_bf[1]
        else:
            att_p = _dft(u, SEQ, BATCH, 0, C_GROUPS)
            att_s = _dft(u, DEC_SEQ, DEC_BATCH, M_P, 1)
            w_o = w_o_bf[2]
        x, u = _oproj(att_p, att_s, w_o, j, x, mod, ln_g4, ln_b4, l)
        x, u = _mlp(u, w_up, w_down, x, mod, ln_g4, ln_b4, l)

    y_p, y_s = x, u
    return (y_p.reshape(BATCH, SEQ, D_MODEL), y_s.reshape(DEC_BATCH, DEC_SEQ, D_MODEL),
            new_a_k.reshape(BATCH, n_a, SEQ, A_HEADS, 2, A_HEAD_DIM),
            _a_v_from_rows(new_a_v, BATCH, SEQ),
            new_b_k.reshape(BATCH, n_b, SEQ, B_KV_HEADS, B_HEAD_DIM),
            new_b_v.reshape(BATCH, n_b, SEQ, B_KV_HEADS, B_HEAD_DIM))
```

```python
import functools
import math

import jax
import jax.numpy as jnp
import numpy as np
from jax import lax
from jax.experimental import pallas as pl
from jax.experimental.pallas import tpu as pltpu

D_MODEL = 2048
BATCH = 16
SEQ = 256
DEPTH = 4
DEC_BATCH = 2
DEC_SEQ = 1024
PAST_LEN = 512
GRID_W = 64
N_MIXERS = 3
A_HEAD_DIM = 128
A_HEADS = D_MODEL // (2 * A_HEAD_DIM)
B_HEAD_DIM = 128
B_HEADS = D_MODEL // B_HEAD_DIM
B_KV_HEADS = B_HEADS // 4
B_GROUP = B_HEADS // B_KV_HEADS
B_KV = B_KV_HEADS * B_HEAD_DIM
C_GROUPS = 4
C_GROUP_DIM = D_MODEL // C_GROUPS
D_FF = 4 * D_MODEL
ROPE_BASE = 10000.0
ALPHA = (2 * DEPTH) ** 0.25
LN_EPS = 1e-5
RMS_EPS = 1e-6
N_MOD = 6
LOG2E = 1.4426950408889634

M_P = BATCH * SEQ
M_S = DEC_BATCH * DEC_SEQ
M_ALL = M_P + M_S
N_COND = 1 + DEC_BATCH
COND_PAD = 8

LANE = 128
VMEM_LIMIT = 58 * 1024 * 1024
ROW_TILE = 512

BF16 = jnp.bfloat16
F32 = jnp.float32


def _params(*sem, vmem=VMEM_LIMIT):
    return pltpu.CompilerParams(dimension_semantics=sem, vmem_limit_bytes=vmem)


def _cond_row(i, tm):
    start = i * tm
    return jnp.where(start < M_P, 0, 1 + (start - M_P) // DEC_SEQ)


def _mod_spec(l, which, tm):
    def idx(*g):
        return (l, _cond_row(g[0], tm), which, 0, 0)
    return pl.BlockSpec((None, None, None, 1, D_MODEL), idx)


def _vec_spec(l, which):
    return pl.BlockSpec((None, None, 1, D_MODEL), lambda *g: (l, which, 0, 0))


def _prompt_rows_spec(tm):
    last = M_P // tm - 1
    return pl.BlockSpec((tm, D_MODEL), lambda i: (jnp.minimum(i, last), 0))


def _latent_rows_spec(tm):
    first = M_P // tm
    return pl.BlockSpec((tm, D_MODEL), lambda i: (jnp.maximum(i - first, 0), 0))


def _mod_kernel(c_ref, w_ref, b_ref, o_ref):
    c = c_ref[...]
    s = c * (1.0 / (1.0 + jnp.exp(-c)))
    o_ref[...] = jnp.dot(s.astype(BF16), w_ref[...].astype(BF16),
                         preferred_element_type=F32) + b_ref[...]


def _modulation(cond, w_mod, b_mod):
    tn = 1024
    n_out = N_MOD * D_MODEL
    return pl.pallas_call(
        _mod_kernel,
        out_shape=jax.ShapeDtypeStruct((DEPTH, COND_PAD, n_out), F32),
        grid=(DEPTH, n_out // tn),
        in_specs=[
            pl.BlockSpec((COND_PAD, D_MODEL), lambda l, n: (0, 0)),
            pl.BlockSpec((None, D_MODEL, tn), lambda l, n: (l, 0, n)),
            pl.BlockSpec((None, 1, tn), lambda l, n: (l, 0, n)),
        ],
        out_specs=pl.BlockSpec((None, COND_PAD, tn), lambda l, n: (l, 0, n)),
        compiler_params=_params("arbitrary", "arbitrary"),
        name="modulation",
    )(cond, w_mod, b_mod.reshape(DEPTH, 1, n_out))


def _cast_kernel(w_ref, o_ref):
    o_ref[...] = w_ref[...].astype(BF16)


def _cast_bf16(w, rows=512):
    n_l, k, n = w.shape
    spec = pl.BlockSpec((None, rows, n), lambda l, r: (l, r, 0))
    return pl.pallas_call(
        _cast_kernel,
        out_shape=jax.ShapeDtypeStruct(w.shape, BF16),
        grid=(n_l, k // rows),
        in_specs=[spec],
        out_specs=spec,
        compiler_params=_params("arbitrary", "arbitrary"),
        name="cast_weight",
    )(w)


def _embed_kernel(xp_ref, xs_ref, sc_ref, sh_ref, u_ref):
    x = jnp.where(pl.program_id(0) * ROW_TILE < M_P, xp_ref[...], xs_ref[...])
    u_ref[...] = (x * (1.0 + sc_ref[...]) + sh_ref[...]).astype(BF16)


def _embed(xp, xs, mod):
    tm = ROW_TILE
    return pl.pallas_call(
        _embed_kernel,
        out_shape=jax.ShapeDtypeStruct((M_ALL, D_MODEL), BF16),
        grid=(M_ALL // tm,),
        in_specs=[_prompt_rows_spec(tm), _latent_rows_spec(tm),
                  _mod_spec(0, 1, tm), _mod_spec(0, 0, tm)],
        out_specs=pl.BlockSpec((tm, D_MODEL), lambda i: (i, 0)),
        compiler_params=_params("arbitrary"),
        name="embed",
    )(xp, xs, mod, mod)


def _rms_chunk(x, g):
    ms = jnp.mean(x * x, axis=-1, keepdims=True)
    return x * lax.rsqrt(ms + RMS_EPS) * g


def _rope_chunk(x, cos, sin_a, sin_b):
    return x * cos + pltpu.roll(x, LANE - 32, 1) * sin_a + pltpu.roll(x, 32, 1) * sin_b


def _proj_kernel(*refs, tn, rms, rope):
    it = iter(refs)
    u_ref, w_ref = next(it), next(it)
    g_ref = next(it) if rms else None
    tabs = (next(it), next(it), next(it)) if rope else None
    o_ref, wbf = next(it), next(it)

    @pl.when(pl.program_id(1) == 0)
    def _():
        wbf[...] = w_ref[...].astype(BF16)

    if rms and rope:
        acc = jnp.dot(u_ref[...], wbf[...], preferred_element_type=F32)
        chunks = [acc[:, c * LANE:(c + 1) * LANE] for c in range(tn // LANE)]
        g = g_ref[...]
        tab = tuple(t[...] for t in tabs)
        rinv = [lax.rsqrt(jnp.mean(x * x, axis=-1, keepdims=True) + RMS_EPS) for x in chunks]

        @pl.when(pl.program_id(1) >= 0)
        def _():
            for c, (x, r) in enumerate(zip(chunks, rinv)):
                o_ref[:, c * LANE:(c + 1) * LANE] = (_rope_chunk(x * g, *tab) * r).astype(o_ref.dtype)
    elif rms or rope:
        rows = u_ref.shape[0] // 2
        g = g_ref[...] if rms else None
        for r in range(2):
            sl = slice(r * rows, (r + 1) * rows)
            acc = jnp.dot(u_ref[sl, :], wbf[...], preferred_element_type=F32)
            for c in range(tn // LANE):
                x = acc[:, c * LANE:(c + 1) * LANE]
                if rms:
                    x = _rms_chunk(x, g)
                else:
                    x = _rope_chunk(x, *(t[sl, :] for t in tabs))
                o_ref[sl, c * LANE:(c + 1) * LANE] = x.astype(o_ref.dtype)
    else:
        acc = jnp.dot(u_ref[...], wbf[...], preferred_element_type=F32)
        o_ref[...] = acc.astype(o_ref.dtype)


def _proj(u, w, w_layer, col0, ncols, m0, m_rows, *, out_dtype, tn=1024, tm=1024,
          gains=None, gain_split=None, rope=None, name="proj"):
    k = u.shape[1]
    assert col0 % tn == 0 and ncols % tn == 0 and m0 % tm == 0 and m_rows % tm == 0
    n_t, m_t, mt0, nt0 = ncols // tn, m_rows // tm, m0 // tm, col0 // tn
    ins = [u, w]
    in_specs = [
        pl.BlockSpec((tm, k), lambda n, m: (mt0 + m, 0)),
        pl.BlockSpec((None, k, tn), lambda n, m: (w_layer, 0, nt0 + n)),
    ]
    if gains is not None:
        ins.append(gains)
        in_specs.append(pl.BlockSpec(
            (None, 1, LANE), lambda n, m: (jnp.where(n * tn < gain_split, 0, 1), 0, 0)))
    if rope is not None:
        assert m0 >= M_P
        per = DEC_SEQ // tm
        for t in rope:
            ins.append(t)
            in_specs.append(pl.BlockSpec((tm, LANE), lambda n, m: ((mt0 + m) % per, 0)))
    kern = functools.partial(_proj_kernel, tn=tn, rms=gains is not None, rope=rope is not None)
    return pl.pallas_call(
        kern,
        out_shape=jax.ShapeDtypeStruct((m_rows, ncols), out_dtype),
        grid=(n_t, m_t),
        in_specs=in_specs,
        out_specs=pl.BlockSpec((tm, tn), lambda n, m: (m, n)),
        scratch_shapes=[pltpu.VMEM((k, tn), BF16)],
        compiler_params=_params("arbitrary", "arbitrary"),
        name=name,
    )(*ins)


def _qkt(q, k):
    return lax.dot_general(q, k, (((1,), (1,)), ((), ())), preferred_element_type=F32)


def _softmax_terms(s, scale):
    m = s.max(axis=-1, keepdims=True)
    e = jnp.exp2((s - m) * (scale * LOG2E))
    return e, 1.0 / e.sum(axis=-1, keepdims=True)


def _diff_lambda(lp_ref, lam_init):
    lp = lp_ref[...]
    s1 = jnp.sum(lp[0:1] * lp[1:2], axis=-1, keepdims=True)
    s2 = jnp.sum(lp[2:3] * lp[3:4], axis=-1, keepdims=True)
    return jnp.exp(s1) - jnp.exp(s2) + lam_init


def _diff_weights(s, lam):
    half = s.shape[0] // 2
    e, r = _softmax_terms(s, A_HEAD_DIM ** -0.5)
    return (e[:half] * r[:half] - e[half:] * (lam * r[half:])).astype(BF16)


def _sub_norm(o, gain):
    ms = jnp.mean(o * o, axis=-1, keepdims=True)
    return (o * lax.rsqrt(ms + RMS_EPS) * gain).astype(BF16)


A_CHUNKS = D_MODEL // LANE
B_CHUNKS = B_KV // LANE


def _a_v_rows(v, batch, seq):
    n_l = v.shape[1]
    v = v.reshape(batch, n_l, seq, A_HEADS, 2, LANE).transpose(0, 1, 2, 4, 3, 5)
    return v.reshape(batch, n_l, seq * A_CHUNKS, LANE)


def _a_v_from_rows(r, batch, seq):
    n_l = r.shape[1]
    v = r.reshape(batch, n_l, seq, 2, A_HEADS, LANE).transpose(0, 1, 2, 4, 3, 5)
    return v.reshape(batch, n_l, seq, A_HEADS, 2 * LANE)


def _diff_prompt_kernel(*refs, lam_init, n_prev, emit):
    lp_ref, g_ref, q_ref, k_ref, v_ref = refs[:5]
    prev = refs[5:5 + n_prev]
    o_ref = refs[5 + n_prev]
    s_scr, a_scr, o_scr = refs[-3:]
    lam = _diff_lambda(lp_ref, lam_init)
    gain = g_ref[...] * (1.0 - lam_init)
    w = 2 * A_HEAD_DIM
    for h in range(A_HEADS):
        for p in range(2):
            cols = slice(h * w + p * A_HEAD_DIM, h * w + (p + 1) * A_HEAD_DIM)
            s_scr[(p * A_HEADS + h) * SEQ:(p * A_HEADS + h + 1) * SEQ, :] = _qkt(
                q_ref[:, cols], k_ref[:, cols].astype(BF16))

    @pl.when(pl.program_id(0) >= 0)
    def _():
        a_scr[...] = _diff_weights(s_scr[...], lam)
        for h in range(A_HEADS):
            rows = slice(h * SEQ, (h + 1) * SEQ)
            o_scr[rows, :] = jnp.dot(a_scr[rows, :], v_ref[:, h * w:(h + 1) * w].astype(BF16),
                                     preferred_element_type=F32)
        y = _sub_norm(o_scr[...], gain)
        for h in range(A_HEADS):
            o_ref[:, h * w:(h + 1) * w] = y[h * SEQ:(h + 1) * SEQ, :]

    if emit:
        nk_ref, nv_ref = refs[6 + n_prev], refs[7 + n_prev]
        layers = [(p, 0, p, D_MODEL) for p in prev] + [(k_ref, 0, v_ref, 0)]
        for jj, (kr, k0, vr, v0) in enumerate(layers):
            nk_ref[jj] = kr[:, k0:k0 + D_MODEL].reshape(SEQ, A_CHUNKS, LANE)
            for t in range(2):
                halves = [vr[:, v0 + (2 * h + t) * LANE:v0 + (2 * h + t + 1) * LANE]
                          for h in range(A_HEADS)]
                nv_ref[jj, :, t * A_HEADS:(t + 1) * A_HEADS, :] = jnp.concatenate(
                    halves, axis=1).reshape(SEQ, A_HEADS, LANE)


def _diff_prompt(lp, g, q, kv, j, lam_init, prev_kv=None):
    emit = prev_kv is not None
    prev_kv = list(prev_kv or [])
    n_layers = len(prev_kv) + 1
    att_shape = jax.ShapeDtypeStruct((M_P, D_MODEL), BF16)
    att_spec = pl.BlockSpec((SEQ, D_MODEL), lambda b: (b, 0))
    if emit:
        new_shape = jax.ShapeDtypeStruct((BATCH, n_layers, SEQ, A_CHUNKS, LANE), F32)
        new_spec = pl.BlockSpec((None, n_layers, SEQ, A_CHUNKS, LANE), lambda b: (b, 0, 0, 0, 0))
        out_shape, out_specs = (att_shape, new_shape, new_shape), (att_spec, new_spec, new_spec)
    else:
        out_shape, out_specs = att_shape, att_spec
    return pl.pallas_call(
        functools.partial(_diff_prompt_kernel, lam_init=lam_init, n_prev=len(prev_kv), emit=emit),
        out_shape=out_shape,
        grid=(BATCH,),
        in_specs=[
            pl.BlockSpec((None, 4, A_HEAD_DIM), lambda b: (j, 0, 0)),
            pl.BlockSpec((None, 1, 2 * A_HEAD_DIM), lambda b: (j, 0, 0)),
            pl.BlockSpec((SEQ, D_MODEL), lambda b: (b, 0)),
            pl.BlockSpec((SEQ, D_MODEL), lambda b: (b, 0)),
            pl.BlockSpec((SEQ, D_MODEL), lambda b: (b, 1)),
        ] + [pl.BlockSpec((SEQ, 2 * D_MODEL), lambda b: (b, 0)) for _ in prev_kv],
        out_specs=out_specs,
        scratch_shapes=[pltpu.VMEM((2 * A_HEADS * SEQ, SEQ), F32),
                        pltpu.VMEM((A_HEADS * SEQ, SEQ), BF16),
                        pltpu.VMEM((A_HEADS * SEQ, 2 * A_HEAD_DIM), F32)],
        compiler_params=_params("arbitrary"),
        name="diff_attn_prompt",
    )(lp, g, q, kv, kv, *prev_kv)


def _diff_latent_kernel(lp_ref, g_ref, q_ref, k_ref, v_ref, ck_ref, cv_ref, o_ref,
                        s_scr, a_scr, *, lam_init):
    lam = _diff_lambda(lp_ref, lam_init)
    gain = g_ref[...] * (1.0 - lam_init)
    h = pl.program_id(1)
    tq, d = q_ref.shape[0], A_HEAD_DIM

    def cached(ref, r0, r1):
        halves = [ref[pl.ds(r, PAST_LEN, stride=A_CHUNKS), :] for r in (r0, r1)]
        return jnp.concatenate(halves, axis=1).astype(BF16)

    ck = cached(ck_ref, 2 * h, 2 * h + 1)
    for p in range(2):
        q = q_ref[:, p * d:(p + 1) * d]
        s_scr[p * tq:(p + 1) * tq, :PAST_LEN] = _qkt(q, ck[:, p * d:(p + 1) * d])
        s_scr[p * tq:(p + 1) * tq, PAST_LEN:] = _qkt(q, k_ref[:, p * d:(p + 1) * d])

    @pl.when(h >= 0)
    def _():
        a_scr[...] = _diff_weights(s_scr[...], lam)
        cv = cached(cv_ref, h, A_HEADS + h)
        o = (jnp.dot(a_scr[:, :PAST_LEN], cv, preferred_element_type=F32)
             + jnp.dot(a_scr[:, PAST_LEN:], v_ref[...], preferred_element_type=F32))
        o_ref[...] = _sub_norm(o, gain)


def _diff_latent(lp, g, qk, v, ck, cv, j, lam_init, tq=1024):
    w = 2 * A_HEAD_DIM
    per = DEC_SEQ // tq
    ctx_spec = pl.BlockSpec((None, None, PAST_LEN * A_CHUNKS, LANE), lambda b, h, t: (b, j, 0, 0))
    return pl.pallas_call(
        functools.partial(_diff_latent_kernel, lam_init=lam_init),
        out_shape=jax.ShapeDtypeStruct((M_S, D_MODEL), BF16),
        grid=(DEC_BATCH, A_HEADS, per),
        in_specs=[
            pl.BlockSpec((None, 4, A_HEAD_DIM), lambda b, h, t: (j, 0, 0)),
            pl.BlockSpec((None, 1, w), lambda b, h, t: (j, 0, 0)),
            pl.BlockSpec((tq, w), lambda b, h, t: (b * per + t, h)),
            pl.BlockSpec((DEC_SEQ, w), lambda b, h, t: (b, A_HEADS + h)),
            pl.BlockSpec((DEC_SEQ, w), lambda b, h, t: (b, h)),
            ctx_spec, ctx_spec,
        ],
        out_specs=pl.BlockSpec((tq, w), lambda b, h, t: (b * per + t, h)),
        scratch_shapes=[pltpu.VMEM((2 * tq, PAST_LEN + DEC_SEQ), F32),
                        pltpu.VMEM((tq, PAST_LEN + DEC_SEQ), BF16)],
        compiler_params=_params("arbitrary", "arbitrary", "arbitrary"),
        name="diff_attn_latent",
    )(lp, g, qk, qk, v, ck, cv)


def _gqa_prompt_kernel(*refs, n_prev, emit):
    q_ref, k_ref, v_ref = refs[:3]
    prev = refs[3:3 + 2 * n_prev]
    o_ref = refs[3 + 2 * n_prev]
    s_scr, a_scr, o_scr = refs[-3:]
    d = B_HEAD_DIM
    for n in range(B_KV_HEADS):
        k = k_ref[:, n * d:(n + 1) * d].astype(BF16)
        for g in range(B_GROUP):
            hd = n * B_GROUP + g
            s_scr[hd * SEQ:(hd + 1) * SEQ, :] = _qkt(q_ref[:, hd * d:(hd + 1) * d], k)

    @pl.when(pl.program_id(0) >= 0)
    def _():
        e, r = _softmax_terms(s_scr[...], B_HEAD_DIM ** -0.5)
        a_scr[...] = (e * r).astype(BF16)
        group_rows = B_GROUP * SEQ
        for n in range(B_KV_HEADS):
            rows = slice(n * group_rows, (n + 1) * group_rows)
            o_scr[rows, :] = jnp.dot(a_scr[rows, :], v_ref[:, n * d:(n + 1) * d].astype(BF16),
                                     preferred_element_type=F32)
        for hd in range(B_HEADS):
            o_ref[:, hd * d:(hd + 1) * d] = o_scr[hd * SEQ:(hd + 1) * SEQ, :].astype(BF16)

    if emit:
        nk_ref, nv_ref = refs[4 + 2 * n_prev], refs[5 + 2 * n_prev]
        layers = [(prev[2 * i], prev[2 * i + 1]) for i in range(n_prev)] + [(k_ref, v_ref)]
        for jj, (kr, vr) in enumerate(layers):
            for n in range(B_CHUNKS):
                rows = pl.ds(n, SEQ, stride=B_CHUNKS)
                nk_ref[jj, rows, :] = kr[:, n * LANE:(n + 1) * LANE]
                nv_ref[jj, rows, :] = vr[:, n * LANE:(n + 1) * LANE]


def _gqa_prompt(q, k, v, prev_kv=None):
    emit = prev_kv is not None
    prev = [a for pair in (prev_kv or []) for a in pair]
    n_layers = len(prev) // 2 + 1
    kv_spec = pl.BlockSpec((SEQ, B_KV), lambda b: (b, 0))
    att_shape = jax.ShapeDtypeStruct((M_P, D_MODEL), BF16)
    att_spec = pl.BlockSpec((SEQ, D_MODEL), lambda b: (b, 0))
    if emit:
        new_shape = jax.ShapeDtypeStruct((BATCH, n_layers, SEQ * B_CHUNKS, LANE), F32)
        new_spec = pl.BlockSpec((None, n_layers, SEQ * B_CHUNKS, LANE), lambda b: (b, 0, 0, 0))
        out_shape, out_specs = (att_shape, new_shape, new_shape), (att_spec, new_spec, new_spec)
    else:
        out_shape, out_specs = att_shape, att_spec
    return pl.pallas_call(
        functools.partial(_gqa_prompt_kernel, n_prev=len(prev) // 2, emit=emit),
        out_shape=out_shape,
        grid=(BATCH,),
        in_specs=[att_spec, kv_spec, kv_spec] + [kv_spec for _ in prev],
        out_specs=out_specs,
        scratch_shapes=[pltpu.VMEM((B_HEADS * SEQ, SEQ), F32),
                        pltpu.VMEM((B_HEADS * SEQ, SEQ), BF16),
                        pltpu.VMEM((B_HEADS * SEQ, B_HEAD_DIM), F32)],
        compiler_params=_params("arbitrary"),
        name="gqa_attn_prompt",
    )(q, k, v, *prev)


def _gqa_latent_kernel(q_ref, k_ref, v_ref, ck_ref, cv_ref, o_ref, s_scr, a_scr):
    d, tq = B_HEAD_DIM, q_ref.shape[0]
    n = pl.program_id(1)
    cached = pl.ds(n, PAST_LEN, stride=B_CHUNKS)
    ck = ck_ref[cached, :].astype(BF16)
    for g in range(B_GROUP):
        q = q_ref[:, g * d:(g + 1) * d]
        s_scr[g * tq:(g + 1) * tq, :PAST_LEN] = _qkt(q, ck)
        s_scr[g * tq:(g + 1) * tq, PAST_LEN:] = _qkt(q, k_ref[...])

    @pl.when(n >= 0)
    def _():
        e, r = _softmax_terms(s_scr[...], B_HEAD_DIM ** -0.5)
        a_scr[...] = (e * r).astype(BF16)
        o = (jnp.dot(a_scr[:, :PAST_LEN], cv_ref[cached, :].astype(BF16),
                     preferred_element_type=F32)
             + jnp.dot(a_scr[:, PAST_LEN:], v_ref[...], preferred_element_type=F32))
        for g in range(B_GROUP):
            o_ref[:, g * d:(g + 1) * d] = o[g * tq:(g + 1) * tq, :].astype(BF16)


def _gqa_latent(qk, v, ck, cv, j, tq=1024):
    d = B_HEAD_DIM
    gw = B_GROUP * d
    per = DEC_SEQ // tq
    ctx_spec = pl.BlockSpec((None, None, PAST_LEN * B_CHUNKS, LANE), lambda b, n, t: (b, j, 0, 0))
    return pl.pallas_call(
        _gqa_latent_kernel,
        out_shape=jax.ShapeDtypeStruct((M_S, D_MODEL), BF16),
        grid=(DEC_BATCH, B_KV_HEADS, per),
        in_specs=[
            pl.BlockSpec((tq, gw), lambda b, n, t: (b * per + t, n)),
            pl.BlockSpec((DEC_SEQ, d), lambda b, n, t: (b, D_MODEL // d + n)),
            pl.BlockSpec((DEC_SEQ, d), lambda b, n, t: (b, n)),
            ctx_spec, ctx_spec,
        ],
        out_specs=pl.BlockSpec((tq, gw), lambda b, n, t: (b * per + t, n)),
        scratch_shapes=[pltpu.VMEM((B_GROUP * tq, PAST_LEN + DEC_SEQ), F32),
                        pltpu.VMEM((B_GROUP * tq, PAST_LEN + DEC_SEQ), BF16)],
        compiler_params=_params("arbitrary", "arbitrary", "arbitrary"),
        name="gqa_attn_latent",
    )(qk, qk, v, ck, cv)


def _dft_tables(s):
    def cs(n):
        idx = (np.arange(n)[:, None] * np.arange(n)[None, :]) % n
        ang = 2.0 * np.pi * idx / n
        return np.cos(ang), np.sin(ang)
    cc, sc = cs(C_GROUP_DIM)
    c_s, s_s = cs(s)
    t_chan = np.concatenate([cc, sc], axis=1).astype(np.float32)
    t_pos = np.concatenate([c_s, -s_s], axis=1).astype(np.float32)
    return jnp.asarray(t_chan).astype(BF16), jnp.asarray(t_pos).astype(BF16)


def _dft_kernel(u_ref, tc_ref, tp_ref, o_ref, *, groups, seqs, s, norm):
    tc, tp = tc_ref[...], tp_ref[...]
    for g in range(groups):
        sl = slice(g * C_GROUP_DIM, (g + 1) * C_GROUP_DIM)
        xcs = jnp.dot(u_ref[:, sl], tc, preferred_element_type=F32).astype(BF16)
        for q in range(seqs):
            rows = slice(q * s, (q + 1) * s)
            stacked = jnp.concatenate([xcs[rows, :C_GROUP_DIM], xcs[rows, C_GROUP_DIM:]], axis=0)
            y = jnp.dot(tp, stacked, preferred_element_type=F32)
            o_ref[rows, sl] = (y * norm).astype(BF16)


def _dft(u, s, batch, row0, groups_per_step, seqs_per_step=1):
    t_chan, t_pos = _dft_tables(s)
    gw = groups_per_step * C_GROUP_DIM
    rows = seqs_per_step * s
    rb0 = row0 // rows
    return pl.pallas_call(
        functools.partial(_dft_kernel, groups=groups_per_step, seqs=seqs_per_step, s=s,
                          norm=1.0 / math.sqrt(s * C_GROUP_DIM)),
        out_shape=jax.ShapeDtypeStruct((batch * s, D_MODEL), BF16),
        grid=(batch // seqs_per_step, C_GROUPS // groups_per_step),
        in_specs=[
            pl.BlockSpec((rows, gw), lambda b, g: (rb0 + b, g)),
            pl.BlockSpec(t_chan.shape, lambda b, g: (0, 0)),
            pl.BlockSpec(t_pos.shape, lambda b, g: (0, 0)),
        ],
        out_specs=pl.BlockSpec((rows, gw), lambda b, g: (b, g)),
        compiler_params=_params("arbitrary", "arbitrary"),
        name="dft",
    )(u, t_chan, t_pos)


ROW_GROUPS = 2


def _post_norm(y, g, b):
    mu = jnp.mean(y, axis=-1, keepdims=True)
    yc = y - mu
    var = jnp.mean(yc * yc, axis=-1, keepdims=True)
    return yc * lax.rsqrt(var + LN_EPS) * g + b


def _oproj_kernel(*refs, split_x):
    ap_ref, as_ref, w_ref = refs[:3]
    x_refs = refs[3:5] if split_x else refs[3:4]
    gate_ref, g_ref, b_ref, sc_ref, sh_ref, x1_ref, u_ref = refs[3 + len(x_refs):]
    is_prompt = pl.program_id(0) * ROW_TILE < M_P
    gate, g, b = gate_ref[...], g_ref[...], b_ref[...]
    sc1, sh = 1.0 + sc_ref[...], sh_ref[...]
    rows = ROW_TILE // ROW_GROUPS
    for r in range(ROW_GROUPS):
        sl = slice(r * rows, (r + 1) * rows)
        a = jnp.where(is_prompt, ap_ref[sl, :], as_ref[sl, :])
        mixed = jnp.dot(a, w_ref[...], preferred_element_type=F32)
        if split_x:
            x = jnp.where(is_prompt, x_refs[0][sl, :], x_refs[1][sl, :])
        else:
            x = x_refs[0][sl, :]
        x1 = _post_norm(ALPHA * x + gate * mixed, g, b)
        x1_ref[sl, :] = x1
        u_ref[sl, :] = (x1 * sc1 + sh).astype(BF16)


def _oproj(a_p, a_s, w_bf, w_layer, x, mod, ln_g, ln_b, l):
    tm = ROW_TILE
    row = pl.BlockSpec((tm, D_MODEL), lambda i: (i, 0))
    split_x = isinstance(x, tuple)
    xs = list(x) if split_x else [x]
    x_specs = [_prompt_rows_spec(tm), _latent_rows_spec(tm)] if split_x else [row]
    return pl.pallas_call(
        functools.partial(_oproj_kernel, split_x=split_x),
        out_shape=(jax.ShapeDtypeStruct((M_ALL, D_MODEL), F32),
                   jax.ShapeDtypeStruct((M_ALL, D_MODEL), BF16)),
        grid=(M_ALL // tm,),
        in_specs=[
            _prompt_rows_spec(tm), _latent_rows_spec(tm),
            pl.BlockSpec((None, D_MODEL, D_MODEL), lambda i: (w_layer, 0, 0),
                         pipeline_mode=pl.Buffered(1)),
            *x_specs,
            _mod_spec(l, 2, tm),
            _vec_spec(l, 0), _vec_spec(l, 0),
            _mod_spec(l, 4, tm),
            _mod_spec(l, 3, tm),
        ],
        out_specs=(row, row),
        compiler_params=_params("arbitrary"),
        name="oproj_ln",
    )(a_p, a_s, w_bf, *xs, mod, ln_g, ln_b, mod, mod)


MLP_FF_PIECE = 512


def _mlp_kernel(*refs, n_tiles, n_ff, last):
    u_ref, wu_ref, wd_ref, x_ref, gate_ref, g_ref, b_ref = refs[:7]
    accs = refs[-2:]
    i, f = pl.program_id(0), pl.program_id(1)
    tm = u_ref.shape[0]
    close_rows = tm // n_ff

    for parity in (0, 1):
        @pl.when(jnp.logical_and(f == 0, i == parity))
        def _():
            accs[parity][...] = jnp.zeros_like(accs[parity])

    def matmuls(acc):
        tf = wu_ref.shape[1]
        for c in range(tf // MLP_FF_PIECE):
            cols = slice(c * MLP_FF_PIECE, (c + 1) * MLP_FF_PIECE)
            h = jnp.dot(u_ref[...], wu_ref[:, cols].astype(BF16), preferred_element_type=F32)
            h = jnp.maximum(h, 0.0)
            h = (h * h).astype(BF16)
            acc[...] += jnp.dot(h, wd_ref[cols, :].astype(BF16), preferred_element_type=F32)

    def close(prev):
        rows = pl.ds(pl.multiple_of(f * close_rows, close_rows), close_rows)
        y = _post_norm(ALPHA * x_ref[...] + gate_ref[...] * prev[rows, :], g_ref[...], b_ref[...])
        prev[rows, :] = jnp.zeros((close_rows, prev.shape[1]), F32)
        if last:
            yp_ref, ys_ref = refs[7:9]
            prev_is_prompt = (i - 1) * tm < M_P

            @pl.when(prev_is_prompt)
            def _():
                yp_ref[...] = y

            @pl.when(jnp.logical_not(prev_is_prompt))
            def _():
                ys_ref[...] = y
        else:
            sc_ref, sh_ref, x2_ref, un_ref = refs[7:11]
            x2_ref[...] = y
            un_ref[...] = (y * (1.0 + sc_ref[...]) + sh_ref[...]).astype(BF16)

    @pl.when(i == 0)
    def _():
        matmuls(accs[0])

    middle = jnp.logical_and(i > 0, i < n_tiles)
    for parity in (0, 1):
        @pl.when(jnp.logical_and(middle, i % 2 == parity))
        def _():
            matmuls(accs[parity])
            close(accs[1 - parity])

    @pl.when(i == n_tiles)
    def _():
        close(accs[(n_tiles - 1) % 2])


def _mlp(u, w_up, w_down, x, mod, ln_g, ln_b, l, tm=1024, tf=512):
    last = l + 1 == DEPTH
    n_tiles, n_ff = M_ALL // tm, D_FF // tf
    close_rows = tm // n_ff

    def close_block(i, f):
        return jnp.where(i == 0, 0, (i - 1) * n_ff + f)

    def mod_prev(layer, which):
        def idx(i, f):
            return (layer, _cond_row(jnp.maximum(i - 1, 0), tm), which, 0, 0)
        return pl.BlockSpec((None, None, None, 1, D_MODEL), idx)

    def ff_block(i, f):
        return jnp.where(i < n_tiles, f, n_ff - 1)

    row = pl.BlockSpec((close_rows, D_MODEL), lambda i, f: (close_block(i, f), 0))
    ins = [u, w_up, w_down, x, mod, ln_g, ln_b]
    in_specs = [
        pl.BlockSpec((tm, D_MODEL), lambda i, f: (jnp.minimum(i, n_tiles - 1), 0)),
        pl.BlockSpec((None, D_MODEL, tf), lambda i, f: (l, 0, ff_block(i, f))),
        pl.BlockSpec((None, tf, D_MODEL), lambda i, f: (l, ff_block(i, f), 0)),
        row, mod_prev(l, 5), _vec_spec(l, 1), _vec_spec(l, 1),
    ]
    if last:
        n_p = M_P // close_rows
        out_shape = (jax.ShapeDtypeStruct((M_P, D_MODEL), F32),
                     jax.ShapeDtypeStruct((M_S, D_MODEL), F32))
        out_specs = (
            pl.BlockSpec((close_rows, D_MODEL),
                         lambda i, f: (jnp.minimum(close_block(i, f), n_p - 1), 0)),
            pl.BlockSpec((close_rows, D_MODEL),
                         lambda i, f: (jnp.maximum(close_block(i, f) - n_p, 0), 0)),
        )
    else:
        ins += [mod, mod]
        in_specs += [mod_prev(l + 1, 1), mod_prev(l + 1, 0)]
        out_shape = (jax.ShapeDtypeStruct((M_ALL, D_MODEL), F32),
                     jax.ShapeDtypeStruct((M_ALL, D_MODEL), BF16))
        out_specs = (row, row)
    return pl.pallas_call(
        functools.partial(_mlp_kernel, n_tiles=n_tiles, n_ff=n_ff, last=last),
        out_shape=out_shape,
        grid=(n_tiles + 1, n_ff),
        in_specs=in_specs,
        out_specs=out_specs,
        scratch_shapes=[pltpu.VMEM((tm, D_MODEL), F32), pltpu.VMEM((tm, D_MODEL), F32)],
        compiler_params=_params("arbitrary", "arbitrary"),
        name="mlp",
    )(*ins)


def _rope_tables():
    n_freq = A_HEAD_DIM // 4
    pos = np.arange(DEC_SEQ)
    row = (pos // GRID_W).astype(np.float32)
    col = (pos % GRID_W).astype(np.float32)
    inv_freq = (ROPE_BASE ** (-np.arange(n_freq, dtype=np.float32) / n_freq)).astype(np.float32)
    ar = row[:, None] * inv_freq
    ac = col[:, None] * inv_freq
    cr, sr, cc, sc = np.cos(ar), np.sin(ar), np.cos(ac), np.sin(ac)
    z = np.zeros_like(sr)
    cos = np.concatenate([cr, cr, cc, cc], axis=1)
    sin_a = np.concatenate([-sr, z, -sc, z], axis=1)
    sin_b = np.concatenate([z, sr, z, sc], axis=1)
    return tuple(jnp.asarray(t.astype(np.float32)) for t in (cos, sin_a, sin_b))


def kernel(x_prompt, x_sample, cache_a_k, cache_a_v, cache_b_k, cache_b_v, c, c_ctx, w_mod, b_mod, ln_g, ln_b, w_up, w_down, a_w_qkv, a_w_o, a_lambda, a_subln_g, b_w_qkv, b_w_o, b_q_norm_g, b_k_norm_g, c_w_f):
    n_a = a_w_qkv.shape[0]
    n_b = b_w_qkv.shape[0]
    cond = jnp.zeros((COND_PAD, D_MODEL), F32).at[0].set(c_ctx).at[1:N_COND].set(c)
    mod = _modulation(cond, w_mod, b_mod).reshape(DEPTH, COND_PAD, N_MOD, 1, D_MODEL)
    ln_g4 = ln_g.reshape(DEPTH, 2, 1, D_MODEL)
    ln_b4 = ln_b.reshape(DEPTH, 2, 1, D_MODEL)
    rope = _rope_tables()
    w_o_bf = (_cast_bf16(a_w_o), _cast_bf16(b_w_o), _cast_bf16(c_w_f))

    ck_a = cache_a_k.reshape(DEC_BATCH, n_a, PAST_LEN * A_CHUNKS, LANE)
    cv_a = _a_v_rows(cache_a_v, DEC_BATCH, PAST_LEN)
    ck_b = cache_b_k.reshape(DEC_BATCH, n_b, PAST_LEN * B_CHUNKS, LANE)
    cv_b = cache_b_v.reshape(DEC_BATCH, n_b, PAST_LEN * B_CHUNKS, LANE)
    subln = a_subln_g.reshape(n_a, 1, 2 * A_HEAD_DIM)
    b_gains = jnp.stack([b_q_norm_g, b_k_norm_g], axis=1).reshape(n_b, 2, 1, B_HEAD_DIM)

    x = (x_prompt.reshape(M_P, D_MODEL), x_sample.reshape(M_S, D_MODEL))
    u = _embed(*x, mod)

    a_kv, b_kv = [], []
    for l in range(DEPTH):
        kind, j = l % N_MIXERS, l // N_MIXERS
        if kind == 0:
            lam_init = 0.8 - 0.6 * math.exp(-0.3 * l)
            q_p = _proj(u, a_w_qkv, j, 0, D_MODEL, 0, M_P, out_dtype=BF16, name="a_q_prompt")
            kv_p = _proj(u, a_w_qkv, j, D_MODEL, 2 * D_MODEL, 0, M_P, out_dtype=F32,
                         name="a_kv_prompt")
            qk_s = _proj(u, a_w_qkv, j, 0, 2 * D_MODEL, M_P, M_S, out_dtype=BF16, rope=rope,
                         name="a_qk_latent")
            v_s = _proj(u, a_w_qkv, j, 2 * D_MODEL, D_MODEL, M_P, M_S, out_dtype=BF16,
                        name="a_v_latent")
            if j + 1 < n_a:
                att_p = _diff_prompt(a_lambda, subln, q_p, kv_p, j, lam_init)
                a_kv.append(kv_p)
            else:
                att_p, new_a_k, new_a_v = _diff_prompt(a_lambda, subln, q_p, kv_p, j, lam_init,
                                                       prev_kv=a_kv)
            att_s = _diff_latent(a_lambda, subln, qk_s, v_s, ck_a, cv_a, j, lam_init)
            w_o = w_o_bf[0]
        elif kind == 1:
            gj = b_gains[j]
            q_p = _proj(u, b_w_qkv, j, 0, D_MODEL, 0, M_P, out_dtype=BF16, gains=gj,
                        gain_split=D_MODEL, name="b_q_prompt")
            k_p = _proj(u, b_w_qkv, j, D_MODEL, B_KV, 0, M_P, out_dtype=F32, tn=B_KV,
                        gains=gj[1:], gain_split=B_KV, name="b_k_prompt")
            v_p = _proj(u, b_w_qkv, j, D_MODEL + B_KV, B_KV, 0, M_P, out_dtype=F32, tn=B_KV,
                        name="b_v_prompt")
            qk_s = _proj(u, b_w_qkv, j, 0, D_MODEL + B_KV, M_P, M_S, out_dtype=BF16, tn=B_KV,
                         gains=gj, gain_split=D_MODEL, rope=rope, name="b_qk_latent")
            v_s = _proj(u, b_w_qkv, j, D_MODEL + B_KV, B_KV, M_P, M_S, out_dtype=BF16, tn=B_KV,
                        name="b_v_latent")
            if j + 1 < n_b:
                att_p = _gqa_prompt(q_p, k_p, v_p)
                b_kv.append((k_p, v_p))
            else:
                att_p, new_b_k, new_b_v = _gqa_prompt(q_p, k_p, v_p, prev_kv=b_kv)
            att_s = _gqa_latent(qk_s, v_s, ck_b, cv_b, j)
            w_o = w_o_bf[1]
        else:
            att_p = _dft(u, SEQ, BATCH, 0, C_GROUPS, seqs_per_step=4)
            att_s = _dft(u, DEC_SEQ, DEC_BATCH, M_P, 1)
            w_o = w_o_bf[2]
        x, u = _oproj(att_p, att_s, w_o, j, x, mod, ln_g4, ln_b4, l)
        x, u = _mlp(u, w_up, w_down, x, mod, ln_g4, ln_b4, l)

    y_p, y_s = x, u
    return (y_p.reshape(BATCH, SEQ, D_MODEL), y_s.reshape(DEC_BATCH, DEC_SEQ, D_MODEL),
            new_a_k.reshape(BATCH, n_a, SEQ, A_HEADS, 2, A_HEAD_DIM),
            _a_v_from_rows(new_a_v, BATCH, SEQ),
            new_b_k.reshape(BATCH, n_b, SEQ, B_KV_HEADS, B_HEAD_DIM),
            new_b_v.reshape(BATCH, n_b, SEQ, B_KV_HEADS, B_HEAD_DIM))
```
